```python
import jax, jax.numpy as jnp
from jax import lax
import numpy as np

D_MODEL = 1024
BATCH = 4
SEQ = 4096
DEPTH = 1

MLA_HEADS = 8
MLA_Q_RANK = 256
MLA_KV_RANK = 128
MLA_NOPE_DIM = 64
MLA_ROPE_DIM = 32
MLA_QK_DIM = MLA_NOPE_DIM + MLA_ROPE_DIM
MLA_V_DIM = 64
MLA_WIDTH = MLA_HEADS * MLA_V_DIM
DSA_HEADS = 8
DSA_KV_HEADS = 2
DSA_HEAD_DIM = 64
DSA_WIDTH = DSA_HEADS * DSA_HEAD_DIM
DSA_ROT_DIM = DSA_HEAD_DIM // 4
IDX_HEADS = 8
IDX_DIM = 32
IDX_ROT_DIM = IDX_DIM // 4
INDEX_TOPK_MAX = 256
ROPE_THETA = 500000.0
Q_BLOCK = 128
NORM_EPS = 1e-6

IN_SPLITS = (
    MLA_Q_RANK, MLA_KV_RANK, MLA_ROPE_DIM, MLA_WIDTH,
    DSA_HEADS * DSA_HEAD_DIM, DSA_KV_HEADS * DSA_HEAD_DIM,
    DSA_KV_HEADS * DSA_HEAD_DIM, DSA_WIDTH,
    IDX_HEADS * IDX_DIM, IDX_DIM, IDX_HEADS,
    D_MODEL, D_MODEL,
)
D_IN = sum(IN_SPLITS)

kernel_name = 'hybrid_mla_dsa_gated_block'


def rms_norm(x, gain):
    xf = x.astype(jnp.float32)
    y = xf * lax.rsqrt(jnp.mean(xf * xf, axis=-1, keepdims=True) + NORM_EPS)
    return (y * gain.astype(jnp.float32)).astype(x.dtype)


def rope_angles(positions, rot_dim):
    inv_freq = ROPE_THETA ** (-jnp.arange(0, rot_dim, 2, dtype=jnp.float32) / rot_dim)
    ang = positions.astype(jnp.float32)[..., None] * inv_freq
    return jnp.cos(ang), jnp.sin(ang)


def apply_rope(x, cos, sin):
    xf = x.astype(jnp.float32)
    half = xf.shape[-1] // 2
    x1, x2 = xf[..., :half], xf[..., half:]
    c, s = cos[:, :, None, :], sin[:, :, None, :]
    return jnp.concatenate([x1 * c - x2 * s, x2 * c + x1 * s], axis=-1).astype(x.dtype)


def partial_rope(x, cos, sin, rot_dim):
    return jnp.concatenate([apply_rope(x[..., :rot_dim], cos, sin), x[..., rot_dim:]], axis=-1)


def to_blocks(a):
    b, s = a.shape[0], a.shape[1]
    a = a.reshape((b, s // Q_BLOCK, Q_BLOCK) + a.shape[2:])
    return jnp.moveaxis(a, 1, 0)


def from_blocks(a):
    a = jnp.moveaxis(a, 0, 1)
    return a.reshape((a.shape[0], a.shape[1] * a.shape[2]) + a.shape[3:])


def mla_attention(q, k, v):
    s_len = q.shape[1]
    scale = MLA_QK_DIM ** -0.5
    key_pos = jnp.arange(s_len, dtype=jnp.int32)
    starts = jnp.arange(s_len // Q_BLOCK, dtype=jnp.int32) * Q_BLOCK

    def one_block(args):
        qb, start = args
        qpos = start + jnp.arange(Q_BLOCK, dtype=jnp.int32)
        sc = jnp.einsum('bqhd,bkhd->bhqk', qb, k).astype(jnp.float32) * scale
        causal = key_pos[None, :] <= qpos[:, None]
        sc = jnp.where(causal[None, None], sc, -jnp.inf)
        p = jax.nn.softmax(sc, axis=-1).astype(v.dtype)
        return jnp.einsum('bhqk,bkhd->bqhd', p, v)

    return from_blocks(lax.map(one_block, (to_blocks(q), starts)))


def dsa_attention(q, k, v, q_idx, k_idx, w_idx, k_top):
    bsz, s_len, n_heads, hd = q.shape
    n_kv = k.shape[2]
    grp = n_heads // n_kv
    scale = hd ** -0.5
    key_pos = jnp.arange(s_len, dtype=jnp.int32)
    starts = jnp.arange(s_len // Q_BLOCK, dtype=jnp.int32) * Q_BLOCK
    gather = jax.vmap(lambda kb, ib: kb[ib])

    def one_block(args):
        qb, qib, wb, start = args
        qpos = start + jnp.arange(Q_BLOCK, dtype=jnp.int32)
        rel = jax.nn.relu(jnp.einsum('bqhd,bsd->bqhs', qib, k_idx).astype(jnp.float32))
        score = jnp.einsum('bqhs,bqh->bqs', rel, wb.astype(jnp.float32))
        causal = key_pos[None, None, :] <= qpos[None, :, None]
        score = jnp.where(causal, score, -jnp.inf)
        _, sel = lax.top_k(score, k_top)
        valid = sel <= qpos[None, :, None]
        kg = gather(k, sel)
        vg = gather(v, sel)
        qg = qb.reshape(bsz, Q_BLOCK, n_kv, grp, hd)
        sc = jnp.einsum('bqkgd,bqjkd->bqkgj', qg, kg).astype(jnp.float32) * scale
        sc = jnp.where(valid[:, :, None, None, :], sc, -jnp.inf)
        p = jax.nn.softmax(sc, axis=-1).astype(v.dtype)
        o = jnp.einsum('bqkgj,bqjkd->bqkgd', p, vg)
        return o.reshape(bsz, Q_BLOCK, n_heads, hd)

    out = lax.map(one_block, (to_blocks(q), to_blocks(q_idx), to_blocks(w_idx), starts))
    return from_blocks(out)


def setup_inputs(seed: int = 0) -> dict:
    key = jax.random.key(seed)
    ks = jax.random.split(key, 20)
    f32 = jnp.float32

    def w(k, shape, fan_in):
        return jax.random.normal(k, shape, f32) * fan_in ** -0.5

    def g(k, n):
        return 1.0 + 0.01 * jax.random.normal(k, (DEPTH, n), f32)

    x = jax.random.normal(ks[0], (BATCH, SEQ, D_MODEL), f32)
    positions = jnp.broadcast_to(jnp.arange(SEQ, dtype=jnp.int32)[None, :], (BATCH, SEQ))
    return {
        'x': x,
        'positions': positions,
        'norm_gain': g(ks[1], D_MODEL),
        'w_in': w(ks[2], (DEPTH, D_MODEL, D_IN), D_MODEL),
        'b_merge': 0.01 * jax.random.normal(ks[3], (DEPTH, 2, D_MODEL), f32),
        'mla_q_norm': g(ks[4], MLA_Q_RANK),
        'mla_w_uq': w(ks[5], (DEPTH, MLA_Q_RANK, MLA_HEADS * MLA_QK_DIM), MLA_Q_RANK),
        'mla_kv_norm': g(ks[6], MLA_KV_RANK),
        'mla_w_ukv': w(ks[7], (DEPTH, MLA_KV_RANK, MLA_HEADS * (MLA_NOPE_DIM + MLA_V_DIM)), MLA_KV_RANK),
        'mla_q_gain': g(ks[8], MLA_QK_DIM),
        'mla_k_gain': g(ks[9], MLA_QK_DIM),
        'dsa_q_gain': g(ks[10], DSA_HEAD_DIM),
        'dsa_k_gain': g(ks[11], DSA_HEAD_DIM),
        'w_branch_mla': w(ks[12], (DEPTH, MLA_WIDTH, D_MODEL), MLA_WIDTH),
        'w_branch_dsa': w(ks[13], (DEPTH, DSA_WIDTH, D_MODEL), DSA_WIDTH),
        'w_out': w(ks[14], (DEPTH, D_MODEL, D_MODEL), D_MODEL),
    }


def reference(x, positions, norm_gain, w_in, b_merge, mla_q_norm, mla_w_uq, mla_kv_norm, mla_w_ukv,
              mla_q_gain, mla_k_gain, dsa_q_gain, dsa_k_gain, w_branch_mla, w_branch_dsa, w_out):
    bsz, s_len, _ = x.shape
    k_top = min(INDEX_TOPK_MAX, s_len // 4)
    offsets = [int(o) for o in np.cumsum(IN_SPLITS)[:-1]]
    cos_m, sin_m = rope_angles(positions, MLA_ROPE_DIM)
    cos_d, sin_d = rope_angles(positions, DSA_ROT_DIM)
    cos_i, sin_i = rope_angles(positions, IDX_ROT_DIM)

    for l in range(DEPTH):
        h = rms_norm(x, norm_gain[l])
        proj = h @ w_in[l]
        (c_q, c_kv, k_pe, gate_a, q_b, k_b, v_b, gate_b,
         q_i, k_i, w_i, m_a, m_b) = jnp.split(proj, offsets, axis=-1)

        q_a = (rms_norm(c_q, mla_q_norm[l]) @ mla_w_uq[l]).reshape(bsz, s_len, MLA_HEADS, MLA_QK_DIM)
        kv_a = (rms_norm(c_kv, mla_kv_norm[l]) @ mla_w_ukv[l]).reshape(
            bsz, s_len, MLA_HEADS, MLA_NOPE_DIM + MLA_V_DIM)
        k_nope, v_a = kv_a[..., :MLA_NOPE_DIM], kv_a[..., MLA_NOPE_DIM:]
        k_rope = jnp.broadcast_to(k_pe[:, :, None, :], (bsz, s_len, MLA_HEADS, MLA_ROPE_DIM))
        k_a = jnp.concatenate([k_nope, k_rope], axis=-1)
        q_a = rms_norm(q_a, mla_q_gain[l])
        k_a = rms_norm(k_a, mla_k_gain[l])
        q_a = jnp.concatenate([q_a[..., :MLA_NOPE_DIM], apply_rope(q_a[..., MLA_NOPE_DIM:], cos_m, sin_m)], axis=-1)
        k_a = jnp.concatenate([k_a[..., :MLA_NOPE_DIM], apply_rope(k_a[..., MLA_NOPE_DIM:], cos_m, sin_m)], axis=-1)
        o_a = mla_attention(q_a, k_a, v_a).reshape(bsz, s_len, MLA_WIDTH) * jax.nn.silu(gate_a)

        q_b = rms_norm(q_b.reshape(bsz, s_len, DSA_HEADS, DSA_HEAD_DIM), dsa_q_gain[l])
        k_b = rms_norm(k_b.reshape(bsz, s_len, DSA_KV_HEADS, DSA_HEAD_DIM), dsa_k_gain[l])
        v_b = v_b.reshape(bsz, s_len, DSA_KV_HEADS, DSA_HEAD_DIM)
        q_b = partial_rope(q_b, cos_d, sin_d, DSA_ROT_DIM)
        k_b = partial_rope(k_b, cos_d, sin_d, DSA_ROT_DIM)
        q_i = partial_rope(q_i.reshape(bsz, s_len, IDX_HEADS, IDX_DIM), cos_i, sin_i, IDX_ROT_DIM)
        k_i = partial_rope(k_i[:, :, None, :], cos_i, sin_i, IDX_ROT_DIM)[:, :, 0, :]
        w_i = w_i * (IDX_HEADS ** -0.5 * IDX_DIM ** -0.5)
        o_b = dsa_attention(q_b, k_b, v_b, q_i, k_i, w_i, k_top).reshape(bsz, s_len, DSA_WIDTH)
        o_b = o_b * jax.nn.silu(gate_b)

        merged = (jax.nn.sigmoid(m_a + b_merge[l, 0]) * (o_a @ w_branch_mla[l])
                  + jax.nn.sigmoid(m_b + b_merge[l, 1]) * (o_b @ w_branch_dsa[l]))
        x = x + merged @ w_out[l]
    return x
```

```python
import functools

import numpy as np
import jax
import jax.numpy as jnp
from jax import lax
from jax.experimental import pallas as pl
from jax.experimental.pallas import tpu as pltpu

D_MODEL = 1024
MLA_HEADS = 8
MLA_Q_RANK = 256
MLA_KV_RANK = 128
MLA_NOPE = 64
MLA_ROPE = 32
MLA_QK = MLA_NOPE + MLA_ROPE
MLA_V = 64
MLA_WIDTH = MLA_HEADS * MLA_V
DSA_HEADS = 8
DSA_KV_HEADS = 2
DSA_GROUP = DSA_HEADS // DSA_KV_HEADS
DSA_DIM = 64
DSA_WIDTH = DSA_HEADS * DSA_DIM
DSA_ROT = DSA_DIM // 4
IDX_HEADS = 8
IDX_DIM = 32
IDX_ROT = IDX_DIM // 4
TOPK_MAX = 256
ROPE_THETA = 500000.0
NORM_EPS = 1e-6

IN_SPLITS = (
    MLA_Q_RANK, MLA_KV_RANK, MLA_ROPE, MLA_WIDTH,
    DSA_HEADS * DSA_DIM, DSA_KV_HEADS * DSA_DIM, DSA_KV_HEADS * DSA_DIM, DSA_WIDTH,
    IDX_HEADS * IDX_DIM, IDX_DIM, IDX_HEADS,
    D_MODEL, D_MODEL,
)

LANES = 128
VMEM_LIMIT = 56 * 1024 * 1024
INT_MIN = -(2 ** 31)
NEG_BIG = -1e30

OFF_A = 0
W_A = MLA_Q_RANK + MLA_KV_RANK + LANES
OFF_B = OFF_A + W_A
W_B = DSA_WIDTH + 2 * DSA_KV_HEADS * DSA_DIM
OFF_I = OFF_B + W_B
W_I = 2 * IDX_HEADS * IDX_DIM + LANES
OFF_G = OFF_I + W_I
W_G = MLA_WIDTH + DSA_WIDTH + 2 * D_MODEL
W_ALL = OFF_G + W_G

DSA_HEAD_ORDER = tuple(h for j in range(DSA_GROUP) for h in (j, DSA_GROUP + j))

ROW_A, ROW_D, ROW_I = 0, 3, 6
TAB_ROWS = 16


def _rms(v, width):
    return lax.rsqrt(jnp.sum(v * v, axis=-1, keepdims=True) * (1.0 / width) + NORM_EPS)


def _rope(v, cos, sin_lo, sin_hi, half):
    return (v * cos + pltpu.roll(v, LANES - half, 1) * sin_lo + pltpu.roll(v, half, 1) * sin_hi)


def _bf16_dot(a, b):
    return jnp.dot(a, b, preferred_element_type=jnp.float32)


def _nt_dot(a, b):
    return lax.dot_general(a, b, (((1,), (1,)), ((), ())), preferred_element_type=jnp.float32)


def _proj_kernel(x_ref, pos_ref, tab_ref, gx_ref, gcq_ref, gckv_ref, gq_ref, gk_ref, gdq_ref, gdk_ref,
                 w_ref, wuq_ref, wuk_ref, wuv_ref,
                 qa_ref, ka_ref, va_ref, qb_ref, kb_ref, vb_ref, qi_ref, ki_ref, wi_ref, g_ref):
    x = x_ref[0]
    h = (x * _rms(x, D_MODEL) * gx_ref[...]).astype(jnp.bfloat16)
    pos = pos_ref[0].astype(jnp.float32)
    lane = lax.broadcasted_iota(jnp.int32, (1, LANES), 1)

    def tables(row):
        ang = pos * tab_ref[row:row + 1, :]
        sn = jnp.sin(ang)
        return jnp.cos(ang), sn * tab_ref[row + 1:row + 2, :], sn * tab_ref[row + 2:row + 3, :]

    pa = _bf16_dot(h, w_ref[:, OFF_A:OFF_A + W_A])
    c_q = pa[:, :MLA_Q_RANK]
    c_kv = pa[:, MLA_Q_RANK:MLA_Q_RANK + MLA_KV_RANK]
    kpe = pa[:, MLA_Q_RANK + MLA_KV_RANK:]
    cqn = (c_q * _rms(c_q, MLA_Q_RANK) * gcq_ref[...]).astype(jnp.bfloat16)
    ckvn = (c_kv * _rms(c_kv, MLA_KV_RANK) * gckv_ref[...]).astype(jnp.bfloat16)
    q_up = _bf16_dot(cqn, wuq_ref[...])
    k_up = _bf16_dot(ckvn, wuk_ref[...])
    va_ref[0] = _bf16_dot(ckvn, wuv_ref[...]).astype(jnp.bfloat16)
    cos_a, sl_a, sh_a = tables(ROW_A)
    q_scale = MLA_QK ** -0.5
    for hd in range(MLA_HEADS):
        qh = q_up[:, hd * LANES:(hd + 1) * LANES]
        qn = qh * _rms(qh, MLA_QK) * gq_ref[...]
        qa_ref[0, hd] = (_rope(qn, cos_a, sl_a, sh_a, MLA_ROPE // 2) * q_scale).astype(jnp.bfloat16)
        kh = k_up[:, hd * LANES:(hd + 1) * LANES] + kpe
        kn = kh * _rms(kh, MLA_QK) * gk_ref[...]
        ka_ref[0, hd] = _rope(kn, cos_a, sl_a, sh_a, MLA_ROPE // 2).astype(jnp.bfloat16)

    pb = _bf16_dot(h, w_ref[:, OFF_B:OFF_B + W_B])
    cos_d, sl_d, sh_d = tables(ROW_D)
    low = lane < DSA_DIM

    def dsa_norm_rope(blk, gain):
        sq = blk * blk
        r_lo = lax.rsqrt(jnp.sum(jnp.where(low, sq, 0.0), axis=-1, keepdims=True) * (1.0 / DSA_DIM) + NORM_EPS)
        r_hi = lax.rsqrt(jnp.sum(jnp.where(low, 0.0, sq), axis=-1, keepdims=True) * (1.0 / DSA_DIM) + NORM_EPS)
        return _rope(blk * jnp.where(low, r_lo, r_hi) * gain, cos_d, sl_d, sh_d, DSA_ROT // 2)

    dsa_scale = DSA_DIM ** -0.5
    for j in range(DSA_WIDTH // LANES):
        blk = pb[:, j * LANES:(j + 1) * LANES]
        qb_ref[0, :, j * LANES:(j + 1) * LANES] = (dsa_norm_rope(blk, gdq_ref[...]) * dsa_scale).astype(jnp.bfloat16)
    kb_ref[0] = dsa_norm_rope(pb[:, DSA_WIDTH:DSA_WIDTH + LANES], gdk_ref[...]).astype(jnp.bfloat16)
    vb_ref[0] = pb[:, DSA_WIDTH + LANES:].astype(jnp.bfloat16)

    pi = _bf16_dot(h, w_ref[:, OFF_I:OFF_I + W_I])
    cos_i, sl_i, sh_i = tables(ROW_I)
    for j in range(2 * IDX_HEADS * IDX_DIM // LANES):
        blk = _rope(pi[:, j * LANES:(j + 1) * LANES], cos_i, sl_i, sh_i, IDX_ROT // 2).astype(jnp.bfloat16)
        if j < IDX_HEADS * IDX_DIM // LANES:
            qi_ref[0, :, j * LANES:(j + 1) * LANES] = blk
        else:
            jj = j - IDX_HEADS * IDX_DIM // LANES
            ki_ref[0, :, jj * LANES:(jj + 1) * LANES] = blk
    wi_ref[0] = pi[:, 2 * IDX_HEADS * IDX_DIM:] * (IDX_HEADS ** -0.5 * IDX_DIM ** -0.5)

    chunk = 512
    for c in range(W_G // chunk):
        g_ref[0, :, c * chunk:(c + 1) * chunk] = _bf16_dot(h, w_ref[:, OFF_G + c * chunk:OFF_G + (c + 1) * chunk])


def _mla_kernel(q_ref, k_ref, v_ref, o_ref, *, tq, tk):
    i = pl.program_id(2)
    q_start = i * tq
    n_full = q_start // tk
    n_all = (q_start + tq + tk - 1) // tk
    qs = (q_ref[0, 0], q_ref[0, 1])
    row = q_start + lax.broadcasted_iota(jnp.int32, (tq, tk), 0)
    col0 = lax.broadcasted_iota(jnp.int32, (tq, tk), 1)

    def step(j, carry, masked):
        off = pl.multiple_of(j * tk, tk)
        v = v_ref[0, pl.ds(off, tk), :]
        out = []
        for hd in range(2):
            m, l, acc = carry[hd]
            s = _nt_dot(qs[hd], k_ref[0, hd, pl.ds(off, tk), :])
            if masked:
                s = jnp.where(col0 + off <= row, s, NEG_BIG)
            m_new = jnp.maximum(m, jnp.max(s, axis=-1, keepdims=True))
            alpha = jnp.exp(m - m_new)
            p = jnp.exp(s - m_new)
            l = alpha * l + jnp.sum(p, axis=-1, keepdims=True)
            acc = alpha * acc + _bf16_dot(p.astype(jnp.bfloat16), v)
            out.append((m_new, l, acc))
        return tuple(out)

    init = tuple((jnp.full((tq, 1), NEG_BIG, jnp.float32), jnp.zeros((tq, 1), jnp.float32),
                  jnp.zeros((tq, LANES), jnp.float32)) for _ in range(2))
    carry = lax.fori_loop(0, n_full, functools.partial(step, masked=False), init)
    carry = lax.fori_loop(n_full, n_all, functools.partial(step, masked=True), carry)
    (_, l0, a0), (_, l1, a1) = carry
    lane = lax.broadcasted_iota(jnp.int32, (tq, LANES), 1)
    o_ref[0] = jnp.where(lane < MLA_V, a0 / l0, a1 / l1)


def _dsa_kernel(qb_ref, kb_ref, vb_ref, qi_ref, ki_ref, wi_ref, o_ref,
                keys_ref, qim_ref, qs_ref, *, tq, tk, k_top):
    i = pl.program_id(1)
    q_start = i * tq
    n_kv = (q_start + tq + tk - 1) // tk
    n_heads = IDX_HEADS

    lane_i = lax.broadcasted_iota(jnp.int32, (tq, IDX_HEADS * IDX_DIM), 1)
    qi = qi_ref[0]
    for hd in range(IDX_HEADS):
        in_head = (lane_i >= hd * IDX_DIM) & (lane_i < (hd + 1) * IDX_DIM)
        qim_ref[hd * tq:(hd + 1) * tq, :] = jnp.where(in_head, qi, jnp.zeros_like(qi))
    lane_q = lax.broadcasted_iota(jnp.int32, (tq, LANES), 1)
    for j in range(DSA_GROUP):
        blk = qb_ref[0, :, j * LANES:(j + 1) * LANES]
        zero = jnp.zeros_like(blk)
        qs_ref[j * tq:(j + 1) * tq, :] = jnp.where(lane_q < DSA_DIM, blk, zero)
        qs_ref[(DSA_GROUP + j) * tq:(DSA_GROUP + j + 1) * tq, :] = jnp.where(lane_q < DSA_DIM, zero, blk)

    row = q_start + lax.broadcasted_iota(jnp.int32, (tq, tk), 0)
    col0 = lax.broadcasted_iota(jnp.int32, (tq, tk), 1)
    w_cols = [wi_ref[0, :, hd:hd + 1] for hd in range(IDX_HEADS)]

    def score_tile(j, _):
        off = pl.multiple_of(j * tk, tk)
        s_all = _nt_dot(qim_ref[...], ki_ref[0, pl.ds(off, tk), :])
        sc = w_cols[0] * jnp.maximum(s_all[0:tq], 0.0)
        for hd in range(1, n_heads):
            sc = sc + w_cols[hd] * jnp.maximum(s_all[hd * tq:(hd + 1) * tq], 0.0)
        bits = lax.bitcast_convert_type(sc, jnp.int32)
        key = bits ^ ((bits >> 31) & 0x7FFFFFFF)
        keys_ref[j] = jnp.where(col0 + off <= row, key, INT_MIN)
        return 0

    lax.fori_loop(0, n_kv, score_tile, 0)

    def count(pred):
        def body(j, acc):
            return acc + jnp.where(pred(keys_ref[j], j), 1, 0)
        acc = lax.fori_loop(0, n_kv, body, jnp.zeros((tq, tk), jnp.int32))
        return jnp.sum(acc, axis=-1, keepdims=True)

    def value_bit(it, carry):
        t, cnt_t = carry
        cand = t | jnp.left_shift(jnp.int32(1), 31 - it)
        cand_s = cand ^ INT_MIN
        cnt = count(lambda key, j: key >= cand_s)
        take = cnt >= k_top
        return jnp.where(take, cand, t), jnp.where(take, cnt, cnt_t)

    t0 = jnp.zeros((tq, 1), jnp.int32)
    t, cnt_ge = lax.fori_loop(0, 32, value_bit, (t0, t0))
    thr = jnp.maximum(t ^ INT_MIN, INT_MIN + 1)
    tied = (t != 0) & (cnt_ge > k_top)

    def tie_cut():
        cnt_gt = count(lambda key, j: key > thr)
        need = k_top - cnt_gt

        def index_bit(it, c):
            cand = c | jnp.left_shift(jnp.int32(1), 12 - it)
            below = count(lambda key, j: (key == thr) & (col0 + j * tk < cand))
            return jnp.where(below < need, cand, c)

        c = lax.fori_loop(0, 13, index_bit, jnp.zeros((tq, 1), jnp.int32))
        return jnp.where(tied, c, jnp.int32(2 ** 30))

    cut = lax.cond(jnp.max(tied.astype(jnp.int32)) > 0, tie_cut,
                   lambda: jnp.full((tq, 1), 2 ** 30, jnp.int32))

    def attend(j, carry):
        m, l, acc = carry
        off = pl.multiple_of(j * tk, tk)
        s = _nt_dot(qs_ref[...], kb_ref[0, pl.ds(off, tk), :])
        key = keys_ref[j]
        sel = (key > thr) | ((key == thr) & (col0 + off <= cut))
        bias = jnp.where(sel, 0.0, NEG_BIG)
        s = (s.reshape(n_heads, tq, tk) + bias[None]).reshape(n_heads * tq, tk)
        m_new = jnp.maximum(m, jnp.max(s, axis=-1, keepdims=True))
        alpha = jnp.exp(m - m_new)
        p = jnp.exp(s - m_new)
        l = alpha * l + jnp.sum(p, axis=-1, keepdims=True)
        acc = alpha * acc + _bf16_dot(p.astype(jnp.bfloat16), vb_ref[0, pl.ds(off, tk), :])
        return m_new, l, acc

    init = (jnp.full((n_heads * tq, 1), NEG_BIG, jnp.float32), jnp.zeros((n_heads * tq, 1), jnp.float32),
            jnp.zeros((n_heads * tq, LANES), jnp.float32))
    _, l, acc = lax.fori_loop(0, n_kv, attend, init)
    o = acc / l
    for j in range(DSA_GROUP):
        o_ref[0, :, j * LANES:(j + 1) * LANES] = jnp.where(
            lane_q < DSA_DIM, o[j * tq:(j + 1) * tq], o[(DSA_GROUP + j) * tq:(DSA_GROUP + j + 1) * tq])


def _merge_kernel(x_ref, oa_ref, ob_ref, g_ref, bm_ref, wa_ref, wb_ref, wo_ref, out_ref):
    ga = g_ref[:, 0:MLA_WIDTH]
    gb = g_ref[:, MLA_WIDTH:MLA_WIDTH + DSA_WIDTH]
    ma = g_ref[:, MLA_WIDTH + DSA_WIDTH:MLA_WIDTH + DSA_WIDTH + D_MODEL]
    mb = g_ref[:, MLA_WIDTH + DSA_WIDTH + D_MODEL:]
    ya = _bf16_dot((oa_ref[...] * (ga * jax.nn.sigmoid(ga))).astype(jnp.bfloat16), wa_ref[...])
    yb = _bf16_dot((ob_ref[...] * (gb * jax.nn.sigmoid(gb))).astype(jnp.bfloat16), wb_ref[...])
    merged = jax.nn.sigmoid(ma + bm_ref[0:1, :]) * ya + jax.nn.sigmoid(mb + bm_ref[1:2, :]) * yb
    out_ref[...] = x_ref[...] + _bf16_dot(merged.astype(jnp.bfloat16), wo_ref[...])


def _inv_freq(rot_dim):
    return ROPE_THETA ** (-jnp.arange(0, rot_dim, 2, dtype=jnp.float32) / rot_dim)


def _rope_table():
    tab = jnp.zeros((TAB_ROWS, LANES), jnp.float32)

    def fill(tab, row, period, start, rot_dim):
        half = rot_dim // 2
        f = _inv_freq(rot_dim)
        for base in range(0, LANES, period):
            a = base + start
            tab = tab.at[row, a:a + half].set(f).at[row, a + half:a + rot_dim].set(f)
            tab = tab.at[row + 1, a:a + half].set(-1.0).at[row + 2, a + half:a + rot_dim].set(1.0)
        return tab

    tab = fill(tab, ROW_A, LANES, MLA_NOPE, MLA_ROPE)
    tab = fill(tab, ROW_D, DSA_DIM, 0, DSA_ROT)
    tab = fill(tab, ROW_I, IDX_DIM, 0, IDX_ROT)
    return tab


def _pad_lanes(a, start=0):
    return jnp.pad(a, ((0, 0), (start, LANES - start - a.shape[1])))


def _layout_params(w_in, mla_w_uq, mla_w_ukv, mla_q_gain, mla_k_gain, dsa_q_gain, dsa_k_gain,
                   w_branch_dsa):
    offs = np.concatenate([[0], np.cumsum(IN_SPLITS)])
    (c_q, c_kv, k_pe, gate_a, q_b, k_b, v_b, gate_b, q_i, k_i, w_i, m_a, m_b) = [
        w_in[:, offs[n]:offs[n + 1]] for n in range(len(IN_SPLITS))]
    order = np.concatenate([np.arange(h * DSA_DIM, (h + 1) * DSA_DIM) for h in DSA_HEAD_ORDER])
    w_all = jnp.concatenate([
        c_q, c_kv, _pad_lanes(k_pe, MLA_NOPE),
        q_b[:, order], k_b, v_b,
        q_i, jnp.tile(k_i, (1, IDX_HEADS)), _pad_lanes(w_i),
        gate_a, gate_b[:, order], m_a, m_b], axis=1).astype(jnp.bfloat16)
    wuq = jnp.concatenate([_pad_lanes(mla_w_uq[:, h * MLA_QK:(h + 1) * MLA_QK]) for h in range(MLA_HEADS)], axis=1)
    kv_w = MLA_NOPE + MLA_V
    wuk = jnp.concatenate([_pad_lanes(mla_w_ukv[:, h * kv_w:h * kv_w + MLA_NOPE]) for h in range(MLA_HEADS)], axis=1)
    wuv = jnp.concatenate([mla_w_ukv[:, h * kv_w + MLA_NOPE:(h + 1) * kv_w] for h in range(MLA_HEADS)], axis=1)
    gq = _pad_lanes(mla_q_gain[None, :])
    gk = _pad_lanes(mla_k_gain[None, :])
    gdq = jnp.tile(dsa_q_gain[None, :], (1, LANES // DSA_DIM))
    gdk = jnp.tile(dsa_k_gain[None, :], (1, LANES // DSA_DIM))
    return (w_all, wuq.astype(jnp.bfloat16), wuk.astype(jnp.bfloat16), wuv.astype(jnp.bfloat16),
            gq, gk, gdq, gdk, w_branch_dsa[order, :].astype(jnp.bfloat16))


def _const_spec(shape):
    return pl.BlockSpec(shape, lambda *_: (0,) * len(shape))


def _layer(x, positions, tab, norm_gain, w_in, b_merge, mla_q_norm, mla_w_uq, mla_kv_norm, mla_w_ukv,
           mla_q_gain, mla_k_gain, dsa_q_gain, dsa_k_gain, w_branch_mla, w_branch_dsa, w_out):
    bsz, s_len, _ = x.shape
    k_top = min(TOPK_MAX, s_len // 4)
    (w_all, wuq, wuk, wuv, gq, gk, gdq, gdk, wb_dsa) = _layout_params(
        w_in, mla_w_uq, mla_w_ukv, mla_q_gain, mla_k_gain, dsa_q_gain, dsa_k_gain, w_branch_dsa)
    bf16, f32 = jnp.bfloat16, jnp.float32

    tm = 256
    tok = lambda w: pl.BlockSpec((1, tm, w), lambda b, i: (b, i, 0))
    head = pl.BlockSpec((1, MLA_HEADS, tm, LANES), lambda b, i: (b, 0, i, 0))
    outs = pl.pallas_call(
        _proj_kernel,
        grid=(bsz, s_len // tm),
        in_specs=[tok(D_MODEL), tok(1), _const_spec((TAB_ROWS, LANES)), _const_spec((1, D_MODEL)),
                  _const_spec((1, MLA_Q_RANK)), _const_spec((1, MLA_KV_RANK)),
                  _const_spec((1, LANES)), _const_spec((1, LANES)), _const_spec((1, LANES)), _const_spec((1, LANES)),
                  _const_spec((D_MODEL, W_ALL)), _const_spec((MLA_Q_RANK, MLA_HEADS * LANES)),
                  _const_spec((MLA_KV_RANK, MLA_HEADS * LANES)), _const_spec((MLA_KV_RANK, MLA_WIDTH))],
        out_specs=[head, head, tok(MLA_WIDTH), tok(DSA_WIDTH), tok(LANES), tok(LANES),
                   tok(IDX_HEADS * IDX_DIM), tok(IDX_HEADS * IDX_DIM), tok(LANES), tok(W_G)],
        out_shape=[jax.ShapeDtypeStruct((bsz, MLA_HEADS, s_len, LANES), bf16),
                   jax.ShapeDtypeStruct((bsz, MLA_HEADS, s_len, LANES), bf16),
                   jax.ShapeDtypeStruct((bsz, s_len, MLA_WIDTH), bf16),
                   jax.ShapeDtypeStruct((bsz, s_len, DSA_WIDTH), bf16),
                   jax.ShapeDtypeStruct((bsz, s_len, LANES), bf16),
                   jax.ShapeDtypeStruct((bsz, s_len, LANES), bf16),
                   jax.ShapeDtypeStruct((bsz, s_len, IDX_HEADS * IDX_DIM), bf16),
                   jax.ShapeDtypeStruct((bsz, s_len, IDX_HEADS * IDX_DIM), bf16),
                   jax.ShapeDtypeStruct((bsz, s_len, LANES), f32),
                   jax.ShapeDtypeStruct((bsz, s_len, W_G), f32)],
        compiler_params=pltpu.CompilerParams(dimension_semantics=("arbitrary", "arbitrary"),
                                             vmem_limit_bytes=VMEM_LIMIT),
        name="proj",
    )(x, positions[..., None], tab, norm_gain[None, :], mla_q_norm[None, :], mla_kv_norm[None, :],
      gq, gk, gdq, gdk, w_all, wuq, wuk, wuv)
    qa, ka, va, qb, kb, vb, qi, ki, wi, gates = outs

    tq_a, tk_a = 256, 512
    o_a = pl.pallas_call(
        functools.partial(_mla_kernel, tq=tq_a, tk=tk_a),
        grid=(bsz, MLA_HEADS // 2, s_len // tq_a),
        in_specs=[pl.BlockSpec((1, 2, tq_a, LANES), lambda b, hp, i: (b, hp, i, 0)),
                  pl.BlockSpec((1, 2, s_len, LANES), lambda b, hp, i: (b, hp, 0, 0)),
                  pl.BlockSpec((1, s_len, LANES), lambda b, hp, i: (b, 0, hp))],
        out_specs=pl.BlockSpec((1, tq_a, LANES), lambda b, hp, i: (b, i, hp)),
        out_shape=jax.ShapeDtypeStruct((bsz, s_len, MLA_WIDTH), f32),
        compiler_params=pltpu.CompilerParams(dimension_semantics=("arbitrary",) * 3,
                                             vmem_limit_bytes=VMEM_LIMIT),
        name="mla",
    )(qa, ka, va)

    tq_b, tk_b = 128, 512
    row_b = lambda w: pl.BlockSpec((1, tq_b, w), lambda b, i: (b, i, 0))
    full_b = lambda w: pl.BlockSpec((1, s_len, w), lambda b, i: (b, 0, 0))
    o_b = pl.pallas_call(
        functools.partial(_dsa_kernel, tq=tq_b, tk=tk_b, k_top=k_top),
        grid=(bsz, s_len // tq_b),
        in_specs=[row_b(DSA_WIDTH), full_b(LANES), full_b(LANES), row_b(IDX_HEADS * IDX_DIM),
                  full_b(IDX_HEADS * IDX_DIM), row_b(LANES)],
        out_specs=row_b(DSA_WIDTH),
        out_shape=jax.ShapeDtypeStruct((bsz, s_len, DSA_WIDTH), f32),
        scratch_shapes=[pltpu.VMEM((s_len // tk_b, tq_b, tk_b), jnp.int32),
                        pltpu.VMEM((IDX_HEADS * tq_b, IDX_HEADS * IDX_DIM), bf16),
                        pltpu.VMEM((DSA_HEADS * tq_b, LANES), bf16)],
        compiler_params=pltpu.CompilerParams(dimension_semantics=("arbitrary", "arbitrary"),
                                             vmem_limit_bytes=VMEM_LIMIT),
        name="dsa",
    )(qb, kb, vb, qi, ki, wi)

    tm_m = 512
    n_tok = bsz * s_len
    flat = lambda w: pl.BlockSpec((tm_m, w), lambda i: (i, 0))
    out = pl.pallas_call(
        _merge_kernel,
        grid=(n_tok // tm_m,),
        in_specs=[flat(D_MODEL), flat(MLA_WIDTH), flat(DSA_WIDTH), flat(W_G), _const_spec((2, D_MODEL)),
                  _const_spec((MLA_WIDTH, D_MODEL)), _const_spec((DSA_WIDTH, D_MODEL)),
                  _const_spec((D_MODEL, D_MODEL))],
        out_specs=flat(D_MODEL),
        out_shape=jax.ShapeDtypeStruct((n_tok, D_MODEL), f32),
        compiler_params=pltpu.CompilerParams(dimension_semantics=("arbitrary",),
                                             vmem_limit_bytes=VMEM_LIMIT),
        name="merge",
    )(x.reshape(n_tok, D_MODEL), o_a.reshape(n_tok, MLA_WIDTH), o_b.reshape(n_tok, DSA_WIDTH),
      gates.reshape(n_tok, W_G), b_merge, w_branch_mla.astype(bf16), wb_dsa, w_out.astype(bf16))
    return out.reshape(bsz, s_len, D_MODEL)


def kernel(x, positions, norm_gain, w_in, b_merge, mla_q_norm, mla_w_uq, mla_kv_norm, mla_w_ukv,
           mla_q_gain, mla_k_gain, dsa_q_gain, dsa_k_gain, w_branch_mla, w_branch_dsa, w_out):
    tab = _rope_table()
    for l in range(norm_gain.shape[0]):
        x = _layer(x, positions, tab, norm_gain[l], w_in[l], b_merge[l], mla_q_norm[l], mla_w_uq[l],
                   mla_kv_norm[l], mla_w_ukv[l], mla_q_gain[l], mla_k_gain[l], dsa_q_gain[l], dsa_k_gain[l],
                   w_branch_mla[l], w_branch_dsa[l], w_out[l])
    return x
```

```python
import functools

import numpy as np
import jax
import jax.numpy as jnp
from jax import lax
from jax.experimental import pallas as pl
from jax.experimental.pallas import tpu as pltpu

D_MODEL = 1024
MLA_HEADS = 8
MLA_Q_RANK = 256
MLA_KV_RANK = 128
MLA_NOPE = 64
MLA_ROPE = 32
MLA_QK = MLA_NOPE + MLA_ROPE
MLA_V = 64
MLA_WIDTH = MLA_HEADS * MLA_V
DSA_HEADS = 8
DSA_KV_HEADS = 2
DSA_GROUP = DSA_HEADS // DSA_KV_HEADS
DSA_DIM = 64
DSA_WIDTH = DSA_HEADS * DSA_DIM
DSA_ROT = DSA_DIM // 4
IDX_HEADS = 8
IDX_DIM = 32
IDX_ROT = IDX_DIM // 4
TOPK_MAX = 256
ROPE_THETA = 500000.0
NORM_EPS = 1e-6

IN_SPLITS = (
    MLA_Q_RANK, MLA_KV_RANK, MLA_ROPE, MLA_WIDTH,
    DSA_HEADS * DSA_DIM, DSA_KV_HEADS * DSA_DIM, DSA_KV_HEADS * DSA_DIM, DSA_WIDTH,
    IDX_HEADS * IDX_DIM, IDX_DIM, IDX_HEADS,
    D_MODEL, D_MODEL,
)

LANES = 128
VMEM_LIMIT = 56 * 1024 * 1024
INT_MIN = -(2 ** 31)
NEG_BIG = -1e30

OFF_A = 0
W_A = MLA_Q_RANK + MLA_KV_RANK + LANES
OFF_B = OFF_A + W_A
W_B = DSA_WIDTH + 2 * DSA_KV_HEADS * DSA_DIM
OFF_I = OFF_B + W_B
W_I = 2 * IDX_HEADS * IDX_DIM + LANES
OFF_G = OFF_I + W_I
W_G = MLA_WIDTH + DSA_WIDTH + 2 * D_MODEL
W_ALL = OFF_G + W_G

DSA_HEAD_ORDER = tuple(h for j in range(DSA_GROUP) for h in (j, DSA_GROUP + j))

ROW_A, ROW_D, ROW_I = 0, 3, 6
TAB_ROWS = 16


def _rms(v, width):
    return lax.rsqrt(jnp.sum(v * v, axis=-1, keepdims=True) * (1.0 / width) + NORM_EPS)


def _rope(v, cos, sin_lo, sin_hi, half):
    return (v * cos + pltpu.roll(v, LANES - half, 1) * sin_lo + pltpu.roll(v, half, 1) * sin_hi)


def _bf16_dot(a, b):
    return jnp.dot(a, b, preferred_element_type=jnp.float32)


def _nt_dot(a, b):
    return lax.dot_general(a, b, (((1,), (1,)), ((), ())), preferred_element_type=jnp.float32)


def _proj_kernel(x_ref, pos_ref, tab_ref, gx_ref, gcq_ref, gckv_ref, gq_ref, gk_ref, gdq_ref, gdk_ref,
                 w_ref, wuq_ref, wuk_ref, wuv_ref,
                 qa_ref, ka_ref, va_ref, qb_ref, kb_ref, vb_ref, qi_ref, ki_ref, wi_ref, g_ref):
    x = x_ref[0]
    h = (x * _rms(x, D_MODEL) * gx_ref[...]).astype(jnp.bfloat16)
    pos = pos_ref[0].astype(jnp.float32)
    lane = lax.broadcasted_iota(jnp.int32, (1, LANES), 1)

    def tables(row):
        ang = pos * tab_ref[row:row + 1, :]
        sn = jnp.sin(ang)
        return jnp.cos(ang), sn * tab_ref[row + 1:row + 2, :], sn * tab_ref[row + 2:row + 3, :]

    pa = _bf16_dot(h, w_ref[:, OFF_A:OFF_A + W_A])
    c_q = pa[:, :MLA_Q_RANK]
    c_kv = pa[:, MLA_Q_RANK:MLA_Q_RANK + MLA_KV_RANK]
    kpe = pa[:, MLA_Q_RANK + MLA_KV_RANK:]
    cqn = (c_q * _rms(c_q, MLA_Q_RANK) * gcq_ref[...]).astype(jnp.bfloat16)
    ckvn = (c_kv * _rms(c_kv, MLA_KV_RANK) * gckv_ref[...]).astype(jnp.bfloat16)
    q_up = _bf16_dot(cqn, wuq_ref[...])
    k_up = _bf16_dot(ckvn, wuk_ref[...])
    va_ref[0] = _bf16_dot(ckvn, wuv_ref[...]).astype(jnp.bfloat16)
    cos_a, sl_a, sh_a = tables(ROW_A)
    q_scale = MLA_QK ** -0.5
    for hd in range(MLA_HEADS):
        qh = q_up[:, hd * LANES:(hd + 1) * LANES]
        qn = qh * _rms(qh, MLA_QK) * gq_ref[...]
        qa_ref[0, hd] = (_rope(qn, cos_a, sl_a, sh_a, MLA_ROPE // 2) * q_scale).astype(jnp.bfloat16)
        kh = k_up[:, hd * LANES:(hd + 1) * LANES] + kpe
        kn = kh * _rms(kh, MLA_QK) * gk_ref[...]
        ka_ref[0, hd] = _rope(kn, cos_a, sl_a, sh_a, MLA_ROPE // 2).astype(jnp.bfloat16)

    pb = _bf16_dot(h, w_ref[:, OFF_B:OFF_B + W_B])
    cos_d, sl_d, sh_d = tables(ROW_D)
    low = lane < DSA_DIM

    def dsa_norm_rope(blk, gain):
        sq = blk * blk
        r_lo = lax.rsqrt(jnp.sum(jnp.where(low, sq, 0.0), axis=-1, keepdims=True) * (1.0 / DSA_DIM) + NORM_EPS)
        r_hi = lax.rsqrt(jnp.sum(jnp.where(low, 0.0, sq), axis=-1, keepdims=True) * (1.0 / DSA_DIM) + NORM_EPS)
        return _rope(blk * jnp.where(low, r_lo, r_hi) * gain, cos_d, sl_d, sh_d, DSA_ROT // 2)

    dsa_scale = DSA_DIM ** -0.5
    for j in range(DSA_WIDTH // LANES):
        blk = pb[:, j * LANES:(j + 1) * LANES]
        qb_ref[0, :, j * LANES:(j + 1) * LANES] = (dsa_norm_rope(blk, gdq_ref[...]) * dsa_scale).astype(jnp.bfloat16)
    kb_ref[0] = dsa_norm_rope(pb[:, DSA_WIDTH:DSA_WIDTH + LANES], gdk_ref[...]).astype(jnp.bfloat16)
    vb_ref[0, 0] = pb[:, DSA_WIDTH + LANES:].T.astype(jnp.bfloat16)

    pi = _bf16_dot(h, w_ref[:, OFF_I:OFF_I + W_I])
    cos_i, sl_i, sh_i = tables(ROW_I)
    for j in range(2 * IDX_HEADS * IDX_DIM // LANES):
        blk = _rope(pi[:, j * LANES:(j + 1) * LANES], cos_i, sl_i, sh_i, IDX_ROT // 2).astype(jnp.bfloat16)
        if j < IDX_HEADS * IDX_DIM // LANES:
            qi_ref[0, :, j * LANES:(j + 1) * LANES] = blk
        else:
            jj = j - IDX_HEADS * IDX_DIM // LANES
            ki_ref[0, :, jj * LANES:(jj + 1) * LANES] = blk
    wi_t = (pi[:, 2 * IDX_HEADS * IDX_DIM:] * (IDX_HEADS ** -0.5 * IDX_DIM ** -0.5)).T
    wi_ref[0] = wi_t[0:IDX_HEADS, :]

    chunk = 512
    for c in range(W_G // chunk):
        g_ref[0, :, c * chunk:(c + 1) * chunk] = _bf16_dot(h, w_ref[:, OFF_G + c * chunk:OFF_G + (c + 1) * chunk])


def _mla_kernel(q_ref, k_ref, v_ref, o_ref, *, tq, tk):
    i = pl.program_id(2)
    q_start = i * tq
    n_full = q_start // tk
    n_all = (q_start + tq + tk - 1) // tk
    qs = (q_ref[0, 0], q_ref[0, 1])
    row = q_start + lax.broadcasted_iota(jnp.int32, (tq, tk), 0)
    col0 = lax.broadcasted_iota(jnp.int32, (tq, tk), 1)

    def step(j, carry, masked):
        off = pl.multiple_of(j * tk, tk)
        v = v_ref[0, pl.ds(off, tk), :]
        out = []
        for hd in range(2):
            m, l, acc = carry[hd]
            s = _nt_dot(qs[hd], k_ref[0, hd, pl.ds(off, tk), :])
            if masked:
                s = jnp.where(col0 + off <= row, s, NEG_BIG)
            m_new = jnp.maximum(m, jnp.max(s, axis=-1, keepdims=True))
            alpha = jnp.exp(m - m_new)
            p = jnp.exp(s - m_new)
            l = alpha * l + jnp.sum(p, axis=-1, keepdims=True)
            acc = alpha * acc + _bf16_dot(p.astype(jnp.bfloat16), v)
            out.append((m_new, l, acc))
        return tuple(out)

    init = tuple((jnp.full((tq, 1), NEG_BIG, jnp.float32), jnp.zeros((tq, 1), jnp.float32),
                  jnp.zeros((tq, LANES), jnp.float32)) for _ in range(2))
    carry = lax.fori_loop(0, n_full, functools.partial(step, masked=False), init)
    carry = lax.fori_loop(n_full, n_all, functools.partial(step, masked=True), carry)
    (_, l0, a0), (_, l1, a1) = carry
    lane = lax.broadcasted_iota(jnp.int32, (tq, LANES), 1)
    o_ref[0] = jnp.where(lane < MLA_V, a0 / l0, a1 / l1)


def _dsa_kernel(qb_ref, kb_ref, vt_ref, qi_ref, ki_ref, wt_ref, o_ref,
                keys_ref, qim_ref, qs_ref, *, tq, tk, k_top):
    i = pl.program_id(1)
    q_start = i * tq
    n_kv = (q_start + tq + tk - 1) // tk
    n_heads = IDX_HEADS
    tv = vt_ref.shape[-1]

    lane_i = lax.broadcasted_iota(jnp.int32, (tq, IDX_HEADS * IDX_DIM), 1)
    qi = qi_ref[0]
    for hd in range(IDX_HEADS):
        in_head = (lane_i >= hd * IDX_DIM) & (lane_i < (hd + 1) * IDX_DIM)
        qim_ref[hd * tq:(hd + 1) * tq, :] = jnp.where(in_head, qi, jnp.zeros_like(qi))
    lane_q = lax.broadcasted_iota(jnp.int32, (tq, LANES), 1)
    for j in range(DSA_GROUP):
        blk = qb_ref[0, :, j * LANES:(j + 1) * LANES]
        zero = jnp.zeros_like(blk)
        qs_ref[j * tq:(j + 1) * tq, :] = jnp.where(lane_q < DSA_DIM, blk, zero)
        qs_ref[(DSA_GROUP + j) * tq:(DSA_GROUP + j + 1) * tq, :] = jnp.where(lane_q < DSA_DIM, zero, blk)

    kpos0 = lax.broadcasted_iota(jnp.int32, (tk, tq), 0)
    qpos = q_start + lax.broadcasted_iota(jnp.int32, (tk, tq), 1)
    w_rows = [wt_ref[0, hd:hd + 1, :] for hd in range(IDX_HEADS)]

    def score_tile(j, _):
        off = pl.multiple_of(j * tk, tk)
        s_all = _nt_dot(ki_ref[0, pl.ds(off, tk), :], qim_ref[...])
        sc = w_rows[0] * jnp.maximum(s_all[:, 0:tq], 0.0)
        for hd in range(1, n_heads):
            sc = sc + w_rows[hd] * jnp.maximum(s_all[:, hd * tq:(hd + 1) * tq], 0.0)
        bits = lax.bitcast_convert_type(sc, jnp.int32)
        key = bits ^ ((bits >> 31) & 0x7FFFFFFF)
        keys_ref[j] = jnp.where(kpos0 + off <= qpos, key, INT_MIN)
        return 0

    lax.fori_loop(0, n_kv, score_tile, 0)

    def count(pred):
        def body(j, acc):
            hit = jnp.where(pred(keys_ref[j], j), 1, 0)
            return acc + jnp.sum(hit.reshape(tk // 8, 8, tq), axis=0)
        acc = lax.fori_loop(0, n_kv, body, jnp.zeros((8, tq), jnp.int32))
        return jnp.sum(acc, axis=0, keepdims=True)

    def value_bit(it, carry):
        t, cnt_t = carry
        cand = t | jnp.left_shift(jnp.int32(1), 31 - it)
        cand_s = cand ^ INT_MIN
        cnt = count(lambda key, j: key >= cand_s)
        take = cnt >= k_top
        return jnp.where(take, cand, t), jnp.where(take, cnt, cnt_t)

    t0 = jnp.zeros((1, tq), jnp.int32)
    t, cnt_ge = lax.fori_loop(0, 32, value_bit, (t0, t0))
    thr = jnp.maximum(t ^ INT_MIN, INT_MIN + 1)
    tied = (t != 0) & (cnt_ge > k_top)

    def tie_cut():
        cnt_gt = count(lambda key, j: key > thr)
        need = k_top - cnt_gt

        def index_bit(it, c):
            cand = c | jnp.left_shift(jnp.int32(1), 12 - it)
            below = count(lambda key, j: (key == thr) & (kpos0 + j * tk < cand))
            return jnp.where(below < need, cand, c)

        c = lax.fori_loop(0, 13, index_bit, jnp.zeros((1, tq), jnp.int32))
        return jnp.where(tied, c, jnp.int32(2 ** 30))

    cut = lax.cond(jnp.max(tied.astype(jnp.int32)) > 0, tie_cut,
                   lambda: jnp.full((1, tq), 2 ** 30, jnp.int32))

    def attend(j, carry):
        m, l, acc = carry
        off = pl.multiple_of(j * tk, tk)
        s = _nt_dot(kb_ref[0, pl.ds(off, tk), :], qs_ref[...])
        key = keys_ref[j]
        sel = (key > thr) | ((key == thr) & (kpos0 + off <= cut))
        bias = jnp.where(sel, 0.0, NEG_BIG)
        s = s + jnp.concatenate([bias] * n_heads, axis=1)
        m_new = jnp.maximum(m, jnp.max(s, axis=0, keepdims=True))
        alpha = jnp.exp(m - m_new)
        p = jnp.exp(s - m_new)
        l = alpha * l + jnp.sum(p, axis=0, keepdims=True)
        pb = p.astype(jnp.bfloat16)
        pv = _bf16_dot(vt_ref[0, j * (tk // tv)], pb[0:tv])
        for c in range(1, tk // tv):
            pv = pv + _bf16_dot(vt_ref[0, j * (tk // tv) + c], pb[c * tv:(c + 1) * tv])
        return m_new, l, alpha * acc + pv

    init = (jnp.full((1, n_heads * tq), NEG_BIG, jnp.float32), jnp.zeros((1, n_heads * tq), jnp.float32),
            jnp.zeros((LANES, n_heads * tq), jnp.float32))
    _, l, acc = lax.fori_loop(0, n_kv, attend, init)
    o = acc / l
    for j in range(DSA_GROUP):
        lo = o[:, j * tq:(j + 1) * tq].T
        hi = o[:, (DSA_GROUP + j) * tq:(DSA_GROUP + j + 1) * tq].T
        o_ref[0, :, j * LANES:(j + 1) * LANES] = jnp.where(lane_q < DSA_DIM, lo, hi)


def _merge_kernel(x_ref, oa_ref, ob_ref, g_ref, bm_ref, wa_ref, wb_ref, wo_ref, out_ref):
    ga = g_ref[:, 0:MLA_WIDTH]
    gb = g_ref[:, MLA_WIDTH:MLA_WIDTH + DSA_WIDTH]
    ma = g_ref[:, MLA_WIDTH + DSA_WIDTH:MLA_WIDTH + DSA_WIDTH + D_MODEL]
    mb = g_ref[:, MLA_WIDTH + DSA_WIDTH + D_MODEL:]
    ya = _bf16_dot((oa_ref[...] * (ga * jax.nn.sigmoid(ga))).astype(jnp.bfloat16), wa_ref[...])
    yb = _bf16_dot((ob_ref[...] * (gb * jax.nn.sigmoid(gb))).astype(jnp.bfloat16), wb_ref[...])
    merged = jax.nn.sigmoid(ma + bm_ref[0:1, :]) * ya + jax.nn.sigmoid(mb + bm_ref[1:2, :]) * yb
    out_ref[...] = x_ref[...] + _bf16_dot(merged.astype(jnp.bfloat16), wo_ref[...])


def _inv_freq(rot_dim):
    return ROPE_THETA ** (-jnp.arange(0, rot_dim, 2, dtype=jnp.float32) / rot_dim)


def _rope_table():
    tab = jnp.zeros((TAB_ROWS, LANES), jnp.float32)

    def fill(tab, row, period, start, rot_dim):
        half = rot_dim // 2
        f = _inv_freq(rot_dim)
        for base in range(0, LANES, period):
            a = base + start
            tab = tab.at[row, a:a + half].set(f).at[row, a + half:a + rot_dim].set(f)
            tab = tab.at[row + 1, a:a + half].set(-1.0).at[row + 2, a + half:a + rot_dim].set(1.0)
        return tab

    tab = fill(tab, ROW_A, LANES, MLA_NOPE, MLA_ROPE)
    tab = fill(tab, ROW_D, DSA_DIM, 0, DSA_ROT)
    tab = fill(tab, ROW_I, IDX_DIM, 0, IDX_ROT)
    return tab


def _pad_lanes(a, start=0):
    return jnp.pad(a, ((0, 0), (start, LANES - start - a.shape[1])))


def _layout_params(w_in, mla_w_uq, mla_w_ukv, mla_q_gain, mla_k_gain, dsa_q_gain, dsa_k_gain,
                   w_branch_dsa):
    offs = np.concatenate([[0], np.cumsum(IN_SPLITS)])
    (c_q, c_kv, k_pe, gate_a, q_b, k_b, v_b, gate_b, q_i, k_i, w_i, m_a, m_b) = [
        w_in[:, offs[n]:offs[n + 1]] for n in range(len(IN_SPLITS))]
    order = np.concatenate([np.arange(h * DSA_DIM, (h + 1) * DSA_DIM) for h in DSA_HEAD_ORDER])
    w_all = jnp.concatenate([
        c_q, c_kv, _pad_lanes(k_pe, MLA_NOPE),
        q_b[:, order], k_b, v_b,
        q_i, jnp.tile(k_i, (1, IDX_HEADS)), _pad_lanes(w_i),
        gate_a, gate_b[:, order], m_a, m_b], axis=1).astype(jnp.bfloat16)
    wuq = jnp.concatenate([_pad_lanes(mla_w_uq[:, h * MLA_QK:(h + 1) * MLA_QK]) for h in range(MLA_HEADS)], axis=1)
    kv_w = MLA_NOPE + MLA_V
    wuk = jnp.concatenate([_pad_lanes(mla_w_ukv[:, h * kv_w:h * kv_w + MLA_NOPE]) for h in range(MLA_HEADS)], axis=1)
    wuv = jnp.concatenate([mla_w_ukv[:, h * kv_w + MLA_NOPE:(h + 1) * kv_w] for h in range(MLA_HEADS)], axis=1)
    gq = _pad_lanes(mla_q_gain[None, :])
    gk = _pad_lanes(mla_k_gain[None, :])
    gdq = jnp.tile(dsa_q_gain[None, :], (1, LANES // DSA_DIM))
    gdk = jnp.tile(dsa_k_gain[None, :], (1, LANES // DSA_DIM))
    return (w_all, wuq.astype(jnp.bfloat16), wuk.astype(jnp.bfloat16), wuv.astype(jnp.bfloat16),
            gq, gk, gdq, gdk, w_branch_dsa[order, :].astype(jnp.bfloat16))


def _const_spec(shape):
    return pl.BlockSpec(shape, lambda *_: (0,) * len(shape))


def _layer(x, positions, tab, norm_gain, w_in, b_merge, mla_q_norm, mla_w_uq, mla_kv_norm, mla_w_ukv,
           mla_q_gain, mla_k_gain, dsa_q_gain, dsa_k_gain, w_branch_mla, w_branch_dsa, w_out):
    bsz, s_len, _ = x.shape
    k_top = min(TOPK_MAX, s_len // 4)
    (w_all, wuq, wuk, wuv, gq, gk, gdq, gdk, wb_dsa) = _layout_params(
        w_in, mla_w_uq, mla_w_ukv, mla_q_gain, mla_k_gain, dsa_q_gain, dsa_k_gain, w_branch_dsa)
    bf16, f32 = jnp.bfloat16, jnp.float32

    tm = 256
    tok = lambda w: pl.BlockSpec((1, tm, w), lambda b, i: (b, i, 0))
    head = pl.BlockSpec((1, MLA_HEADS, tm, LANES), lambda b, i: (b, 0, i, 0))
    outs = pl.pallas_call(
        _proj_kernel,
        grid=(bsz, s_len // tm),
        in_specs=[tok(D_MODEL), tok(1), _const_spec((TAB_ROWS, LANES)), _const_spec((1, D_MODEL)),
                  _const_spec((1, MLA_Q_RANK)), _const_spec((1, MLA_KV_RANK)),
                  _const_spec((1, LANES)), _const_spec((1, LANES)), _const_spec((1, LANES)), _const_spec((1, LANES)),
                  _const_spec((D_MODEL, W_ALL)), _const_spec((MLA_Q_RANK, MLA_HEADS * LANES)),
                  _const_spec((MLA_KV_RANK, MLA_HEADS * LANES)), _const_spec((MLA_KV_RANK, MLA_WIDTH))],
        out_specs=[head, head, tok(MLA_WIDTH), tok(DSA_WIDTH), tok(LANES),
                   pl.BlockSpec((1, 1, LANES, tm), lambda b, i: (b, i, 0, 0)),
                   tok(IDX_HEADS * IDX_DIM), tok(IDX_HEADS * IDX_DIM),
                   pl.BlockSpec((1, IDX_HEADS, tm), lambda b, i: (b, 0, i)), tok(W_G)],
        out_shape=[jax.ShapeDtypeStruct((bsz, MLA_HEADS, s_len, LANES), bf16),
                   jax.ShapeDtypeStruct((bsz, MLA_HEADS, s_len, LANES), bf16),
                   jax.ShapeDtypeStruct((bsz, s_len, MLA_WIDTH), bf16),
                   jax.ShapeDtypeStruct((bsz, s_len, DSA_WIDTH), bf16),
                   jax.ShapeDtypeStruct((bsz, s_len, LANES), bf16),
                   jax.ShapeDtypeStruct((bsz, s_len // tm, LANES, tm), bf16),
                   jax.ShapeDtypeStruct((bsz, s_len, IDX_HEADS * IDX_DIM), bf16),
                   jax.ShapeDtypeStruct((bsz, s_len, IDX_HEADS * IDX_DIM), bf16),
                   jax.ShapeDtypeStruct((bsz, IDX_HEADS, s_len), f32),
                   jax.ShapeDtypeStruct((bsz, s_len, W_G), f32)],
        compiler_params=pltpu.CompilerParams(dimension_semantics=("arbitrary", "arbitrary"),
                                             vmem_limit_bytes=VMEM_LIMIT),
        name="proj",
    )(x, positions[..., None], tab, norm_gain[None, :], mla_q_norm[None, :], mla_kv_norm[None, :],
      gq, gk, gdq, gdk, w_all, wuq, wuk, wuv)
    qa, ka, va, qb, kb, vbt, qi, ki, wit, gates = outs

    tq_a, tk_a = 256, 512
    o_a = pl.pallas_call(
        functools.partial(_mla_kernel, tq=tq_a, tk=tk_a),
        grid=(bsz, MLA_HEADS // 2, s_len // tq_a),
        in_specs=[pl.BlockSpec((1, 2, tq_a, LANES), lambda b, hp, i: (b, hp, i, 0)),
                  pl.BlockSpec((1, 2, s_len, LANES), lambda b, hp, i: (b, hp, 0, 0)),
                  pl.BlockSpec((1, s_len, LANES), lambda b, hp, i: (b, 0, hp))],
        out_specs=pl.BlockSpec((1, tq_a, LANES), lambda b, hp, i: (b, i, hp)),
        out_shape=jax.ShapeDtypeStruct((bsz, s_len, MLA_WIDTH), f32),
        compiler_params=pltpu.CompilerParams(dimension_semantics=("arbitrary",) * 3,
                                             vmem_limit_bytes=VMEM_LIMIT),
        name="mla",
    )(qa, ka, va)

    tq_b, tk_b = 128, 512
    row_b = lambda w: pl.BlockSpec((1, tq_b, w), lambda b, i: (b, i, 0))
    full_b = lambda w: pl.BlockSpec((1, s_len, w), lambda b, i: (b, 0, 0))
    o_b = pl.pallas_call(
        functools.partial(_dsa_kernel, tq=tq_b, tk=tk_b, k_top=k_top),
        grid=(bsz, s_len // tq_b),
        in_specs=[row_b(DSA_WIDTH), full_b(LANES),
                  pl.BlockSpec((1, s_len // tm, LANES, tm), lambda b, i: (b, 0, 0, 0)),
                  row_b(IDX_HEADS * IDX_DIM), full_b(IDX_HEADS * IDX_DIM),
                  pl.BlockSpec((1, IDX_HEADS, tq_b), lambda b, i: (b, 0, i))],
        out_specs=row_b(DSA_WIDTH),
        out_shape=jax.ShapeDtypeStruct((bsz, s_len, DSA_WIDTH), f32),
        scratch_shapes=[pltpu.VMEM((s_len // tk_b, tk_b, tq_b), jnp.int32),
                        pltpu.VMEM((IDX_HEADS * tq_b, IDX_HEADS * IDX_DIM), bf16),
                        pltpu.VMEM((DSA_HEADS * tq_b, LANES), bf16)],
        compiler_params=pltpu.CompilerParams(dimension_semantics=("arbitrary", "arbitrary"),
                                             vmem_limit_bytes=VMEM_LIMIT),
        name="dsa",
    )(qb, kb, vbt, qi, ki, wit)

    tm_m = 512
    n_tok = bsz * s_len
    flat = lambda w: pl.BlockSpec((tm_m, w), lambda i: (i, 0))
    out = pl.pallas_call(
        _merge_kernel,
        grid=(n_tok // tm_m,),
        in_specs=[flat(D_MODEL), flat(MLA_WIDTH), flat(DSA_WIDTH), flat(W_G), _const_spec((2, D_MODEL)),
                  _const_spec((MLA_WIDTH, D_MODEL)), _const_spec((DSA_WIDTH, D_MODEL)),
                  _const_spec((D_MODEL, D_MODEL))],
        out_specs=flat(D_MODEL),
        out_shape=jax.ShapeDtypeStruct((n_tok, D_MODEL), f32),
        compiler_params=pltpu.CompilerParams(dimension_semantics=("arbitrary",),
                                             vmem_limit_bytes=VMEM_LIMIT),
        name="merge",
    )(x.reshape(n_tok, D_MODEL), o_a.reshape(n_tok, MLA_WIDTH), o_b.reshape(n_tok, DSA_WIDTH),
      gates.reshape(n_tok, W_G), b_merge, w_branch_mla.astype(bf16), wb_dsa, w_out.astype(bf16))
    return out.reshape(bsz, s_len, D_MODEL)


def kernel(x, positions, norm_gain, w_in, b_merge, mla_q_norm, mla_w_uq, mla_kv_norm, mla_w_ukv,
           mla_q_gain, mla_k_gain, dsa_q_gain, dsa_k_gain, w_branch_mla, w_branch_dsa, w_out):
    tab = _rope_table()
    for l in range(norm_gain.shape[0]):
        x = _layer(x, positions, tab, norm_gain[l], w_in[l], b_merge[l], mla_q_norm[l], mla_w_uq[l],
                   mla_kv_norm[l], mla_w_ukv[l], mla_q_gain[l], mla_k_gain[l], dsa_q_gain[l], dsa_k_gain[l],
                   w_branch_mla[l], w_branch_dsa[l], w_out[l])
    return x
```

```python
import functools

import numpy as np
import jax
import jax.numpy as jnp
from jax import lax
from jax.experimental import pallas as pl
from jax.experimental.pallas import tpu as pltpu

D_MODEL = 1024
MLA_HEADS = 8
MLA_Q_RANK = 256
MLA_KV_RANK = 128
MLA_NOPE = 64
MLA_ROPE = 32
MLA_QK = MLA_NOPE + MLA_ROPE
MLA_V = 64
MLA_WIDTH = MLA_HEADS * MLA_V
DSA_HEADS = 8
DSA_KV_HEADS = 2
DSA_GROUP = DSA_HEADS // DSA_KV_HEADS
DSA_DIM = 64
DSA_WIDTH = DSA_HEADS * DSA_DIM
DSA_ROT = DSA_DIM // 4
IDX_HEADS = 8
IDX_DIM = 32
IDX_ROT = IDX_DIM // 4
TOPK_MAX = 256
ROPE_THETA = 500000.0
NORM_EPS = 1e-6

IN_SPLITS = (
    MLA_Q_RANK, MLA_KV_RANK, MLA_ROPE, MLA_WIDTH,
    DSA_HEADS * DSA_DIM, DSA_KV_HEADS * DSA_DIM, DSA_KV_HEADS * DSA_DIM, DSA_WIDTH,
    IDX_HEADS * IDX_DIM, IDX_DIM, IDX_HEADS,
    D_MODEL, D_MODEL,
)

LANES = 128
VMEM_LIMIT = 56 * 1024 * 1024
INT_MIN = -(2 ** 31)
F32_LOWEST = float(np.finfo(np.float32).min)
NEG_BIG = -1e30
LOG2_E = 1.4426950408889634

OFF_A = 0
W_A = MLA_Q_RANK + MLA_KV_RANK + LANES
OFF_B = OFF_A + W_A
W_B = DSA_WIDTH + 2 * DSA_KV_HEADS * DSA_DIM
OFF_I = OFF_B + W_B
W_I = 2 * IDX_HEADS * IDX_DIM + LANES
OFF_G = OFF_I + W_I
W_G = MLA_WIDTH + DSA_WIDTH + 2 * D_MODEL
W_ALL = OFF_G + W_G

DSA_HEAD_ORDER = tuple(h for j in range(DSA_GROUP) for h in (j, DSA_GROUP + j))

ROW_A, ROW_D, ROW_I = 0, 3, 6
TAB_ROWS = 16


def _rms(v, width):
    return lax.rsqrt(jnp.sum(v * v, axis=-1, keepdims=True) * (1.0 / width) + NORM_EPS)


def _rope(v, cos, sin_lo, sin_hi, half):
    return (v * cos + pltpu.roll(v, LANES - half, 1) * sin_lo + pltpu.roll(v, half, 1) * sin_hi)


def _bf16_dot(a, b):
    return jnp.dot(a, b, preferred_element_type=jnp.float32)


def _nt_dot(a, b):
    return lax.dot_general(a, b, (((1,), (1,)), ((), ())), preferred_element_type=jnp.float32)


def _proj_kernel(x_ref, pos_ref, tab_ref, gx_ref, gcq_ref, gckv_ref, gq_ref, gk_ref, gdq_ref, gdk_ref,
                 w_ref, wuq_ref, wuk_ref, wuv_ref,
                 qa_ref, ka_ref, va_ref, qb_ref, kb_ref, vb_ref, qi_ref, ki_ref, wi_ref, g_ref):
    x = x_ref[0]
    h = (x * _rms(x, D_MODEL) * gx_ref[...]).astype(jnp.bfloat16)
    pos = pos_ref[0].astype(jnp.float32)
    lane = lax.broadcasted_iota(jnp.int32, (1, LANES), 1)

    def tables(row):
        ang = pos * tab_ref[row:row + 1, :]
        sn = jnp.sin(ang)
        return jnp.cos(ang), sn * tab_ref[row + 1:row + 2, :], sn * tab_ref[row + 2:row + 3, :]

    pa = _bf16_dot(h, w_ref[:, OFF_A:OFF_A + W_A])
    c_q = pa[:, :MLA_Q_RANK]
    c_kv = pa[:, MLA_Q_RANK:MLA_Q_RANK + MLA_KV_RANK]
    kpe = pa[:, MLA_Q_RANK + MLA_KV_RANK:]
    cqn = (c_q * _rms(c_q, MLA_Q_RANK) * gcq_ref[...]).astype(jnp.bfloat16)
    ckvn = (c_kv * _rms(c_kv, MLA_KV_RANK) * gckv_ref[...]).astype(jnp.bfloat16)
    q_up = _bf16_dot(cqn, wuq_ref[...])
    k_up = _bf16_dot(ckvn, wuk_ref[...])
    v_up = _bf16_dot(ckvn, wuv_ref[...])
    for c in range(MLA_WIDTH // LANES):
        va_ref[0, 0, c * LANES:(c + 1) * LANES, :] = v_up[:, c * LANES:(c + 1) * LANES].T.astype(jnp.bfloat16)
    cos_a, sl_a, sh_a = tables(ROW_A)
    q_scale = MLA_QK ** -0.5 * LOG2_E
    for hd in range(MLA_HEADS):
        qh = q_up[:, hd * LANES:(hd + 1) * LANES]
        qn = qh * _rms(qh, MLA_QK) * gq_ref[...]
        qa_ref[0, hd] = (_rope(qn, cos_a, sl_a, sh_a, MLA_ROPE // 2) * q_scale).astype(jnp.bfloat16)
        kh = k_up[:, hd * LANES:(hd + 1) * LANES] + kpe
        kn = kh * _rms(kh, MLA_QK) * gk_ref[...]
        ka_ref[0, hd] = _rope(kn, cos_a, sl_a, sh_a, MLA_ROPE // 2).astype(jnp.bfloat16)

    pb = _bf16_dot(h, w_ref[:, OFF_B:OFF_B + W_B])
    cos_d, sl_d, sh_d = tables(ROW_D)
    low = lane < DSA_DIM

    def dsa_norm_rope(blk, gain):
        sq = blk * blk
        r_lo = lax.rsqrt(jnp.sum(jnp.where(low, sq, 0.0), axis=-1, keepdims=True) * (1.0 / DSA_DIM) + NORM_EPS)
        r_hi = lax.rsqrt(jnp.sum(jnp.where(low, 0.0, sq), axis=-1, keepdims=True) * (1.0 / DSA_DIM) + NORM_EPS)
        return _rope(blk * jnp.where(low, r_lo, r_hi) * gain, cos_d, sl_d, sh_d, DSA_ROT // 2)

    dsa_scale = DSA_DIM ** -0.5 * LOG2_E
    for j in range(DSA_WIDTH // LANES):
        blk = pb[:, j * LANES:(j + 1) * LANES]
        qb_ref[0, :, j * LANES:(j + 1) * LANES] = (dsa_norm_rope(blk, gdq_ref[...]) * dsa_scale).astype(jnp.bfloat16)
    kb_ref[0] = dsa_norm_rope(pb[:, DSA_WIDTH:DSA_WIDTH + LANES], gdk_ref[...]).astype(jnp.bfloat16)
    vb_ref[0, 0] = pb[:, DSA_WIDTH + LANES:].T.astype(jnp.bfloat16)

    pi = _bf16_dot(h, w_ref[:, OFF_I:OFF_I + W_I])
    cos_i, sl_i, sh_i = tables(ROW_I)
    for j in range(2 * IDX_HEADS * IDX_DIM // LANES):
        blk = _rope(pi[:, j * LANES:(j + 1) * LANES], cos_i, sl_i, sh_i, IDX_ROT // 2).astype(jnp.bfloat16)
        if j < IDX_HEADS * IDX_DIM // LANES:
            qi_ref[0, :, j * LANES:(j + 1) * LANES] = blk
        else:
            jj = j - IDX_HEADS * IDX_DIM // LANES
            ki_ref[0, :, jj * LANES:(jj + 1) * LANES] = blk
    wi_t = (pi[:, 2 * IDX_HEADS * IDX_DIM:] * (IDX_HEADS ** -0.5 * IDX_DIM ** -0.5)).T
    wi_ref[0] = wi_t[0:IDX_HEADS, :]

    chunk = 512
    for c in range(W_G // chunk):
        g_ref[0, :, c * chunk:(c + 1) * chunk] = _bf16_dot(h, w_ref[:, OFF_G + c * chunk:OFF_G + (c + 1) * chunk])


def _mla_kernel(q_ref, k_ref, vt_ref, o_ref, *, tq, tk, heads):
    i = pl.program_id(2)
    q_start = i * tq
    n_full = q_start // tk
    n_all = (q_start + tq + tk - 1) // tk
    tv = vt_ref.shape[-1]
    kpos0 = lax.broadcasted_iota(jnp.int32, (tk, tq), 0)
    qpos = q_start + lax.broadcasted_iota(jnp.int32, (tk, tq), 1)

    def step(j, carry, masked):
        off = pl.multiple_of(j * tk, tk)
        out = []
        scores = [_nt_dot(k_ref[0, hd, pl.ds(off, tk), :], q_ref[0, hd]) for hd in range(heads)]
        for hd in range(heads):
            m, l, acc = carry[hd]
            s = scores[hd]
            if masked:
                s = jnp.where(kpos0 + off <= qpos, s, NEG_BIG)
            m_new = jnp.maximum(m, jnp.max(s, axis=0, keepdims=True))
            alpha = jnp.exp2(m - m_new)
            p = jnp.exp2(s - m_new)
            l = alpha * l + jnp.sum(p, axis=0, keepdims=True)
            pb = p.astype(jnp.bfloat16)
            acc = alpha * acc
            for c in range(tk // tv):
                vt = vt_ref[0, j * (tk // tv) + c, hd * MLA_V:(hd + 1) * MLA_V, :]
                acc = acc + _bf16_dot(vt, pb[c * tv:(c + 1) * tv])
            out.append((m_new, l, acc))
        return tuple(out)

    init = tuple((jnp.full((1, tq), NEG_BIG, jnp.float32), jnp.zeros((1, tq), jnp.float32),
                  jnp.zeros((MLA_V, tq), jnp.float32)) for _ in range(heads))
    carry = lax.fori_loop(0, n_full, functools.partial(step, masked=False), init)
    carry = lax.fori_loop(n_full, n_all, functools.partial(step, masked=True), carry)
    outs = [acc / l for (_, l, acc) in carry]
    for c in range(heads // 2):
        o_ref[0, :, c * LANES:(c + 1) * LANES] = jnp.concatenate(outs[2 * c:2 * c + 2], axis=0).T


def _dsa_kernel(qb_ref, kb_ref, vt_ref, qi_ref, ki_ref, wt_ref, o_ref,
                sc_ref, qim_ref, qs_ref, *, tq, tk, k_top):
    i = pl.program_id(1)
    q_start = i * tq
    n_kv = (q_start + tq + tk - 1) // tk
    n_heads = IDX_HEADS
    tv = vt_ref.shape[-1]

    lane_i = lax.broadcasted_iota(jnp.int32, (tq, IDX_HEADS * IDX_DIM), 1)
    qi = qi_ref[0]
    for hd in range(IDX_HEADS):
        in_head = (lane_i >= hd * IDX_DIM) & (lane_i < (hd + 1) * IDX_DIM)
        qim_ref[hd * tq:(hd + 1) * tq, :] = jnp.where(in_head, qi, jnp.zeros_like(qi))
    lane_q = lax.broadcasted_iota(jnp.int32, (tq, LANES), 1)
    for j in range(DSA_GROUP):
        blk = qb_ref[0, :, j * LANES:(j + 1) * LANES]
        zero = jnp.zeros_like(blk)
        qs_ref[j * tq:(j + 1) * tq, :] = jnp.where(lane_q < DSA_DIM, blk, zero)
        qs_ref[(DSA_GROUP + j) * tq:(DSA_GROUP + j + 1) * tq, :] = jnp.where(lane_q < DSA_DIM, zero, blk)

    kpos0 = lax.broadcasted_iota(jnp.int32, (tk, tq), 0)
    qpos = q_start + lax.broadcasted_iota(jnp.int32, (tk, tq), 1)
    w_rows = [wt_ref[0, hd:hd + 1, :] for hd in range(IDX_HEADS)]

    def score_tile(j, _):
        off = pl.multiple_of(j * tk, tk)
        s_all = _nt_dot(ki_ref[0, pl.ds(off, tk), :], qim_ref[...])
        sc = w_rows[0] * jnp.maximum(s_all[:, 0:tq], 0.0)
        for hd in range(1, n_heads):
            sc = sc + w_rows[hd] * jnp.maximum(s_all[:, hd * tq:(hd + 1) * tq], 0.0)
        sc_ref[j] = jnp.where(kpos0 + off <= qpos, sc, -jnp.inf)
        return 0

    lax.fori_loop(0, n_kv, score_tile, 0)

    def pattern_value(u):
        key = u ^ INT_MIN
        return lax.bitcast_convert_type(key ^ ((key >> 31) & 0x7FFFFFFF), jnp.float32)

    def count(pred):
        def body(j, acc):
            hit = jnp.where(pred(sc_ref[j], j), 1, 0)
            return acc + jnp.sum(hit.reshape(tk // 8, 8, tq), axis=0)
        acc = lax.fori_loop(0, n_kv, body, jnp.zeros((8, tq), jnp.int32))
        return jnp.sum(acc, axis=0, keepdims=True)

    def value_bit(it, carry):
        t, cnt_t = carry
        cand = t | jnp.left_shift(jnp.int32(1), 31 - it)
        cand_f = pattern_value(cand)
        cnt = count(lambda sc, j: sc >= cand_f)
        take = cnt >= k_top
        return jnp.where(take, cand, t), jnp.where(take, cnt, cnt_t)

    t0 = jnp.zeros((1, tq), jnp.int32)
    t, cnt_ge = lax.fori_loop(0, 32, value_bit, (t0, t0))
    few = (t >= 0) & (t < 0x00800000)
    thr = jnp.where(few, F32_LOWEST, pattern_value(t))
    tied = jnp.logical_not(few) & (cnt_ge > k_top)

    def tie_cut():
        cnt_gt = count(lambda sc, j: sc > thr)
        need = k_top - cnt_gt

        def index_bit(it, c):
            cand = c | jnp.left_shift(jnp.int32(1), 12 - it)
            below = count(lambda sc, j: (sc == thr) & (kpos0 + j * tk < cand))
            return jnp.where(below < need, cand, c)

        c = lax.fori_loop(0, 13, index_bit, jnp.zeros((1, tq), jnp.int32))
        return jnp.where(tied, c, jnp.int32(2 ** 30))

    cut = lax.cond(jnp.max(tied.astype(jnp.int32)) > 0, tie_cut,
                   lambda: jnp.full((1, tq), 2 ** 30, jnp.int32))

    def attend(j, carry):
        m, l, acc = carry
        off = pl.multiple_of(j * tk, tk)
        s = _nt_dot(kb_ref[0, pl.ds(off, tk), :], qs_ref[...])
        sc = sc_ref[j]
        sel = (sc > thr) | ((sc == thr) & (kpos0 + off <= cut))
        bias = jnp.where(sel, 0.0, NEG_BIG)
        s = s + jnp.concatenate([bias] * n_heads, axis=1)
        m_new = jnp.maximum(m, jnp.max(s, axis=0, keepdims=True))
        alpha = jnp.exp2(m - m_new)
        p = jnp.exp2(s - m_new)
        l = alpha * l + jnp.sum(p, axis=0, keepdims=True)
        pb = p.astype(jnp.bfloat16)
        pv = _bf16_dot(vt_ref[0, j * (tk // tv)], pb[0:tv])
        for c in range(1, tk // tv):
            pv = pv + _bf16_dot(vt_ref[0, j * (tk // tv) + c], pb[c * tv:(c + 1) * tv])
        return m_new, l, alpha * acc + pv

    init = (jnp.full((1, n_heads * tq), NEG_BIG, jnp.float32), jnp.zeros((1, n_heads * tq), jnp.float32),
            jnp.zeros((LANES, n_heads * tq), jnp.float32))
    _, l, acc = lax.fori_loop(0, n_kv, attend, init)
    o = acc / l
    for j in range(DSA_GROUP):
        lo = o[:, j * tq:(j + 1) * tq].T
        hi = o[:, (DSA_GROUP + j) * tq:(DSA_GROUP + j + 1) * tq].T
        o_ref[0, :, j * LANES:(j + 1) * LANES] = jnp.where(lane_q < DSA_DIM, lo, hi)


def _merge_kernel(x_ref, oa_ref, ob_ref, g_ref, bm_ref, wa_ref, wb_ref, wo_ref, out_ref):
    ga = g_ref[:, 0:MLA_WIDTH]
    gb = g_ref[:, MLA_WIDTH:MLA_WIDTH + DSA_WIDTH]
    ma = g_ref[:, MLA_WIDTH + DSA_WIDTH:MLA_WIDTH + DSA_WIDTH + D_MODEL]
    mb = g_ref[:, MLA_WIDTH + DSA_WIDTH + D_MODEL:]
    ya = _bf16_dot((oa_ref[...] * (ga * jax.nn.sigmoid(ga))).astype(jnp.bfloat16), wa_ref[...])
    yb = _bf16_dot((ob_ref[...] * (gb * jax.nn.sigmoid(gb))).astype(jnp.bfloat16), wb_ref[...])
    merged = jax.nn.sigmoid(ma + bm_ref[0:1, :]) * ya + jax.nn.sigmoid(mb + bm_ref[1:2, :]) * yb
    out_ref[...] = x_ref[...] + _bf16_dot(merged.astype(jnp.bfloat16), wo_ref[...])


def _inv_freq(rot_dim):
    return ROPE_THETA ** (-jnp.arange(0, rot_dim, 2, dtype=jnp.float32) / rot_dim)


def _rope_table():
    tab = jnp.zeros((TAB_ROWS, LANES), jnp.float32)

    def fill(tab, row, period, start, rot_dim):
        half = rot_dim // 2
        f = _inv_freq(rot_dim)
        for base in range(0, LANES, period):
            a = base + start
            tab = tab.at[row, a:a + half].set(f).at[row, a + half:a + rot_dim].set(f)
            tab = tab.at[row + 1, a:a + half].set(-1.0).at[row + 2, a + half:a + rot_dim].set(1.0)
        return tab

    tab = fill(tab, ROW_A, LANES, MLA_NOPE, MLA_ROPE)
    tab = fill(tab, ROW_D, DSA_DIM, 0, DSA_ROT)
    tab = fill(tab, ROW_I, IDX_DIM, 0, IDX_ROT)
    return tab


def _pad_lanes(a, start=0):
    return jnp.pad(a, ((0, 0), (start, LANES - start - a.shape[1])))


def _layout_params(w_in, mla_w_uq, mla_w_ukv, mla_q_gain, mla_k_gain, dsa_q_gain, dsa_k_gain,
                   w_branch_dsa):
    offs = np.concatenate([[0], np.cumsum(IN_SPLITS)])
    (c_q, c_kv, k_pe, gate_a, q_b, k_b, v_b, gate_b, q_i, k_i, w_i, m_a, m_b) = [
        w_in[:, offs[n]:offs[n + 1]] for n in range(len(IN_SPLITS))]
    order = np.concatenate([np.arange(h * DSA_DIM, (h + 1) * DSA_DIM) for h in DSA_HEAD_ORDER])
    w_all = jnp.concatenate([
        c_q, c_kv, _pad_lanes(k_pe, MLA_NOPE),
        q_b[:, order], k_b, v_b,
        q_i, jnp.tile(k_i, (1, IDX_HEADS)), _pad_lanes(w_i),
        gate_a, gate_b[:, order], m_a, m_b], axis=1).astype(jnp.bfloat16)
    wuq = jnp.concatenate([_pad_lanes(mla_w_uq[:, h * MLA_QK:(h + 1) * MLA_QK]) for h in range(MLA_HEADS)], axis=1)
    kv_w = MLA_NOPE + MLA_V
    wuk = jnp.concatenate([_pad_lanes(mla_w_ukv[:, h * kv_w:h * kv_w + MLA_NOPE]) for h in range(MLA_HEADS)], axis=1)
    wuv = jnp.concatenate([mla_w_ukv[:, h * kv_w + MLA_NOPE:(h + 1) * kv_w] for h in range(MLA_HEADS)], axis=1)
    gq = _pad_lanes(mla_q_gain[None, :])
    gk = _pad_lanes(mla_k_gain[None, :])
    gdq = jnp.tile(dsa_q_gain[None, :], (1, LANES // DSA_DIM))
    gdk = jnp.tile(dsa_k_gain[None, :], (1, LANES // DSA_DIM))
    return (w_all, wuq.astype(jnp.bfloat16), wuk.astype(jnp.bfloat16), wuv.astype(jnp.bfloat16),
            gq, gk, gdq, gdk, w_branch_dsa[order, :].astype(jnp.bfloat16))


def _const_spec(shape):
    return pl.BlockSpec(shape, lambda *_: (0,) * len(shape))


def _layer(x, positions, tab, norm_gain, w_in, b_merge, mla_q_norm, mla_w_uq, mla_kv_norm, mla_w_ukv,
           mla_q_gain, mla_k_gain, dsa_q_gain, dsa_k_gain, w_branch_mla, w_branch_dsa, w_out):
    bsz, s_len, _ = x.shape
    k_top = min(TOPK_MAX, s_len // 4)
    (w_all, wuq, wuk, wuv, gq, gk, gdq, gdk, wb_dsa) = _layout_params(
        w_in, mla_w_uq, mla_w_ukv, mla_q_gain, mla_k_gain, dsa_q_gain, dsa_k_gain, w_branch_dsa)
    bf16, f32 = jnp.bfloat16, jnp.float32

    tm = 256
    tok = lambda w: pl.BlockSpec((1, tm, w), lambda b, i: (b, i, 0))
    head = pl.BlockSpec((1, MLA_HEADS, tm, LANES), lambda b, i: (b, 0, i, 0))
    outs = pl.pallas_call(
        _proj_kernel,
        grid=(bsz, s_len // tm),
        in_specs=[tok(D_MODEL), tok(1), _const_spec((TAB_ROWS, LANES)), _const_spec((1, D_MODEL)),
                  _const_spec((1, MLA_Q_RANK)), _const_spec((1, MLA_KV_RANK)),
                  _const_spec((1, LANES)), _const_spec((1, LANES)), _const_spec((1, LANES)), _const_spec((1, LANES)),
                  _const_spec((D_MODEL, W_ALL)), _const_spec((MLA_Q_RANK, MLA_HEADS * LANES)),
                  _const_spec((MLA_KV_RANK, MLA_HEADS * LANES)), _const_spec((MLA_KV_RANK, MLA_WIDTH))],
        out_specs=[head, head, pl.BlockSpec((1, 1, MLA_WIDTH, tm), lambda b, i: (b, i, 0, 0)),
                   tok(DSA_WIDTH), tok(LANES),
                   pl.BlockSpec((1, 1, LANES, tm), lambda b, i: (b, i, 0, 0)),
                   tok(IDX_HEADS * IDX_DIM), tok(IDX_HEADS * IDX_DIM),
                   pl.BlockSpec((1, IDX_HEADS, tm), lambda b, i: (b, 0, i)), tok(W_G)],
        out_shape=[jax.ShapeDtypeStruct((bsz, MLA_HEADS, s_len, LANES), bf16),
                   jax.ShapeDtypeStruct((bsz, MLA_HEADS, s_len, LANES), bf16),
                   jax.ShapeDtypeStruct((bsz, s_len // tm, MLA_WIDTH, tm), bf16),
                   jax.ShapeDtypeStruct((bsz, s_len, DSA_WIDTH), bf16),
                   jax.ShapeDtypeStruct((bsz, s_len, LANES), bf16),
                   jax.ShapeDtypeStruct((bsz, s_len // tm, LANES, tm), bf16),
                   jax.ShapeDtypeStruct((bsz, s_len, IDX_HEADS * IDX_DIM), bf16),
                   jax.ShapeDtypeStruct((bsz, s_len, IDX_HEADS * IDX_DIM), bf16),
                   jax.ShapeDtypeStruct((bsz, IDX_HEADS, s_len), f32),
                   jax.ShapeDtypeStruct((bsz, s_len, W_G), f32)],
        compiler_params=pltpu.CompilerParams(dimension_semantics=("arbitrary", "arbitrary"),
                                             vmem_limit_bytes=VMEM_LIMIT),
        name="proj",
    )(x, positions[..., None], tab, norm_gain[None, :], mla_q_norm[None, :], mla_kv_norm[None, :],
      gq, gk, gdq, gdk, w_all, wuq, wuk, wuv)
    qa, ka, vat, qb, kb, vbt, qi, ki, wit, gates = outs

    tq_a, tk_a, hps = 256, 512, 8
    o_a = pl.pallas_call(
        functools.partial(_mla_kernel, tq=tq_a, tk=tk_a, heads=hps),
        grid=(bsz, MLA_HEADS // hps, s_len // tq_a),
        in_specs=[pl.BlockSpec((1, hps, tq_a, LANES), lambda b, hp, i: (b, hp, i, 0)),
                  pl.BlockSpec((1, hps, s_len, LANES), lambda b, hp, i: (b, hp, 0, 0)),
                  pl.BlockSpec((1, s_len // tm, hps * MLA_V, tm), lambda b, hp, i: (b, 0, hp, 0))],
        out_specs=pl.BlockSpec((1, tq_a, hps * MLA_V), lambda b, hp, i: (b, i, hp)),
        out_shape=jax.ShapeDtypeStruct((bsz, s_len, MLA_WIDTH), f32),
        compiler_params=pltpu.CompilerParams(dimension_semantics=("arbitrary",) * 3,
                                             vmem_limit_bytes=VMEM_LIMIT),
        name="mla",
    )(qa, ka, vat)

    tq_b, tk_b = 128, 512
    row_b = lambda w: pl.BlockSpec((1, tq_b, w), lambda b, i: (b, i, 0))
    full_b = lambda w: pl.BlockSpec((1, s_len, w), lambda b, i: (b, 0, 0))
    o_b = pl.pallas_call(
        functools.partial(_dsa_kernel, tq=tq_b, tk=tk_b, k_top=k_top),
        grid=(bsz, s_len // tq_b),
        in_specs=[row_b(DSA_WIDTH), full_b(LANES),
                  pl.BlockSpec((1, s_len // tm, LANES, tm), lambda b, i: (b, 0, 0, 0)),
                  row_b(IDX_HEADS * IDX_DIM), full_b(IDX_HEADS * IDX_DIM),
                  pl.BlockSpec((1, IDX_HEADS, tq_b), lambda b, i: (b, 0, i))],
        out_specs=row_b(DSA_WIDTH),
        out_shape=jax.ShapeDtypeStruct((bsz, s_len, DSA_WIDTH), f32),
        scratch_shapes=[pltpu.VMEM((s_len // tk_b, tk_b, tq_b), f32),
                        pltpu.VMEM((IDX_HEADS * tq_b, IDX_HEADS * IDX_DIM), bf16),
                        pltpu.VMEM((DSA_HEADS * tq_b, LANES), bf16)],
        compiler_params=pltpu.CompilerParams(dimension_semantics=("arbitrary", "arbitrary"),
                                             vmem_limit_bytes=VMEM_LIMIT),
        name="dsa",
    )(qb, kb, vbt, qi, ki, wit)

    tm_m = 512
    n_tok = bsz * s_len
    flat = lambda w: pl.BlockSpec((tm_m, w), lambda i: (i, 0))
    out = pl.pallas_call(
        _merge_kernel,
        grid=(n_tok // tm_m,),
        in_specs=[flat(D_MODEL), flat(MLA_WIDTH), flat(DSA_WIDTH), flat(W_G), _const_spec((2, D_MODEL)),
                  _const_spec((MLA_WIDTH, D_MODEL)), _const_spec((DSA_WIDTH, D_MODEL)),
                  _const_spec((D_MODEL, D_MODEL))],
        out_specs=flat(D_MODEL),
        out_shape=jax.ShapeDtypeStruct((n_tok, D_MODEL), f32),
        compiler_params=pltpu.CompilerParams(dimension_semantics=("arbitrary",),
                                             vmem_limit_bytes=VMEM_LIMIT),
        name="merge",
    )(x.reshape(n_tok, D_MODEL), o_a.reshape(n_tok, MLA_WIDTH), o_b.reshape(n_tok, DSA_WIDTH),
      gates.reshape(n_tok, W_G), b_merge, w_branch_mla.astype(bf16), wb_dsa, w_out.astype(bf16))
    return out.reshape(bsz, s_len, D_MODEL)


def kernel(x, positions, norm_gain, w_in, b_merge, mla_q_norm, mla_w_uq, mla_kv_norm, mla_w_ukv,
           mla_q_gain, mla_k_gain, dsa_q_gain, dsa_k_gain, w_branch_mla, w_branch_dsa, w_out):
    tab = _rope_table()
    for l in range(norm_gain.shape[0]):
        x = _layer(x, positions, tab, norm_gain[l], w_in[l], b_merge[l], mla_q_norm[l], mla_w_uq[l],
                   mla_kv_norm[l], mla_w_ukv[l], mla_q_gain[l], mla_k_gain[l], dsa_q_gain[l], dsa_k_gain[l],
                   w_branch_mla[l], w_branch_dsa[l], w_out[l])
    return x
```

```python
import functools

import numpy as np
import jax
import jax.numpy as jnp
from jax import lax
from jax.experimental import pallas as pl
from jax.experimental.pallas import tpu as pltpu

D_MODEL = 1024
MLA_HEADS = 8
MLA_Q_RANK = 256
MLA_KV_RANK = 128
MLA_NOPE = 64
MLA_ROPE = 32
MLA_QK = MLA_NOPE + MLA_ROPE
MLA_V = 64
MLA_WIDTH = MLA_HEADS * MLA_V
DSA_HEADS = 8
DSA_KV_HEADS = 2
DSA_GROUP = DSA_HEADS // DSA_KV_HEADS
DSA_DIM = 64
DSA_WIDTH = DSA_HEADS * DSA_DIM
DSA_ROT = DSA_DIM // 4
IDX_HEADS = 8
IDX_DIM = 32
IDX_ROT = IDX_DIM // 4
TOPK_MAX = 256
ROPE_THETA = 500000.0
NORM_EPS = 1e-6

IN_SPLITS = (
    MLA_Q_RANK, MLA_KV_RANK, MLA_ROPE, MLA_WIDTH,
    DSA_HEADS * DSA_DIM, DSA_KV_HEADS * DSA_DIM, DSA_KV_HEADS * DSA_DIM, DSA_WIDTH,
    IDX_HEADS * IDX_DIM, IDX_DIM, IDX_HEADS,
    D_MODEL, D_MODEL,
)

LANES = 128
VMEM_LIMIT = 56 * 1024 * 1024
INT_MIN = -(2 ** 31)
F32_LOWEST = float(np.finfo(np.float32).min)
NEG_BIG = -1e30
LOG2_E = 1.4426950408889634
FAST_MAX_SHIFT = 60.0
BOUND_MARGIN = 1.02

OFF_A = 0
W_A = MLA_Q_RANK + MLA_KV_RANK + LANES
OFF_B = OFF_A + W_A
W_B = DSA_WIDTH + 2 * DSA_KV_HEADS * DSA_DIM
OFF_I = OFF_B + W_B
W_I = 2 * IDX_HEADS * IDX_DIM + LANES
OFF_G = OFF_I + W_I
W_G = MLA_WIDTH + DSA_WIDTH + 2 * D_MODEL
W_ALL = OFF_G + W_G

DSA_HEAD_ORDER = tuple(h for j in range(DSA_GROUP) for h in (j, DSA_GROUP + j))

ROW_A, ROW_D, ROW_I = 0, 3, 6
TAB_ROWS = 16


def _rms(v, width):
    return lax.rsqrt(jnp.sum(v * v, axis=-1, keepdims=True) * (1.0 / width) + NORM_EPS)


def _rope(v, cos, sin_lo, sin_hi, half):
    return (v * cos + pltpu.roll(v, LANES - half, 1) * sin_lo + pltpu.roll(v, half, 1) * sin_hi)


def _bf16_dot(a, b):
    return jnp.dot(a, b, preferred_element_type=jnp.float32)


def _nt_dot(a, b):
    return lax.dot_general(a, b, (((1,), (1,)), ((), ())), preferred_element_type=jnp.float32)


def _proj_kernel(x_ref, pos_ref, tab_ref, gx_ref, gcq_ref, gckv_ref, gq_ref, gk_ref, gdq_ref, gdk_ref,
                 w_ref, wuq_ref, wuk_ref, wuv_ref,
                 qa_ref, ka_ref, va_ref, qb_ref, kb_ref, vb_ref, qi_ref, ki_ref, wi_ref, g_ref):
    x = x_ref[0]
    h = (x * _rms(x, D_MODEL) * gx_ref[...]).astype(jnp.bfloat16)
    pos = pos_ref[0].astype(jnp.float32)
    lane = lax.broadcasted_iota(jnp.int32, (1, LANES), 1)

    def tables(row):
        ang = pos * tab_ref[row:row + 1, :]
        sn = jnp.sin(ang)
        return jnp.cos(ang), sn * tab_ref[row + 1:row + 2, :], sn * tab_ref[row + 2:row + 3, :]

    pa = _bf16_dot(h, w_ref[:, OFF_A:OFF_A + W_A])
    c_q = pa[:, :MLA_Q_RANK]
    c_kv = pa[:, MLA_Q_RANK:MLA_Q_RANK + MLA_KV_RANK]
    kpe = pa[:, MLA_Q_RANK + MLA_KV_RANK:]
    cqn = (c_q * _rms(c_q, MLA_Q_RANK) * gcq_ref[...]).astype(jnp.bfloat16)
    ckvn = (c_kv * _rms(c_kv, MLA_KV_RANK) * gckv_ref[...]).astype(jnp.bfloat16)
    q_up = _bf16_dot(cqn, wuq_ref[...])
    k_up = _bf16_dot(ckvn, wuk_ref[...])
    v_up = _bf16_dot(ckvn, wuv_ref[...])
    for c in range(MLA_WIDTH // LANES):
        va_ref[0, 0, c * LANES:(c + 1) * LANES, :] = v_up[:, c * LANES:(c + 1) * LANES].T.astype(jnp.bfloat16)
    cos_a, sl_a, sh_a = tables(ROW_A)
    q_scale = MLA_QK ** -0.5 * LOG2_E
    for hd in range(MLA_HEADS):
        qh = q_up[:, hd * LANES:(hd + 1) * LANES]
        qn = qh * _rms(qh, MLA_QK) * gq_ref[...]
        qa_ref[0, hd] = (_rope(qn, cos_a, sl_a, sh_a, MLA_ROPE // 2) * q_scale).astype(jnp.bfloat16)
        kh = k_up[:, hd * LANES:(hd + 1) * LANES] + kpe
        kn = kh * _rms(kh, MLA_QK) * gk_ref[...]
        ka_ref[0, hd] = _rope(kn, cos_a, sl_a, sh_a, MLA_ROPE // 2).astype(jnp.bfloat16)

    pb = _bf16_dot(h, w_ref[:, OFF_B:OFF_B + W_B])
    cos_d, sl_d, sh_d = tables(ROW_D)
    low = lane < DSA_DIM

    def dsa_norm_rope(blk, gain):
        sq = blk * blk
        r_lo = lax.rsqrt(jnp.sum(jnp.where(low, sq, 0.0), axis=-1, keepdims=True) * (1.0 / DSA_DIM) + NORM_EPS)
        r_hi = lax.rsqrt(jnp.sum(jnp.where(low, 0.0, sq), axis=-1, keepdims=True) * (1.0 / DSA_DIM) + NORM_EPS)
        return _rope(blk * jnp.where(low, r_lo, r_hi) * gain, cos_d, sl_d, sh_d, DSA_ROT // 2)

    dsa_scale = DSA_DIM ** -0.5 * LOG2_E
    for j in range(DSA_WIDTH // LANES):
        blk = pb[:, j * LANES:(j + 1) * LANES]
        qn = (dsa_norm_rope(blk, gdq_ref[...]) * dsa_scale).astype(jnp.bfloat16)
        qb_ref[0, j] = jnp.where(low, qn, jnp.zeros_like(qn))
        qb_ref[0, DSA_GROUP + j] = jnp.where(low, jnp.zeros_like(qn), qn)
    kb_ref[0] = dsa_norm_rope(pb[:, DSA_WIDTH:DSA_WIDTH + LANES], gdk_ref[...]).astype(jnp.bfloat16)
    vb_ref[0, 0] = pb[:, DSA_WIDTH + LANES:].T.astype(jnp.bfloat16)

    pi = _bf16_dot(h, w_ref[:, OFF_I:OFF_I + W_I])
    cos_i, sl_i, sh_i = tables(ROW_I)
    for j in range(2 * IDX_HEADS * IDX_DIM // LANES):
        blk = _rope(pi[:, j * LANES:(j + 1) * LANES], cos_i, sl_i, sh_i, IDX_ROT // 2).astype(jnp.bfloat16)
        if j < IDX_HEADS * IDX_DIM // LANES:
            qi_ref[0, :, j * LANES:(j + 1) * LANES] = blk
        else:
            jj = j - IDX_HEADS * IDX_DIM // LANES
            ki_ref[0, :, jj * LANES:(jj + 1) * LANES] = blk
    wi_t = (pi[:, 2 * IDX_HEADS * IDX_DIM:] * (IDX_HEADS ** -0.5 * IDX_DIM ** -0.5)).T
    wi_ref[0] = wi_t[0:IDX_HEADS, :]

    chunk = 512
    for c in range(W_G // chunk):
        g_ref[0, :, c * chunk:(c + 1) * chunk] = _bf16_dot(h, w_ref[:, OFF_G + c * chunk:OFF_G + (c + 1) * chunk])


def _softmax_tile(s, m, l, shift):
    if shift is None:
        m_new = jnp.maximum(m, jnp.max(s, axis=0, keepdims=True))
        alpha = jnp.exp2(m - m_new)
        p = jnp.exp2(s - m_new)
        return m_new, alpha * l + jnp.sum(p, axis=0, keepdims=True), alpha, p.astype(jnp.bfloat16)
    p = jnp.exp2(s - shift)
    return m, l + jnp.sum(p, axis=0, keepdims=True), None, p.astype(jnp.bfloat16)


def _by_shift(shift_ref, body):
    shift = shift_ref[0]

    @pl.when(shift <= FAST_MAX_SHIFT)
    def _():
        body(shift)

    @pl.when(shift > FAST_MAX_SHIFT)
    def _():
        body(None)


def _mla_kernel(shift_ref, q_ref, k_ref, vt_ref, o_ref, *, tq, tk, heads):
    i = pl.program_id(2)
    q_start = i * tq
    n_full = q_start // tk
    n_all = (q_start + tq + tk - 1) // tk
    tv = vt_ref.shape[-1]
    kpos0 = lax.broadcasted_iota(jnp.int32, (tk, tq), 0)
    qpos = q_start + lax.broadcasted_iota(jnp.int32, (tk, tq), 1)

    def attend(shift):
        def step(j, carry, masked):
            off = pl.multiple_of(j * tk, tk)
            scores = [_nt_dot(k_ref[0, hd, pl.ds(off, tk), :], q_ref[0, hd]) for hd in range(heads)]
            out = []
            for hd in range(heads):
                m, l, acc = carry[hd]
                s = scores[hd]
                if masked:
                    s = jnp.where(kpos0 + off <= qpos, s, NEG_BIG)
                m, l, alpha, pb = _softmax_tile(s, m, l, shift)
                if alpha is not None:
                    acc = alpha * acc
                for c in range(tk // tv):
                    vt = vt_ref[0, j * (tk // tv) + c, hd * MLA_V:(hd + 1) * MLA_V, :]
                    acc = acc + _bf16_dot(vt, pb[c * tv:(c + 1) * tv])
                out.append((m, l, acc))
            return tuple(out)

        init = tuple((jnp.full((1, tq), NEG_BIG, jnp.float32), jnp.zeros((1, tq), jnp.float32),
                      jnp.zeros((MLA_V, tq), jnp.float32)) for _ in range(heads))
        carry = lax.fori_loop(0, n_full, functools.partial(step, masked=False), init)
        carry = lax.fori_loop(n_full, n_all, functools.partial(step, masked=True), carry)
        outs = [acc / l for (_, l, acc) in carry]
        for c in range(heads // 2):
            o_ref[0, :, c * LANES:(c + 1) * LANES] = jnp.concatenate(outs[2 * c:2 * c + 2], axis=0).T

    _by_shift(shift_ref, attend)


def _dsa_kernel(shift_ref, qb_ref, kb_ref, vt_ref, qi_ref, ki_ref, wt_ref, o_ref,
                sc_ref, qim_ref, *, tq, tk, k_top):
    i = pl.program_id(1)
    q_start = i * tq
    n_kv = (q_start + tq + tk - 1) // tk
    n_heads = IDX_HEADS
    tv = vt_ref.shape[-1]

    lane_i = lax.broadcasted_iota(jnp.int32, (tq, IDX_HEADS * IDX_DIM), 1)
    qi = qi_ref[0]
    for hd in range(IDX_HEADS):
        in_head = (lane_i >= hd * IDX_DIM) & (lane_i < (hd + 1) * IDX_DIM)
        qim_ref[hd * tq:(hd + 1) * tq, :] = jnp.where(in_head, qi, jnp.zeros_like(qi))
    lane_q = lax.broadcasted_iota(jnp.int32, (tq, LANES), 1)

    kpos0 = lax.broadcasted_iota(jnp.int32, (tk, tq), 0)
    qpos = q_start + lax.broadcasted_iota(jnp.int32, (tk, tq), 1)
    w_rows = [wt_ref[0, hd:hd + 1, :] for hd in range(IDX_HEADS)]

    def score_tile(j, _):
        off = pl.multiple_of(j * tk, tk)
        s_all = _nt_dot(ki_ref[0, pl.ds(off, tk), :], qim_ref[...])
        sc = w_rows[0] * jnp.maximum(s_all[:, 0:tq], 0.0)
        for hd in range(1, n_heads):
            sc = sc + w_rows[hd] * jnp.maximum(s_all[:, hd * tq:(hd + 1) * tq], 0.0)
        sc_ref[j] = jnp.where(kpos0 + off <= qpos, sc, -jnp.inf)
        return 0

    lax.fori_loop(0, n_kv, score_tile, 0)

    def pattern_value(u):
        key = u ^ INT_MIN
        return lax.bitcast_convert_type(key ^ ((key >> 31) & 0x7FFFFFFF), jnp.float32)

    def count(pred):
        def body(j, acc):
            hit = jnp.where(pred(sc_ref[j], j), 1, 0)
            return acc + jnp.sum(hit.reshape(tk // 8, 8, tq), axis=0)
        acc = lax.fori_loop(0, n_kv, body, jnp.zeros((8, tq), jnp.int32))
        return jnp.sum(acc, axis=0, keepdims=True)

    def value_bit(it, carry):
        t, cnt_t = carry
        cand = t | jnp.left_shift(jnp.int32(1), 31 - it)
        cand_f = pattern_value(cand)
        cnt = count(lambda sc, j: sc >= cand_f)
        take = cnt >= k_top
        return jnp.where(take, cand, t), jnp.where(take, cnt, cnt_t)

    t0 = jnp.zeros((1, tq), jnp.int32)
    t, cnt_ge = lax.fori_loop(0, 32, value_bit, (t0, t0))
    few = (t >= 0) & (t < 0x00800000)
    thr = jnp.where(few, F32_LOWEST, pattern_value(t))
    tied = jnp.logical_not(few) & (cnt_ge > k_top)

    def tie_cut():
        cnt_gt = count(lambda sc, j: sc > thr)
        need = k_top - cnt_gt

        def index_bit(it, c):
            cand = c | jnp.left_shift(jnp.int32(1), 12 - it)
            below = count(lambda sc, j: (sc == thr) & (kpos0 + j * tk < cand))
            return jnp.where(below < need, cand, c)

        c = lax.fori_loop(0, 13, index_bit, jnp.zeros((1, tq), jnp.int32))
        return jnp.where(tied, c, jnp.int32(2 ** 30))

    cut = lax.cond(jnp.max(tied.astype(jnp.int32)) > 0, tie_cut,
                   lambda: jnp.full((1, tq), 2 ** 30, jnp.int32))

    n_pairs = n_heads // 2

    def attend_all(shift):
        def attend(j, carry):
            off = pl.multiple_of(j * tk, tk)
            kt = kb_ref[0, pl.ds(off, tk), :]
            scores = [_nt_dot(kt, qb_ref[0, 2 * hp:2 * hp + 2].reshape(2 * tq, LANES)) for hp in range(n_pairs)]
            sc = sc_ref[j]
            sel = (sc > thr) | ((sc == thr) & (kpos0 + off <= cut))
            sel2 = jnp.concatenate([sel, sel], axis=1)
            out = []
            for hp in range(n_pairs):
                m, l, acc = carry[hp]
                m, l, alpha, pb = _softmax_tile(jnp.where(sel2, scores[hp], NEG_BIG), m, l, shift)
                if alpha is not None:
                    acc = alpha * acc
                for c in range(tk // tv):
                    acc = acc + _bf16_dot(vt_ref[0, j * (tk // tv) + c], pb[c * tv:(c + 1) * tv])
                out.append((m, l, acc))
            return tuple(out)

        init = tuple((jnp.full((1, 2 * tq), NEG_BIG, jnp.float32), jnp.zeros((1, 2 * tq), jnp.float32),
                      jnp.zeros((LANES, 2 * tq), jnp.float32)) for _ in range(n_pairs))
        carry = lax.fori_loop(0, n_kv, attend, init)
        o = jnp.concatenate([acc / l for (_, l, acc) in carry], axis=1)
        for j in range(DSA_GROUP):
            lo = o[:, j * tq:(j + 1) * tq].T
            hi = o[:, (DSA_GROUP + j) * tq:(DSA_GROUP + j + 1) * tq].T
            o_ref[0, :, j * LANES:(j + 1) * LANES] = jnp.where(lane_q < DSA_DIM, lo, hi)

    _by_shift(shift_ref, attend_all)


def _merge_kernel(x_ref, oa_ref, ob_ref, g_ref, bm_ref, wa_ref, wb_ref, wo_ref, out_ref):
    ga = g_ref[:, 0:MLA_WIDTH]
    gb = g_ref[:, MLA_WIDTH:MLA_WIDTH + DSA_WIDTH]
    ma = g_ref[:, MLA_WIDTH + DSA_WIDTH:MLA_WIDTH + DSA_WIDTH + D_MODEL]
    mb = g_ref[:, MLA_WIDTH + DSA_WIDTH + D_MODEL:]
    ya = _bf16_dot((oa_ref[...] * (ga * jax.nn.sigmoid(ga))).astype(jnp.bfloat16), wa_ref[...])
    yb = _bf16_dot((ob_ref[...] * (gb * jax.nn.sigmoid(gb))).astype(jnp.bfloat16), wb_ref[...])
    merged = jax.nn.sigmoid(ma + bm_ref[0:1, :]) * ya + jax.nn.sigmoid(mb + bm_ref[1:2, :]) * yb
    out_ref[...] = x_ref[...] + _bf16_dot(merged.astype(jnp.bfloat16), wo_ref[...])


def _inv_freq(rot_dim):
    return ROPE_THETA ** (-jnp.arange(0, rot_dim, 2, dtype=jnp.float32) / rot_dim)


def _rope_table():
    tab = jnp.zeros((TAB_ROWS, LANES), jnp.float32)

    def fill(tab, row, period, start, rot_dim):
        half = rot_dim // 2
        f = _inv_freq(rot_dim)
        for base in range(0, LANES, period):
            a = base + start
            tab = tab.at[row, a:a + half].set(f).at[row, a + half:a + rot_dim].set(f)
            tab = tab.at[row + 1, a:a + half].set(-1.0).at[row + 2, a + half:a + rot_dim].set(1.0)
        return tab

    tab = fill(tab, ROW_A, LANES, MLA_NOPE, MLA_ROPE)
    tab = fill(tab, ROW_D, DSA_DIM, 0, DSA_ROT)
    tab = fill(tab, ROW_I, IDX_DIM, 0, IDX_ROT)
    return tab


def _pad_lanes(a, start=0):
    return jnp.pad(a, ((0, 0), (start, LANES - start - a.shape[1])))


def _layout_params(w_in, mla_w_uq, mla_w_ukv, mla_q_gain, mla_k_gain, dsa_q_gain, dsa_k_gain,
                   w_branch_dsa):
    offs = np.concatenate([[0], np.cumsum(IN_SPLITS)])
    (c_q, c_kv, k_pe, gate_a, q_b, k_b, v_b, gate_b, q_i, k_i, w_i, m_a, m_b) = [
        w_in[:, offs[n]:offs[n + 1]] for n in range(len(IN_SPLITS))]
    order = np.concatenate([np.arange(h * DSA_DIM, (h + 1) * DSA_DIM) for h in DSA_HEAD_ORDER])
    w_all = jnp.concatenate([
        c_q, c_kv, _pad_lanes(k_pe, MLA_NOPE),
        q_b[:, order], k_b, v_b,
        q_i, jnp.tile(k_i, (1, IDX_HEADS)), _pad_lanes(w_i),
        gate_a, gate_b[:, order], m_a, m_b], axis=1).astype(jnp.bfloat16)
    wuq = jnp.concatenate([_pad_lanes(mla_w_uq[:, h * MLA_QK:(h + 1) * MLA_QK]) for h in range(MLA_HEADS)], axis=1)
    kv_w = MLA_NOPE + MLA_V
    wuk = jnp.concatenate([_pad_lanes(mla_w_ukv[:, h * kv_w:h * kv_w + MLA_NOPE]) for h in range(MLA_HEADS)], axis=1)
    wuv = jnp.concatenate([mla_w_ukv[:, h * kv_w + MLA_NOPE:(h + 1) * kv_w] for h in range(MLA_HEADS)], axis=1)
    gq = _pad_lanes(mla_q_gain[None, :])
    gk = _pad_lanes(mla_k_gain[None, :])
    gdq = jnp.tile(dsa_q_gain[None, :], (1, LANES // DSA_DIM))
    gdk = jnp.tile(dsa_k_gain[None, :], (1, LANES // DSA_DIM))
    return (w_all, wuq.astype(jnp.bfloat16), wuk.astype(jnp.bfloat16), wuv.astype(jnp.bfloat16),
            gq, gk, gdq, gdk, w_branch_dsa[order, :].astype(jnp.bfloat16))


def _const_spec(shape):
    return pl.BlockSpec(shape, lambda *_: (0,) * len(shape))


def _layer(x, positions, tab, norm_gain, w_in, b_merge, mla_q_norm, mla_w_uq, mla_kv_norm, mla_w_ukv,
           mla_q_gain, mla_k_gain, dsa_q_gain, dsa_k_gain, w_branch_mla, w_branch_dsa, w_out):
    bsz, s_len, _ = x.shape
    k_top = min(TOPK_MAX, s_len // 4)
    (w_all, wuq, wuk, wuv, gq, gk, gdq, gdk, wb_dsa) = _layout_params(
        w_in, mla_w_uq, mla_w_ukv, mla_q_gain, mla_k_gain, dsa_q_gain, dsa_k_gain, w_branch_dsa)
    bf16, f32 = jnp.bfloat16, jnp.float32
    gmax = lambda g: jnp.max(jnp.abs(g))
    shift_a = (BOUND_MARGIN * MLA_QK ** 0.5 * LOG2_E * gmax(mla_q_gain) * gmax(mla_k_gain)).reshape(1).astype(f32)
    shift_b = (BOUND_MARGIN * DSA_DIM ** 0.5 * LOG2_E * gmax(dsa_q_gain) * gmax(dsa_k_gain)).reshape(1).astype(f32)
    smem = pl.BlockSpec(memory_space=pltpu.SMEM)

    tm = 256
    tok = lambda w: pl.BlockSpec((1, tm, w), lambda b, i: (b, i, 0))
    head = pl.BlockSpec((1, MLA_HEADS, tm, LANES), lambda b, i: (b, 0, i, 0))
    outs = pl.pallas_call(
        _proj_kernel,
        grid=(bsz, s_len // tm),
        in_specs=[tok(D_MODEL), tok(1), _const_spec((TAB_ROWS, LANES)), _const_spec((1, D_MODEL)),
                  _const_spec((1, MLA_Q_RANK)), _const_spec((1, MLA_KV_RANK)),
                  _const_spec((1, LANES)), _const_spec((1, LANES)), _const_spec((1, LANES)), _const_spec((1, LANES)),
                  _const_spec((D_MODEL, W_ALL)), _const_spec((MLA_Q_RANK, MLA_HEADS * LANES)),
                  _const_spec((MLA_KV_RANK, MLA_HEADS * LANES)), _const_spec((MLA_KV_RANK, MLA_WIDTH))],
        out_specs=[head, head, pl.BlockSpec((1, 1, MLA_WIDTH, tm), lambda b, i: (b, i, 0, 0)),
                   head, tok(LANES),
                   pl.BlockSpec((1, 1, LANES, tm), lambda b, i: (b, i, 0, 0)),
                   tok(IDX_HEADS * IDX_DIM), tok(IDX_HEADS * IDX_DIM),
                   pl.BlockSpec((1, IDX_HEADS, tm), lambda b, i: (b, 0, i)), tok(W_G)],
        out_shape=[jax.ShapeDtypeStruct((bsz, MLA_HEADS, s_len, LANES), bf16),
                   jax.ShapeDtypeStruct((bsz, MLA_HEADS, s_len, LANES), bf16),
                   jax.ShapeDtypeStruct((bsz, s_len // tm, MLA_WIDTH, tm), bf16),
                   jax.ShapeDtypeStruct((bsz, DSA_HEADS, s_len, LANES), bf16),
                   jax.ShapeDtypeStruct((bsz, s_len, LANES), bf16),
                   jax.ShapeDtypeStruct((bsz, s_len // tm, LANES, tm), bf16),
                   jax.ShapeDtypeStruct((bsz, s_len, IDX_HEADS * IDX_DIM), bf16),
                   jax.ShapeDtypeStruct((bsz, s_len, IDX_HEADS * IDX_DIM), bf16),
                   jax.ShapeDtypeStruct((bsz, IDX_HEADS, s_len), f32),
                   jax.ShapeDtypeStruct((bsz, s_len, W_G), f32)],
        compiler_params=pltpu.CompilerParams(dimension_semantics=("arbitrary", "arbitrary"),
                                             vmem_limit_bytes=VMEM_LIMIT),
        name="proj",
    )(x, positions[..., None], tab, norm_gain[None, :], mla_q_norm[None, :], mla_kv_norm[None, :],
      gq, gk, gdq, gdk, w_all, wuq, wuk, wuv)
    qa, ka, vat, qb, kb, vbt, qi, ki, wit, gates = outs

    tq_a, tk_a, hps = 256, 512, 8
    o_a = pl.pallas_call(
        functools.partial(_mla_kernel, tq=tq_a, tk=tk_a, heads=hps),
        grid=(bsz, MLA_HEADS // hps, s_len // tq_a),
        in_specs=[smem, pl.BlockSpec((1, hps, tq_a, LANES), lambda b, hp, i: (b, hp, i, 0)),
                  pl.BlockSpec((1, hps, s_len, LANES), lambda b, hp, i: (b, hp, 0, 0)),
                  pl.BlockSpec((1, s_len // tm, hps * MLA_V, tm), lambda b, hp, i: (b, 0, hp, 0))],
        out_specs=pl.BlockSpec((1, tq_a, hps * MLA_V), lambda b, hp, i: (b, i, hp)),
        out_shape=jax.ShapeDtypeStruct((bsz, s_len, MLA_WIDTH), f32),
        compiler_params=pltpu.CompilerParams(dimension_semantics=("arbitrary",) * 3,
                                             vmem_limit_bytes=VMEM_LIMIT),
        name="mla",
    )(shift_a, qa, ka, vat)

    tq_b, tk_b = 128, 512
    row_b = lambda w: pl.BlockSpec((1, tq_b, w), lambda b, i: (b, i, 0))
    full_b = lambda w: pl.BlockSpec((1, s_len, w), lambda b, i: (b, 0, 0))
    o_b = pl.pallas_call(
        functools.partial(_dsa_kernel, tq=tq_b, tk=tk_b, k_top=k_top),
        grid=(bsz, s_len // tq_b),
        in_specs=[smem, pl.BlockSpec((1, DSA_HEADS, tq_b, LANES), lambda b, i: (b, 0, i, 0)), full_b(LANES),
                  pl.BlockSpec((1, s_len // tm, LANES, tm), lambda b, i: (b, 0, 0, 0)),
                  row_b(IDX_HEADS * IDX_DIM), full_b(IDX_HEADS * IDX_DIM),
                  pl.BlockSpec((1, IDX_HEADS, tq_b), lambda b, i: (b, 0, i))],
        out_specs=row_b(DSA_WIDTH),
        out_shape=jax.ShapeDtypeStruct((bsz, s_len, DSA_WIDTH), f32),
        scratch_shapes=[pltpu.VMEM((s_len // tk_b, tk_b, tq_b), f32),
                        pltpu.VMEM((IDX_HEADS * tq_b, IDX_HEADS * IDX_DIM), bf16)],
        compiler_params=pltpu.CompilerParams(dimension_semantics=("arbitrary", "arbitrary"),
                                             vmem_limit_bytes=VMEM_LIMIT),
        name="dsa",
    )(shift_b, qb, kb, vbt, qi, ki, wit)

    tm_m = 512
    n_tok = bsz * s_len
    flat = lambda w: pl.BlockSpec((tm_m, w), lambda i: (i, 0))
    out = pl.pallas_call(
        _merge_kernel,
        grid=(n_tok // tm_m,),
        in_specs=[flat(D_MODEL), flat(MLA_WIDTH), flat(DSA_WIDTH), flat(W_G), _const_spec((2, D_MODEL)),
                  _const_spec((MLA_WIDTH, D_MODEL)), _const_spec((DSA_WIDTH, D_MODEL)),
                  _const_spec((D_MODEL, D_MODEL))],
        out_specs=flat(D_MODEL),
        out_shape=jax.ShapeDtypeStruct((n_tok, D_MODEL), f32),
        compiler_params=pltpu.CompilerParams(dimension_semantics=("arbitrary",),
                                             vmem_limit_bytes=VMEM_LIMIT),
        name="merge",
    )(x.reshape(n_tok, D_MODEL), o_a.reshape(n_tok, MLA_WIDTH), o_b.reshape(n_tok, DSA_WIDTH),
      gates.reshape(n_tok, W_G), b_merge, w_branch_mla.astype(bf16), wb_dsa, w_out.astype(bf16))
    return out.reshape(bsz, s_len, D_MODEL)


def kernel(x, positions, norm_gain, w_in, b_merge, mla_q_norm, mla_w_uq, mla_kv_norm, mla_w_ukv,
           mla_q_gain, mla_k_gain, dsa_q_gain, dsa_k_gain, w_branch_mla, w_branch_dsa, w_out):
    tab = _rope_table()
    for l in range(norm_gain.shape[0]):
        x = _layer(x, positions, tab, norm_gain[l], w_in[l], b_merge[l], mla_q_norm[l], mla_w_uq[l],
                   mla_kv_norm[l], mla_w_ukv[l], mla_q_gain[l], mla_k_gain[l], dsa_q_gain[l], dsa_k_gain[l],
                   w_branch_mla[l], w_branch_dsa[l], w_out[l])
    return x
```

```python
import functools

import numpy as np
import jax
import jax.numpy as jnp
from jax import lax
from jax.experimental import pallas as pl
from jax.experimental.pallas import tpu as pltpu

D_MODEL = 1024
MLA_HEADS = 8
MLA_Q_RANK = 256
MLA_KV_RANK = 128
MLA_NOPE = 64
MLA_ROPE = 32
MLA_QK = MLA_NOPE + MLA_ROPE
MLA_V = 64
MLA_WIDTH = MLA_HEADS * MLA_V
DSA_HEADS = 8
DSA_KV_HEADS = 2
DSA_GROUP = DSA_HEADS // DSA_KV_HEADS
DSA_DIM = 64
DSA_WIDTH = DSA_HEADS * DSA_DIM
DSA_ROT = DSA_DIM // 4
IDX_HEADS = 8
IDX_DIM = 32
IDX_ROT = IDX_DIM // 4
TOPK_MAX = 256
ROPE_THETA = 500000.0
NORM_EPS = 1e-6

IN_SPLITS = (
    MLA_Q_RANK, MLA_KV_RANK, MLA_ROPE, MLA_WIDTH,
    DSA_HEADS * DSA_DIM, DSA_KV_HEADS * DSA_DIM, DSA_KV_HEADS * DSA_DIM, DSA_WIDTH,
    IDX_HEADS * IDX_DIM, IDX_DIM, IDX_HEADS,
    D_MODEL, D_MODEL,
)

LANES = 128
MXU_WIDTH = 256
VMEM_LIMIT = 56 * 1024 * 1024
INT_MIN = -(2 ** 31)
F32_LOWEST = float(np.finfo(np.float32).min)
NEG_BIG = -1e30
LOG2_E = 1.4426950408889634
FAST_MAX_SHIFT = 60.0
BOUND_MARGIN = 1.02

OFF_A = 0
W_A = MLA_Q_RANK + MLA_KV_RANK + LANES
OFF_B = OFF_A + W_A
W_B = DSA_WIDTH + 2 * DSA_KV_HEADS * DSA_DIM
OFF_I = OFF_B + W_B
W_I = 2 * IDX_HEADS * IDX_DIM + LANES
OFF_G = OFF_I + W_I
W_G = MLA_WIDTH + DSA_WIDTH + 2 * D_MODEL
W_ALL = OFF_G + W_G

DSA_HEAD_ORDER = tuple(h for j in range(DSA_GROUP) for h in (j, DSA_GROUP + j))

ROW_A, ROW_D, ROW_I = 0, 3, 6
TAB_ROWS = 16


def _rms(v, width):
    return lax.rsqrt(jnp.sum(v * v, axis=-1, keepdims=True) * (1.0 / width) + NORM_EPS)


def _rope(v, cos, sin_lo, sin_hi, half):
    return (v * cos + pltpu.roll(v, LANES - half, 1) * sin_lo + pltpu.roll(v, half, 1) * sin_hi)


def _bf16_dot(a, b):
    return jnp.dot(a, b, preferred_element_type=jnp.float32)


def _nt_dot(a, b):
    return lax.dot_general(a, b, (((1,), (1,)), ((), ())), preferred_element_type=jnp.float32)


def _proj_kernel(x_ref, pos_ref, tab_ref, gx_ref, gcq_ref, gckv_ref, gq_ref, gk_ref, gdq_ref, gdk_ref,
                 w_ref, wuq_ref, wuk_ref, wuv_ref,
                 qa_ref, ka_ref, va_ref, qb_ref, kb_ref, vb_ref, qi_ref, ki_ref, wi_ref, g_ref):
    x = x_ref[0]
    h = (x * _rms(x, D_MODEL) * gx_ref[...]).astype(jnp.bfloat16)
    pos = pos_ref[0].astype(jnp.float32)
    lane = lax.broadcasted_iota(jnp.int32, (1, LANES), 1)

    def tables(row):
        ang = pos * tab_ref[row:row + 1, :]
        sn = jnp.sin(ang)
        return jnp.cos(ang), sn * tab_ref[row + 1:row + 2, :], sn * tab_ref[row + 2:row + 3, :]

    pa = _bf16_dot(h, w_ref[:, OFF_A:OFF_A + W_A])
    c_q = pa[:, :MLA_Q_RANK]
    c_kv = pa[:, MLA_Q_RANK:MLA_Q_RANK + MLA_KV_RANK]
    kpe = pa[:, MLA_Q_RANK + MLA_KV_RANK:]
    cqn = (c_q * _rms(c_q, MLA_Q_RANK) * gcq_ref[...]).astype(jnp.bfloat16)
    ckvn = (c_kv * _rms(c_kv, MLA_KV_RANK) * gckv_ref[...]).astype(jnp.bfloat16)
    q_up = _bf16_dot(cqn, wuq_ref[...])
    k_up = _bf16_dot(ckvn, wuk_ref[...])
    v_up = _bf16_dot(ckvn, wuv_ref[...])
    for c in range(MLA_WIDTH // LANES):
        va_ref[0, 0, c * LANES:(c + 1) * LANES, :] = v_up[:, c * LANES:(c + 1) * LANES].T.astype(jnp.bfloat16)
    cos_a, sl_a, sh_a = tables(ROW_A)
    q_scale = MLA_QK ** -0.5 * LOG2_E
    for hd in range(MLA_HEADS):
        qh = q_up[:, hd * LANES:(hd + 1) * LANES]
        qn = qh * _rms(qh, MLA_QK) * gq_ref[...]
        qa_ref[0, hd] = (_rope(qn, cos_a, sl_a, sh_a, MLA_ROPE // 2) * q_scale).astype(jnp.bfloat16)
        kh = k_up[:, hd * LANES:(hd + 1) * LANES] + kpe
        kn = kh * _rms(kh, MLA_QK) * gk_ref[...]
        ka_ref[0, hd] = _rope(kn, cos_a, sl_a, sh_a, MLA_ROPE // 2).astype(jnp.bfloat16)

    pb = _bf16_dot(h, w_ref[:, OFF_B:OFF_B + W_B])
    cos_d, sl_d, sh_d = tables(ROW_D)
    low = lane < DSA_DIM

    def dsa_norm_rope(blk, gain):
        sq = blk * blk
        r_lo = lax.rsqrt(jnp.sum(jnp.where(low, sq, 0.0), axis=-1, keepdims=True) * (1.0 / DSA_DIM) + NORM_EPS)
        r_hi = lax.rsqrt(jnp.sum(jnp.where(low, 0.0, sq), axis=-1, keepdims=True) * (1.0 / DSA_DIM) + NORM_EPS)
        return _rope(blk * jnp.where(low, r_lo, r_hi) * gain, cos_d, sl_d, sh_d, DSA_ROT // 2)

    dsa_scale = DSA_DIM ** -0.5 * LOG2_E
    for j in range(DSA_WIDTH // LANES):
        blk = pb[:, j * LANES:(j + 1) * LANES]
        qn = (dsa_norm_rope(blk, gdq_ref[...]) * dsa_scale).astype(jnp.bfloat16)
        qb_ref[0, j] = jnp.where(low, qn, jnp.zeros_like(qn))
        qb_ref[0, DSA_GROUP + j] = jnp.where(low, jnp.zeros_like(qn), qn)
    kb_ref[0] = dsa_norm_rope(pb[:, DSA_WIDTH:DSA_WIDTH + LANES], gdk_ref[...]).astype(jnp.bfloat16)
    vb_ref[0, 0] = pb[:, DSA_WIDTH + LANES:].T.astype(jnp.bfloat16)

    pi = _bf16_dot(h, w_ref[:, OFF_I:OFF_I + W_I])
    cos_i, sl_i, sh_i = tables(ROW_I)
    for j in range(2 * IDX_HEADS * IDX_DIM // LANES):
        blk = _rope(pi[:, j * LANES:(j + 1) * LANES], cos_i, sl_i, sh_i, IDX_ROT // 2).astype(jnp.bfloat16)
        if j < IDX_HEADS * IDX_DIM // LANES:
            qi_ref[0, :, j * LANES:(j + 1) * LANES] = blk
        else:
            jj = j - IDX_HEADS * IDX_DIM // LANES
            ki_ref[0, :, jj * LANES:(jj + 1) * LANES] = blk
    wi_t = (pi[:, 2 * IDX_HEADS * IDX_DIM:] * (IDX_HEADS ** -0.5 * IDX_DIM ** -0.5)).T
    wi_ref[0] = wi_t[0:IDX_HEADS, :]

    chunk = 512
    for c in range(W_G // chunk):
        g_ref[0, :, c * chunk:(c + 1) * chunk] = _bf16_dot(h, w_ref[:, OFF_G + c * chunk:OFF_G + (c + 1) * chunk])


def _softmax_tile(s, m, l, shift):
    if shift is None:
        m_new = jnp.maximum(m, jnp.max(s, axis=0, keepdims=True))
        alpha = jnp.exp2(m - m_new)
        p = jnp.exp2(s - m_new)
        return m_new, alpha * l + jnp.sum(p, axis=0, keepdims=True), alpha, p.astype(jnp.bfloat16)
    p = jnp.exp2(s - shift)
    return m, l + jnp.sum(p, axis=0, keepdims=True), None, p.astype(jnp.bfloat16)


def _by_shift(shift_ref, body):
    shift = shift_ref[0]

    @pl.when(shift <= FAST_MAX_SHIFT)
    def _():
        body(shift)

    @pl.when(shift > FAST_MAX_SHIFT)
    def _():
        body(None)


def _mla_kernel(shift_ref, q_ref, k_ref, vt_ref, o_ref, *, tq, tk, heads):
    i = pl.program_id(2)
    q_start = i * tq
    n_full = q_start // tk
    n_all = (q_start + tq + tk - 1) // tk
    tv = vt_ref.shape[-1]
    kpos0 = lax.broadcasted_iota(jnp.int32, (tk, tq), 0)
    qpos = q_start + lax.broadcasted_iota(jnp.int32, (tk, tq), 1)

    def attend(shift):
        def step(j, carry, masked):
            off = pl.multiple_of(j * tk, tk)
            scores = [_nt_dot(k_ref[0, hd, pl.ds(off, tk), :], q_ref[0, hd]) for hd in range(heads)]
            out = []
            for hd in range(heads):
                m, l, acc = carry[hd]
                s = scores[hd]
                if masked:
                    s = jnp.where(kpos0 + off <= qpos, s, NEG_BIG)
                m, l, alpha, pb = _softmax_tile(s, m, l, shift)
                if alpha is not None:
                    acc = alpha * acc
                for c in range(tk // tv):
                    vt = vt_ref[0, j * (tk // tv) + c, hd * MLA_V:(hd + 1) * MLA_V, :]
                    acc = acc + _bf16_dot(vt, pb[c * tv:(c + 1) * tv])
                out.append((m, l, acc))
            return tuple(out)

        init = tuple((jnp.full((1, tq), NEG_BIG, jnp.float32), jnp.zeros((1, tq), jnp.float32),
                      jnp.zeros((MLA_V, tq), jnp.float32)) for _ in range(heads))
        carry = lax.fori_loop(0, n_full, functools.partial(step, masked=False), init)
        carry = lax.fori_loop(n_full, n_all, functools.partial(step, masked=True), carry)
        outs = [acc / l for (_, l, acc) in carry]
        for c in range(heads // 2):
            o_ref[0, :, c * LANES:(c + 1) * LANES] = jnp.concatenate(outs[2 * c:2 * c + 2], axis=0).T

    _by_shift(shift_ref, attend)


def _dsa_kernel(shift_ref, qb_ref, kb_ref, vt_ref, qi_ref, ki_ref, wt_ref, o_ref,
                sc_ref, qim_ref, *, tq, tk, k_top):
    i = pl.program_id(1)
    q_start = i * tq
    n_kv = (q_start + tq + tk - 1) // tk
    n_heads = IDX_HEADS
    tv = vt_ref.shape[-1]

    lane_i = lax.broadcasted_iota(jnp.int32, (tq, IDX_HEADS * IDX_DIM), 1)
    qi = qi_ref[0]
    for hd in range(IDX_HEADS):
        in_head = (lane_i >= hd * IDX_DIM) & (lane_i < (hd + 1) * IDX_DIM)
        qim_ref[hd * tq:(hd + 1) * tq, :] = jnp.where(in_head, qi, jnp.zeros_like(qi))
    lane_q = lax.broadcasted_iota(jnp.int32, (tq, LANES), 1)

    kpos0 = lax.broadcasted_iota(jnp.int32, (tk, tq), 0)
    qpos = q_start + lax.broadcasted_iota(jnp.int32, (tk, tq), 1)
    w_rows = [wt_ref[0, hd:hd + 1, :] for hd in range(IDX_HEADS)]

    def score_tile(j, _):
        off = pl.multiple_of(j * tk, tk)
        s_all = _nt_dot(ki_ref[0, pl.ds(off, tk), :], qim_ref[...])
        sc = w_rows[0] * jnp.maximum(s_all[:, 0:tq], 0.0)
        for hd in range(1, n_heads):
            sc = sc + w_rows[hd] * jnp.maximum(s_all[:, hd * tq:(hd + 1) * tq], 0.0)
        sc_ref[j] = jnp.where(kpos0 + off <= qpos, sc, -jnp.inf)
        return 0

    lax.fori_loop(0, n_kv, score_tile, 0)

    def pattern_value(u):
        key = u ^ INT_MIN
        return lax.bitcast_convert_type(key ^ ((key >> 31) & 0x7FFFFFFF), jnp.float32)

    def count(pred):
        def body(j, acc):
            hit = jnp.where(pred(sc_ref[j], j), 1, 0)
            return acc + jnp.sum(hit.reshape(tk // 8, 8, tq), axis=0)
        acc = lax.fori_loop(0, n_kv, body, jnp.zeros((8, tq), jnp.int32))
        return jnp.sum(acc, axis=0, keepdims=True)

    def value_bit(it, carry):
        t, cnt_t = carry
        cand = t | jnp.left_shift(jnp.int32(1), 31 - it)
        cand_f = pattern_value(cand)
        cnt = count(lambda sc, j: sc >= cand_f)
        take = cnt >= k_top
        return jnp.where(take, cand, t), jnp.where(take, cnt, cnt_t)

    t0 = jnp.zeros((1, tq), jnp.int32)
    t, cnt_ge = lax.fori_loop(0, 32, value_bit, (t0, t0))
    few = (t >= 0) & (t < 0x00800000)
    thr = jnp.where(few, F32_LOWEST, pattern_value(t))
    tied = jnp.logical_not(few) & (cnt_ge > k_top)

    def tie_cut():
        cnt_gt = count(lambda sc, j: sc > thr)
        need = k_top - cnt_gt

        def index_bit(it, c):
            cand = c | jnp.left_shift(jnp.int32(1), 12 - it)
            below = count(lambda sc, j: (sc == thr) & (kpos0 + j * tk < cand))
            return jnp.where(below < need, cand, c)

        c = lax.fori_loop(0, 13, index_bit, jnp.zeros((1, tq), jnp.int32))
        return jnp.where(tied, c, jnp.int32(2 ** 30))

    cut = lax.cond(jnp.max(tied.astype(jnp.int32)) > 0, tie_cut,
                   lambda: jnp.full((1, tq), 2 ** 30, jnp.int32))

    hpd = max(1, MXU_WIDTH // tq)
    n_dots = n_heads // hpd
    wd = hpd * tq

    def attend_all(shift):
        def attend(j, carry):
            off = pl.multiple_of(j * tk, tk)
            kt = kb_ref[0, pl.ds(off, tk), :]
            scores = [_nt_dot(kt, qb_ref[0, hpd * d:hpd * (d + 1)].reshape(wd, LANES)) for d in range(n_dots)]
            sc = sc_ref[j]
            sel = (sc > thr) | ((sc == thr) & (kpos0 + off <= cut))
            sel_d = jnp.concatenate([sel] * hpd, axis=1) if hpd > 1 else sel
            out = []
            for d in range(n_dots):
                m, l, acc = carry[d]
                m, l, alpha, pb = _softmax_tile(jnp.where(sel_d, scores[d], NEG_BIG), m, l, shift)
                if alpha is not None:
                    acc = alpha * acc
                for c in range(tk // tv):
                    acc = acc + _bf16_dot(vt_ref[0, j * (tk // tv) + c], pb[c * tv:(c + 1) * tv])
                out.append((m, l, acc))
            return tuple(out)

        init = tuple((jnp.full((1, wd), NEG_BIG, jnp.float32), jnp.zeros((1, wd), jnp.float32),
                      jnp.zeros((LANES, wd), jnp.float32)) for _ in range(n_dots))
        carry = lax.fori_loop(0, n_kv, attend, init)
        o = jnp.concatenate([acc / l for (_, l, acc) in carry], axis=1)
        for j in range(DSA_GROUP):
            lo = o[:, j * tq:(j + 1) * tq].T
            hi = o[:, (DSA_GROUP + j) * tq:(DSA_GROUP + j + 1) * tq].T
            o_ref[0, :, j * LANES:(j + 1) * LANES] = jnp.where(lane_q < DSA_DIM, lo, hi)

    _by_shift(shift_ref, attend_all)


def _merge_kernel(x_ref, oa_ref, ob_ref, g_ref, bm_ref, wa_ref, wb_ref, wo_ref, out_ref):
    ga = g_ref[:, 0:MLA_WIDTH]
    gb = g_ref[:, MLA_WIDTH:MLA_WIDTH + DSA_WIDTH]
    ma = g_ref[:, MLA_WIDTH + DSA_WIDTH:MLA_WIDTH + DSA_WIDTH + D_MODEL]
    mb = g_ref[:, MLA_WIDTH + DSA_WIDTH + D_MODEL:]
    ya = _bf16_dot((oa_ref[...] * (ga * jax.nn.sigmoid(ga))).astype(jnp.bfloat16), wa_ref[...])
    yb = _bf16_dot((ob_ref[...] * (gb * jax.nn.sigmoid(gb))).astype(jnp.bfloat16), wb_ref[...])
    merged = jax.nn.sigmoid(ma + bm_ref[0:1, :]) * ya + jax.nn.sigmoid(mb + bm_ref[1:2, :]) * yb
    out_ref[...] = x_ref[...] + _bf16_dot(merged.astype(jnp.bfloat16), wo_ref[...])


def _inv_freq(rot_dim):
    return ROPE_THETA ** (-jnp.arange(0, rot_dim, 2, dtype=jnp.float32) / rot_dim)


def _rope_table():
    tab = jnp.zeros((TAB_ROWS, LANES), jnp.float32)

    def fill(tab, row, period, start, rot_dim):
        half = rot_dim // 2
        f = _inv_freq(rot_dim)
        for base in range(0, LANES, period):
            a = base + start
            tab = tab.at[row, a:a + half].set(f).at[row, a + half:a + rot_dim].set(f)
            tab = tab.at[row + 1, a:a + half].set(-1.0).at[row + 2, a + half:a + rot_dim].set(1.0)
        return tab

    tab = fill(tab, ROW_A, LANES, MLA_NOPE, MLA_ROPE)
    tab = fill(tab, ROW_D, DSA_DIM, 0, DSA_ROT)
    tab = fill(tab, ROW_I, IDX_DIM, 0, IDX_ROT)
    return tab


def _pad_lanes(a, start=0):
    return jnp.pad(a, ((0, 0), (start, LANES - start - a.shape[1])))


def _layout_params(w_in, mla_w_uq, mla_w_ukv, mla_q_gain, mla_k_gain, dsa_q_gain, dsa_k_gain,
                   w_branch_dsa):
    offs = np.concatenate([[0], np.cumsum(IN_SPLITS)])
    (c_q, c_kv, k_pe, gate_a, q_b, k_b, v_b, gate_b, q_i, k_i, w_i, m_a, m_b) = [
        w_in[:, offs[n]:offs[n + 1]] for n in range(len(IN_SPLITS))]
    order = np.concatenate([np.arange(h * DSA_DIM, (h + 1) * DSA_DIM) for h in DSA_HEAD_ORDER])
    w_all = jnp.concatenate([
        c_q, c_kv, _pad_lanes(k_pe, MLA_NOPE),
        q_b[:, order], k_b, v_b,
        q_i, jnp.tile(k_i, (1, IDX_HEADS)), _pad_lanes(w_i),
        gate_a, gate_b[:, order], m_a, m_b], axis=1).astype(jnp.bfloat16)
    wuq = jnp.concatenate([_pad_lanes(mla_w_uq[:, h * MLA_QK:(h + 1) * MLA_QK]) for h in range(MLA_HEADS)], axis=1)
    kv_w = MLA_NOPE + MLA_V
    wuk = jnp.concatenate([_pad_lanes(mla_w_ukv[:, h * kv_w:h * kv_w + MLA_NOPE]) for h in range(MLA_HEADS)], axis=1)
    wuv = jnp.concatenate([mla_w_ukv[:, h * kv_w + MLA_NOPE:(h + 1) * kv_w] for h in range(MLA_HEADS)], axis=1)
    gq = _pad_lanes(mla_q_gain[None, :])
    gk = _pad_lanes(mla_k_gain[None, :])
    gdq = jnp.tile(dsa_q_gain[None, :], (1, LANES // DSA_DIM))
    gdk = jnp.tile(dsa_k_gain[None, :], (1, LANES // DSA_DIM))
    return (w_all, wuq.astype(jnp.bfloat16), wuk.astype(jnp.bfloat16), wuv.astype(jnp.bfloat16),
            gq, gk, gdq, gdk, w_branch_dsa[order, :].astype(jnp.bfloat16))


def _const_spec(shape):
    return pl.BlockSpec(shape, lambda *_: (0,) * len(shape))


def _layer(x, positions, tab, norm_gain, w_in, b_merge, mla_q_norm, mla_w_uq, mla_kv_norm, mla_w_ukv,
           mla_q_gain, mla_k_gain, dsa_q_gain, dsa_k_gain, w_branch_mla, w_branch_dsa, w_out):
    bsz, s_len, _ = x.shape
    k_top = min(TOPK_MAX, s_len // 4)
    (w_all, wuq, wuk, wuv, gq, gk, gdq, gdk, wb_dsa) = _layout_params(
        w_in, mla_w_uq, mla_w_ukv, mla_q_gain, mla_k_gain, dsa_q_gain, dsa_k_gain, w_branch_dsa)
    bf16, f32 = jnp.bfloat16, jnp.float32
    gmax = lambda g: jnp.max(jnp.abs(g))
    shift_a = (BOUND_MARGIN * MLA_QK ** 0.5 * LOG2_E * gmax(mla_q_gain) * gmax(mla_k_gain)).reshape(1).astype(f32)
    shift_b = (BOUND_MARGIN * DSA_DIM ** 0.5 * LOG2_E * gmax(dsa_q_gain) * gmax(dsa_k_gain)).reshape(1).astype(f32)
    smem = pl.BlockSpec(memory_space=pltpu.SMEM)

    tm = 256
    tok = lambda w: pl.BlockSpec((1, tm, w), lambda b, i: (b, i, 0))
    head = pl.BlockSpec((1, MLA_HEADS, tm, LANES), lambda b, i: (b, 0, i, 0))
    outs = pl.pallas_call(
        _proj_kernel,
        grid=(bsz, s_len // tm),
        in_specs=[tok(D_MODEL), tok(1), _const_spec((TAB_ROWS, LANES)), _const_spec((1, D_MODEL)),
                  _const_spec((1, MLA_Q_RANK)), _const_spec((1, MLA_KV_RANK)),
                  _const_spec((1, LANES)), _const_spec((1, LANES)), _const_spec((1, LANES)), _const_spec((1, LANES)),
                  _const_spec((D_MODEL, W_ALL)), _const_spec((MLA_Q_RANK, MLA_HEADS * LANES)),
                  _const_spec((MLA_KV_RANK, MLA_HEADS * LANES)), _const_spec((MLA_KV_RANK, MLA_WIDTH))],
        out_specs=[head, head, pl.BlockSpec((1, 1, MLA_WIDTH, tm), lambda b, i: (b, i, 0, 0)),
                   head, tok(LANES),
                   pl.BlockSpec((1, 1, LANES, tm), lambda b, i: (b, i, 0, 0)),
                   tok(IDX_HEADS * IDX_DIM), tok(IDX_HEADS * IDX_DIM),
                   pl.BlockSpec((1, IDX_HEADS, tm), lambda b, i: (b, 0, i)), tok(W_G)],
        out_shape=[jax.ShapeDtypeStruct((bsz, MLA_HEADS, s_len, LANES), bf16),
                   jax.ShapeDtypeStruct((bsz, MLA_HEADS, s_len, LANES), bf16),
                   jax.ShapeDtypeStruct((bsz, s_len // tm, MLA_WIDTH, tm), bf16),
                   jax.ShapeDtypeStruct((bsz, DSA_HEADS, s_len, LANES), bf16),
                   jax.ShapeDtypeStruct((bsz, s_len, LANES), bf16),
                   jax.ShapeDtypeStruct((bsz, s_len // tm, LANES, tm), bf16),
                   jax.ShapeDtypeStruct((bsz, s_len, IDX_HEADS * IDX_DIM), bf16),
                   jax.ShapeDtypeStruct((bsz, s_len, IDX_HEADS * IDX_DIM), bf16),
                   jax.ShapeDtypeStruct((bsz, IDX_HEADS, s_len), f32),
                   jax.ShapeDtypeStruct((bsz, s_len, W_G), f32)],
        compiler_params=pltpu.CompilerParams(dimension_semantics=("arbitrary", "arbitrary"),
                                             vmem_limit_bytes=VMEM_LIMIT),
        name="proj",
    )(x, positions[..., None], tab, norm_gain[None, :], mla_q_norm[None, :], mla_kv_norm[None, :],
      gq, gk, gdq, gdk, w_all, wuq, wuk, wuv)
    qa, ka, vat, qb, kb, vbt, qi, ki, wit, gates = outs

    tq_a, tk_a, hps = 512, 512, 8
    o_a = pl.pallas_call(
        functools.partial(_mla_kernel, tq=tq_a, tk=tk_a, heads=hps),
        grid=(bsz, MLA_HEADS // hps, s_len // tq_a),
        in_specs=[smem, pl.BlockSpec((1, hps, tq_a, LANES), lambda b, hp, i: (b, hp, i, 0)),
                  pl.BlockSpec((1, hps, s_len, LANES), lambda b, hp, i: (b, hp, 0, 0)),
                  pl.BlockSpec((1, s_len // tm, hps * MLA_V, tm), lambda b, hp, i: (b, 0, hp, 0))],
        out_specs=pl.BlockSpec((1, tq_a, hps * MLA_V), lambda b, hp, i: (b, i, hp)),
        out_shape=jax.ShapeDtypeStruct((bsz, s_len, MLA_WIDTH), f32),
        compiler_params=pltpu.CompilerParams(dimension_semantics=("arbitrary",) * 3,
                                             vmem_limit_bytes=VMEM_LIMIT),
        name="mla",
    )(shift_a, qa, ka, vat)

    tq_b, tk_b = 256, 512
    row_b = lambda w: pl.BlockSpec((1, tq_b, w), lambda b, i: (b, i, 0))
    full_b = lambda w: pl.BlockSpec((1, s_len, w), lambda b, i: (b, 0, 0))
    o_b = pl.pallas_call(
        functools.partial(_dsa_kernel, tq=tq_b, tk=tk_b, k_top=k_top),
        grid=(bsz, s_len // tq_b),
        in_specs=[smem, pl.BlockSpec((1, DSA_HEADS, tq_b, LANES), lambda b, i: (b, 0, i, 0)), full_b(LANES),
                  pl.BlockSpec((1, s_len // tm, LANES, tm), lambda b, i: (b, 0, 0, 0)),
                  row_b(IDX_HEADS * IDX_DIM), full_b(IDX_HEADS * IDX_DIM),
                  pl.BlockSpec((1, IDX_HEADS, tq_b), lambda b, i: (b, 0, i))],
        out_specs=row_b(DSA_WIDTH),
        out_shape=jax.ShapeDtypeStruct((bsz, s_len, DSA_WIDTH), f32),
        scratch_shapes=[pltpu.VMEM((s_len // tk_b, tk_b, tq_b), f32),
                        pltpu.VMEM((IDX_HEADS * tq_b, IDX_HEADS * IDX_DIM), bf16)],
        compiler_params=pltpu.CompilerParams(dimension_semantics=("arbitrary", "arbitrary"),
                                             vmem_limit_bytes=VMEM_LIMIT),
        name="dsa",
    )(shift_b, qb, kb, vbt, qi, ki, wit)

    tm_m = 512
    n_tok = bsz * s_len
    flat = lambda w: pl.BlockSpec((tm_m, w), lambda i: (i, 0))
    out = pl.pallas_call(
        _merge_kernel,
        grid=(n_tok // tm_m,),
        in_specs=[flat(D_MODEL), flat(MLA_WIDTH), flat(DSA_WIDTH), flat(W_G), _const_spec((2, D_MODEL)),
                  _const_spec((MLA_WIDTH, D_MODEL)), _const_spec((DSA_WIDTH, D_MODEL)),
                  _const_spec((D_MODEL, D_MODEL))],
        out_specs=flat(D_MODEL),
        out_shape=jax.ShapeDtypeStruct((n_tok, D_MODEL), f32),
        compiler_params=pltpu.CompilerParams(dimension_semantics=("arbitrary",),
                                             vmem_limit_bytes=VMEM_LIMIT),
        name="merge",
    )(x.reshape(n_tok, D_MODEL), o_a.reshape(n_tok, MLA_WIDTH), o_b.reshape(n_tok, DSA_WIDTH),
      gates.reshape(n_tok, W_G), b_merge, w_branch_mla.astype(bf16), wb_dsa, w_out.astype(bf16))
    return out.reshape(bsz, s_len, D_MODEL)


def kernel(x, positions, norm_gain, w_in, b_merge, mla_q_norm, mla_w_uq, mla_kv_norm, mla_w_ukv,
           mla_q_gain, mla_k_gain, dsa_q_gain, dsa_k_gain, w_branch_mla, w_branch_dsa, w_out):
    tab = _rope_table()
    for l in range(norm_gain.shape[0]):
        x = _layer(x, positions, tab, norm_gain[l], w_in[l], b_merge[l], mla_q_norm[l], mla_w_uq[l],
                   mla_kv_norm[l], mla_w_ukv[l], mla_q_gain[l], mla_k_gain[l], dsa_q_gain[l], dsa_k_gain[l],
                   w_branch_mla[l], w_branch_dsa[l], w_out[l])
    return x
```

```python
import functools

import numpy as np
import jax
import jax.numpy as jnp
from jax import lax
from jax.experimental import pallas as pl
from jax.experimental.pallas import tpu as pltpu

D_MODEL = 1024
MLA_HEADS = 8
MLA_Q_RANK = 256
MLA_KV_RANK = 128
MLA_NOPE = 64
MLA_ROPE = 32
MLA_QK = MLA_NOPE + MLA_ROPE
MLA_V = 64
MLA_WIDTH = MLA_HEADS * MLA_V
DSA_HEADS = 8
DSA_KV_HEADS = 2
DSA_GROUP = DSA_HEADS // DSA_KV_HEADS
DSA_DIM = 64
DSA_WIDTH = DSA_HEADS * DSA_DIM
DSA_ROT = DSA_DIM // 4
IDX_HEADS = 8
IDX_DIM = 32
IDX_ROT = IDX_DIM // 4
TOPK_MAX = 256
ROPE_THETA = 500000.0
NORM_EPS = 1e-6

IN_SPLITS = (
    MLA_Q_RANK, MLA_KV_RANK, MLA_ROPE, MLA_WIDTH,
    DSA_HEADS * DSA_DIM, DSA_KV_HEADS * DSA_DIM, DSA_KV_HEADS * DSA_DIM, DSA_WIDTH,
    IDX_HEADS * IDX_DIM, IDX_DIM, IDX_HEADS,
    D_MODEL, D_MODEL,
)

LANES = 128
MXU_WIDTH = 256
VMEM_LIMIT = 56 * 1024 * 1024
INT_MIN = -(2 ** 31)
F32_LOWEST = float(np.finfo(np.float32).min)
NEG_BIG = -1e30
LOG2_E = 1.4426950408889634
FAST_MAX_SHIFT = 60.0
BOUND_MARGIN = 1.02

OFF_A = 0
W_A = MLA_Q_RANK + MLA_KV_RANK + LANES
OFF_B = OFF_A + W_A
W_B = DSA_WIDTH + 2 * DSA_KV_HEADS * DSA_DIM
OFF_I = OFF_B + W_B
W_I = IDX_HEADS * IDX_DIM + 2 * LANES
OFF_G = OFF_I + W_I
W_G = MLA_WIDTH + DSA_WIDTH + 2 * D_MODEL
W_ALL = OFF_G + W_G
GATE_CHUNK = 512
N_GATE_CHUNKS = W_G // GATE_CHUNK

DSA_HEAD_ORDER = tuple(h for j in range(DSA_GROUP) for h in (j, DSA_GROUP + j))

ROPE_LAYOUTS = ((LANES, MLA_NOPE, MLA_ROPE, 0), (DSA_DIM, 0, DSA_ROT, MLA_ROPE // 2),
                (IDX_DIM, 0, IDX_ROT, MLA_ROPE // 2 + DSA_ROT // 2))
ROW_A, ROW_D, ROW_I = 0, 1, 2
FREQ_ROWS = 32
ONE_ROW = FREQ_ROWS - 1


def _rms(v, width):
    return lax.rsqrt(jnp.sum(v * v, axis=-1, keepdims=True) * (1.0 / width) + NORM_EPS)


def _rope(v, cos, sin_lo, sin_hi, half):
    return (v * cos + pltpu.roll(v, LANES - half, 1) * sin_lo + pltpu.roll(v, half, 1) * sin_hi)


def _bf16_dot(a, b):
    return jnp.dot(a, b, preferred_element_type=jnp.float32)


def _nt_dot(a, b):
    return lax.dot_general(a, b, (((1,), (1,)), ((), ())), preferred_element_type=jnp.float32)


def _proj_kernel(x_ref, pos_ref, invf_ref, spread_ref, gx_ref, gcq_ref, gckv_ref, gq_ref, gk_ref, gdq_ref, gdk_ref,
                 w_ref, wuq_ref, wuk_ref, wuv_ref,
                 qa_ref, ka_ref, va_ref, qb_ref, kb_ref, vb_ref, qi_ref, ki_ref, wi_ref, g_ref):
    x = x_ref[0]
    h = (x * _rms(x, D_MODEL) * gx_ref[...]).astype(jnp.bfloat16)
    lane = lax.broadcasted_iota(jnp.int32, (1, LANES), 1)
    bf16 = jnp.bfloat16

    assert N_GATE_CHUNKS == 6

    def gate_logits(c):
        cols = slice(c * GATE_CHUNK, (c + 1) * GATE_CHUNK)
        g_ref[0, :, cols] = _bf16_dot(h, w_ref[:, OFF_G + c * GATE_CHUNK:OFF_G + (c + 1) * GATE_CHUNK]).astype(g_ref.dtype)

    pa = _bf16_dot(h, w_ref[:, OFF_A:OFF_A + W_A])
    pb = _bf16_dot(h, w_ref[:, OFF_B:OFF_B + W_B])
    pi = _bf16_dot(h, w_ref[:, OFF_I:OFF_I + W_I])

    ang = invf_ref[...] * pos_ref[0].astype(jnp.float32)
    trig = jnp.concatenate([jnp.cos(ang), jnp.sin(ang)], axis=0)
    t1 = trig.astype(bf16)
    r1 = trig - t1.astype(jnp.float32)
    t2 = r1.astype(bf16)
    t3 = (r1 - t2.astype(jnp.float32)).astype(bf16)
    tabs = lax.dot_general(jnp.concatenate([t1, t2, t3], axis=0), spread_ref[...], (((0,), (0,)), ((), ())),
                           preferred_element_type=jnp.float32)

    def tables(n):
        return tuple(tabs[:, (3 * n + c) * LANES:(3 * n + c + 1) * LANES] for c in range(3))

    gate_logits(0)
    gate_logits(1)

    c_q = pa[:, :MLA_Q_RANK]
    c_kv = pa[:, MLA_Q_RANK:MLA_Q_RANK + MLA_KV_RANK]
    kpe = pa[:, MLA_Q_RANK + MLA_KV_RANK:]
    cqn = (c_q * _rms(c_q, MLA_Q_RANK) * gcq_ref[...]).astype(bf16)
    ckvn = (c_kv * _rms(c_kv, MLA_KV_RANK) * gckv_ref[...]).astype(bf16)
    q_up = _bf16_dot(cqn, wuq_ref[...])
    k_up = _bf16_dot(ckvn, wuk_ref[...])
    v_up = _bf16_dot(ckvn, wuv_ref[...])
    gate_logits(2)

    cos_d, sl_d, sh_d = tables(ROW_D)
    low = lane < DSA_DIM

    def dsa_norm_rope(blk, gain):
        sq = blk * blk
        r_lo = lax.rsqrt(jnp.sum(jnp.where(low, sq, 0.0), axis=-1, keepdims=True) * (1.0 / DSA_DIM) + NORM_EPS)
        r_hi = lax.rsqrt(jnp.sum(jnp.where(low, 0.0, sq), axis=-1, keepdims=True) * (1.0 / DSA_DIM) + NORM_EPS)
        return _rope(blk * jnp.where(low, r_lo, r_hi) * gain, cos_d, sl_d, sh_d, DSA_ROT // 2)

    dsa_scale = DSA_DIM ** -0.5 * LOG2_E
    for j in range(DSA_WIDTH // LANES):
        blk = pb[:, j * LANES:(j + 1) * LANES]
        qn = (dsa_norm_rope(blk, gdq_ref[...]) * dsa_scale).astype(bf16)
        qb_ref[0, j] = jnp.where(low, qn, jnp.zeros_like(qn))
        qb_ref[0, DSA_GROUP + j] = jnp.where(low, jnp.zeros_like(qn), qn)
    kb_ref[0] = dsa_norm_rope(pb[:, DSA_WIDTH:DSA_WIDTH + LANES], gdk_ref[...]).astype(bf16)
    vb_ref[0, 0] = pb[:, DSA_WIDTH + LANES:].T.astype(bf16)
    gate_logits(3)

    cos_i, sl_i, sh_i = tables(ROW_I)
    n_qi = IDX_HEADS * IDX_DIM // LANES
    for j in range(n_qi + 1):
        blk = _rope(pi[:, j * LANES:(j + 1) * LANES], cos_i, sl_i, sh_i, IDX_ROT // 2).astype(bf16)
        if j < n_qi:
            qi_ref[0, :, j * LANES:(j + 1) * LANES] = blk
        else:
            ki_ref[0] = blk
    wi_t = (pi[:, (n_qi + 1) * LANES:] * (IDX_HEADS ** -0.5 * IDX_DIM ** -0.5)).T
    wi_ref[0] = wi_t[0:IDX_HEADS, :]
    gate_logits(4)

    for c in range(MLA_WIDTH // LANES):
        va_ref[0, 0, c * LANES:(c + 1) * LANES, :] = v_up[:, c * LANES:(c + 1) * LANES].T.astype(bf16)
    cos_a, sl_a, sh_a = tables(ROW_A)
    q_scale = MLA_QK ** -0.5 * LOG2_E
    kpe_g = kpe * gk_ref[...]
    k_rot = (pltpu.roll(kpe_g, LANES - MLA_ROPE // 2, 1) * sl_a + pltpu.roll(kpe_g, MLA_ROPE // 2, 1) * sh_a)
    for hd in range(MLA_HEADS):
        qh = q_up[:, hd * LANES:(hd + 1) * LANES]
        qn = qh * _rms(qh, MLA_QK) * gq_ref[...]
        qa_ref[0, hd] = (_rope(qn, cos_a, sl_a, sh_a, MLA_ROPE // 2) * q_scale).astype(bf16)
        kh = k_up[:, hd * LANES:(hd + 1) * LANES] + kpe
        ka_ref[0, hd] = ((kh * gk_ref[...] * cos_a + k_rot) * _rms(kh, MLA_QK)).astype(bf16)
    gate_logits(5)


def _softmax_tile(s, m, l, shift):
    if shift is None:
        m_new = jnp.maximum(m, jnp.max(s, axis=0, keepdims=True))
        alpha = jnp.exp2(m - m_new)
        p = jnp.exp2(s - m_new)
        return m_new, alpha * l + jnp.sum(p, axis=0, keepdims=True), alpha, p.astype(jnp.bfloat16)
    p = jnp.exp2(s - shift)
    return m, l + jnp.sum(p, axis=0, keepdims=True), None, p.astype(jnp.bfloat16)


def _by_shift(shift_ref, body):
    shift = shift_ref[0]

    @pl.when(shift <= FAST_MAX_SHIFT)
    def _():
        body(shift)

    @pl.when(shift > FAST_MAX_SHIFT)
    def _():
        body(None)


def _mla_kernel(shift_ref, q_ref, k_ref, vt_ref, o_ref, *, tq, tk, heads):
    i = pl.program_id(2)
    q_start = i * tq
    n_full = q_start // tk
    n_all = (q_start + tq + tk - 1) // tk
    tv = vt_ref.shape[-1]
    kpos0 = lax.broadcasted_iota(jnp.int32, (tk, tq), 0)
    qpos = q_start + lax.broadcasted_iota(jnp.int32, (tk, tq), 1)

    def attend(shift):
        def step(j, carry, masked):
            off = pl.multiple_of(j * tk, tk)
            scores = [_nt_dot(k_ref[0, hd, pl.ds(off, tk), :], q_ref[0, hd]) for hd in range(heads)]
            out = []
            for hd in range(heads):
                m, l, acc = carry[hd]
                s = scores[hd]
                if masked:
                    s = jnp.where(kpos0 + off <= qpos, s, NEG_BIG)
                m, l, alpha, pb = _softmax_tile(s, m, l, shift)
                if alpha is not None:
                    acc = alpha * acc
                for c in range(tk // tv):
                    vt = vt_ref[0, j * (tk // tv) + c, hd * MLA_V:(hd + 1) * MLA_V, :]
                    acc = acc + _bf16_dot(vt, pb[c * tv:(c + 1) * tv])
                out.append((m, l, acc))
            return tuple(out)

        init = tuple((jnp.full((1, tq), NEG_BIG, jnp.float32), jnp.zeros((1, tq), jnp.float32),
                      jnp.zeros((MLA_V, tq), jnp.float32)) for _ in range(heads))
        carry = lax.fori_loop(0, n_full, functools.partial(step, masked=False), init)
        carry = lax.fori_loop(n_full, n_all, functools.partial(step, masked=True), carry)
        outs = [acc / l for (_, l, acc) in carry]
        for c in range(heads // 2):
            o_ref[0, :, c * LANES:(c + 1) * LANES] = jnp.concatenate(outs[2 * c:2 * c + 2], axis=0).T

    _by_shift(shift_ref, attend)


def _dsa_kernel(shift_ref, qb_ref, kb_ref, vt_ref, qi_ref, ki_ref, wt_ref, o_ref,
                sc_ref, qim_ref, *, tq, tk, k_top):
    i = pl.program_id(1)
    q_start = i * tq
    n_kv = (q_start + tq + tk - 1) // tk
    n_heads = IDX_HEADS
    tv = vt_ref.shape[-1]

    lane_q = lax.broadcasted_iota(jnp.int32, (tq, LANES), 1)
    per_blk = LANES // IDX_DIM
    for hd in range(IDX_HEADS):
        blk = qi_ref[0, :, (hd // per_blk) * LANES:(hd // per_blk + 1) * LANES]
        slot = hd % per_blk
        in_head = (lane_q >= slot * IDX_DIM) & (lane_q < (slot + 1) * IDX_DIM)
        qim_ref[hd * tq:(hd + 1) * tq, :] = jnp.where(in_head, blk, jnp.zeros_like(blk))

    kpos0 = lax.broadcasted_iota(jnp.int32, (tk, tq), 0)
    qpos = q_start + lax.broadcasted_iota(jnp.int32, (tk, tq), 1)
    w_rows = [wt_ref[0, hd:hd + 1, :] for hd in range(IDX_HEADS)]

    def score_tile(j, _):
        off = pl.multiple_of(j * tk, tk)
        s_all = _nt_dot(ki_ref[0, pl.ds(off, tk), :], qim_ref[...])
        sc = w_rows[0] * jnp.maximum(s_all[:, 0:tq], 0.0)
        for hd in range(1, n_heads):
            sc = sc + w_rows[hd] * jnp.maximum(s_all[:, hd * tq:(hd + 1) * tq], 0.0)
        sc_ref[j] = jnp.where(kpos0 + off <= qpos, sc, -jnp.inf)
        return 0

    lax.fori_loop(0, n_kv, score_tile, 0)

    def pattern_value(u):
        key = u ^ INT_MIN
        return lax.bitcast_convert_type(key ^ ((key >> 31) & 0x7FFFFFFF), jnp.float32)

    def count(pred):
        def body(j, acc):
            hit = jnp.where(pred(sc_ref[j], j), 1, 0)
            return acc + jnp.sum(hit.reshape(tk // 8, 8, tq), axis=0)
        acc = lax.fori_loop(0, n_kv, body, jnp.zeros((8, tq), jnp.int32))
        return jnp.sum(acc, axis=0, keepdims=True)

    def value_bit(it, carry):
        t, cnt_t = carry
        cand = t | jnp.left_shift(jnp.int32(1), 31 - it)
        cand_f = pattern_value(cand)
        cnt = count(lambda sc, j: sc >= cand_f)
        take = cnt >= k_top
        return jnp.where(take, cand, t), jnp.where(take, cnt, cnt_t)

    t0 = jnp.zeros((1, tq), jnp.int32)
    t, cnt_ge = lax.fori_loop(0, 32, value_bit, (t0, t0))
    few = (t >= 0) & (t < 0x00800000)
    thr = jnp.where(few, F32_LOWEST, pattern_value(t))
    tied = jnp.logical_not(few) & (cnt_ge > k_top)

    def tie_cut():
        cnt_gt = count(lambda sc, j: sc > thr)
        need = k_top - cnt_gt

        def index_bit(it, c):
            cand = c | jnp.left_shift(jnp.int32(1), 12 - it)
            below = count(lambda sc, j: (sc == thr) & (kpos0 + j * tk < cand))
            return jnp.where(below < need, cand, c)

        c = lax.fori_loop(0, 13, index_bit, jnp.zeros((1, tq), jnp.int32))
        return jnp.where(tied, c, jnp.int32(2 ** 30))

    cut = lax.cond(jnp.max(tied.astype(jnp.int32)) > 0, tie_cut,
                   lambda: jnp.full((1, tq), 2 ** 30, jnp.int32))

    hpd = max(1, MXU_WIDTH // tq)
    n_dots = n_heads // hpd
    wd = hpd * tq

    def attend_all(shift):
        def attend(j, carry):
            off = pl.multiple_of(j * tk, tk)
            kt = kb_ref[0, pl.ds(off, tk), :]
            scores = [_nt_dot(kt, qb_ref[0, hpd * d:hpd * (d + 1)].reshape(wd, LANES)) for d in range(n_dots)]
            sc = sc_ref[j]
            sel = (sc > thr) | ((sc == thr) & (kpos0 + off <= cut))
            sel_d = jnp.concatenate([sel] * hpd, axis=1) if hpd > 1 else sel
            out = []
            for d in range(n_dots):
                m, l, acc = carry[d]
                m, l, alpha, pb = _softmax_tile(jnp.where(sel_d, scores[d], NEG_BIG), m, l, shift)
                if alpha is not None:
                    acc = alpha * acc
                for c in range(tk // tv):
                    acc = acc + _bf16_dot(vt_ref[0, j * (tk // tv) + c], pb[c * tv:(c + 1) * tv])
                out.append((m, l, acc))
            return tuple(out)

        init = tuple((jnp.full((1, wd), NEG_BIG, jnp.float32), jnp.zeros((1, wd), jnp.float32),
                      jnp.zeros((LANES, wd), jnp.float32)) for _ in range(n_dots))
        carry = lax.fori_loop(0, n_kv, attend, init)
        o = jnp.concatenate([acc / l for (_, l, acc) in carry], axis=1)
        for j in range(DSA_GROUP):
            lo = o[:, j * tq:(j + 1) * tq].T
            hi = o[:, (DSA_GROUP + j) * tq:(DSA_GROUP + j + 1) * tq].T
            o_ref[0, :, j * LANES:(j + 1) * LANES] = jnp.where(lane_q < DSA_DIM, lo, hi)

    _by_shift(shift_ref, attend_all)


def _merge_kernel(x_ref, oa_ref, ob_ref, g_ref, bm_ref, wa_ref, wb_ref, wo_ref, out_ref):
    f32 = jnp.float32
    ga = g_ref[:, 0:MLA_WIDTH].astype(f32)
    gb = g_ref[:, MLA_WIDTH:MLA_WIDTH + DSA_WIDTH].astype(f32)
    ma = g_ref[:, MLA_WIDTH + DSA_WIDTH:MLA_WIDTH + DSA_WIDTH + D_MODEL].astype(f32)
    mb = g_ref[:, MLA_WIDTH + DSA_WIDTH + D_MODEL:].astype(f32)
    ya = _bf16_dot((oa_ref[...] * (ga * jax.nn.sigmoid(ga))).astype(jnp.bfloat16), wa_ref[...])
    yb = _bf16_dot((ob_ref[...] * (gb * jax.nn.sigmoid(gb))).astype(jnp.bfloat16), wb_ref[...])
    merged = jax.nn.sigmoid(ma + bm_ref[0:1, :]) * ya + jax.nn.sigmoid(mb + bm_ref[1:2, :]) * yb
    out_ref[...] = x_ref[...] + _bf16_dot(merged.astype(jnp.bfloat16), wo_ref[...])


def _inv_freq(rot_dim):
    return ROPE_THETA ** (-jnp.arange(0, rot_dim, 2, dtype=jnp.float32) / rot_dim)


def _rope_inputs():
    invf = jnp.concatenate([_inv_freq(rot) for (_, _, rot, _) in ROPE_LAYOUTS])
    invf = jnp.pad(invf, (0, FREQ_ROWS - invf.shape[0]))[:, None]
    spread = np.zeros((2 * FREQ_ROWS, 3 * len(ROPE_LAYOUTS) * LANES), np.float32)
    for n, (period, start, rot, row0) in enumerate(ROPE_LAYOUTS):
        half = rot // 2
        for lane in range(LANES):
            p = lane % period - start
            cos_col, lo_col, hi_col = ((3 * n + c) * LANES + lane for c in range(3))
            if 0 <= p < rot:
                spread[row0 + p % half, cos_col] = 1.0
                spread[FREQ_ROWS + row0 + p % half, lo_col if p < half else hi_col] = -1.0 if p < half else 1.0
            else:
                spread[ONE_ROW, cos_col] = 1.0
    return invf, jnp.asarray(np.tile(spread, (3, 1)), jnp.bfloat16)


def _pad_lanes(a, start=0):
    return jnp.pad(a, ((0, 0), (start, LANES - start - a.shape[1])))


def _layout_params(w_in, mla_w_uq, mla_w_ukv, mla_q_gain, mla_k_gain, dsa_q_gain, dsa_k_gain,
                   w_branch_dsa):
    offs = np.concatenate([[0], np.cumsum(IN_SPLITS)])
    (c_q, c_kv, k_pe, gate_a, q_b, k_b, v_b, gate_b, q_i, k_i, w_i, m_a, m_b) = [
        w_in[:, offs[n]:offs[n + 1]] for n in range(len(IN_SPLITS))]

    def pair_heads(w):
        rows = w.shape[0]
        w = w.reshape(rows, DSA_KV_HEADS, DSA_GROUP, DSA_DIM)
        return jnp.transpose(w, (0, 2, 1, 3)).reshape(rows, DSA_WIDTH)

    w_all = jnp.concatenate([
        c_q, c_kv, _pad_lanes(k_pe, MLA_NOPE),
        pair_heads(q_b), k_b, v_b,
        q_i, jnp.tile(k_i, (1, LANES // IDX_DIM)), _pad_lanes(w_i),
        gate_a, pair_heads(gate_b), m_a, m_b], axis=1).astype(jnp.bfloat16)
    uq = mla_w_uq.reshape(MLA_Q_RANK, MLA_HEADS, MLA_QK)
    wuq = jnp.pad(uq, ((0, 0), (0, 0), (0, LANES - MLA_QK))).reshape(MLA_Q_RANK, MLA_HEADS * LANES)
    ukv = mla_w_ukv.reshape(MLA_KV_RANK, MLA_HEADS, MLA_NOPE + MLA_V)
    wuk = jnp.pad(ukv[:, :, :MLA_NOPE], ((0, 0), (0, 0), (0, LANES - MLA_NOPE))).reshape(MLA_KV_RANK, MLA_HEADS * LANES)
    wuv = ukv[:, :, MLA_NOPE:].reshape(MLA_KV_RANK, MLA_WIDTH)
    gq = _pad_lanes(mla_q_gain[None, :])
    gk = _pad_lanes(mla_k_gain[None, :])
    gdq = jnp.tile(dsa_q_gain[None, :], (1, LANES // DSA_DIM))
    gdk = jnp.tile(dsa_k_gain[None, :], (1, LANES // DSA_DIM))
    wb_dsa = jnp.transpose(w_branch_dsa.reshape(DSA_KV_HEADS, DSA_GROUP, DSA_DIM, D_MODEL),
                           (1, 0, 2, 3)).reshape(DSA_WIDTH, D_MODEL)
    return (w_all, wuq.astype(jnp.bfloat16), wuk.astype(jnp.bfloat16), wuv.astype(jnp.bfloat16),
            gq, gk, gdq, gdk, wb_dsa.astype(jnp.bfloat16))


def _const_spec(shape):
    return pl.BlockSpec(shape, lambda *_: (0,) * len(shape))


def _layer(x, positions, invf, spread, norm_gain, w_in, b_merge, mla_q_norm, mla_w_uq, mla_kv_norm, mla_w_ukv,
           mla_q_gain, mla_k_gain, dsa_q_gain, dsa_k_gain, w_branch_mla, w_branch_dsa, w_out):
    bsz, s_len, _ = x.shape
    k_top = min(TOPK_MAX, s_len // 4)
    (w_all, wuq, wuk, wuv, gq, gk, gdq, gdk, wb_dsa) = _layout_params(
        w_in, mla_w_uq, mla_w_ukv, mla_q_gain, mla_k_gain, dsa_q_gain, dsa_k_gain, w_branch_dsa)
    bf16, f32 = jnp.bfloat16, jnp.float32
    gmax = lambda g: jnp.max(jnp.abs(g))
    shift_a = (BOUND_MARGIN * MLA_QK ** 0.5 * LOG2_E * gmax(mla_q_gain) * gmax(mla_k_gain)).reshape(1).astype(f32)
    shift_b = (BOUND_MARGIN * DSA_DIM ** 0.5 * LOG2_E * gmax(dsa_q_gain) * gmax(dsa_k_gain)).reshape(1).astype(f32)
    smem = pl.BlockSpec(memory_space=pltpu.SMEM)

    tm = 256
    tok = lambda w: pl.BlockSpec((1, tm, w), lambda b, i: (b, i, 0))
    head = pl.BlockSpec((1, MLA_HEADS, tm, LANES), lambda b, i: (b, 0, i, 0))
    outs = pl.pallas_call(
        _proj_kernel,
        grid=(bsz, s_len // tm),
        in_specs=[tok(D_MODEL), pl.BlockSpec((1, 1, tm), lambda b, i: (b, 0, i)),
                  _const_spec(invf.shape), _const_spec(spread.shape), _const_spec((1, D_MODEL)),
                  _const_spec((1, MLA_Q_RANK)), _const_spec((1, MLA_KV_RANK)),
                  _const_spec((1, LANES)), _const_spec((1, LANES)), _const_spec((1, LANES)), _const_spec((1, LANES)),
                  _const_spec((D_MODEL, W_ALL)), _const_spec((MLA_Q_RANK, MLA_HEADS * LANES)),
                  _const_spec((MLA_KV_RANK, MLA_HEADS * LANES)), _const_spec((MLA_KV_RANK, MLA_WIDTH))],
        out_specs=[head, head, pl.BlockSpec((1, 1, MLA_WIDTH, tm), lambda b, i: (b, i, 0, 0)),
                   head, tok(LANES),
                   pl.BlockSpec((1, 1, LANES, tm), lambda b, i: (b, i, 0, 0)),
                   tok(IDX_HEADS * IDX_DIM), tok(LANES),
                   pl.BlockSpec((1, IDX_HEADS, tm), lambda b, i: (b, 0, i)), tok(W_G)],
        out_shape=[jax.ShapeDtypeStruct((bsz, MLA_HEADS, s_len, LANES), bf16),
                   jax.ShapeDtypeStruct((bsz, MLA_HEADS, s_len, LANES), bf16),
                   jax.ShapeDtypeStruct((bsz, s_len // tm, MLA_WIDTH, tm), bf16),
                   jax.ShapeDtypeStruct((bsz, DSA_HEADS, s_len, LANES), bf16),
                   jax.ShapeDtypeStruct((bsz, s_len, LANES), bf16),
                   jax.ShapeDtypeStruct((bsz, s_len // tm, LANES, tm), bf16),
                   jax.ShapeDtypeStruct((bsz, s_len, IDX_HEADS * IDX_DIM), bf16),
                   jax.ShapeDtypeStruct((bsz, s_len, LANES), bf16),
                   jax.ShapeDtypeStruct((bsz, IDX_HEADS, s_len), f32),
                   jax.ShapeDtypeStruct((bsz, s_len, W_G), bf16)],
        compiler_params=pltpu.CompilerParams(dimension_semantics=("arbitrary", "arbitrary"),
                                             vmem_limit_bytes=VMEM_LIMIT),
        name="proj",
    )(x, positions[:, None, :], invf, spread, norm_gain[None, :], mla_q_norm[None, :], mla_kv_norm[None, :],
      gq, gk, gdq, gdk, w_all, wuq, wuk, wuv)
    qa, ka, vat, qb, kb, vbt, qi, ki, wit, gates = outs

    tq_a, tk_a, hps = 512, 512, 8
    o_a = pl.pallas_call(
        functools.partial(_mla_kernel, tq=tq_a, tk=tk_a, heads=hps),
        grid=(bsz, MLA_HEADS // hps, s_len // tq_a),
        in_specs=[smem, pl.BlockSpec((1, hps, tq_a, LANES), lambda b, hp, i: (b, hp, i, 0)),
                  pl.BlockSpec((1, hps, s_len, LANES), lambda b, hp, i: (b, hp, 0, 0)),
                  pl.BlockSpec((1, s_len // tm, hps * MLA_V, tm), lambda b, hp, i: (b, 0, hp, 0))],
        out_specs=pl.BlockSpec((1, tq_a, hps * MLA_V), lambda b, hp, i: (b, i, hp)),
        out_shape=jax.ShapeDtypeStruct((bsz, s_len, MLA_WIDTH), f32),
        compiler_params=pltpu.CompilerParams(dimension_semantics=("arbitrary",) * 3,
                                             vmem_limit_bytes=VMEM_LIMIT),
        name="mla",
    )(shift_a, qa, ka, vat)

    tq_b, tk_b = 256, 512
    row_b = lambda w: pl.BlockSpec((1, tq_b, w), lambda b, i: (b, i, 0))
    full_b = lambda w: pl.BlockSpec((1, s_len, w), lambda b, i: (b, 0, 0))
    o_b = pl.pallas_call(
        functools.partial(_dsa_kernel, tq=tq_b, tk=tk_b, k_top=k_top),
        grid=(bsz, s_len // tq_b),
        in_specs=[smem, pl.BlockSpec((1, DSA_HEADS, tq_b, LANES), lambda b, i: (b, 0, i, 0)), full_b(LANES),
                  pl.BlockSpec((1, s_len // tm, LANES, tm), lambda b, i: (b, 0, 0, 0)),
                  row_b(IDX_HEADS * IDX_DIM), full_b(LANES),
                  pl.BlockSpec((1, IDX_HEADS, tq_b), lambda b, i: (b, 0, i))],
        out_specs=row_b(DSA_WIDTH),
        out_shape=jax.ShapeDtypeStruct((bsz, s_len, DSA_WIDTH), f32),
        scratch_shapes=[pltpu.VMEM((s_len // tk_b, tk_b, tq_b), f32),
                        pltpu.VMEM((IDX_HEADS * tq_b, LANES), bf16)],
        compiler_params=pltpu.CompilerParams(dimension_semantics=("arbitrary", "arbitrary"),
                                             vmem_limit_bytes=VMEM_LIMIT),
        name="dsa",
    )(shift_b, qb, kb, vbt, qi, ki, wit)

    tm_m = 512
    n_tok = bsz * s_len
    flat = lambda w: pl.BlockSpec((tm_m, w), lambda i: (i, 0))
    out = pl.pallas_call(
        _merge_kernel,
        grid=(n_tok // tm_m,),
        in_specs=[flat(D_MODEL), flat(MLA_WIDTH), flat(DSA_WIDTH), flat(W_G), _const_spec((2, D_MODEL)),
                  _const_spec((MLA_WIDTH, D_MODEL)), _const_spec((DSA_WIDTH, D_MODEL)),
                  _const_spec((D_MODEL, D_MODEL))],
        out_specs=flat(D_MODEL),
        out_shape=jax.ShapeDtypeStruct((n_tok, D_MODEL), f32),
        compiler_params=pltpu.CompilerParams(dimension_semantics=("arbitrary",),
                                             vmem_limit_bytes=VMEM_LIMIT),
        name="merge",
    )(x.reshape(n_tok, D_MODEL), o_a.reshape(n_tok, MLA_WIDTH), o_b.reshape(n_tok, DSA_WIDTH),
      gates.reshape(n_tok, W_G), b_merge, w_branch_mla.astype(bf16), wb_dsa, w_out.astype(bf16))
    return out.reshape(bsz, s_len, D_MODEL)


def kernel(x, positions, norm_gain, w_in, b_merge, mla_q_norm, mla_w_uq, mla_kv_norm, mla_w_ukv,
           mla_q_gain, mla_k_gain, dsa_q_gain, dsa_k_gain, w_branch_mla, w_branch_dsa, w_out):
    invf, spread = _rope_inputs()
    for l in range(norm_gain.shape[0]):
        x = _layer(x, positions, invf, spread, norm_gain[l], w_in[l], b_merge[l], mla_q_norm[l], mla_w_uq[l],
                   mla_kv_norm[l], mla_w_ukv[l], mla_q_gain[l], mla_k_gain[l], dsa_q_gain[l], dsa_k_gain[l],
                   w_branch_mla[l], w_branch_dsa[l], w_out[l])
    return x
```

```python
import functools

import numpy as np
import jax
import jax.numpy as jnp
from jax import lax
from jax.experimental import pallas as pl
from jax.experimental.pallas import tpu as pltpu

D_MODEL = 1024
MLA_HEADS = 8
MLA_Q_RANK = 256
MLA_KV_RANK = 128
MLA_NOPE = 64
MLA_ROPE = 32
MLA_QK = MLA_NOPE + MLA_ROPE
MLA_V = 64
MLA_WIDTH = MLA_HEADS * MLA_V
DSA_HEADS = 8
DSA_KV_HEADS = 2
DSA_GROUP = DSA_HEADS // DSA_KV_HEADS
DSA_DIM = 64
DSA_WIDTH = DSA_HEADS * DSA_DIM
DSA_ROT = DSA_DIM // 4
IDX_HEADS = 8
IDX_DIM = 32
IDX_ROT = IDX_DIM // 4
TOPK_MAX = 256
ROPE_THETA = 500000.0
NORM_EPS = 1e-6

IN_SPLITS = (
    MLA_Q_RANK, MLA_KV_RANK, MLA_ROPE, MLA_WIDTH,
    DSA_HEADS * DSA_DIM, DSA_KV_HEADS * DSA_DIM, DSA_KV_HEADS * DSA_DIM, DSA_WIDTH,
    IDX_HEADS * IDX_DIM, IDX_DIM, IDX_HEADS,
    D_MODEL, D_MODEL,
)

LANES = 128
MXU_WIDTH = 256
VMEM_LIMIT = 56 * 1024 * 1024
INT_MIN = -(2 ** 31)
F32_LOWEST = float(np.finfo(np.float32).min)
BF16_ROWS = 16
FINE_BITS = 18
NEG_BIG = -1e30
LOG2_E = 1.4426950408889634
FAST_MAX_SHIFT = 60.0
BOUND_MARGIN = 1.02

OFF_A = 0
W_A = MLA_Q_RANK + MLA_KV_RANK + LANES
OFF_B = OFF_A + W_A
W_B = DSA_WIDTH + 2 * DSA_KV_HEADS * DSA_DIM
OFF_I = OFF_B + W_B
W_I = IDX_HEADS * IDX_DIM + 2 * LANES
OFF_G = OFF_I + W_I
W_G = MLA_WIDTH + DSA_WIDTH + 2 * D_MODEL
W_ALL = OFF_G + W_G
GATE_CHUNK = 512
N_GATE_CHUNKS = W_G // GATE_CHUNK

DSA_HEAD_ORDER = tuple(h for j in range(DSA_GROUP) for h in (j, DSA_GROUP + j))

ROPE_LAYOUTS = ((LANES, MLA_NOPE, MLA_ROPE, 0), (DSA_DIM, 0, DSA_ROT, MLA_ROPE // 2),
                (IDX_DIM, 0, IDX_ROT, MLA_ROPE // 2 + DSA_ROT // 2))
ROW_A, ROW_D, ROW_I = 0, 1, 2
FREQ_ROWS = 32
ONE_ROW = FREQ_ROWS - 1


def _rms(v, width):
    return lax.rsqrt(jnp.sum(v * v, axis=-1, keepdims=True) * (1.0 / width) + NORM_EPS)


def _rope(v, cos, sin_lo, sin_hi, half):
    return (v * cos + pltpu.roll(v, LANES - half, 1) * sin_lo + pltpu.roll(v, half, 1) * sin_hi)


def _bf16_dot(a, b):
    return jnp.dot(a, b, preferred_element_type=jnp.float32)


def _nt_dot(a, b):
    return lax.dot_general(a, b, (((1,), (1,)), ((), ())), preferred_element_type=jnp.float32)


def _proj_kernel(x_ref, pos_ref, invf_ref, spread_ref, gx_ref, gcq_ref, gckv_ref, gq_ref, gk_ref, gdq_ref, gdk_ref,
                 w_ref, wuq_ref, wuk_ref, wuv_ref,
                 qa_ref, ka_ref, va_ref, qb_ref, kb_ref, vb_ref, qi_ref, ki_ref, wi_ref, g_ref):
    x = x_ref[0]
    h = (x * _rms(x, D_MODEL) * gx_ref[...]).astype(jnp.bfloat16)
    lane = lax.broadcasted_iota(jnp.int32, (1, LANES), 1)
    bf16 = jnp.bfloat16

    assert N_GATE_CHUNKS == 6

    def gate_logits(c):
        cols = slice(c * GATE_CHUNK, (c + 1) * GATE_CHUNK)
        g_ref[0, :, cols] = _bf16_dot(h, w_ref[:, OFF_G + c * GATE_CHUNK:OFF_G + (c + 1) * GATE_CHUNK]).astype(g_ref.dtype)

    pa = _bf16_dot(h, w_ref[:, OFF_A:OFF_A + W_A])
    pb = _bf16_dot(h, w_ref[:, OFF_B:OFF_B + W_B])
    pi = _bf16_dot(h, w_ref[:, OFF_I:OFF_I + W_I])

    ang = invf_ref[...] * pos_ref[0].astype(jnp.float32)
    trig = jnp.concatenate([jnp.cos(ang), jnp.sin(ang)], axis=0)
    t1 = trig.astype(bf16)
    r1 = trig - t1.astype(jnp.float32)
    t2 = r1.astype(bf16)
    t3 = (r1 - t2.astype(jnp.float32)).astype(bf16)
    tabs = lax.dot_general(jnp.concatenate([t1, t2, t3], axis=0), spread_ref[...], (((0,), (0,)), ((), ())),
                           preferred_element_type=jnp.float32)

    def tables(n):
        return tuple(tabs[:, (3 * n + c) * LANES:(3 * n + c + 1) * LANES] for c in range(3))

    gate_logits(0)
    gate_logits(1)

    c_q = pa[:, :MLA_Q_RANK]
    c_kv = pa[:, MLA_Q_RANK:MLA_Q_RANK + MLA_KV_RANK]
    kpe = pa[:, MLA_Q_RANK + MLA_KV_RANK:]
    cqn = (c_q * _rms(c_q, MLA_Q_RANK) * gcq_ref[...]).astype(bf16)
    ckvn = (c_kv * _rms(c_kv, MLA_KV_RANK) * gckv_ref[...]).astype(bf16)
    q_up = _bf16_dot(cqn, wuq_ref[...])
    k_up = _bf16_dot(ckvn, wuk_ref[...])
    v_up = _bf16_dot(ckvn, wuv_ref[...])
    gate_logits(2)

    cos_d, sl_d, sh_d = tables(ROW_D)
    low = lane < DSA_DIM

    def dsa_norm_rope(blk, gain):
        sq = blk * blk
        r_lo = lax.rsqrt(jnp.sum(jnp.where(low, sq, 0.0), axis=-1, keepdims=True) * (1.0 / DSA_DIM) + NORM_EPS)
        r_hi = lax.rsqrt(jnp.sum(jnp.where(low, 0.0, sq), axis=-1, keepdims=True) * (1.0 / DSA_DIM) + NORM_EPS)
        return _rope(blk * jnp.where(low, r_lo, r_hi) * gain, cos_d, sl_d, sh_d, DSA_ROT // 2)

    dsa_scale = DSA_DIM ** -0.5 * LOG2_E
    for j in range(DSA_WIDTH // LANES):
        blk = pb[:, j * LANES:(j + 1) * LANES]
        qn = (dsa_norm_rope(blk, gdq_ref[...]) * dsa_scale).astype(bf16)
        qb_ref[0, j] = jnp.where(low, qn, jnp.zeros_like(qn))
        qb_ref[0, DSA_GROUP + j] = jnp.where(low, jnp.zeros_like(qn), qn)
    kb_ref[0] = dsa_norm_rope(pb[:, DSA_WIDTH:DSA_WIDTH + LANES], gdk_ref[...]).astype(bf16)
    vb_ref[0, 0] = pb[:, DSA_WIDTH + LANES:].T.astype(bf16)
    gate_logits(3)

    cos_i, sl_i, sh_i = tables(ROW_I)
    n_qi = IDX_HEADS * IDX_DIM // LANES
    for j in range(n_qi + 1):
        blk = _rope(pi[:, j * LANES:(j + 1) * LANES], cos_i, sl_i, sh_i, IDX_ROT // 2).astype(bf16)
        if j < n_qi:
            qi_ref[0, :, j * LANES:(j + 1) * LANES] = blk
        else:
            ki_ref[0] = blk
    wi_t = (pi[:, (n_qi + 1) * LANES:] * (IDX_HEADS ** -0.5 * IDX_DIM ** -0.5)).T
    wi_ref[0] = wi_t[0:IDX_HEADS, :]
    gate_logits(4)

    for c in range(MLA_WIDTH // LANES):
        va_ref[0, 0, c * LANES:(c + 1) * LANES, :] = v_up[:, c * LANES:(c + 1) * LANES].T.astype(bf16)
    cos_a, sl_a, sh_a = tables(ROW_A)
    q_scale = MLA_QK ** -0.5 * LOG2_E
    kpe_g = kpe * gk_ref[...]
    k_rot = (pltpu.roll(kpe_g, LANES - MLA_ROPE // 2, 1) * sl_a + pltpu.roll(kpe_g, MLA_ROPE // 2, 1) * sh_a)
    for hd in range(MLA_HEADS):
        qh = q_up[:, hd * LANES:(hd + 1) * LANES]
        qn = qh * _rms(qh, MLA_QK) * gq_ref[...]
        qa_ref[0, hd] = (_rope(qn, cos_a, sl_a, sh_a, MLA_ROPE // 2) * q_scale).astype(bf16)
        kh = k_up[:, hd * LANES:(hd + 1) * LANES] + kpe
        ka_ref[0, hd] = ((kh * gk_ref[...] * cos_a + k_rot) * _rms(kh, MLA_QK)).astype(bf16)
    gate_logits(5)


def _softmax_tile(s, m, l, shift):
    if shift is None:
        m_new = jnp.maximum(m, jnp.max(s, axis=0, keepdims=True))
        alpha = jnp.exp2(m - m_new)
        p = jnp.exp2(s - m_new)
        return m_new, alpha * l + jnp.sum(p, axis=0, keepdims=True), alpha, p.astype(jnp.bfloat16)
    p = jnp.exp2(s - shift)
    return m, l + jnp.sum(p, axis=0, keepdims=True), None, p.astype(jnp.bfloat16)


def _by_shift(shift_ref, body):
    shift = shift_ref[0]

    @pl.when(shift <= FAST_MAX_SHIFT)
    def _():
        body(shift)

    @pl.when(shift > FAST_MAX_SHIFT)
    def _():
        body(None)


def _mla_kernel(shift_ref, q_ref, k_ref, vt_ref, o_ref, *, tq, tk, heads):
    i = pl.program_id(2)
    q_start = i * tq
    n_full = q_start // tk
    n_all = (q_start + tq + tk - 1) // tk
    tv = vt_ref.shape[-1]
    kpos0 = lax.broadcasted_iota(jnp.int32, (tk, tq), 0)
    qpos = q_start + lax.broadcasted_iota(jnp.int32, (tk, tq), 1)

    def attend(shift):
        def step(j, carry, masked):
            off = pl.multiple_of(j * tk, tk)
            scores = [_nt_dot(k_ref[0, hd, pl.ds(off, tk), :], q_ref[0, hd]) for hd in range(heads)]
            out = []
            for hd in range(heads):
                m, l, acc = carry[hd]
                s = scores[hd]
                if masked:
                    s = jnp.where(kpos0 + off <= qpos, s, NEG_BIG)
                m, l, alpha, pb = _softmax_tile(s, m, l, shift)
                if alpha is not None:
                    acc = alpha * acc
                for c in range(tk // tv):
                    vt = vt_ref[0, j * (tk // tv) + c, hd * MLA_V:(hd + 1) * MLA_V, :]
                    acc = acc + _bf16_dot(vt, pb[c * tv:(c + 1) * tv])
                out.append((m, l, acc))
            return tuple(out)

        init = tuple((jnp.full((1, tq), NEG_BIG, jnp.float32), jnp.zeros((1, tq), jnp.float32),
                      jnp.zeros((MLA_V, tq), jnp.float32)) for _ in range(heads))
        carry = lax.fori_loop(0, n_full, functools.partial(step, masked=False), init)
        carry = lax.fori_loop(n_full, n_all, functools.partial(step, masked=True), carry)
        outs = [acc / l for (_, l, acc) in carry]
        for c in range(heads // 2):
            o_ref[0, :, c * LANES:(c + 1) * LANES] = jnp.concatenate(outs[2 * c:2 * c + 2], axis=0).T

    _by_shift(shift_ref, attend)


def _dsa_kernel(shift_ref, qb_ref, kb_ref, vt_ref, qi_ref, ki_ref, wt_ref, o_ref,
                sc_ref, sb_ref, qim_ref, *, tq, tk, k_top):
    i = pl.program_id(1)
    q_start = i * tq
    n_kv = (q_start + tq + tk - 1) // tk
    n_heads = IDX_HEADS
    tv = vt_ref.shape[-1]

    lane_q = lax.broadcasted_iota(jnp.int32, (tq, LANES), 1)
    per_blk = LANES // IDX_DIM
    for hd in range(IDX_HEADS):
        blk = qi_ref[0, :, (hd // per_blk) * LANES:(hd // per_blk + 1) * LANES]
        slot = hd % per_blk
        in_head = (lane_q >= slot * IDX_DIM) & (lane_q < (slot + 1) * IDX_DIM)
        qim_ref[hd * tq:(hd + 1) * tq, :] = jnp.where(in_head, blk, jnp.zeros_like(blk))

    kpos0 = lax.broadcasted_iota(jnp.int32, (tk, tq), 0)
    qpos = q_start + lax.broadcasted_iota(jnp.int32, (tk, tq), 1)
    w_rows = [wt_ref[0, hd:hd + 1, :] for hd in range(IDX_HEADS)]

    def score_tile(j, _):
        off = pl.multiple_of(j * tk, tk)
        s_all = _nt_dot(ki_ref[0, pl.ds(off, tk), :], qim_ref[...])
        sc = w_rows[0] * jnp.maximum(s_all[:, 0:tq], 0.0)
        for hd in range(1, n_heads):
            sc = sc + w_rows[hd] * jnp.maximum(s_all[:, hd * tq:(hd + 1) * tq], 0.0)
        sc = jnp.where(kpos0 + off <= qpos, sc, -jnp.inf)
        sc_ref[j] = sc
        sb_ref[j] = sc.astype(jnp.bfloat16)
        return 0

    lax.fori_loop(0, n_kv, score_tile, 0)

    def pattern_value(u):
        key = u ^ INT_MIN
        return lax.bitcast_convert_type(key ^ ((key >> 31) & 0x7FFFFFFF), jnp.float32)

    def count(pred):
        def body(j, acc):
            hit = jnp.where(pred(sc_ref[j], j), 1, 0)
            return acc + jnp.sum(hit.reshape(tk // 8, 8, tq), axis=0)
        acc = lax.fori_loop(0, n_kv, body, jnp.zeros((8, tq), jnp.int32))
        return jnp.sum(acc, axis=0, keepdims=True)

    def count_coarse(cand_b):
        one, zero = jnp.ones((), jnp.bfloat16), jnp.zeros((), jnp.bfloat16)

        def body(j, acc):
            hit = jnp.where(sb_ref[j] >= cand_b, one, zero)
            rows = [hit[r * BF16_ROWS:(r + 1) * BF16_ROWS] for r in range(tk // BF16_ROWS)]
            while len(rows) > 1:
                rows = [rows[r] + rows[r + 1] for r in range(0, len(rows), 2)]
            return acc + rows[0].astype(jnp.float32)
        acc = lax.fori_loop(0, n_kv, body, jnp.zeros((BF16_ROWS, tq), jnp.float32))
        return jnp.sum(acc, axis=0, keepdims=True)

    def coarse_bit(it, t16):
        cand = t16 | jnp.left_shift(jnp.int32(1), 15 - it)
        cand32 = jnp.left_shift(cand, 16) | jnp.where(cand < 0x8000, 0xFFFF, 0)
        cnt = count_coarse(pattern_value(cand32).astype(jnp.bfloat16))
        return jnp.where(cnt >= k_top, cand, t16)

    t16 = lax.fori_loop(0, 16, coarse_bit, jnp.zeros((1, tq), jnp.int32))
    few = t16 < 0x0080
    base = jnp.where(few, 0x00800000, jnp.left_shift(t16, 16) - 0x8000)

    def fine_bit(it, carry):
        t, cnt_t = carry
        cand = t | jnp.left_shift(jnp.int32(1), FINE_BITS - 1 - it)
        cand_f = pattern_value(base + cand)
        cnt = count(lambda sc, j: sc >= cand_f)
        take = cnt >= k_top
        return jnp.where(take, cand, t), jnp.where(take, cnt, cnt_t)

    t, cnt_ge = lax.fori_loop(0, FINE_BITS, fine_bit,
                              (jnp.zeros((1, tq), jnp.int32), jnp.full((1, tq), k_top, jnp.int32)))
    thr = jnp.where(few, F32_LOWEST, pattern_value(base + t))
    tied = jnp.logical_not(few) & ((cnt_ge > k_top) | (t == 0))

    def tie_cut():
        cnt_gt = count(lambda sc, j: sc > thr)
        need = k_top - cnt_gt

        def index_bit(it, c):
            cand = c | jnp.left_shift(jnp.int32(1), 12 - it)
            below = count(lambda sc, j: (sc == thr) & (kpos0 + j * tk < cand))
            return jnp.where(below < need, cand, c)

        c = lax.fori_loop(0, 13, index_bit, jnp.zeros((1, tq), jnp.int32))
        return jnp.where(tied, c, jnp.int32(2 ** 30))

    cut = lax.cond(jnp.max(tied.astype(jnp.int32)) > 0, tie_cut,
                   lambda: jnp.full((1, tq), 2 ** 30, jnp.int32))

    hpd = max(1, MXU_WIDTH // tq)
    n_dots = n_heads // hpd
    wd = hpd * tq

    def attend_all(shift):
        def attend(j, carry):
            off = pl.multiple_of(j * tk, tk)
            kt = kb_ref[0, pl.ds(off, tk), :]
            scores = [_nt_dot(kt, qb_ref[0, hpd * d:hpd * (d + 1)].reshape(wd, LANES)) for d in range(n_dots)]
            sc = sc_ref[j]
            sel = (sc > thr) | ((sc == thr) & (kpos0 + off <= cut))
            sel_d = jnp.concatenate([sel] * hpd, axis=1) if hpd > 1 else sel
            out = []
            for d in range(n_dots):
                m, l, acc = carry[d]
                m, l, alpha, pb = _softmax_tile(jnp.where(sel_d, scores[d], NEG_BIG), m, l, shift)
                if alpha is not None:
                    acc = alpha * acc
                for c in range(tk // tv):
                    acc = acc + _bf16_dot(vt_ref[0, j * (tk // tv) + c], pb[c * tv:(c + 1) * tv])
                out.append((m, l, acc))
            return tuple(out)

        init = tuple((jnp.full((1, wd), NEG_BIG, jnp.float32), jnp.zeros((1, wd), jnp.float32),
                      jnp.zeros((LANES, wd), jnp.float32)) for _ in range(n_dots))
        carry = lax.fori_loop(0, n_kv, attend, init)
        o = jnp.concatenate([acc / l for (_, l, acc) in carry], axis=1)
        for j in range(DSA_GROUP):
            lo = o[:, j * tq:(j + 1) * tq].T
            hi = o[:, (DSA_GROUP + j) * tq:(DSA_GROUP + j + 1) * tq].T
            o_ref[0, :, j * LANES:(j + 1) * LANES] = jnp.where(lane_q < DSA_DIM, lo, hi)

    _by_shift(shift_ref, attend_all)


def _merge_kernel(x_ref, oa_ref, ob_ref, g_ref, bm_ref, wa_ref, wb_ref, wo_ref, out_ref):
    f32 = jnp.float32
    ga = g_ref[:, 0:MLA_WIDTH].astype(f32)
    gb = g_ref[:, MLA_WIDTH:MLA_WIDTH + DSA_WIDTH].astype(f32)
    ma = g_ref[:, MLA_WIDTH + DSA_WIDTH:MLA_WIDTH + DSA_WIDTH + D_MODEL].astype(f32)
    mb = g_ref[:, MLA_WIDTH + DSA_WIDTH + D_MODEL:].astype(f32)
    ya = _bf16_dot((oa_ref[...] * (ga * jax.nn.sigmoid(ga))).astype(jnp.bfloat16), wa_ref[...])
    yb = _bf16_dot((ob_ref[...] * (gb * jax.nn.sigmoid(gb))).astype(jnp.bfloat16), wb_ref[...])
    merged = jax.nn.sigmoid(ma + bm_ref[0:1, :]) * ya + jax.nn.sigmoid(mb + bm_ref[1:2, :]) * yb
    out_ref[...] = x_ref[...] + _bf16_dot(merged.astype(jnp.bfloat16), wo_ref[...])


def _inv_freq(rot_dim):
    return ROPE_THETA ** (-jnp.arange(0, rot_dim, 2, dtype=jnp.float32) / rot_dim)


def _rope_inputs():
    invf = jnp.concatenate([_inv_freq(rot) for (_, _, rot, _) in ROPE_LAYOUTS])
    invf = jnp.pad(invf, (0, FREQ_ROWS - invf.shape[0]))[:, None]
    spread = np.zeros((2 * FREQ_ROWS, 3 * len(ROPE_LAYOUTS) * LANES), np.float32)
    for n, (period, start, rot, row0) in enumerate(ROPE_LAYOUTS):
        half = rot // 2
        for lane in range(LANES):
            p = lane % period - start
            cos_col, lo_col, hi_col = ((3 * n + c) * LANES + lane for c in range(3))
            if 0 <= p < rot:
                spread[row0 + p % half, cos_col] = 1.0
                spread[FREQ_ROWS + row0 + p % half, lo_col if p < half else hi_col] = -1.0 if p < half else 1.0
            else:
                spread[ONE_ROW, cos_col] = 1.0
    return invf, jnp.asarray(np.tile(spread, (3, 1)), jnp.bfloat16)


def _pad_lanes(a, start=0):
    return jnp.pad(a, ((0, 0), (start, LANES - start - a.shape[1])))


def _layout_params(w_in, mla_w_uq, mla_w_ukv, mla_q_gain, mla_k_gain, dsa_q_gain, dsa_k_gain,
                   w_branch_dsa):
    offs = np.concatenate([[0], np.cumsum(IN_SPLITS)])
    (c_q, c_kv, k_pe, gate_a, q_b, k_b, v_b, gate_b, q_i, k_i, w_i, m_a, m_b) = [
        w_in[:, offs[n]:offs[n + 1]] for n in range(len(IN_SPLITS))]

    def pair_heads(w):
        rows = w.shape[0]
        w = w.reshape(rows, DSA_KV_HEADS, DSA_GROUP, DSA_DIM)
        return jnp.transpose(w, (0, 2, 1, 3)).reshape(rows, DSA_WIDTH)

    w_all = jnp.concatenate([
        c_q, c_kv, _pad_lanes(k_pe, MLA_NOPE),
        pair_heads(q_b), k_b, v_b,
        q_i, jnp.tile(k_i, (1, LANES // IDX_DIM)), _pad_lanes(w_i),
        gate_a, pair_heads(gate_b), m_a, m_b], axis=1).astype(jnp.bfloat16)
    uq = mla_w_uq.reshape(MLA_Q_RANK, MLA_HEADS, MLA_QK)
    wuq = jnp.pad(uq, ((0, 0), (0, 0), (0, LANES - MLA_QK))).reshape(MLA_Q_RANK, MLA_HEADS * LANES)
    ukv = mla_w_ukv.reshape(MLA_KV_RANK, MLA_HEADS, MLA_NOPE + MLA_V)
    wuk = jnp.pad(ukv[:, :, :MLA_NOPE], ((0, 0), (0, 0), (0, LANES - MLA_NOPE))).reshape(MLA_KV_RANK, MLA_HEADS * LANES)
    wuv = ukv[:, :, MLA_NOPE:].reshape(MLA_KV_RANK, MLA_WIDTH)
    gq = _pad_lanes(mla_q_gain[None, :])
    gk = _pad_lanes(mla_k_gain[None, :])
    gdq = jnp.tile(dsa_q_gain[None, :], (1, LANES // DSA_DIM))
    gdk = jnp.tile(dsa_k_gain[None, :], (1, LANES // DSA_DIM))
    wb_dsa = jnp.transpose(w_branch_dsa.reshape(DSA_KV_HEADS, DSA_GROUP, DSA_DIM, D_MODEL),
                           (1, 0, 2, 3)).reshape(DSA_WIDTH, D_MODEL)
    return (w_all, wuq.astype(jnp.bfloat16), wuk.astype(jnp.bfloat16), wuv.astype(jnp.bfloat16),
            gq, gk, gdq, gdk, wb_dsa.astype(jnp.bfloat16))


def _const_spec(shape):
    return pl.BlockSpec(shape, lambda *_: (0,) * len(shape))


def _layer(x, positions, invf, spread, norm_gain, w_in, b_merge, mla_q_norm, mla_w_uq, mla_kv_norm, mla_w_ukv,
           mla_q_gain, mla_k_gain, dsa_q_gain, dsa_k_gain, w_branch_mla, w_branch_dsa, w_out):
    bsz, s_len, _ = x.shape
    k_top = min(TOPK_MAX, s_len // 4)
    (w_all, wuq, wuk, wuv, gq, gk, gdq, gdk, wb_dsa) = _layout_params(
        w_in, mla_w_uq, mla_w_ukv, mla_q_gain, mla_k_gain, dsa_q_gain, dsa_k_gain, w_branch_dsa)
    bf16, f32 = jnp.bfloat16, jnp.float32
    gmax = lambda g: jnp.max(jnp.abs(g))
    shift_a = (BOUND_MARGIN * MLA_QK ** 0.5 * LOG2_E * gmax(mla_q_gain) * gmax(mla_k_gain)).reshape(1).astype(f32)
    shift_b = (BOUND_MARGIN * DSA_DIM ** 0.5 * LOG2_E * gmax(dsa_q_gain) * gmax(dsa_k_gain)).reshape(1).astype(f32)
    smem = pl.BlockSpec(memory_space=pltpu.SMEM)

    tm = 256
    tok = lambda w: pl.BlockSpec((1, tm, w), lambda b, i: (b, i, 0))
    head = pl.BlockSpec((1, MLA_HEADS, tm, LANES), lambda b, i: (b, 0, i, 0))
    outs = pl.pallas_call(
        _proj_kernel,
        grid=(bsz, s_len // tm),
        in_specs=[tok(D_MODEL), pl.BlockSpec((1, 1, tm), lambda b, i: (b, 0, i)),
                  _const_spec(invf.shape), _const_spec(spread.shape), _const_spec((1, D_MODEL)),
                  _const_spec((1, MLA_Q_RANK)), _const_spec((1, MLA_KV_RANK)),
                  _const_spec((1, LANES)), _const_spec((1, LANES)), _const_spec((1, LANES)), _const_spec((1, LANES)),
                  _const_spec((D_MODEL, W_ALL)), _const_spec((MLA_Q_RANK, MLA_HEADS * LANES)),
                  _const_spec((MLA_KV_RANK, MLA_HEADS * LANES)), _const_spec((MLA_KV_RANK, MLA_WIDTH))],
        out_specs=[head, head, pl.BlockSpec((1, 1, MLA_WIDTH, tm), lambda b, i: (b, i, 0, 0)),
                   head, tok(LANES),
                   pl.BlockSpec((1, 1, LANES, tm), lambda b, i: (b, i, 0, 0)),
                   tok(IDX_HEADS * IDX_DIM), tok(LANES),
                   pl.BlockSpec((1, IDX_HEADS, tm), lambda b, i: (b, 0, i)), tok(W_G)],
        out_shape=[jax.ShapeDtypeStruct((bsz, MLA_HEADS, s_len, LANES), bf16),
                   jax.ShapeDtypeStruct((bsz, MLA_HEADS, s_len, LANES), bf16),
                   jax.ShapeDtypeStruct((bsz, s_len // tm, MLA_WIDTH, tm), bf16),
                   jax.ShapeDtypeStruct((bsz, DSA_HEADS, s_len, LANES), bf16),
                   jax.ShapeDtypeStruct((bsz, s_len, LANES), bf16),
                   jax.ShapeDtypeStruct((bsz, s_len // tm, LANES, tm), bf16),
                   jax.ShapeDtypeStruct((bsz, s_len, IDX_HEADS * IDX_DIM), bf16),
                   jax.ShapeDtypeStruct((bsz, s_len, LANES), bf16),
                   jax.ShapeDtypeStruct((bsz, IDX_HEADS, s_len), f32),
                   jax.ShapeDtypeStruct((bsz, s_len, W_G), bf16)],
        compiler_params=pltpu.CompilerParams(dimension_semantics=("arbitrary", "arbitrary"),
                                             vmem_limit_bytes=VMEM_LIMIT),
        name="proj",
    )(x, positions[:, None, :], invf, spread, norm_gain[None, :], mla_q_norm[None, :], mla_kv_norm[None, :],
      gq, gk, gdq, gdk, w_all, wuq, wuk, wuv)
    qa, ka, vat, qb, kb, vbt, qi, ki, wit, gates = outs

    tq_a, tk_a, hps = 512, 512, 8
    o_a = pl.pallas_call(
        functools.partial(_mla_kernel, tq=tq_a, tk=tk_a, heads=hps),
        grid=(bsz, MLA_HEADS // hps, s_len // tq_a),
        in_specs=[smem, pl.BlockSpec((1, hps, tq_a, LANES), lambda b, hp, i: (b, hp, i, 0)),
                  pl.BlockSpec((1, hps, s_len, LANES), lambda b, hp, i: (b, hp, 0, 0)),
                  pl.BlockSpec((1, s_len // tm, hps * MLA_V, tm), lambda b, hp, i: (b, 0, hp, 0))],
        out_specs=pl.BlockSpec((1, tq_a, hps * MLA_V), lambda b, hp, i: (b, i, hp)),
        out_shape=jax.ShapeDtypeStruct((bsz, s_len, MLA_WIDTH), f32),
        compiler_params=pltpu.CompilerParams(dimension_semantics=("arbitrary",) * 3,
                                             vmem_limit_bytes=VMEM_LIMIT),
        name="mla",
    )(shift_a, qa, ka, vat)

    tq_b, tk_b = 256, 512
    row_b = lambda w: pl.BlockSpec((1, tq_b, w), lambda b, i: (b, i, 0))
    full_b = lambda w: pl.BlockSpec((1, s_len, w), lambda b, i: (b, 0, 0))
    o_b = pl.pallas_call(
        functools.partial(_dsa_kernel, tq=tq_b, tk=tk_b, k_top=k_top),
        grid=(bsz, s_len // tq_b),
        in_specs=[smem, pl.BlockSpec((1, DSA_HEADS, tq_b, LANES), lambda b, i: (b, 0, i, 0)), full_b(LANES),
                  pl.BlockSpec((1, s_len // tm, LANES, tm), lambda b, i: (b, 0, 0, 0)),
                  row_b(IDX_HEADS * IDX_DIM), full_b(LANES),
                  pl.BlockSpec((1, IDX_HEADS, tq_b), lambda b, i: (b, 0, i))],
        out_specs=row_b(DSA_WIDTH),
        out_shape=jax.ShapeDtypeStruct((bsz, s_len, DSA_WIDTH), f32),
        scratch_shapes=[pltpu.VMEM((s_len // tk_b, tk_b, tq_b), f32),
                        pltpu.VMEM((s_len // tk_b, tk_b, tq_b), bf16),
                        pltpu.VMEM((IDX_HEADS * tq_b, LANES), bf16)],
        compiler_params=pltpu.CompilerParams(dimension_semantics=("arbitrary", "arbitrary"),
                                             vmem_limit_bytes=VMEM_LIMIT),
        name="dsa",
    )(shift_b, qb, kb, vbt, qi, ki, wit)

    tm_m = 512
    n_tok = bsz * s_len
    flat = lambda w: pl.BlockSpec((tm_m, w), lambda i: (i, 0))
    out = pl.pallas_call(
        _merge_kernel,
        grid=(n_tok // tm_m,),
        in_specs=[flat(D_MODEL), flat(MLA_WIDTH), flat(DSA_WIDTH), flat(W_G), _const_spec((2, D_MODEL)),
                  _const_spec((MLA_WIDTH, D_MODEL)), _const_spec((DSA_WIDTH, D_MODEL)),
                  _const_spec((D_MODEL, D_MODEL))],
        out_specs=flat(D_MODEL),
        out_shape=jax.ShapeDtypeStruct((n_tok, D_MODEL), f32),
        compiler_params=pltpu.CompilerParams(dimension_semantics=("arbitrary",),
                                             vmem_limit_bytes=VMEM_LIMIT),
        name="merge",
    )(x.reshape(n_tok, D_MODEL), o_a.reshape(n_tok, MLA_WIDTH), o_b.reshape(n_tok, DSA_WIDTH),
      gates.reshape(n_tok, W_G), b_merge, w_branch_mla.astype(bf16), wb_dsa, w_out.astype(bf16))
    return out.reshape(bsz, s_len, D_MODEL)


def kernel(x, positions, norm_gain, w_in, b_merge, mla_q_norm, mla_w_uq, mla_kv_norm, mla_w_ukv,
           mla_q_gain, mla_k_gain, dsa_q_gain, dsa_k_gain, w_branch_mla, w_branch_dsa, w_out):
    invf, spread = _rope_inputs()
    for l in range(norm_gain.shape[0]):
        x = _layer(x, positions, invf, spread, norm_gain[l], w_in[l], b_merge[l], mla_q_norm[l], mla_w_uq[l],
                   mla_kv_norm[l], mla_w_ukv[l], mla_q_gain[l], mla_k_gain[l], dsa_q_gain[l], dsa_k_gain[l],
                   w_branch_mla[l], w_branch_dsa[l], w_out[l])
    return x
```

```python
import functools

import numpy as np
import jax
import jax.numpy as jnp
from jax import lax
from jax.experimental import pallas as pl
from jax.experimental.pallas import tpu as pltpu

D_MODEL = 1024
MLA_HEADS = 8
MLA_Q_RANK = 256
MLA_KV_RANK = 128
MLA_NOPE = 64
MLA_ROPE = 32
MLA_QK = MLA_NOPE + MLA_ROPE
MLA_V = 64
MLA_WIDTH = MLA_HEADS * MLA_V
DSA_HEADS = 8
DSA_KV_HEADS = 2
DSA_GROUP = DSA_HEADS // DSA_KV_HEADS
DSA_DIM = 64
DSA_WIDTH = DSA_HEADS * DSA_DIM
DSA_ROT = DSA_DIM // 4
IDX_HEADS = 8
IDX_DIM = 32
IDX_ROT = IDX_DIM // 4
TOPK_MAX = 256
ROPE_THETA = 500000.0
NORM_EPS = 1e-6

IN_SPLITS = (
    MLA_Q_RANK, MLA_KV_RANK, MLA_ROPE, MLA_WIDTH,
    DSA_HEADS * DSA_DIM, DSA_KV_HEADS * DSA_DIM, DSA_KV_HEADS * DSA_DIM, DSA_WIDTH,
    IDX_HEADS * IDX_DIM, IDX_DIM, IDX_HEADS,
    D_MODEL, D_MODEL,
)

LANES = 128
MXU_WIDTH = 256
VMEM_LIMIT = 56 * 1024 * 1024
INT_MIN = -(2 ** 31)
F32_LOWEST = float(np.finfo(np.float32).min)
BF16_ROWS = 16
FINE_BITS = 18
NEG_BIG = -1e30
LOG2_E = 1.4426950408889634
FAST_MAX_SHIFT = 60.0
BOUND_MARGIN = 1.02

OFF_A = 0
W_A = MLA_Q_RANK + MLA_KV_RANK + LANES
OFF_B = OFF_A + W_A
W_B = DSA_WIDTH + 2 * DSA_KV_HEADS * DSA_DIM
OFF_I = OFF_B + W_B
W_I = IDX_HEADS * IDX_DIM + 2 * LANES
OFF_G = OFF_I + W_I
W_G = MLA_WIDTH + DSA_WIDTH + 2 * D_MODEL
W_ALL = OFF_G + W_G
GATE_CHUNK = 512
N_GATE_CHUNKS = W_G // GATE_CHUNK

DSA_HEAD_ORDER = tuple(h for j in range(DSA_GROUP) for h in (j, DSA_GROUP + j))

ROPE_LAYOUTS = ((LANES, MLA_NOPE, MLA_ROPE, 0), (DSA_DIM, 0, DSA_ROT, MLA_ROPE // 2),
                (IDX_DIM, 0, IDX_ROT, MLA_ROPE // 2 + DSA_ROT // 2))
ROW_A, ROW_D, ROW_I = 0, 1, 2
FREQ_ROWS = 32
ONE_ROW = FREQ_ROWS - 1


def _rms(v, width):
    return lax.rsqrt(jnp.sum(v * v, axis=-1, keepdims=True) * (1.0 / width) + NORM_EPS)


def _rope(v, cos, sin_lo, sin_hi, half):
    return (v * cos + pltpu.roll(v, LANES - half, 1) * sin_lo + pltpu.roll(v, half, 1) * sin_hi)


def _bf16_dot(a, b):
    return jnp.dot(a, b, preferred_element_type=jnp.float32)


def _nt_dot(a, b):
    return lax.dot_general(a, b, (((1,), (1,)), ((), ())), preferred_element_type=jnp.float32)


def _proj_kernel(x_ref, pos_ref, invf_ref, spread_ref, gx_ref, gcq_ref, gckv_ref, gq_ref, gk_ref, gdq_ref, gdk_ref,
                 w_ref, wuq_ref, wuk_ref, wuv_ref,
                 qa_ref, ka_ref, va_ref, qb_ref, kb_ref, vb_ref, qi_ref, ki_ref, wi_ref, g_ref):
    x = x_ref[0]
    h = (x * _rms(x, D_MODEL) * gx_ref[...]).astype(jnp.bfloat16)
    lane = lax.broadcasted_iota(jnp.int32, (1, LANES), 1)
    bf16 = jnp.bfloat16

    assert N_GATE_CHUNKS == 6

    def gate_logits(c):
        cols = slice(c * GATE_CHUNK, (c + 1) * GATE_CHUNK)
        g_ref[0, :, cols] = _bf16_dot(h, w_ref[:, OFF_G + c * GATE_CHUNK:OFF_G + (c + 1) * GATE_CHUNK]).astype(g_ref.dtype)

    pa = _bf16_dot(h, w_ref[:, OFF_A:OFF_A + W_A])
    pb = _bf16_dot(h, w_ref[:, OFF_B:OFF_B + W_B])
    pi = _bf16_dot(h, w_ref[:, OFF_I:OFF_I + W_I])

    ang = invf_ref[...] * pos_ref[0].astype(jnp.float32)
    trig = jnp.concatenate([jnp.cos(ang), jnp.sin(ang)], axis=0)
    t1 = trig.astype(bf16)
    r1 = trig - t1.astype(jnp.float32)
    t2 = r1.astype(bf16)
    t3 = (r1 - t2.astype(jnp.float32)).astype(bf16)
    tabs = lax.dot_general(jnp.concatenate([t1, t2, t3], axis=0), spread_ref[...], (((0,), (0,)), ((), ())),
                           preferred_element_type=jnp.float32)

    def tables(n):
        return tuple(tabs[:, (3 * n + c) * LANES:(3 * n + c + 1) * LANES] for c in range(3))

    gate_logits(0)
    gate_logits(1)

    c_q = pa[:, :MLA_Q_RANK]
    c_kv = pa[:, MLA_Q_RANK:MLA_Q_RANK + MLA_KV_RANK]
    kpe = pa[:, MLA_Q_RANK + MLA_KV_RANK:]
    cqn = (c_q * _rms(c_q, MLA_Q_RANK) * gcq_ref[...]).astype(bf16)
    ckvn = (c_kv * _rms(c_kv, MLA_KV_RANK) * gckv_ref[...]).astype(bf16)
    q_up = _bf16_dot(cqn, wuq_ref[...])
    k_up = _bf16_dot(ckvn, wuk_ref[...])
    v_up = _bf16_dot(ckvn, wuv_ref[...])
    gate_logits(2)

    cos_d, sl_d, sh_d = tables(ROW_D)
    low = lane < DSA_DIM

    def dsa_norm_rope(blk, gain):
        sq = blk * blk
        r_lo = lax.rsqrt(jnp.sum(jnp.where(low, sq, 0.0), axis=-1, keepdims=True) * (1.0 / DSA_DIM) + NORM_EPS)
        r_hi = lax.rsqrt(jnp.sum(jnp.where(low, 0.0, sq), axis=-1, keepdims=True) * (1.0 / DSA_DIM) + NORM_EPS)
        return _rope(blk * jnp.where(low, r_lo, r_hi) * gain, cos_d, sl_d, sh_d, DSA_ROT // 2)

    dsa_scale = DSA_DIM ** -0.5 * LOG2_E
    for j in range(DSA_WIDTH // LANES):
        blk = pb[:, j * LANES:(j + 1) * LANES]
        qn = (dsa_norm_rope(blk, gdq_ref[...]) * dsa_scale).astype(bf16)
        qb_ref[0, j] = jnp.where(low, qn, jnp.zeros_like(qn))
        qb_ref[0, DSA_GROUP + j] = jnp.where(low, jnp.zeros_like(qn), qn)
    kb_ref[0] = dsa_norm_rope(pb[:, DSA_WIDTH:DSA_WIDTH + LANES], gdk_ref[...]).astype(bf16)
    vb_ref[0, 0] = pb[:, DSA_WIDTH + LANES:].T.astype(bf16)
    gate_logits(3)

    cos_i, sl_i, sh_i = tables(ROW_I)
    n_qi = IDX_HEADS * IDX_DIM // LANES
    for j in range(n_qi + 1):
        blk = _rope(pi[:, j * LANES:(j + 1) * LANES], cos_i, sl_i, sh_i, IDX_ROT // 2).astype(bf16)
        if j < n_qi:
            qi_ref[0, :, j * LANES:(j + 1) * LANES] = blk
        else:
            ki_ref[0] = blk
    wi_t = (pi[:, (n_qi + 1) * LANES:] * (IDX_HEADS ** -0.5 * IDX_DIM ** -0.5)).T
    wi_ref[0] = wi_t[0:IDX_HEADS, :]
    gate_logits(4)

    for c in range(MLA_WIDTH // LANES):
        va_ref[0, 0, c * LANES:(c + 1) * LANES, :] = v_up[:, c * LANES:(c + 1) * LANES].T.astype(bf16)
    cos_a, sl_a, sh_a = tables(ROW_A)
    q_scale = MLA_QK ** -0.5 * LOG2_E
    kpe_g = kpe * gk_ref[...]
    k_rot = (pltpu.roll(kpe_g, LANES - MLA_ROPE // 2, 1) * sl_a + pltpu.roll(kpe_g, MLA_ROPE // 2, 1) * sh_a)
    for hd in range(MLA_HEADS):
        qh = q_up[:, hd * LANES:(hd + 1) * LANES]
        qn = qh * _rms(qh, MLA_QK) * gq_ref[...]
        qa_ref[0, hd] = (_rope(qn, cos_a, sl_a, sh_a, MLA_ROPE // 2) * q_scale).astype(bf16)
        kh = k_up[:, hd * LANES:(hd + 1) * LANES] + kpe
        ka_ref[0, hd] = ((kh * gk_ref[...] * cos_a + k_rot) * _rms(kh, MLA_QK)).astype(bf16)
    gate_logits(5)


def _softmax_tile(s, m, l, shift):
    if shift is None:
        m_new = jnp.maximum(m, jnp.max(s, axis=0, keepdims=True))
        alpha = jnp.exp2(m - m_new)
        p = jnp.exp2(s - m_new)
        return m_new, alpha * l + jnp.sum(p, axis=0, keepdims=True), alpha, p.astype(jnp.bfloat16)
    p = jnp.exp2(s - shift)
    return m, l + jnp.sum(p, axis=0, keepdims=True), None, p.astype(jnp.bfloat16)


def _by_shift(shift_ref, body):
    shift = shift_ref[0]

    @pl.when(shift <= FAST_MAX_SHIFT)
    def _():
        body(shift)

    @pl.when(shift > FAST_MAX_SHIFT)
    def _():
        body(None)


def _mla_kernel(shift_ref, q_ref, k_ref, vt_ref, o_ref, *, tq, tk, heads):
    i = pl.program_id(2)
    q_start = i * tq
    n_full = q_start // tk
    n_all = (q_start + tq + tk - 1) // tk
    tv = vt_ref.shape[-1]
    kpos0 = lax.broadcasted_iota(jnp.int32, (tk, tq), 0)
    qpos = q_start + lax.broadcasted_iota(jnp.int32, (tk, tq), 1)

    def attend(shift):
        def step(j, carry, masked):
            off = pl.multiple_of(j * tk, tk)
            scores = [_nt_dot(k_ref[0, hd, pl.ds(off, tk), :], q_ref[0, hd]) for hd in range(heads)]
            out = []
            for hd in range(heads):
                m, l, acc = carry[hd]
                s = scores[hd]
                if masked:
                    s = jnp.where(kpos0 + off <= qpos, s, NEG_BIG)
                m, l, alpha, pb = _softmax_tile(s, m, l, shift)
                if alpha is not None:
                    acc = alpha * acc
                for c in range(tk // tv):
                    vt = vt_ref[0, j * (tk // tv) + c, hd * MLA_V:(hd + 1) * MLA_V, :]
                    acc = acc + _bf16_dot(vt, pb[c * tv:(c + 1) * tv])
                out.append((m, l, acc))
            return tuple(out)

        init = tuple((jnp.full((1, tq), NEG_BIG, jnp.float32), jnp.zeros((1, tq), jnp.float32),
                      jnp.zeros((MLA_V, tq), jnp.float32)) for _ in range(heads))
        carry = lax.fori_loop(0, n_full, functools.partial(step, masked=False), init)
        carry = lax.fori_loop(n_full, n_all, functools.partial(step, masked=True), carry)
        outs = [acc / l for (_, l, acc) in carry]
        for c in range(heads // 2):
            o_ref[0, :, c * LANES:(c + 1) * LANES] = jnp.concatenate(outs[2 * c:2 * c + 2], axis=0).T

    _by_shift(shift_ref, attend)


def _dsa_kernel(shift_ref, qb_ref, kb_ref, vt_ref, qi_ref, ki_ref, wt_ref, o_ref,
                sc_ref, sb_ref, qim_ref, *, tq, tk, k_top):
    i = pl.program_id(1)
    q_start = i * tq
    n_kv = (q_start + tq + tk - 1) // tk
    n_heads = IDX_HEADS
    tv = vt_ref.shape[-1]

    lane_q = lax.broadcasted_iota(jnp.int32, (tq, LANES), 1)
    per_blk = LANES // IDX_DIM
    for hd in range(IDX_HEADS):
        blk = qi_ref[0, :, (hd // per_blk) * LANES:(hd // per_blk + 1) * LANES]
        slot = hd % per_blk
        in_head = (lane_q >= slot * IDX_DIM) & (lane_q < (slot + 1) * IDX_DIM)
        qim_ref[hd * tq:(hd + 1) * tq, :] = jnp.where(in_head, blk, jnp.zeros_like(blk))

    kpos0 = lax.broadcasted_iota(jnp.int32, (tk, tq), 0)
    qpos = q_start + lax.broadcasted_iota(jnp.int32, (tk, tq), 1)
    w_rows = [wt_ref[0, hd:hd + 1, :] for hd in range(IDX_HEADS)]

    def score_tile(j, _):
        off = pl.multiple_of(j * tk, tk)
        s_all = _nt_dot(ki_ref[0, pl.ds(off, tk), :], qim_ref[...])
        sc = w_rows[0] * jnp.maximum(s_all[:, 0:tq], 0.0)
        for hd in range(1, n_heads):
            sc = sc + w_rows[hd] * jnp.maximum(s_all[:, hd * tq:(hd + 1) * tq], 0.0)
        sc = jnp.where(kpos0 + off <= qpos, sc, -jnp.inf)
        sc_ref[j] = sc
        sb_ref[j] = sc.astype(jnp.bfloat16)
        return 0

    lax.fori_loop(0, n_kv, score_tile, 0)

    def pattern_value(u):
        key = u ^ INT_MIN
        return lax.bitcast_convert_type(key ^ ((key >> 31) & 0x7FFFFFFF), jnp.float32)

    def count(pred):
        def body(j, acc):
            hit = jnp.where(pred(sc_ref[j], j), 1, 0)
            return acc + jnp.sum(hit.reshape(tk // 8, 8, tq), axis=0)
        acc = lax.fori_loop(0, n_kv, body, jnp.zeros((8, tq), jnp.int32))
        return jnp.sum(acc, axis=0, keepdims=True)

    def count_coarse(cand_b):
        one, zero = jnp.ones((), jnp.bfloat16), jnp.zeros((), jnp.bfloat16)

        def body(j, acc):
            hit = jnp.where(sb_ref[j] >= cand_b, one, zero)
            rows = [hit[r * BF16_ROWS:(r + 1) * BF16_ROWS] for r in range(tk // BF16_ROWS)]
            while len(rows) > 1:
                rows = [rows[r] + rows[r + 1] for r in range(0, len(rows), 2)]
            return acc + rows[0].astype(jnp.float32)
        acc = lax.fori_loop(0, n_kv, body, jnp.zeros((BF16_ROWS, tq), jnp.float32))
        return jnp.sum(acc, axis=0, keepdims=True)

    def coarse_bit(it, t16):
        cand = t16 | jnp.left_shift(jnp.int32(1), 15 - it)
        cand32 = jnp.left_shift(cand, 16) | jnp.where(cand < 0x8000, 0xFFFF, 0)
        cnt = count_coarse(pattern_value(cand32).astype(jnp.bfloat16))
        return jnp.where(cnt >= k_top, cand, t16)

    t16 = lax.fori_loop(0, 16, coarse_bit, jnp.zeros((1, tq), jnp.int32))
    few = t16 < 0x0080
    base = jnp.where(few, 0x00800000, jnp.left_shift(t16, 16) - 0x8000)

    def fine_bit(it, carry):
        t, cnt_t = carry
        cand = t | jnp.left_shift(jnp.int32(1), FINE_BITS - 1 - it)
        cand_f = pattern_value(base + cand)
        cnt = count(lambda sc, j: sc >= cand_f)
        take = cnt >= k_top
        return jnp.where(take, cand, t), jnp.where(take, cnt, cnt_t)

    t, cnt_ge = lax.fori_loop(0, FINE_BITS, fine_bit,
                              (jnp.zeros((1, tq), jnp.int32), jnp.full((1, tq), k_top, jnp.int32)))
    thr = jnp.where(few, F32_LOWEST, pattern_value(base + t))
    tied = jnp.logical_not(few) & ((cnt_ge > k_top) | (t == 0))

    def tie_cut():
        cnt_gt = count(lambda sc, j: sc > thr)
        need = k_top - cnt_gt

        def index_bit(it, c):
            cand = c | jnp.left_shift(jnp.int32(1), 12 - it)
            below = count(lambda sc, j: (sc == thr) & (kpos0 + j * tk < cand))
            return jnp.where(below < need, cand, c)

        c = lax.fori_loop(0, 13, index_bit, jnp.zeros((1, tq), jnp.int32))
        return jnp.where(tied, c, jnp.int32(2 ** 30))

    cut = lax.cond(jnp.max(tied.astype(jnp.int32)) > 0, tie_cut,
                   lambda: jnp.full((1, tq), 2 ** 30, jnp.int32))

    hpd = max(1, MXU_WIDTH // tq)
    n_dots = n_heads // hpd
    wd = hpd * tq

    def attend_all(shift):
        def attend(j, carry):
            off = pl.multiple_of(j * tk, tk)
            kt = kb_ref[0, pl.ds(off, tk), :]
            scores = [_nt_dot(kt, qb_ref[0, hpd * d:hpd * (d + 1)].reshape(wd, LANES)) for d in range(n_dots)]
            sc = sc_ref[j]
            sel = (sc > thr) | ((sc == thr) & (kpos0 + off <= cut))
            sel_d = jnp.concatenate([sel] * hpd, axis=1) if hpd > 1 else sel
            out = []
            for d in range(n_dots):
                m, l, acc = carry[d]
                m, l, alpha, pb = _softmax_tile(jnp.where(sel_d, scores[d], NEG_BIG), m, l, shift)
                if alpha is not None:
                    acc = alpha * acc
                for c in range(tk // tv):
                    acc = acc + _bf16_dot(vt_ref[0, j * (tk // tv) + c], pb[c * tv:(c + 1) * tv])
                out.append((m, l, acc))
            return tuple(out)

        init = tuple((jnp.full((1, wd), NEG_BIG, jnp.float32), jnp.zeros((1, wd), jnp.float32),
                      jnp.zeros((LANES, wd), jnp.float32)) for _ in range(n_dots))
        carry = lax.fori_loop(0, n_kv, attend, init)
        o = jnp.concatenate([acc / l for (_, l, acc) in carry], axis=1)
        for j in range(DSA_GROUP):
            lo = o[:, j * tq:(j + 1) * tq].T
            hi = o[:, (DSA_GROUP + j) * tq:(DSA_GROUP + j + 1) * tq].T
            o_ref[0, :, j * LANES:(j + 1) * LANES] = jnp.where(lane_q < DSA_DIM, lo, hi)

    _by_shift(shift_ref, attend_all)


def _merge_kernel(x_ref, oa_ref, ob_ref, g_ref, bm_ref, wa_ref, wb_ref, wo_ref, out_ref):
    f32 = jnp.float32
    ga = g_ref[:, 0:MLA_WIDTH].astype(f32)
    gb = g_ref[:, MLA_WIDTH:MLA_WIDTH + DSA_WIDTH].astype(f32)
    ma = g_ref[:, MLA_WIDTH + DSA_WIDTH:MLA_WIDTH + DSA_WIDTH + D_MODEL].astype(f32)
    mb = g_ref[:, MLA_WIDTH + DSA_WIDTH + D_MODEL:].astype(f32)
    ya = _bf16_dot((oa_ref[...] * (ga * jax.nn.sigmoid(ga))).astype(jnp.bfloat16), wa_ref[...])
    yb = _bf16_dot((ob_ref[...] * (gb * jax.nn.sigmoid(gb))).astype(jnp.bfloat16), wb_ref[...])
    merged = jax.nn.sigmoid(ma + bm_ref[0:1, :]) * ya + jax.nn.sigmoid(mb + bm_ref[1:2, :]) * yb
    out_ref[...] = x_ref[...] + _bf16_dot(merged.astype(jnp.bfloat16), wo_ref[...])


def _inv_freq(rot_dim):
    return ROPE_THETA ** (-jnp.arange(0, rot_dim, 2, dtype=jnp.float32) / rot_dim)


def _rope_inputs():
    invf = jnp.concatenate([_inv_freq(rot) for (_, _, rot, _) in ROPE_LAYOUTS])
    invf = jnp.pad(invf, (0, FREQ_ROWS - invf.shape[0]))[:, None]
    spread = np.zeros((2 * FREQ_ROWS, 3 * len(ROPE_LAYOUTS) * LANES), np.float32)
    for n, (period, start, rot, row0) in enumerate(ROPE_LAYOUTS):
        half = rot // 2
        for lane in range(LANES):
            p = lane % period - start
            cos_col, lo_col, hi_col = ((3 * n + c) * LANES + lane for c in range(3))
            if 0 <= p < rot:
                spread[row0 + p % half, cos_col] = 1.0
                spread[FREQ_ROWS + row0 + p % half, lo_col if p < half else hi_col] = -1.0 if p < half else 1.0
            else:
                spread[ONE_ROW, cos_col] = 1.0
    return invf, jnp.asarray(np.tile(spread, (3, 1)), jnp.bfloat16)


def _pad_lanes(a, start=0):
    return jnp.pad(a, ((0, 0), (start, LANES - start - a.shape[1])))


def _layout_params(w_in, mla_w_uq, mla_w_ukv, mla_q_gain, mla_k_gain, dsa_q_gain, dsa_k_gain,
                   w_branch_dsa):
    offs = np.concatenate([[0], np.cumsum(IN_SPLITS)])
    (c_q, c_kv, k_pe, gate_a, q_b, k_b, v_b, gate_b, q_i, k_i, w_i, m_a, m_b) = [
        w_in[:, offs[n]:offs[n + 1]] for n in range(len(IN_SPLITS))]

    def pair_heads(w):
        rows = w.shape[0]
        w = w.reshape(rows, DSA_KV_HEADS, DSA_GROUP, DSA_DIM)
        return jnp.transpose(w, (0, 2, 1, 3)).reshape(rows, DSA_WIDTH)

    w_all = jnp.concatenate([
        c_q, c_kv, _pad_lanes(k_pe, MLA_NOPE),
        pair_heads(q_b), k_b, v_b,
        q_i, jnp.tile(k_i, (1, LANES // IDX_DIM)), _pad_lanes(w_i),
        gate_a, pair_heads(gate_b), m_a, m_b], axis=1).astype(jnp.bfloat16)
    uq = mla_w_uq.reshape(MLA_Q_RANK, MLA_HEADS, MLA_QK)
    wuq = jnp.pad(uq, ((0, 0), (0, 0), (0, LANES - MLA_QK))).reshape(MLA_Q_RANK, MLA_HEADS * LANES)
    ukv = mla_w_ukv.reshape(MLA_KV_RANK, MLA_HEADS, MLA_NOPE + MLA_V)
    wuk = jnp.pad(ukv[:, :, :MLA_NOPE], ((0, 0), (0, 0), (0, LANES - MLA_NOPE))).reshape(MLA_KV_RANK, MLA_HEADS * LANES)
    wuv = ukv[:, :, MLA_NOPE:].reshape(MLA_KV_RANK, MLA_WIDTH)
    gq = _pad_lanes(mla_q_gain[None, :])
    gk = _pad_lanes(mla_k_gain[None, :])
    gdq = jnp.tile(dsa_q_gain[None, :], (1, LANES // DSA_DIM))
    gdk = jnp.tile(dsa_k_gain[None, :], (1, LANES // DSA_DIM))
    wb_dsa = jnp.transpose(w_branch_dsa.reshape(DSA_KV_HEADS, DSA_GROUP, DSA_DIM, D_MODEL),
                           (1, 0, 2, 3)).reshape(DSA_WIDTH, D_MODEL)
    return (w_all, wuq.astype(jnp.bfloat16), wuk.astype(jnp.bfloat16), wuv.astype(jnp.bfloat16),
            gq, gk, gdq, gdk, wb_dsa.astype(jnp.bfloat16))


def _const_spec(shape):
    return pl.BlockSpec(shape, lambda *_: (0,) * len(shape))


def _layer(x, positions, invf, spread, norm_gain, w_in, b_merge, mla_q_norm, mla_w_uq, mla_kv_norm, mla_w_ukv,
           mla_q_gain, mla_k_gain, dsa_q_gain, dsa_k_gain, w_branch_mla, w_branch_dsa, w_out):
    bsz, s_len, _ = x.shape
    k_top = min(TOPK_MAX, s_len // 4)
    (w_all, wuq, wuk, wuv, gq, gk, gdq, gdk, wb_dsa) = _layout_params(
        w_in, mla_w_uq, mla_w_ukv, mla_q_gain, mla_k_gain, dsa_q_gain, dsa_k_gain, w_branch_dsa)
    bf16, f32 = jnp.bfloat16, jnp.float32
    gmax = lambda g: jnp.max(jnp.abs(g))
    shift_a = (BOUND_MARGIN * MLA_QK ** 0.5 * LOG2_E * gmax(mla_q_gain) * gmax(mla_k_gain)).reshape(1).astype(f32)
    shift_b = (BOUND_MARGIN * DSA_DIM ** 0.5 * LOG2_E * gmax(dsa_q_gain) * gmax(dsa_k_gain)).reshape(1).astype(f32)
    smem = pl.BlockSpec(memory_space=pltpu.SMEM)

    tm = 256
    tok = lambda w: pl.BlockSpec((1, tm, w), lambda b, i: (b, i, 0))
    head = pl.BlockSpec((1, MLA_HEADS, tm, LANES), lambda b, i: (b, 0, i, 0))
    outs = pl.pallas_call(
        _proj_kernel,
        grid=(bsz, s_len // tm),
        in_specs=[tok(D_MODEL), pl.BlockSpec((1, 1, tm), lambda b, i: (b, 0, i)),
                  _const_spec(invf.shape), _const_spec(spread.shape), _const_spec((1, D_MODEL)),
                  _const_spec((1, MLA_Q_RANK)), _const_spec((1, MLA_KV_RANK)),
                  _const_spec((1, LANES)), _const_spec((1, LANES)), _const_spec((1, LANES)), _const_spec((1, LANES)),
                  _const_spec((D_MODEL, W_ALL)), _const_spec((MLA_Q_RANK, MLA_HEADS * LANES)),
                  _const_spec((MLA_KV_RANK, MLA_HEADS * LANES)), _const_spec((MLA_KV_RANK, MLA_WIDTH))],
        out_specs=[head, head, pl.BlockSpec((1, 1, MLA_WIDTH, tm), lambda b, i: (b, i, 0, 0)),
                   head, tok(LANES),
                   pl.BlockSpec((1, 1, LANES, tm), lambda b, i: (b, i, 0, 0)),
                   tok(IDX_HEADS * IDX_DIM), tok(LANES),
                   pl.BlockSpec((1, IDX_HEADS, tm), lambda b, i: (b, 0, i)), tok(W_G)],
        out_shape=[jax.ShapeDtypeStruct((bsz, MLA_HEADS, s_len, LANES), bf16),
                   jax.ShapeDtypeStruct((bsz, MLA_HEADS, s_len, LANES), bf16),
                   jax.ShapeDtypeStruct((bsz, s_len // tm, MLA_WIDTH, tm), bf16),
                   jax.ShapeDtypeStruct((bsz, DSA_HEADS, s_len, LANES), bf16),
                   jax.ShapeDtypeStruct((bsz, s_len, LANES), bf16),
                   jax.ShapeDtypeStruct((bsz, s_len // tm, LANES, tm), bf16),
                   jax.ShapeDtypeStruct((bsz, s_len, IDX_HEADS * IDX_DIM), bf16),
                   jax.ShapeDtypeStruct((bsz, s_len, LANES), bf16),
                   jax.ShapeDtypeStruct((bsz, IDX_HEADS, s_len), f32),
                   jax.ShapeDtypeStruct((bsz, s_len, W_G), bf16)],
        compiler_params=pltpu.CompilerParams(dimension_semantics=("arbitrary", "arbitrary"),
                                             vmem_limit_bytes=VMEM_LIMIT),
        name="proj",
    )(x, positions[:, None, :], invf, spread, norm_gain[None, :], mla_q_norm[None, :], mla_kv_norm[None, :],
      gq, gk, gdq, gdk, w_all, wuq, wuk, wuv)
    qa, ka, vat, qb, kb, vbt, qi, ki, wit, gates = outs

    tq_a, tk_a, hps = 512, 512, 8
    o_a = pl.pallas_call(
        functools.partial(_mla_kernel, tq=tq_a, tk=tk_a, heads=hps),
        grid=(bsz, MLA_HEADS // hps, s_len // tq_a),
        in_specs=[smem, pl.BlockSpec((1, hps, tq_a, LANES), lambda b, hp, i: (b, hp, i, 0)),
                  pl.BlockSpec((1, hps, s_len, LANES), lambda b, hp, i: (b, hp, 0, 0)),
                  pl.BlockSpec((1, s_len // tm, hps * MLA_V, tm), lambda b, hp, i: (b, 0, hp, 0))],
        out_specs=pl.BlockSpec((1, tq_a, hps * MLA_V), lambda b, hp, i: (b, i, hp)),
        out_shape=jax.ShapeDtypeStruct((bsz, s_len, MLA_WIDTH), f32),
        compiler_params=pltpu.CompilerParams(dimension_semantics=("arbitrary",) * 3,
                                             vmem_limit_bytes=VMEM_LIMIT),
        name="mla",
    )(shift_a, qa, ka, vat)

    tq_b, tk_b = 512, 512
    row_b = lambda w: pl.BlockSpec((1, tq_b, w), lambda b, i: (b, i, 0))
    full_b = lambda w: pl.BlockSpec((1, s_len, w), lambda b, i: (b, 0, 0))
    o_b = pl.pallas_call(
        functools.partial(_dsa_kernel, tq=tq_b, tk=tk_b, k_top=k_top),
        grid=(bsz, s_len // tq_b),
        in_specs=[smem, pl.BlockSpec((1, DSA_HEADS, tq_b, LANES), lambda b, i: (b, 0, i, 0)), full_b(LANES),
                  pl.BlockSpec((1, s_len // tm, LANES, tm), lambda b, i: (b, 0, 0, 0)),
                  row_b(IDX_HEADS * IDX_DIM), full_b(LANES),
                  pl.BlockSpec((1, IDX_HEADS, tq_b), lambda b, i: (b, 0, i))],
        out_specs=row_b(DSA_WIDTH),
        out_shape=jax.ShapeDtypeStruct((bsz, s_len, DSA_WIDTH), f32),
        scratch_shapes=[pltpu.VMEM((s_len // tk_b, tk_b, tq_b), f32),
                        pltpu.VMEM((s_len // tk_b, tk_b, tq_b), bf16),
                        pltpu.VMEM((IDX_HEADS * tq_b, LANES), bf16)],
        compiler_params=pltpu.CompilerParams(dimension_semantics=("arbitrary", "arbitrary"),
                                             vmem_limit_bytes=VMEM_LIMIT),
        name="dsa",
    )(shift_b, qb, kb, vbt, qi, ki, wit)

    tm_m = 512
    n_tok = bsz * s_len
    flat = lambda w: pl.BlockSpec((tm_m, w), lambda i: (i, 0))
    out = pl.pallas_call(
        _merge_kernel,
        grid=(n_tok // tm_m,),
        in_specs=[flat(D_MODEL), flat(MLA_WIDTH), flat(DSA_WIDTH), flat(W_G), _const_spec((2, D_MODEL)),
                  _const_spec((MLA_WIDTH, D_MODEL)), _const_spec((DSA_WIDTH, D_MODEL)),
                  _const_spec((D_MODEL, D_MODEL))],
        out_specs=flat(D_MODEL),
        out_shape=jax.ShapeDtypeStruct((n_tok, D_MODEL), f32),
        compiler_params=pltpu.CompilerParams(dimension_semantics=("arbitrary",),
                                             vmem_limit_bytes=VMEM_LIMIT),
        name="merge",
    )(x.reshape(n_tok, D_MODEL), o_a.reshape(n_tok, MLA_WIDTH), o_b.reshape(n_tok, DSA_WIDTH),
      gates.reshape(n_tok, W_G), b_merge, w_branch_mla.astype(bf16), wb_dsa, w_out.astype(bf16))
    return out.reshape(bsz, s_len, D_MODEL)


def kernel(x, positions, norm_gain, w_in, b_merge, mla_q_norm, mla_w_uq, mla_kv_norm, mla_w_ukv,
           mla_q_gain, mla_k_gain, dsa_q_gain, dsa_k_gain, w_branch_mla, w_branch_dsa, w_out):
    invf, spread = _rope_inputs()
    for l in range(norm_gain.shape[0]):
        x = _layer(x, positions, invf, spread, norm_gain[l], w_in[l], b_merge[l], mla_q_norm[l], mla_w_uq[l],
                   mla_kv_norm[l], mla_w_ukv[l], mla_q_gain[l], mla_k_gain[l], dsa_q_gain[l], dsa_k_gain[l],
                   w_branch_mla[l], w_branch_dsa[l], w_out[l])
    return x
```

```python
import functools

import numpy as np
import jax
import jax.numpy as jnp
from jax import lax
from jax.experimental import pallas as pl
from jax.experimental.pallas import tpu as pltpu

D_MODEL = 1024
MLA_HEADS = 8
MLA_Q_RANK = 256
MLA_KV_RANK = 128
MLA_NOPE = 64
MLA_ROPE = 32
MLA_QK = MLA_NOPE + MLA_ROPE
MLA_V = 64
MLA_WIDTH = MLA_HEADS * MLA_V
DSA_HEADS = 8
DSA_KV_HEADS = 2
DSA_GROUP = DSA_HEADS // DSA_KV_HEADS
DSA_DIM = 64
DSA_WIDTH = DSA_HEADS * DSA_DIM
DSA_ROT = DSA_DIM // 4
IDX_HEADS = 8
IDX_DIM = 32
IDX_ROT = IDX_DIM // 4
TOPK_MAX = 256
ROPE_THETA = 500000.0
NORM_EPS = 1e-6

IN_SPLITS = (
    MLA_Q_RANK, MLA_KV_RANK, MLA_ROPE, MLA_WIDTH,
    DSA_HEADS * DSA_DIM, DSA_KV_HEADS * DSA_DIM, DSA_KV_HEADS * DSA_DIM, DSA_WIDTH,
    IDX_HEADS * IDX_DIM, IDX_DIM, IDX_HEADS,
    D_MODEL, D_MODEL,
)

LANES = 128
MXU_WIDTH = 256
VMEM_LIMIT = 56 * 1024 * 1024
INT_MIN = -(2 ** 31)
F32_LOWEST = float(np.finfo(np.float32).min)
BF16_ROWS = 16
FINE_BITS = 17
NEG_BIG = -1e30
LOG2_E = 1.4426950408889634
FAST_MAX_SHIFT = 60.0
BOUND_MARGIN = 1.02

OFF_A = 0
W_A = MLA_Q_RANK + MLA_KV_RANK + LANES
OFF_B = OFF_A + W_A
W_B = DSA_WIDTH + 2 * DSA_KV_HEADS * DSA_DIM
OFF_I = OFF_B + W_B
W_I = IDX_HEADS * IDX_DIM + 2 * LANES
OFF_G = OFF_I + W_I
W_G = MLA_WIDTH + DSA_WIDTH + 2 * D_MODEL
W_ALL = OFF_G + W_G
GATE_CHUNK = 512
N_GATE_CHUNKS = W_G // GATE_CHUNK

DSA_HEAD_ORDER = tuple(h for j in range(DSA_GROUP) for h in (j, DSA_GROUP + j))

ROPE_LAYOUTS = ((LANES, MLA_NOPE, MLA_ROPE, 0), (DSA_DIM, 0, DSA_ROT, MLA_ROPE // 2),
                (IDX_DIM, 0, IDX_ROT, MLA_ROPE // 2 + DSA_ROT // 2))
ROW_A, ROW_D, ROW_I = 0, 1, 2
FREQ_ROWS = 32
ONE_ROW = FREQ_ROWS - 1


def _rms(v, width):
    return lax.rsqrt(jnp.sum(v * v, axis=-1, keepdims=True) * (1.0 / width) + NORM_EPS)


def _rope(v, cos, sin_lo, sin_hi, half):
    return (v * cos + pltpu.roll(v, LANES - half, 1) * sin_lo + pltpu.roll(v, half, 1) * sin_hi)


def _bf16_dot(a, b):
    return jnp.dot(a, b, preferred_element_type=jnp.float32)


def _nt_dot(a, b):
    return lax.dot_general(a, b, (((1,), (1,)), ((), ())), preferred_element_type=jnp.float32)


def _proj_kernel(x_ref, pos_ref, invf_ref, spread_ref, gx_ref, gcq_ref, gckv_ref, gq_ref, gk_ref, gdq_ref, gdk_ref,
                 w_ref, wuq_ref, wuk_ref, wuv_ref,
                 qa_ref, ka_ref, va_ref, qb_ref, kb_ref, vb_ref, qi_ref, ki_ref, wi_ref, g_ref):
    x = x_ref[0]
    h = (x * _rms(x, D_MODEL) * gx_ref[...]).astype(jnp.bfloat16)
    lane = lax.broadcasted_iota(jnp.int32, (1, LANES), 1)
    bf16 = jnp.bfloat16

    assert N_GATE_CHUNKS == 6

    def gate_logits(c):
        cols = slice(c * GATE_CHUNK, (c + 1) * GATE_CHUNK)
        g_ref[0, :, cols] = _bf16_dot(h, w_ref[:, OFF_G + c * GATE_CHUNK:OFF_G + (c + 1) * GATE_CHUNK]).astype(g_ref.dtype)

    pa = _bf16_dot(h, w_ref[:, OFF_A:OFF_A + W_A])
    pb = _bf16_dot(h, w_ref[:, OFF_B:OFF_B + W_B])
    pi = _bf16_dot(h, w_ref[:, OFF_I:OFF_I + W_I])

    ang = invf_ref[...] * pos_ref[0].astype(jnp.float32)
    trig = jnp.concatenate([jnp.cos(ang), jnp.sin(ang)], axis=0)
    t1 = trig.astype(bf16)
    r1 = trig - t1.astype(jnp.float32)
    t2 = r1.astype(bf16)
    t3 = (r1 - t2.astype(jnp.float32)).astype(bf16)
    tabs = lax.dot_general(jnp.concatenate([t1, t2, t3], axis=0), spread_ref[...], (((0,), (0,)), ((), ())),
                           preferred_element_type=jnp.float32)

    def tables(n):
        return tuple(tabs[:, (3 * n + c) * LANES:(3 * n + c + 1) * LANES] for c in range(3))

    gate_logits(0)
    gate_logits(1)

    c_q = pa[:, :MLA_Q_RANK]
    c_kv = pa[:, MLA_Q_RANK:MLA_Q_RANK + MLA_KV_RANK]
    kpe = pa[:, MLA_Q_RANK + MLA_KV_RANK:]
    cqn = (c_q * _rms(c_q, MLA_Q_RANK) * gcq_ref[...]).astype(bf16)
    ckvn = (c_kv * _rms(c_kv, MLA_KV_RANK) * gckv_ref[...]).astype(bf16)
    q_up = _bf16_dot(cqn, wuq_ref[...])
    k_up = _bf16_dot(ckvn, wuk_ref[...])
    v_up = _bf16_dot(ckvn, wuv_ref[...])
    gate_logits(2)

    cos_d, sl_d, sh_d = tables(ROW_D)
    low = lane < DSA_DIM

    def dsa_norm_rope(blk, gain):
        sq = blk * blk
        r_lo = lax.rsqrt(jnp.sum(jnp.where(low, sq, 0.0), axis=-1, keepdims=True) * (1.0 / DSA_DIM) + NORM_EPS)
        r_hi = lax.rsqrt(jnp.sum(jnp.where(low, 0.0, sq), axis=-1, keepdims=True) * (1.0 / DSA_DIM) + NORM_EPS)
        return _rope(blk * jnp.where(low, r_lo, r_hi) * gain, cos_d, sl_d, sh_d, DSA_ROT // 2)

    dsa_scale = DSA_DIM ** -0.5 * LOG2_E
    for j in range(DSA_WIDTH // LANES):
        blk = pb[:, j * LANES:(j + 1) * LANES]
        qn = (dsa_norm_rope(blk, gdq_ref[...]) * dsa_scale).astype(bf16)
        qb_ref[0, j] = jnp.where(low, qn, jnp.zeros_like(qn))
        qb_ref[0, DSA_GROUP + j] = jnp.where(low, jnp.zeros_like(qn), qn)
    kb_ref[0] = dsa_norm_rope(pb[:, DSA_WIDTH:DSA_WIDTH + LANES], gdk_ref[...]).astype(bf16)
    vb_ref[0, 0] = pb[:, DSA_WIDTH + LANES:].T.astype(bf16)
    gate_logits(3)

    cos_i, sl_i, sh_i = tables(ROW_I)
    n_qi = IDX_HEADS * IDX_DIM // LANES
    for j in range(n_qi + 1):
        blk = _rope(pi[:, j * LANES:(j + 1) * LANES], cos_i, sl_i, sh_i, IDX_ROT // 2).astype(bf16)
        if j < n_qi:
            qi_ref[0, :, j * LANES:(j + 1) * LANES] = blk
        else:
            ki_ref[0] = blk
    wi_t = (pi[:, (n_qi + 1) * LANES:] * (IDX_HEADS ** -0.5 * IDX_DIM ** -0.5)).T
    wi_ref[0] = wi_t[0:IDX_HEADS, :]
    gate_logits(4)

    for c in range(MLA_WIDTH // LANES):
        va_ref[0, 0, c * LANES:(c + 1) * LANES, :] = v_up[:, c * LANES:(c + 1) * LANES].T.astype(bf16)
    cos_a, sl_a, sh_a = tables(ROW_A)
    q_scale = MLA_QK ** -0.5 * LOG2_E
    kpe_g = kpe * gk_ref[...]
    k_rot = (pltpu.roll(kpe_g, LANES - MLA_ROPE // 2, 1) * sl_a + pltpu.roll(kpe_g, MLA_ROPE // 2, 1) * sh_a)
    for hd in range(MLA_HEADS):
        qh = q_up[:, hd * LANES:(hd + 1) * LANES]
        qn = qh * _rms(qh, MLA_QK) * gq_ref[...]
        qa_ref[0, hd] = (_rope(qn, cos_a, sl_a, sh_a, MLA_ROPE // 2) * q_scale).astype(bf16)
        kh = k_up[:, hd * LANES:(hd + 1) * LANES] + kpe
        ka_ref[0, hd] = ((kh * gk_ref[...] * cos_a + k_rot) * _rms(kh, MLA_QK)).astype(bf16)
    gate_logits(5)


def _softmax_tile(s, m, l, shift):
    if shift is None:
        m_new = jnp.maximum(m, jnp.max(s, axis=0, keepdims=True))
        alpha = jnp.exp2(m - m_new)
        p = jnp.exp2(s - m_new)
        return m_new, alpha * l + jnp.sum(p, axis=0, keepdims=True), alpha, p.astype(jnp.bfloat16)
    p = jnp.exp2(s - shift)
    return m, l + jnp.sum(p, axis=0, keepdims=True), None, p.astype(jnp.bfloat16)


def _by_shift(shift_ref, body):
    shift = shift_ref[0]

    @pl.when(shift <= FAST_MAX_SHIFT)
    def _():
        body(shift)

    @pl.when(shift > FAST_MAX_SHIFT)
    def _():
        body(None)


def _mla_kernel(shift_ref, q_ref, k_ref, vt_ref, o_ref, *, tq, tk, heads):
    i = pl.program_id(2)
    q_start = i * tq
    n_full = q_start // tk
    n_all = (q_start + tq + tk - 1) // tk
    tv = vt_ref.shape[-1]
    kpos0 = lax.broadcasted_iota(jnp.int32, (tk, tq), 0)
    qpos = q_start + lax.broadcasted_iota(jnp.int32, (tk, tq), 1)

    def attend(shift):
        def step(j, carry, masked):
            off = pl.multiple_of(j * tk, tk)
            scores = [_nt_dot(k_ref[0, hd, pl.ds(off, tk), :], q_ref[0, hd]) for hd in range(heads)]
            out = []
            for hd in range(heads):
                m, l, acc = carry[hd]
                s = scores[hd]
                if masked:
                    s = jnp.where(kpos0 + off <= qpos, s, NEG_BIG)
                m, l, alpha, pb = _softmax_tile(s, m, l, shift)
                if alpha is not None:
                    acc = alpha * acc
                for c in range(tk // tv):
                    vt = vt_ref[0, j * (tk // tv) + c, hd * MLA_V:(hd + 1) * MLA_V, :]
                    acc = acc + _bf16_dot(vt, pb[c * tv:(c + 1) * tv])
                out.append((m, l, acc))
            return tuple(out)

        init = tuple((jnp.full((1, tq), NEG_BIG, jnp.float32), jnp.zeros((1, tq), jnp.float32),
                      jnp.zeros((MLA_V, tq), jnp.float32)) for _ in range(heads))
        carry = lax.fori_loop(0, n_full, functools.partial(step, masked=False), init)
        carry = lax.fori_loop(n_full, n_all, functools.partial(step, masked=True), carry)
        outs = [acc / l for (_, l, acc) in carry]
        for c in range(heads // 2):
            o_ref[0, :, c * LANES:(c + 1) * LANES] = jnp.concatenate(outs[2 * c:2 * c + 2], axis=0).T

    _by_shift(shift_ref, attend)


def _dsa_kernel(shift_ref, qb_ref, kb_ref, vt_ref, qi_ref, ki_ref, wt_ref, o_ref,
                sc_ref, sb_ref, qim_ref, *, tq, tk, k_top):
    i = pl.program_id(1)
    q_start = i * tq
    n_kv = (q_start + tq + tk - 1) // tk
    n_heads = IDX_HEADS
    tv = vt_ref.shape[-1]

    lane_q = lax.broadcasted_iota(jnp.int32, (tq, LANES), 1)
    per_blk = LANES // IDX_DIM
    for hd in range(IDX_HEADS):
        blk = qi_ref[0, :, (hd // per_blk) * LANES:(hd // per_blk + 1) * LANES]
        slot = hd % per_blk
        in_head = (lane_q >= slot * IDX_DIM) & (lane_q < (slot + 1) * IDX_DIM)
        qim_ref[hd * tq:(hd + 1) * tq, :] = jnp.where(in_head, blk, jnp.zeros_like(blk))

    kpos0 = lax.broadcasted_iota(jnp.int32, (tk, tq), 0)
    qpos = q_start + lax.broadcasted_iota(jnp.int32, (tk, tq), 1)
    w_rows = [wt_ref[0, hd:hd + 1, :] for hd in range(IDX_HEADS)]

    def score_tile(j, _):
        off = pl.multiple_of(j * tk, tk)
        s_all = _nt_dot(ki_ref[0, pl.ds(off, tk), :], qim_ref[...])
        sc = w_rows[0] * jnp.maximum(s_all[:, 0:tq], 0.0)
        for hd in range(1, n_heads):
            sc = sc + w_rows[hd] * jnp.maximum(s_all[:, hd * tq:(hd + 1) * tq], 0.0)
        sc = jnp.where(kpos0 + off <= qpos, sc, -jnp.inf)
        sc_ref[j] = sc
        sb_ref[j] = sc.astype(jnp.bfloat16)
        return 0

    lax.fori_loop(0, n_kv, score_tile, 0)

    def pattern_value(u):
        key = u ^ INT_MIN
        return lax.bitcast_convert_type(key ^ ((key >> 31) & 0x7FFFFFFF), jnp.float32)

    def count(pred):
        def body(j, acc):
            hit = jnp.where(pred(sc_ref[j], j), 1, 0)
            return acc + jnp.sum(hit.reshape(tk // 8, 8, tq), axis=0)
        acc = lax.fori_loop(0, n_kv, body, jnp.zeros((8, tq), jnp.int32))
        return jnp.sum(acc, axis=0, keepdims=True)

    def count_coarse(cand_b):
        one, zero = jnp.ones((), jnp.bfloat16), jnp.zeros((), jnp.bfloat16)

        def body(j, acc):
            hit = jnp.where(sb_ref[j] >= cand_b, one, zero)
            rows = [hit[r * BF16_ROWS:(r + 1) * BF16_ROWS] for r in range(tk // BF16_ROWS)]
            while len(rows) > 1:
                rows = [rows[r] + rows[r + 1] for r in range(0, len(rows), 2)]
            return acc + rows[0].astype(jnp.float32)
        acc = lax.fori_loop(0, n_kv, body, jnp.zeros((BF16_ROWS, tq), jnp.float32))
        return jnp.sum(acc, axis=0, keepdims=True)

    def coarse_bit(it, t16):
        cand = t16 | jnp.left_shift(jnp.int32(1), 15 - it)
        cand32 = jnp.left_shift(cand, 16) | jnp.where(cand < 0x8000, 0xFFFF, 0)
        cnt = count_coarse(pattern_value(cand32).astype(jnp.bfloat16))
        return jnp.where(cnt >= k_top, cand, t16)

    t16 = lax.fori_loop(0, 16, coarse_bit, jnp.zeros((1, tq), jnp.int32))
    few = t16 < 0x0080
    base = jnp.left_shift(t16, 16) + jnp.where(t16 < 0x8000, 0x7FFF, -0x8000)
    base = jnp.where(few, 0x00800000, base)

    def fine_bit(it, carry):
        t, cnt_t = carry
        cand = t | jnp.left_shift(jnp.int32(1), FINE_BITS - 1 - it)
        cand_f = pattern_value(base + cand)
        cnt = count(lambda sc, j: sc >= cand_f)
        take = cnt >= k_top
        return jnp.where(take, cand, t), jnp.where(take, cnt, cnt_t)

    t, cnt_ge = lax.fori_loop(0, FINE_BITS, fine_bit,
                              (jnp.zeros((1, tq), jnp.int32), jnp.full((1, tq), k_top, jnp.int32)))
    thr = jnp.where(few, F32_LOWEST, pattern_value(base + t))
    tied = jnp.logical_not(few) & ((cnt_ge > k_top) | (t == 0))

    def tie_cut():
        cnt_gt = count(lambda sc, j: sc > thr)
        need = k_top - cnt_gt

        def index_bit(it, c):
            cand = c | jnp.left_shift(jnp.int32(1), 12 - it)
            below = count(lambda sc, j: (sc == thr) & (kpos0 + j * tk < cand))
            return jnp.where(below < need, cand, c)

        c = lax.fori_loop(0, 13, index_bit, jnp.zeros((1, tq), jnp.int32))
        return jnp.where(tied, c, jnp.int32(2 ** 30))

    cut = lax.cond(jnp.max(tied.astype(jnp.int32)) > 0, tie_cut,
                   lambda: jnp.full((1, tq), 2 ** 30, jnp.int32))

    hpd = max(1, MXU_WIDTH // tq)
    n_dots = n_heads // hpd
    wd = hpd * tq

    def attend_all(shift):
        def attend(j, carry):
            off = pl.multiple_of(j * tk, tk)
            kt = kb_ref[0, pl.ds(off, tk), :]
            scores = [_nt_dot(kt, qb_ref[0, hpd * d:hpd * (d + 1)].reshape(wd, LANES)) for d in range(n_dots)]
            sc = sc_ref[j]
            sel = (sc > thr) | ((sc == thr) & (kpos0 + off <= cut))
            sel_d = jnp.concatenate([sel] * hpd, axis=1) if hpd > 1 else sel
            shift_d = None if shift is None else jnp.where(sel_d, shift, -NEG_BIG)
            out = []
            for d in range(n_dots):
                m, l, acc = carry[d]
                s = jnp.where(sel_d, scores[d], NEG_BIG) if shift is None else scores[d]
                m, l, alpha, pb = _softmax_tile(s, m, l, shift_d)
                if alpha is not None:
                    acc = alpha * acc
                for c in range(tk // tv):
                    acc = acc + _bf16_dot(vt_ref[0, j * (tk // tv) + c], pb[c * tv:(c + 1) * tv])
                out.append((m, l, acc))
            return tuple(out)

        init = tuple((jnp.full((1, wd), NEG_BIG, jnp.float32), jnp.zeros((1, wd), jnp.float32),
                      jnp.zeros((LANES, wd), jnp.float32)) for _ in range(n_dots))
        carry = lax.fori_loop(0, n_kv, attend, init)
        o = jnp.concatenate([acc / l for (_, l, acc) in carry], axis=1)
        for j in range(DSA_GROUP):
            lo = o[:, j * tq:(j + 1) * tq].T
            hi = o[:, (DSA_GROUP + j) * tq:(DSA_GROUP + j + 1) * tq].T
            o_ref[0, :, j * LANES:(j + 1) * LANES] = jnp.where(lane_q < DSA_DIM, lo, hi)

    _by_shift(shift_ref, attend_all)


def _merge_kernel(x_ref, oa_ref, ob_ref, g_ref, bm_ref, wa_ref, wb_ref, wo_ref, out_ref):
    f32 = jnp.float32
    ga = g_ref[:, 0:MLA_WIDTH].astype(f32)
    gb = g_ref[:, MLA_WIDTH:MLA_WIDTH + DSA_WIDTH].astype(f32)
    ma = g_ref[:, MLA_WIDTH + DSA_WIDTH:MLA_WIDTH + DSA_WIDTH + D_MODEL].astype(f32)
    mb = g_ref[:, MLA_WIDTH + DSA_WIDTH + D_MODEL:].astype(f32)
    ya = _bf16_dot((oa_ref[...] * (ga * jax.nn.sigmoid(ga))).astype(jnp.bfloat16), wa_ref[...])
    yb = _bf16_dot((ob_ref[...] * (gb * jax.nn.sigmoid(gb))).astype(jnp.bfloat16), wb_ref[...])
    merged = jax.nn.sigmoid(ma + bm_ref[0:1, :]) * ya + jax.nn.sigmoid(mb + bm_ref[1:2, :]) * yb
    out_ref[...] = x_ref[...] + _bf16_dot(merged.astype(jnp.bfloat16), wo_ref[...])


def _inv_freq(rot_dim):
    return ROPE_THETA ** (-jnp.arange(0, rot_dim, 2, dtype=jnp.float32) / rot_dim)


def _rope_inputs():
    invf = jnp.concatenate([_inv_freq(rot) for (_, _, rot, _) in ROPE_LAYOUTS])
    invf = jnp.pad(invf, (0, FREQ_ROWS - invf.shape[0]))[:, None]
    spread = np.zeros((2 * FREQ_ROWS, 3 * len(ROPE_LAYOUTS) * LANES), np.float32)
    for n, (period, start, rot, row0) in enumerate(ROPE_LAYOUTS):
        half = rot // 2
        for lane in range(LANES):
            p = lane % period - start
            cos_col, lo_col, hi_col = ((3 * n + c) * LANES + lane for c in range(3))
            if 0 <= p < rot:
                spread[row0 + p % half, cos_col] = 1.0
                spread[FREQ_ROWS + row0 + p % half, lo_col if p < half else hi_col] = -1.0 if p < half else 1.0
            else:
                spread[ONE_ROW, cos_col] = 1.0
    return invf, jnp.asarray(np.tile(spread, (3, 1)), jnp.bfloat16)


def _pad_lanes(a, start=0):
    return jnp.pad(a, ((0, 0), (start, LANES - start - a.shape[1])))


def _layout_params(w_in, mla_w_uq, mla_w_ukv, mla_q_gain, mla_k_gain, dsa_q_gain, dsa_k_gain,
                   w_branch_dsa):
    offs = np.concatenate([[0], np.cumsum(IN_SPLITS)])
    (c_q, c_kv, k_pe, gate_a, q_b, k_b, v_b, gate_b, q_i, k_i, w_i, m_a, m_b) = [
        w_in[:, offs[n]:offs[n + 1]] for n in range(len(IN_SPLITS))]

    def pair_heads(w):
        rows = w.shape[0]
        w = w.reshape(rows, DSA_KV_HEADS, DSA_GROUP, DSA_DIM)
        return jnp.transpose(w, (0, 2, 1, 3)).reshape(rows, DSA_WIDTH)

    w_all = jnp.concatenate([
        c_q, c_kv, _pad_lanes(k_pe, MLA_NOPE),
        pair_heads(q_b), k_b, v_b,
        q_i, jnp.tile(k_i, (1, LANES // IDX_DIM)), _pad_lanes(w_i),
        gate_a, pair_heads(gate_b), m_a, m_b], axis=1).astype(jnp.bfloat16)
    uq = mla_w_uq.reshape(MLA_Q_RANK, MLA_HEADS, MLA_QK)
    wuq = jnp.pad(uq, ((0, 0), (0, 0), (0, LANES - MLA_QK))).reshape(MLA_Q_RANK, MLA_HEADS * LANES)
    ukv = mla_w_ukv.reshape(MLA_KV_RANK, MLA_HEADS, MLA_NOPE + MLA_V)
    wuk = jnp.pad(ukv[:, :, :MLA_NOPE], ((0, 0), (0, 0), (0, LANES - MLA_NOPE))).reshape(MLA_KV_RANK, MLA_HEADS * LANES)
    wuv = ukv[:, :, MLA_NOPE:].reshape(MLA_KV_RANK, MLA_WIDTH)
    gq = _pad_lanes(mla_q_gain[None, :])
    gk = _pad_lanes(mla_k_gain[None, :])
    gdq = jnp.tile(dsa_q_gain[None, :], (1, LANES // DSA_DIM))
    gdk = jnp.tile(dsa_k_gain[None, :], (1, LANES // DSA_DIM))
    wb_dsa = jnp.transpose(w_branch_dsa.reshape(DSA_KV_HEADS, DSA_GROUP, DSA_DIM, D_MODEL),
                           (1, 0, 2, 3)).reshape(DSA_WIDTH, D_MODEL)
    return (w_all, wuq.astype(jnp.bfloat16), wuk.astype(jnp.bfloat16), wuv.astype(jnp.bfloat16),
            gq, gk, gdq, gdk, wb_dsa.astype(jnp.bfloat16))


def _const_spec(shape):
    return pl.BlockSpec(shape, lambda *_: (0,) * len(shape))


def _layer(x, positions, invf, spread, norm_gain, w_in, b_merge, mla_q_norm, mla_w_uq, mla_kv_norm, mla_w_ukv,
           mla_q_gain, mla_k_gain, dsa_q_gain, dsa_k_gain, w_branch_mla, w_branch_dsa, w_out):
    bsz, s_len, _ = x.shape
    k_top = min(TOPK_MAX, s_len // 4)
    (w_all, wuq, wuk, wuv, gq, gk, gdq, gdk, wb_dsa) = _layout_params(
        w_in, mla_w_uq, mla_w_ukv, mla_q_gain, mla_k_gain, dsa_q_gain, dsa_k_gain, w_branch_dsa)
    bf16, f32 = jnp.bfloat16, jnp.float32
    gmax = lambda g: jnp.max(jnp.abs(g))
    shift_a = (BOUND_MARGIN * MLA_QK ** 0.5 * LOG2_E * gmax(mla_q_gain) * gmax(mla_k_gain)).reshape(1).astype(f32)
    shift_b = (BOUND_MARGIN * DSA_DIM ** 0.5 * LOG2_E * gmax(dsa_q_gain) * gmax(dsa_k_gain)).reshape(1).astype(f32)
    smem = pl.BlockSpec(memory_space=pltpu.SMEM)

    tm = 256
    tok = lambda w: pl.BlockSpec((1, tm, w), lambda b, i: (b, i, 0))
    head = pl.BlockSpec((1, MLA_HEADS, tm, LANES), lambda b, i: (b, 0, i, 0))
    outs = pl.pallas_call(
        _proj_kernel,
        grid=(bsz, s_len // tm),
        in_specs=[tok(D_MODEL), pl.BlockSpec((1, 1, tm), lambda b, i: (b, 0, i)),
                  _const_spec(invf.shape), _const_spec(spread.shape), _const_spec((1, D_MODEL)),
                  _const_spec((1, MLA_Q_RANK)), _const_spec((1, MLA_KV_RANK)),
                  _const_spec((1, LANES)), _const_spec((1, LANES)), _const_spec((1, LANES)), _const_spec((1, LANES)),
                  _const_spec((D_MODEL, W_ALL)), _const_spec((MLA_Q_RANK, MLA_HEADS * LANES)),
                  _const_spec((MLA_KV_RANK, MLA_HEADS * LANES)), _const_spec((MLA_KV_RANK, MLA_WIDTH))],
        out_specs=[head, head, pl.BlockSpec((1, 1, MLA_WIDTH, tm), lambda b, i: (b, i, 0, 0)),
                   head, tok(LANES),
                   pl.BlockSpec((1, 1, LANES, tm), lambda b, i: (b, i, 0, 0)),
                   tok(IDX_HEADS * IDX_DIM), tok(LANES),
                   pl.BlockSpec((1, IDX_HEADS, tm), lambda b, i: (b, 0, i)), tok(W_G)],
        out_shape=[jax.ShapeDtypeStruct((bsz, MLA_HEADS, s_len, LANES), bf16),
                   jax.ShapeDtypeStruct((bsz, MLA_HEADS, s_len, LANES), bf16),
                   jax.ShapeDtypeStruct((bsz, s_len // tm, MLA_WIDTH, tm), bf16),
                   jax.ShapeDtypeStruct((bsz, DSA_HEADS, s_len, LANES), bf16),
                   jax.ShapeDtypeStruct((bsz, s_len, LANES), bf16),
                   jax.ShapeDtypeStruct((bsz, s_len // tm, LANES, tm), bf16),
                   jax.ShapeDtypeStruct((bsz, s_len, IDX_HEADS * IDX_DIM), bf16),
                   jax.ShapeDtypeStruct((bsz, s_len, LANES), bf16),
                   jax.ShapeDtypeStruct((bsz, IDX_HEADS, s_len), f32),
                   jax.ShapeDtypeStruct((bsz, s_len, W_G), bf16)],
        compiler_params=pltpu.CompilerParams(dimension_semantics=("arbitrary", "arbitrary"),
                                             vmem_limit_bytes=VMEM_LIMIT),
        name="proj",
    )(x, positions[:, None, :], invf, spread, norm_gain[None, :], mla_q_norm[None, :], mla_kv_norm[None, :],
      gq, gk, gdq, gdk, w_all, wuq, wuk, wuv)
    qa, ka, vat, qb, kb, vbt, qi, ki, wit, gates = outs

    tq_a, tk_a, hps = 512, 512, 8
    o_a = pl.pallas_call(
        functools.partial(_mla_kernel, tq=tq_a, tk=tk_a, heads=hps),
        grid=(bsz, MLA_HEADS // hps, s_len // tq_a),
        in_specs=[smem, pl.BlockSpec((1, hps, tq_a, LANES), lambda b, hp, i: (b, hp, i, 0)),
                  pl.BlockSpec((1, hps, s_len, LANES), lambda b, hp, i: (b, hp, 0, 0)),
                  pl.BlockSpec((1, s_len // tm, hps * MLA_V, tm), lambda b, hp, i: (b, 0, hp, 0))],
        out_specs=pl.BlockSpec((1, tq_a, hps * MLA_V), lambda b, hp, i: (b, i, hp)),
        out_shape=jax.ShapeDtypeStruct((bsz, s_len, MLA_WIDTH), f32),
        compiler_params=pltpu.CompilerParams(dimension_semantics=("arbitrary",) * 3,
                                             vmem_limit_bytes=VMEM_LIMIT),
        name="mla",
    )(shift_a, qa, ka, vat)

    tq_b, tk_b = 512, 512
    row_b = lambda w: pl.BlockSpec((1, tq_b, w), lambda b, i: (b, i, 0))
    full_b = lambda w: pl.BlockSpec((1, s_len, w), lambda b, i: (b, 0, 0))
    o_b = pl.pallas_call(
        functools.partial(_dsa_kernel, tq=tq_b, tk=tk_b, k_top=k_top),
        grid=(bsz, s_len // tq_b),
        in_specs=[smem, pl.BlockSpec((1, DSA_HEADS, tq_b, LANES), lambda b, i: (b, 0, i, 0)), full_b(LANES),
                  pl.BlockSpec((1, s_len // tm, LANES, tm), lambda b, i: (b, 0, 0, 0)),
                  row_b(IDX_HEADS * IDX_DIM), full_b(LANES),
                  pl.BlockSpec((1, IDX_HEADS, tq_b), lambda b, i: (b, 0, i))],
        out_specs=row_b(DSA_WIDTH),
        out_shape=jax.ShapeDtypeStruct((bsz, s_len, DSA_WIDTH), f32),
        scratch_shapes=[pltpu.VMEM((s_len // tk_b, tk_b, tq_b), f32),
                        pltpu.VMEM((s_len // tk_b, tk_b, tq_b), bf16),
                        pltpu.VMEM((IDX_HEADS * tq_b, LANES), bf16)],
        compiler_params=pltpu.CompilerParams(dimension_semantics=("arbitrary", "arbitrary"),
                                             vmem_limit_bytes=VMEM_LIMIT),
        name="dsa",
    )(shift_b, qb, kb, vbt, qi, ki, wit)

    tm_m = 512
    n_tok = bsz * s_len
    flat = lambda w: pl.BlockSpec((tm_m, w), lambda i: (i, 0))
    out = pl.pallas_call(
        _merge_kernel,
        grid=(n_tok // tm_m,),
        in_specs=[flat(D_MODEL), flat(MLA_WIDTH), flat(DSA_WIDTH), flat(W_G), _const_spec((2, D_MODEL)),
                  _const_spec((MLA_WIDTH, D_MODEL)), _const_spec((DSA_WIDTH, D_MODEL)),
                  _const_spec((D_MODEL, D_MODEL))],
        out_specs=flat(D_MODEL),
        out_shape=jax.ShapeDtypeStruct((n_tok, D_MODEL), f32),
        compiler_params=pltpu.CompilerParams(dimension_semantics=("arbitrary",),
                                             vmem_limit_bytes=VMEM_LIMIT),
        name="merge",
    )(x.reshape(n_tok, D_MODEL), o_a.reshape(n_tok, MLA_WIDTH), o_b.reshape(n_tok, DSA_WIDTH),
      gates.reshape(n_tok, W_G), b_merge, w_branch_mla.astype(bf16), wb_dsa, w_out.astype(bf16))
    return out.reshape(bsz, s_len, D_MODEL)


def kernel(x, positions, norm_gain, w_in, b_merge, mla_q_norm, mla_w_uq, mla_kv_norm, mla_w_ukv,
           mla_q_gain, mla_k_gain, dsa_q_gain, dsa_k_gain, w_branch_mla, w_branch_dsa, w_out):
    invf, spread = _rope_inputs()
    for l in range(norm_gain.shape[0]):
        x = _layer(x, positions, invf, spread, norm_gain[l], w_in[l], b_merge[l], mla_q_norm[l], mla_w_uq[l],
                   mla_kv_norm[l], mla_w_ukv[l], mla_q_gain[l], mla_k_gain[l], dsa_q_gain[l], dsa_k_gain[l],
                   w_branch_mla[l], w_branch_dsa[l], w_out[l])
    return x
```

```python
import functools

import numpy as np
import jax
import jax.numpy as jnp
from jax import lax
from jax.experimental import pallas as pl
from jax.experimental.pallas import tpu as pltpu

D_MODEL = 1024
MLA_HEADS = 8
MLA_Q_RANK = 256
MLA_KV_RANK = 128
MLA_NOPE = 64
MLA_ROPE = 32
MLA_QK = MLA_NOPE + MLA_ROPE
MLA_V = 64
MLA_WIDTH = MLA_HEADS * MLA_V
DSA_HEADS = 8
DSA_KV_HEADS = 2
DSA_GROUP = DSA_HEADS // DSA_KV_HEADS
DSA_DIM = 64
DSA_WIDTH = DSA_HEADS * DSA_DIM
DSA_ROT = DSA_DIM // 4
IDX_HEADS = 8
IDX_DIM = 32
IDX_ROT = IDX_DIM // 4
TOPK_MAX = 256
ROPE_THETA = 500000.0
NORM_EPS = 1e-6

IN_SPLITS = (
    MLA_Q_RANK, MLA_KV_RANK, MLA_ROPE, MLA_WIDTH,
    DSA_HEADS * DSA_DIM, DSA_KV_HEADS * DSA_DIM, DSA_KV_HEADS * DSA_DIM, DSA_WIDTH,
    IDX_HEADS * IDX_DIM, IDX_DIM, IDX_HEADS,
    D_MODEL, D_MODEL,
)

LANES = 128
MXU_WIDTH = 256
VMEM_LIMIT = 56 * 1024 * 1024
INT_MIN = -(2 ** 31)
F32_LOWEST = float(np.finfo(np.float32).min)
BF16_ROWS = 16
FINE_BITS = 17
NEG_BIG = -1e30
LOG2_E = 1.4426950408889634
FAST_MAX_SHIFT = 60.0
BOUND_MARGIN = 1.02

OFF_A = 0
W_A = MLA_Q_RANK + MLA_KV_RANK + LANES
OFF_B = OFF_A + W_A
W_B = DSA_WIDTH + 2 * DSA_KV_HEADS * DSA_DIM
OFF_I = OFF_B + W_B
W_I = IDX_HEADS * IDX_DIM + 2 * LANES
OFF_G = OFF_I + W_I
W_G = MLA_WIDTH + DSA_WIDTH + 2 * D_MODEL
W_ALL = OFF_G + W_G
PROJ_ROWS = 256
GATE_CHUNK = 512
N_GATE_CHUNKS = W_G // GATE_CHUNK

DSA_HEAD_ORDER = tuple(h for j in range(DSA_GROUP) for h in (j, DSA_GROUP + j))

ROPE_LAYOUTS = ((LANES, MLA_NOPE, MLA_ROPE, 0), (DSA_DIM, 0, DSA_ROT, MLA_ROPE // 2),
                (IDX_DIM, 0, IDX_ROT, MLA_ROPE // 2 + DSA_ROT // 2))
ROW_A, ROW_D, ROW_I = 0, 1, 2
FREQ_ROWS = 32
ONE_ROW = FREQ_ROWS - 1


def _rms(v, width):
    return lax.rsqrt(jnp.sum(v * v, axis=-1, keepdims=True) * (1.0 / width) + NORM_EPS)


def _rope(v, cos, sin_lo, sin_hi, half):
    return (v * cos + pltpu.roll(v, LANES - half, 1) * sin_lo + pltpu.roll(v, half, 1) * sin_hi)


def _bf16_dot(a, b):
    return jnp.dot(a, b, preferred_element_type=jnp.float32)


def _nt_dot(a, b):
    return lax.dot_general(a, b, (((1,), (1,)), ((), ())), preferred_element_type=jnp.float32)


def _proj_kernel(x_ref, pos_ref, invf_ref, spread_ref, gx_ref, gcq_ref, gckv_ref, gq_ref, gk_ref, gdq_ref, gdk_ref,
                 w_ref, wuq_ref, wuk_ref, wuv_ref,
                 qa_ref, ka_ref, va_ref, qb_ref, kb_ref, vb_ref, qi_ref, ki_ref, wi_ref, g_ref):
    bf16 = jnp.bfloat16
    subs = [slice(r0, r0 + PROJ_ROWS) for r0 in range(0, x_ref.shape[1], PROJ_ROWS)]
    hs = []
    for rows in subs:
        x = x_ref[0, rows]
        hs.append((x * _rms(x, D_MODEL) * gx_ref[...]).astype(bf16))
    pas = [_bf16_dot(h, w_ref[:, OFF_A:OFF_A + W_A]) for h in hs]
    tabs = [_rope_tables(pos_ref[0, :, rows], invf_ref, spread_ref) for rows in subs]
    lat = []
    for pa in pas:
        c_q = pa[:, :MLA_Q_RANK]
        c_kv = pa[:, MLA_Q_RANK:MLA_Q_RANK + MLA_KV_RANK]
        lat.append(((c_q * _rms(c_q, MLA_Q_RANK) * gcq_ref[...]).astype(bf16),
                    (c_kv * _rms(c_kv, MLA_KV_RANK) * gckv_ref[...]).astype(bf16)))
    pbs = [_bf16_dot(h, w_ref[:, OFF_B:OFF_B + W_B]) for h in hs]
    pis = [_bf16_dot(h, w_ref[:, OFF_I:OFF_I + W_I]) for h in hs]
    ups = [(_bf16_dot(cqn, wuq_ref[...]),
            _bf16_dot(ckvn, wuk_ref[...]),
            _bf16_dot(ckvn, wuv_ref[...])) for cqn, ckvn in lat]
    for n, rows in enumerate(subs):
        kpe = pas[n][:, MLA_Q_RANK + MLA_KV_RANK:]
        _proj_epilogue(rows, (tabs[n], kpe) + ups[n] + (pbs[n], pis[n]), gq_ref, gk_ref, gdq_ref, gdk_ref,
                       qa_ref, ka_ref, va_ref, qb_ref, kb_ref, vb_ref, qi_ref, ki_ref, wi_ref)
    for rows, h in zip(subs, hs):
        for c in range(N_GATE_CHUNKS):
            cols = slice(c * GATE_CHUNK, (c + 1) * GATE_CHUNK)
            g_ref[0, rows, cols] = _bf16_dot(h, w_ref[:, OFF_G + c * GATE_CHUNK:OFF_G + (c + 1) * GATE_CHUNK]).astype(g_ref.dtype)


def _rope_tables(pos_row, invf_ref, spread_ref):
    bf16 = jnp.bfloat16
    ang = invf_ref[...] * pos_row.astype(jnp.float32)
    trig = jnp.concatenate([jnp.cos(ang), jnp.sin(ang)], axis=0)
    t1 = trig.astype(bf16)
    r1 = trig - t1.astype(jnp.float32)
    t2 = r1.astype(bf16)
    t3 = (r1 - t2.astype(jnp.float32)).astype(bf16)
    return lax.dot_general(jnp.concatenate([t1, t2, t3], axis=0), spread_ref[...], (((0,), (0,)), ((), ())),
                           preferred_element_type=jnp.float32)


def _proj_epilogue(rows, vals, gq_ref, gk_ref, gdq_ref, gdk_ref,
                   qa_ref, ka_ref, va_ref, qb_ref, kb_ref, vb_ref, qi_ref, ki_ref, wi_ref):
    tabs, kpe, q_up, k_up, v_up, pb, pi = vals
    lane = lax.broadcasted_iota(jnp.int32, (1, LANES), 1)
    bf16 = jnp.bfloat16

    def tables(n):
        return tuple(tabs[:, (3 * n + c) * LANES:(3 * n + c + 1) * LANES] for c in range(3))

    for c in range(MLA_WIDTH // LANES):
        va_ref[0, 0, c * LANES:(c + 1) * LANES, rows] = v_up[:, c * LANES:(c + 1) * LANES].T.astype(bf16)
    cos_a, sl_a, sh_a = tables(ROW_A)
    sin_a = sl_a + sh_a
    q_scale = MLA_QK ** -0.5 * LOG2_E
    kpe_g = kpe * gk_ref[...]
    k_rot = (pltpu.roll(kpe_g, LANES - MLA_ROPE // 2, 1) * sl_a + pltpu.roll(kpe_g, MLA_ROPE // 2, 1) * sh_a)
    for hd in range(MLA_HEADS):
        qh = q_up[:, hd * LANES:(hd + 1) * LANES]
        q_sw = q_up[:, (MLA_HEADS + hd) * LANES:(MLA_HEADS + hd + 1) * LANES]
        q_rot = qh * gq_ref[0:1, :] * cos_a + q_sw * gq_ref[1:2, :] * sin_a
        qa_ref[0, hd, rows] = (q_rot * (_rms(qh, MLA_QK) * q_scale)).astype(bf16)
        kh = k_up[:, hd * LANES:(hd + 1) * LANES] + kpe
        ka_ref[0, hd, rows] = ((kh * gk_ref[...] * cos_a + k_rot) * _rms(kh, MLA_QK)).astype(bf16)

    cos_d, sl_d, sh_d = tables(ROW_D)
    low = lane < DSA_DIM

    def dsa_norm_rope(blk, gain):
        sq = blk * blk
        r_lo = lax.rsqrt(jnp.sum(jnp.where(low, sq, 0.0), axis=-1, keepdims=True) * (1.0 / DSA_DIM) + NORM_EPS)
        r_hi = lax.rsqrt(jnp.sum(jnp.where(low, 0.0, sq), axis=-1, keepdims=True) * (1.0 / DSA_DIM) + NORM_EPS)
        return _rope(blk * jnp.where(low, r_lo, r_hi) * gain, cos_d, sl_d, sh_d, DSA_ROT // 2)

    dsa_scale = DSA_DIM ** -0.5 * LOG2_E
    for j in range(DSA_WIDTH // LANES):
        blk = pb[:, j * LANES:(j + 1) * LANES]
        qn = (dsa_norm_rope(blk, gdq_ref[...]) * dsa_scale).astype(bf16)
        qb_ref[0, j, rows] = jnp.where(low, qn, jnp.zeros_like(qn))
        qb_ref[0, DSA_GROUP + j, rows] = jnp.where(low, jnp.zeros_like(qn), qn)
    kb_ref[0, rows] = dsa_norm_rope(pb[:, DSA_WIDTH:DSA_WIDTH + LANES], gdk_ref[...]).astype(bf16)
    vb_ref[0, 0, :, rows] = pb[:, DSA_WIDTH + LANES:].T.astype(bf16)

    cos_i, sl_i, sh_i = tables(ROW_I)
    n_qi = IDX_HEADS * IDX_DIM // LANES
    for j in range(n_qi + 1):
        blk = _rope(pi[:, j * LANES:(j + 1) * LANES], cos_i, sl_i, sh_i, IDX_ROT // 2).astype(bf16)
        if j < n_qi:
            qi_ref[0, rows, j * LANES:(j + 1) * LANES] = blk
        else:
            ki_ref[0, rows] = blk
    wi_t = (pi[:, (n_qi + 1) * LANES:] * (IDX_HEADS ** -0.5 * IDX_DIM ** -0.5)).T
    wi_ref[0, :, rows] = wi_t[0:IDX_HEADS, :]


def _softmax_tile(s, m, l, shift):
    if shift is None:
        m_new = jnp.maximum(m, jnp.max(s, axis=0, keepdims=True))
        alpha = jnp.exp2(m - m_new)
        p = jnp.exp2(s - m_new)
        return m_new, alpha * l + jnp.sum(p, axis=0, keepdims=True), alpha, p.astype(jnp.bfloat16)
    p = jnp.exp2(s - shift)
    return m, l + jnp.sum(p, axis=0, keepdims=True), None, p.astype(jnp.bfloat16)


def _by_shift(shift_ref, body):
    shift = shift_ref[0]

    @pl.when(shift <= FAST_MAX_SHIFT)
    def _():
        body(shift)

    @pl.when(shift > FAST_MAX_SHIFT)
    def _():
        body(None)


def _mla_kernel(shift_ref, q_ref, k_ref, vt_ref, o_ref, *, tq, tk, heads):
    i = pl.program_id(2)
    q_start = i * tq
    n_full = q_start // tk
    n_all = (q_start + tq + tk - 1) // tk
    tv = vt_ref.shape[-1]
    kpos0 = lax.broadcasted_iota(jnp.int32, (tk, tq), 0)
    qpos = q_start + lax.broadcasted_iota(jnp.int32, (tk, tq), 1)

    def attend(shift):
        def step(j, carry, masked):
            off = pl.multiple_of(j * tk, tk)
            scores = [_nt_dot(k_ref[0, hd, pl.ds(off, tk), :], q_ref[0, hd]) for hd in range(heads)]
            out = []
            visible = kpos0 + off <= qpos if masked else None
            shift_t = jnp.where(visible, shift, -NEG_BIG) if masked and shift is not None else shift
            for hd in range(heads):
                m, l, acc = carry[hd]
                s = scores[hd]
                if masked and shift is None:
                    s = jnp.where(visible, s, NEG_BIG)
                m, l, alpha, pb = _softmax_tile(s, m, l, shift_t)
                if alpha is not None:
                    acc = alpha * acc
                for c in range(tk // tv):
                    vt = vt_ref[0, j * (tk // tv) + c, hd * MLA_V:(hd + 1) * MLA_V, :]
                    acc = acc + _bf16_dot(vt, pb[c * tv:(c + 1) * tv])
                out.append((m, l, acc))
            return tuple(out)

        init = tuple((jnp.full((1, tq), NEG_BIG, jnp.float32), jnp.zeros((1, tq), jnp.float32),
                      jnp.zeros((MLA_V, tq), jnp.float32)) for _ in range(heads))
        carry = lax.fori_loop(0, n_full, functools.partial(step, masked=False), init)
        carry = lax.fori_loop(n_full, n_all, functools.partial(step, masked=True), carry)
        outs = [acc / l for (_, l, acc) in carry]
        for c in range(heads // 2):
            o_ref[0, :, c * LANES:(c + 1) * LANES] = jnp.concatenate(outs[2 * c:2 * c + 2], axis=0).T

    _by_shift(shift_ref, attend)


def _dsa_kernel(shift_ref, qb_ref, kb_ref, vt_ref, qi_ref, ki_ref, wt_ref, o_ref,
                sc_ref, sb_ref, qim_ref, *, tq, tk, k_top):
    i = pl.program_id(1)
    q_start = i * tq
    n_kv = (q_start + tq + tk - 1) // tk
    n_heads = IDX_HEADS
    tv = vt_ref.shape[-1]

    lane_q = lax.broadcasted_iota(jnp.int32, (tq, LANES), 1)
    per_blk = LANES // IDX_DIM
    for hd in range(IDX_HEADS):
        blk = qi_ref[0, :, (hd // per_blk) * LANES:(hd // per_blk + 1) * LANES]
        slot = hd % per_blk
        in_head = (lane_q >= slot * IDX_DIM) & (lane_q < (slot + 1) * IDX_DIM)
        qim_ref[hd * tq:(hd + 1) * tq, :] = jnp.where(in_head, blk, jnp.zeros_like(blk))

    kpos0 = lax.broadcasted_iota(jnp.int32, (tk, tq), 0)
    qpos = q_start + lax.broadcasted_iota(jnp.int32, (tk, tq), 1)
    w_rows = [wt_ref[0, hd:hd + 1, :] for hd in range(IDX_HEADS)]

    def score_tile(j, _):
        off = pl.multiple_of(j * tk, tk)
        s_all = _nt_dot(ki_ref[0, pl.ds(off, tk), :], qim_ref[...])
        sc = w_rows[0] * jnp.maximum(s_all[:, 0:tq], 0.0)
        for hd in range(1, n_heads):
            sc = sc + w_rows[hd] * jnp.maximum(s_all[:, hd * tq:(hd + 1) * tq], 0.0)
        sc = jnp.where(kpos0 + off <= qpos, sc, -jnp.inf)
        sc_ref[j] = sc
        sb_ref[j] = sc.astype(jnp.bfloat16)
        return 0

    lax.fori_loop(0, n_kv, score_tile, 0)

    def pattern_value(u):
        key = u ^ INT_MIN
        return lax.bitcast_convert_type(key ^ ((key >> 31) & 0x7FFFFFFF), jnp.float32)

    def count(pred):
        def body(j, acc):
            hit = jnp.where(pred(sc_ref[j], j), 1, 0)
            return acc + jnp.sum(hit.reshape(tk // 8, 8, tq), axis=0)
        acc = lax.fori_loop(0, n_kv, body, jnp.zeros((8, tq), jnp.int32))
        return jnp.sum(acc, axis=0, keepdims=True)

    def count_coarse(cand_b):
        one, zero = jnp.ones((), jnp.bfloat16), jnp.zeros((), jnp.bfloat16)

        def body(j, acc):
            part = None
            for r in range(tk // BF16_ROWS):
                rows = sb_ref[j, r * BF16_ROWS:(r + 1) * BF16_ROWS, :]
                hit = jnp.where(rows >= cand_b, one, zero)
                part = hit if part is None else part + hit
            return acc + part.astype(jnp.float32)
        acc = lax.fori_loop(0, n_kv, body, jnp.zeros((BF16_ROWS, tq), jnp.float32))
        return jnp.sum(acc, axis=0, keepdims=True)

    def coarse_bit(it, t16):
        cand = t16 | jnp.left_shift(jnp.int32(1), 15 - it)
        cand32 = jnp.left_shift(cand, 16) | jnp.where(cand < 0x8000, 0xFFFF, 0)
        cnt = count_coarse(pattern_value(cand32).astype(jnp.bfloat16))
        return jnp.where(cnt >= k_top, cand, t16)

    t16 = lax.fori_loop(0, 16, coarse_bit, jnp.zeros((1, tq), jnp.int32))
    few = t16 < 0x0080
    base = jnp.left_shift(t16, 16) + jnp.where(t16 < 0x8000, 0x7FFF, -0x8000)
    base = jnp.where(few, 0x00800000, base)

    def fine_bit(it, carry):
        t, cnt_t = carry
        cand = t | jnp.left_shift(jnp.int32(1), FINE_BITS - 1 - it)
        cand_f = pattern_value(base + cand)
        cnt = count(lambda sc, j: sc >= cand_f)
        take = cnt >= k_top
        return jnp.where(take, cand, t), jnp.where(take, cnt, cnt_t)

    t, cnt_ge = lax.fori_loop(0, FINE_BITS, fine_bit,
                              (jnp.zeros((1, tq), jnp.int32), jnp.full((1, tq), k_top, jnp.int32)))
    thr = jnp.where(few, F32_LOWEST, pattern_value(base + t))
    tied = jnp.logical_not(few) & ((cnt_ge > k_top) | (t == 0))

    def tie_cut():
        cnt_gt = count(lambda sc, j: sc > thr)
        need = k_top - cnt_gt

        def index_bit(it, c):
            cand = c | jnp.left_shift(jnp.int32(1), 12 - it)
            below = count(lambda sc, j: (sc == thr) & (kpos0 + j * tk < cand))
            return jnp.where(below < need, cand, c)

        c = lax.fori_loop(0, 13, index_bit, jnp.zeros((1, tq), jnp.int32))
        return jnp.where(tied, c, jnp.int32(2 ** 30))

    cut = lax.cond(jnp.max(tied.astype(jnp.int32)) > 0, tie_cut,
                   lambda: jnp.full((1, tq), 2 ** 30, jnp.int32))

    hpd = max(1, MXU_WIDTH // tq)
    n_dots = n_heads // hpd
    wd = hpd * tq

    def attend_all(shift):
        def attend(j, carry):
            off = pl.multiple_of(j * tk, tk)
            kt = kb_ref[0, pl.ds(off, tk), :]
            scores = [_nt_dot(kt, qb_ref[0, hpd * d:hpd * (d + 1)].reshape(wd, LANES)) for d in range(n_dots)]
            sc = sc_ref[j]
            sel = (sc > thr) | ((sc == thr) & (kpos0 + off <= cut))
            sel_d = jnp.concatenate([sel] * hpd, axis=1) if hpd > 1 else sel
            shift_d = None if shift is None else jnp.where(sel_d, shift, -NEG_BIG)
            out = []
            for d in range(n_dots):
                m, l, acc = carry[d]
                s = jnp.where(sel_d, scores[d], NEG_BIG) if shift is None else scores[d]
                m, l, alpha, pb = _softmax_tile(s, m, l, shift_d)
                if alpha is not None:
                    acc = alpha * acc
                for c in range(tk // tv):
                    acc = acc + _bf16_dot(vt_ref[0, j * (tk // tv) + c], pb[c * tv:(c + 1) * tv])
                out.append((m, l, acc))
            return tuple(out)

        init = tuple((jnp.full((1, wd), NEG_BIG, jnp.float32), jnp.zeros((1, wd), jnp.float32),
                      jnp.zeros((LANES, wd), jnp.float32)) for _ in range(n_dots))
        carry = lax.fori_loop(0, n_kv, attend, init)
        o = jnp.concatenate([acc / l for (_, l, acc) in carry], axis=1)
        for j in range(DSA_GROUP):
            lo = o[:, j * tq:(j + 1) * tq].T
            hi = o[:, (DSA_GROUP + j) * tq:(DSA_GROUP + j + 1) * tq].T
            o_ref[0, :, j * LANES:(j + 1) * LANES] = jnp.where(lane_q < DSA_DIM, lo, hi)

    _by_shift(shift_ref, attend_all)


def _merge_kernel(x_ref, oa_ref, ob_ref, g_ref, bm_ref, wa_ref, wb_ref, wo_ref, out_ref):
    f32 = jnp.float32
    ga = g_ref[:, 0:MLA_WIDTH].astype(f32)
    gb = g_ref[:, MLA_WIDTH:MLA_WIDTH + DSA_WIDTH].astype(f32)
    ma = g_ref[:, MLA_WIDTH + DSA_WIDTH:MLA_WIDTH + DSA_WIDTH + D_MODEL].astype(f32)
    mb = g_ref[:, MLA_WIDTH + DSA_WIDTH + D_MODEL:].astype(f32)
    ya = _bf16_dot((oa_ref[...] * (ga * jax.nn.sigmoid(ga))).astype(jnp.bfloat16), wa_ref[...])
    yb = _bf16_dot((ob_ref[...] * (gb * jax.nn.sigmoid(gb))).astype(jnp.bfloat16), wb_ref[...])
    merged = jax.nn.sigmoid(ma + bm_ref[0:1, :]) * ya + jax.nn.sigmoid(mb + bm_ref[1:2, :]) * yb
    out_ref[...] = x_ref[...] + _bf16_dot(merged.astype(jnp.bfloat16), wo_ref[...])


def _inv_freq(rot_dim):
    return ROPE_THETA ** (-jnp.arange(0, rot_dim, 2, dtype=jnp.float32) / rot_dim)


def _rope_inputs():
    invf = jnp.concatenate([_inv_freq(rot) for (_, _, rot, _) in ROPE_LAYOUTS])
    invf = jnp.pad(invf, (0, FREQ_ROWS - invf.shape[0]))[:, None]
    spread = np.zeros((2 * FREQ_ROWS, 3 * len(ROPE_LAYOUTS) * LANES), np.float32)
    for n, (period, start, rot, row0) in enumerate(ROPE_LAYOUTS):
        half = rot // 2
        for lane in range(LANES):
            p = lane % period - start
            cos_col, lo_col, hi_col = ((3 * n + c) * LANES + lane for c in range(3))
            if 0 <= p < rot:
                spread[row0 + p % half, cos_col] = 1.0
                spread[FREQ_ROWS + row0 + p % half, lo_col if p < half else hi_col] = -1.0 if p < half else 1.0
            else:
                spread[ONE_ROW, cos_col] = 1.0
    return invf, jnp.asarray(np.tile(spread, (3, 1)), jnp.bfloat16)


def _pad_lanes(a, start=0):
    return jnp.pad(a, ((0, 0), (start, LANES - start - a.shape[1])))


def _layout_params(w_in, mla_w_uq, mla_w_ukv, mla_q_gain, mla_k_gain, dsa_q_gain, dsa_k_gain,
                   w_branch_dsa):
    offs = np.concatenate([[0], np.cumsum(IN_SPLITS)])
    (c_q, c_kv, k_pe, gate_a, q_b, k_b, v_b, gate_b, q_i, k_i, w_i, m_a, m_b) = [
        w_in[:, offs[n]:offs[n + 1]] for n in range(len(IN_SPLITS))]

    def pair_heads(w):
        rows = w.shape[0]
        w = w.reshape(rows, DSA_KV_HEADS, DSA_GROUP, DSA_DIM)
        return jnp.transpose(w, (0, 2, 1, 3)).reshape(rows, DSA_WIDTH)

    w_all = jnp.concatenate([
        c_q, c_kv, _pad_lanes(k_pe, MLA_NOPE),
        pair_heads(q_b), k_b, v_b,
        q_i, jnp.tile(k_i, (1, LANES // IDX_DIM)), _pad_lanes(w_i),
        gate_a, pair_heads(gate_b), m_a, m_b], axis=1).astype(jnp.bfloat16)
    uq = mla_w_uq.reshape(MLA_Q_RANK, MLA_HEADS, MLA_QK)
    half = MLA_ROPE // 2

    def swap_rope(a):
        return jnp.concatenate([jnp.zeros_like(a[..., :MLA_NOPE]), a[..., MLA_NOPE + half:], a[..., MLA_NOPE:MLA_NOPE + half]], -1)

    pad_qk = lambda a: jnp.pad(a, ((0, 0), (0, 0), (0, LANES - MLA_QK))).reshape(MLA_Q_RANK, MLA_HEADS * LANES)
    wuq = jnp.concatenate([pad_qk(uq), pad_qk(swap_rope(uq))], axis=1)
    ukv = mla_w_ukv.reshape(MLA_KV_RANK, MLA_HEADS, MLA_NOPE + MLA_V)
    wuk = jnp.pad(ukv[:, :, :MLA_NOPE], ((0, 0), (0, 0), (0, LANES - MLA_NOPE))).reshape(MLA_KV_RANK, MLA_HEADS * LANES)
    wuv = ukv[:, :, MLA_NOPE:].reshape(MLA_KV_RANK, MLA_WIDTH)
    gq = jnp.concatenate([_pad_lanes(mla_q_gain[None, :]), _pad_lanes(swap_rope(mla_q_gain[None, :]))], axis=0)
    gk = _pad_lanes(mla_k_gain[None, :])
    gdq = jnp.tile(dsa_q_gain[None, :], (1, LANES // DSA_DIM))
    gdk = jnp.tile(dsa_k_gain[None, :], (1, LANES // DSA_DIM))
    wb_dsa = jnp.transpose(w_branch_dsa.reshape(DSA_KV_HEADS, DSA_GROUP, DSA_DIM, D_MODEL),
                           (1, 0, 2, 3)).reshape(DSA_WIDTH, D_MODEL)
    return (w_all, wuq.astype(jnp.bfloat16), wuk.astype(jnp.bfloat16), wuv.astype(jnp.bfloat16),
            gq, gk, gdq, gdk, wb_dsa.astype(jnp.bfloat16))


def _const_spec(shape):
    return pl.BlockSpec(shape, lambda *_: (0,) * len(shape))


def _layer(x, positions, invf, spread, norm_gain, w_in, b_merge, mla_q_norm, mla_w_uq, mla_kv_norm, mla_w_ukv,
           mla_q_gain, mla_k_gain, dsa_q_gain, dsa_k_gain, w_branch_mla, w_branch_dsa, w_out):
    bsz, s_len, _ = x.shape
    k_top = min(TOPK_MAX, s_len // 4)
    (w_all, wuq, wuk, wuv, gq, gk, gdq, gdk, wb_dsa) = _layout_params(
        w_in, mla_w_uq, mla_w_ukv, mla_q_gain, mla_k_gain, dsa_q_gain, dsa_k_gain, w_branch_dsa)
    bf16, f32 = jnp.bfloat16, jnp.float32
    gmax = lambda g: jnp.max(jnp.abs(g))
    shift_a = (BOUND_MARGIN * MLA_QK ** 0.5 * LOG2_E * gmax(mla_q_gain) * gmax(mla_k_gain)).reshape(1).astype(f32)
    shift_b = (BOUND_MARGIN * DSA_DIM ** 0.5 * LOG2_E * gmax(dsa_q_gain) * gmax(dsa_k_gain)).reshape(1).astype(f32)
    smem = pl.BlockSpec(memory_space=pltpu.SMEM)

    tm = 2 * PROJ_ROWS
    tok = lambda w: pl.BlockSpec((1, tm, w), lambda b, i: (b, i, 0))
    head = pl.BlockSpec((1, MLA_HEADS, tm, LANES), lambda b, i: (b, 0, i, 0))
    outs = pl.pallas_call(
        _proj_kernel,
        grid=(bsz, s_len // tm),
        in_specs=[tok(D_MODEL), pl.BlockSpec((1, 1, tm), lambda b, i: (b, 0, i)),
                  _const_spec(invf.shape), _const_spec(spread.shape), _const_spec((1, D_MODEL)),
                  _const_spec((1, MLA_Q_RANK)), _const_spec((1, MLA_KV_RANK)),
                  _const_spec((2, LANES)), _const_spec((1, LANES)), _const_spec((1, LANES)), _const_spec((1, LANES)),
                  _const_spec((D_MODEL, W_ALL)), _const_spec((MLA_Q_RANK, 2 * MLA_HEADS * LANES)),
                  _const_spec((MLA_KV_RANK, MLA_HEADS * LANES)), _const_spec((MLA_KV_RANK, MLA_WIDTH))],
        out_specs=[head, head, pl.BlockSpec((1, 1, MLA_WIDTH, tm), lambda b, i: (b, i, 0, 0)),
                   head, tok(LANES),
                   pl.BlockSpec((1, 1, LANES, tm), lambda b, i: (b, i, 0, 0)),
                   tok(IDX_HEADS * IDX_DIM), tok(LANES),
                   pl.BlockSpec((1, IDX_HEADS, tm), lambda b, i: (b, 0, i)), tok(W_G)],
        out_shape=[jax.ShapeDtypeStruct((bsz, MLA_HEADS, s_len, LANES), bf16),
                   jax.ShapeDtypeStruct((bsz, MLA_HEADS, s_len, LANES), bf16),
                   jax.ShapeDtypeStruct((bsz, s_len // tm, MLA_WIDTH, tm), bf16),
                   jax.ShapeDtypeStruct((bsz, DSA_HEADS, s_len, LANES), bf16),
                   jax.ShapeDtypeStruct((bsz, s_len, LANES), bf16),
                   jax.ShapeDtypeStruct((bsz, s_len // tm, LANES, tm), bf16),
                   jax.ShapeDtypeStruct((bsz, s_len, IDX_HEADS * IDX_DIM), bf16),
                   jax.ShapeDtypeStruct((bsz, s_len, LANES), bf16),
                   jax.ShapeDtypeStruct((bsz, IDX_HEADS, s_len), f32),
                   jax.ShapeDtypeStruct((bsz, s_len, W_G), bf16)],
        compiler_params=pltpu.CompilerParams(dimension_semantics=("arbitrary", "arbitrary"),
                                             vmem_limit_bytes=VMEM_LIMIT),
        name="proj",
    )(x, positions[:, None, :], invf, spread, norm_gain[None, :], mla_q_norm[None, :], mla_kv_norm[None, :],
      gq, gk, gdq, gdk, w_all, wuq, wuk, wuv)
    qa, ka, vat, qb, kb, vbt, qi, ki, wit, gates = outs

    tq_a, tk_a, hps = 512, 512, 8
    o_a = pl.pallas_call(
        functools.partial(_mla_kernel, tq=tq_a, tk=tk_a, heads=hps),
        grid=(bsz, MLA_HEADS // hps, s_len // tq_a),
        in_specs=[smem, pl.BlockSpec((1, hps, tq_a, LANES), lambda b, hp, i: (b, hp, i, 0)),
                  pl.BlockSpec((1, hps, s_len, LANES), lambda b, hp, i: (b, hp, 0, 0)),
                  pl.BlockSpec((1, s_len // tm, hps * MLA_V, tm), lambda b, hp, i: (b, 0, hp, 0))],
        out_specs=pl.BlockSpec((1, tq_a, hps * MLA_V), lambda b, hp, i: (b, i, hp)),
        out_shape=jax.ShapeDtypeStruct((bsz, s_len, MLA_WIDTH), f32),
        compiler_params=pltpu.CompilerParams(dimension_semantics=("arbitrary",) * 3,
                                             vmem_limit_bytes=VMEM_LIMIT),
        name="mla",
    )(shift_a, qa, ka, vat)

    tq_b, tk_b = 512, 512
    row_b = lambda w: pl.BlockSpec((1, tq_b, w), lambda b, i: (b, i, 0))
    full_b = lambda w: pl.BlockSpec((1, s_len, w), lambda b, i: (b, 0, 0))
    o_b = pl.pallas_call(
        functools.partial(_dsa_kernel, tq=tq_b, tk=tk_b, k_top=k_top),
        grid=(bsz, s_len // tq_b),
        in_specs=[smem, pl.BlockSpec((1, DSA_HEADS, tq_b, LANES), lambda b, i: (b, 0, i, 0)), full_b(LANES),
                  pl.BlockSpec((1, s_len // tm, LANES, tm), lambda b, i: (b, 0, 0, 0)),
                  row_b(IDX_HEADS * IDX_DIM), full_b(LANES),
                  pl.BlockSpec((1, IDX_HEADS, tq_b), lambda b, i: (b, 0, i))],
        out_specs=row_b(DSA_WIDTH),
        out_shape=jax.ShapeDtypeStruct((bsz, s_len, DSA_WIDTH), f32),
        scratch_shapes=[pltpu.VMEM((s_len // tk_b, tk_b, tq_b), f32),
                        pltpu.VMEM((s_len // tk_b, tk_b, tq_b), bf16),
                        pltpu.VMEM((IDX_HEADS * tq_b, LANES), bf16)],
        compiler_params=pltpu.CompilerParams(dimension_semantics=("arbitrary", "arbitrary"),
                                             vmem_limit_bytes=VMEM_LIMIT),
        name="dsa",
    )(shift_b, qb, kb, vbt, qi, ki, wit)

    tm_m = 512
    n_tok = bsz * s_len
    flat = lambda w: pl.BlockSpec((tm_m, w), lambda i: (i, 0))
    out = pl.pallas_call(
        _merge_kernel,
        grid=(n_tok // tm_m,),
        in_specs=[flat(D_MODEL), flat(MLA_WIDTH), flat(DSA_WIDTH), flat(W_G), _const_spec((2, D_MODEL)),
                  _const_spec((MLA_WIDTH, D_MODEL)), _const_spec((DSA_WIDTH, D_MODEL)),
                  _const_spec((D_MODEL, D_MODEL))],
        out_specs=flat(D_MODEL),
        out_shape=jax.ShapeDtypeStruct((n_tok, D_MODEL), f32),
        compiler_params=pltpu.CompilerParams(dimension_semantics=("arbitrary",),
                                             vmem_limit_bytes=VMEM_LIMIT),
        name="merge",
    )(x.reshape(n_tok, D_MODEL), o_a.reshape(n_tok, MLA_WIDTH), o_b.reshape(n_tok, DSA_WIDTH),
      gates.reshape(n_tok, W_G), b_merge, w_branch_mla.astype(bf16), wb_dsa, w_out.astype(bf16))
    return out.reshape(bsz, s_len, D_MODEL)


def kernel(x, positions, norm_gain, w_in, b_merge, mla_q_norm, mla_w_uq, mla_kv_norm, mla_w_ukv,
           mla_q_gain, mla_k_gain, dsa_q_gain, dsa_k_gain, w_branch_mla, w_branch_dsa, w_out):
    invf, spread = _rope_inputs()
    for l in range(norm_gain.shape[0]):
        x = _layer(x, positions, invf, spread, norm_gain[l], w_in[l], b_merge[l], mla_q_norm[l], mla_w_uq[l],
                   mla_kv_norm[l], mla_w_ukv[l], mla_q_gain[l], mla_k_gain[l], dsa_q_gain[l], dsa_k_gain[l],
                   w_branch_mla[l], w_branch_dsa[l], w_out[l])
    return x
```

```python
import functools

import numpy as np
import jax
import jax.numpy as jnp
from jax import lax
from jax.experimental import pallas as pl
from jax.experimental.pallas import tpu as pltpu

D_MODEL = 1024
MLA_HEADS = 8
MLA_Q_RANK = 256
MLA_KV_RANK = 128
MLA_NOPE = 64
MLA_ROPE = 32
MLA_QK = MLA_NOPE + MLA_ROPE
MLA_V = 64
MLA_WIDTH = MLA_HEADS * MLA_V
DSA_HEADS = 8
DSA_KV_HEADS = 2
DSA_GROUP = DSA_HEADS // DSA_KV_HEADS
DSA_DIM = 64
DSA_WIDTH = DSA_HEADS * DSA_DIM
DSA_ROT = DSA_DIM // 4
IDX_HEADS = 8
IDX_DIM = 32
IDX_ROT = IDX_DIM // 4
TOPK_MAX = 256
ROPE_THETA = 500000.0
NORM_EPS = 1e-6

IN_SPLITS = (
    MLA_Q_RANK, MLA_KV_RANK, MLA_ROPE, MLA_WIDTH,
    DSA_HEADS * DSA_DIM, DSA_KV_HEADS * DSA_DIM, DSA_KV_HEADS * DSA_DIM, DSA_WIDTH,
    IDX_HEADS * IDX_DIM, IDX_DIM, IDX_HEADS,
    D_MODEL, D_MODEL,
)

LANES = 128
MXU_WIDTH = 256
VMEM_LIMIT = 56 * 1024 * 1024
INT_MIN = -(2 ** 31)
F32_LOWEST = float(np.finfo(np.float32).min)
BF16_ROWS = 16
FINE_BITS = 17
NEG_BIG = -1e30
LOG2_E = 1.4426950408889634
FAST_MAX_SHIFT = 60.0
BOUND_MARGIN = 1.02

OFF_A = 0
W_A = MLA_Q_RANK + MLA_KV_RANK + LANES
OFF_B = OFF_A + W_A
W_B = DSA_WIDTH + 2 * DSA_KV_HEADS * DSA_DIM
OFF_I = OFF_B + W_B
W_I = IDX_HEADS * IDX_DIM + 2 * LANES
OFF_G = OFF_I + W_I
W_G = MLA_WIDTH + DSA_WIDTH + 2 * D_MODEL
W_ALL = OFF_G + W_G
PROJ_ROWS = 256
GATE_CHUNK = 512
N_GATE_CHUNKS = W_G // GATE_CHUNK


TM_PROJ = 512
TQ_MLA, TK_MLA = 512, 512
MLA_HEADS_PER_STEP = 8
TQ_DSA, TK_DSA = 512, 512
TM_MERGE = 512

ROPE_LAYOUTS = ((LANES, MLA_NOPE, MLA_ROPE, 0), (DSA_DIM, 0, DSA_ROT, MLA_ROPE // 2),
                (IDX_DIM, 0, IDX_ROT, MLA_ROPE // 2 + DSA_ROT // 2))
ROW_A, ROW_D, ROW_I = 0, 1, 2
FREQ_ROWS = 32
ONE_ROW = FREQ_ROWS - 1


def _rms(v, width):
    return lax.rsqrt(jnp.sum(v * v, axis=-1, keepdims=True) * (1.0 / width) + NORM_EPS)


def _rope(v, cos, sin_lo, sin_hi, half):
    return (v * cos + pltpu.roll(v, LANES - half, 1) * sin_lo + pltpu.roll(v, half, 1) * sin_hi)


def _bf16_dot(a, b):
    return jnp.dot(a, b, preferred_element_type=jnp.float32)


def _nt_dot(a, b):
    return lax.dot_general(a, b, (((1,), (1,)), ((), ())), preferred_element_type=jnp.float32)


def _proj_kernel(x_ref, pos_ref, invf_ref, spread_ref, gx_ref, gcq_ref, gckv_ref, gq_ref, gk_ref, gdq_ref, gdk_ref,
                 w_ref, wuq_ref, wuk_ref, wuv_ref,
                 qa_ref, ka_ref, va_ref, qb_ref, kb_ref, vb_ref, qi_ref, ki_ref, wi_ref, g_ref):
    bf16 = jnp.bfloat16
    subs = [slice(r0, r0 + PROJ_ROWS) for r0 in range(0, x_ref.shape[1], PROJ_ROWS)]
    hs = []
    for rows in subs:
        x = x_ref[0, rows]
        hs.append((x * _rms(x, D_MODEL) * gx_ref[...]).astype(bf16))
    pas = [_bf16_dot(h, w_ref[:, OFF_A:OFF_A + W_A]) for h in hs]
    tabs = [_rope_tables(pos_ref[0, :, rows], invf_ref, spread_ref) for rows in subs]
    lat = []
    for pa in pas:
        c_q = pa[:, :MLA_Q_RANK]
        c_kv = pa[:, MLA_Q_RANK:MLA_Q_RANK + MLA_KV_RANK]
        lat.append(((c_q * _rms(c_q, MLA_Q_RANK) * gcq_ref[...]).astype(bf16),
                    (c_kv * _rms(c_kv, MLA_KV_RANK) * gckv_ref[...]).astype(bf16)))
    pbs = [_bf16_dot(h, w_ref[:, OFF_B:OFF_B + W_B]) for h in hs]
    pis = [_bf16_dot(h, w_ref[:, OFF_I:OFF_I + W_I]) for h in hs]
    ups = [(_bf16_dot(cqn, wuq_ref[...]),
            _bf16_dot(ckvn, wuk_ref[...]),
            _bf16_dot(ckvn, wuv_ref[...])) for cqn, ckvn in lat]
    for n, rows in enumerate(subs):
        kpe = pas[n][:, MLA_Q_RANK + MLA_KV_RANK:]
        _proj_epilogue(rows, (tabs[n], kpe) + ups[n] + (pbs[n], pis[n]), gq_ref, gk_ref, gdq_ref, gdk_ref,
                       qa_ref, ka_ref, va_ref, qb_ref, kb_ref, vb_ref, qi_ref, ki_ref, wi_ref)
    for rows, h in zip(subs, hs):
        for c in range(N_GATE_CHUNKS):
            cols = slice(c * GATE_CHUNK, (c + 1) * GATE_CHUNK)
            g_ref[0, rows, cols] = _bf16_dot(h, w_ref[:, OFF_G + c * GATE_CHUNK:OFF_G + (c + 1) * GATE_CHUNK]).astype(g_ref.dtype)


def _rope_tables(pos_row, invf_ref, spread_ref):
    bf16 = jnp.bfloat16
    ang = invf_ref[...] * pos_row.astype(jnp.float32)
    trig = jnp.concatenate([jnp.cos(ang), jnp.sin(ang)], axis=0)
    t1 = trig.astype(bf16)
    r1 = trig - t1.astype(jnp.float32)
    t2 = r1.astype(bf16)
    t3 = (r1 - t2.astype(jnp.float32)).astype(bf16)
    return lax.dot_general(jnp.concatenate([t1, t2, t3], axis=0), spread_ref[...], (((0,), (0,)), ((), ())),
                           preferred_element_type=jnp.float32)


def _proj_epilogue(rows, vals, gq_ref, gk_ref, gdq_ref, gdk_ref,
                   qa_ref, ka_ref, va_ref, qb_ref, kb_ref, vb_ref, qi_ref, ki_ref, wi_ref):
    tabs, kpe, q_up, k_up, v_up, pb, pi = vals
    lane = lax.broadcasted_iota(jnp.int32, (1, LANES), 1)
    bf16 = jnp.bfloat16

    def tables(n):
        return tuple(tabs[:, (3 * n + c) * LANES:(3 * n + c + 1) * LANES] for c in range(3))

    for c in range(MLA_WIDTH // LANES):
        va_ref[0, 0, c * LANES:(c + 1) * LANES, rows] = v_up[:, c * LANES:(c + 1) * LANES].T.astype(bf16)
    cos_a, sl_a, sh_a = tables(ROW_A)
    sin_a = sl_a + sh_a
    q_scale = MLA_QK ** -0.5 * LOG2_E
    kpe_g = kpe * gk_ref[...]
    k_rot = (pltpu.roll(kpe_g, LANES - MLA_ROPE // 2, 1) * sl_a + pltpu.roll(kpe_g, MLA_ROPE // 2, 1) * sh_a)
    for hd in range(MLA_HEADS):
        qh = q_up[:, hd * LANES:(hd + 1) * LANES]
        q_sw = q_up[:, (MLA_HEADS + hd) * LANES:(MLA_HEADS + hd + 1) * LANES]
        q_rot = qh * gq_ref[0:1, :] * cos_a + q_sw * gq_ref[1:2, :] * sin_a
        qa_ref[0, hd, rows] = (q_rot * (_rms(qh, MLA_QK) * q_scale)).astype(bf16)
        kh = k_up[:, hd * LANES:(hd + 1) * LANES] + kpe
        ka_ref[0, hd, rows] = ((kh * gk_ref[...] * cos_a + k_rot) * _rms(kh, MLA_QK)).astype(bf16)

    cos_d, sl_d, sh_d = tables(ROW_D)
    low = lane < DSA_DIM

    def dsa_norm_rope(blk, gain):
        sq = blk * blk
        r_lo = lax.rsqrt(jnp.sum(jnp.where(low, sq, 0.0), axis=-1, keepdims=True) * (1.0 / DSA_DIM) + NORM_EPS)
        r_hi = lax.rsqrt(jnp.sum(jnp.where(low, 0.0, sq), axis=-1, keepdims=True) * (1.0 / DSA_DIM) + NORM_EPS)
        return _rope(blk * jnp.where(low, r_lo, r_hi) * gain, cos_d, sl_d, sh_d, DSA_ROT // 2)

    dsa_scale = DSA_DIM ** -0.5 * LOG2_E
    for j in range(DSA_WIDTH // LANES):
        blk = pb[:, j * LANES:(j + 1) * LANES]
        qn = (dsa_norm_rope(blk, gdq_ref[...]) * dsa_scale).astype(bf16)
        qb_ref[0, j, rows] = jnp.where(low, qn, jnp.zeros_like(qn))
        qb_ref[0, DSA_GROUP + j, rows] = jnp.where(low, jnp.zeros_like(qn), qn)
    kb_ref[0, rows] = dsa_norm_rope(pb[:, DSA_WIDTH:DSA_WIDTH + LANES], gdk_ref[...]).astype(bf16)
    vb_ref[0, 0, :, rows] = pb[:, DSA_WIDTH + LANES:].T.astype(bf16)

    cos_i, sl_i, sh_i = tables(ROW_I)
    n_qi = IDX_HEADS * IDX_DIM // LANES
    for j in range(n_qi + 1):
        blk = _rope(pi[:, j * LANES:(j + 1) * LANES], cos_i, sl_i, sh_i, IDX_ROT // 2).astype(bf16)
        if j < n_qi:
            qi_ref[0, rows, j * LANES:(j + 1) * LANES] = blk
        else:
            ki_ref[0, rows] = blk
    wi_t = (pi[:, (n_qi + 1) * LANES:] * (IDX_HEADS ** -0.5 * IDX_DIM ** -0.5)).T
    wi_ref[0, :, rows] = wi_t[0:IDX_HEADS, :]


def _softmax_tile(s, m, l, shift):
    if shift is None:
        m_new = jnp.maximum(m, jnp.max(s, axis=0, keepdims=True))
        alpha = jnp.exp2(m - m_new)
        p = jnp.exp2(s - m_new)
        return m_new, alpha * l + jnp.sum(p, axis=0, keepdims=True), alpha, p.astype(jnp.bfloat16)
    p = jnp.exp2(s - shift)
    return m, l + jnp.sum(p, axis=0, keepdims=True), None, p.astype(jnp.bfloat16)


def _by_shift(shift_ref, body):
    shift = shift_ref[0]

    @pl.when(shift <= FAST_MAX_SHIFT)
    def _():
        body(shift)

    @pl.when(shift > FAST_MAX_SHIFT)
    def _():
        body(None)


def _mla_kernel(shift_ref, q_ref, k_ref, vt_ref, o_ref, *, tq, tk, heads):
    i = pl.program_id(2)
    q_start = i * tq
    n_full = q_start // tk
    n_all = (q_start + tq + tk - 1) // tk
    tv = vt_ref.shape[-1]
    kpos0 = lax.broadcasted_iota(jnp.int32, (tk, tq), 0)
    qpos = q_start + lax.broadcasted_iota(jnp.int32, (tk, tq), 1)

    def attend(shift):
        def step(j, carry, masked):
            off = pl.multiple_of(j * tk, tk)
            scores = [_nt_dot(k_ref[0, hd, pl.ds(off, tk), :], q_ref[0, hd]) for hd in range(heads)]
            out = []
            visible = kpos0 + off <= qpos if masked else None
            shift_t = jnp.where(visible, shift, -NEG_BIG) if masked and shift is not None else shift
            for hd in range(heads):
                m, l, acc = carry[hd]
                s = scores[hd]
                if masked and shift is None:
                    s = jnp.where(visible, s, NEG_BIG)
                m, l, alpha, pb = _softmax_tile(s, m, l, shift_t)
                if alpha is not None:
                    acc = alpha * acc
                for c in range(tk // tv):
                    vt = vt_ref[0, j * (tk // tv) + c, hd * MLA_V:(hd + 1) * MLA_V, :]
                    acc = acc + _bf16_dot(vt, pb[c * tv:(c + 1) * tv])
                out.append((m, l, acc))
            return tuple(out)

        init = tuple((jnp.full((1, tq), NEG_BIG, jnp.float32), jnp.zeros((1, tq), jnp.float32),
                      jnp.zeros((MLA_V, tq), jnp.float32)) for _ in range(heads))
        carry = lax.fori_loop(0, n_full, functools.partial(step, masked=False), init)
        carry = lax.fori_loop(n_full, n_all, functools.partial(step, masked=True), carry)
        outs = [acc / l for (_, l, acc) in carry]
        for c in range(heads // 2):
            o_ref[0, :, c * LANES:(c + 1) * LANES] = jnp.concatenate(outs[2 * c:2 * c + 2], axis=0).T

    _by_shift(shift_ref, attend)


def _dsa_kernel(shift_ref, qb_ref, kb_ref, vt_ref, qi_ref, ki_ref, wt_ref, o_ref,
                sc_ref, sb_ref, qim_ref, *, tq, tk, k_top):
    i = pl.program_id(1)
    q_start = i * tq
    n_kv = (q_start + tq + tk - 1) // tk
    n_heads = IDX_HEADS
    tv = vt_ref.shape[-1]

    lane_q = lax.broadcasted_iota(jnp.int32, (tq, LANES), 1)
    per_blk = LANES // IDX_DIM
    for hd in range(IDX_HEADS):
        blk = qi_ref[0, :, (hd // per_blk) * LANES:(hd // per_blk + 1) * LANES]
        slot = hd % per_blk
        in_head = (lane_q >= slot * IDX_DIM) & (lane_q < (slot + 1) * IDX_DIM)
        qim_ref[hd * tq:(hd + 1) * tq, :] = jnp.where(in_head, blk, jnp.zeros_like(blk))

    kpos0 = lax.broadcasted_iota(jnp.int32, (tk, tq), 0)
    qpos = q_start + lax.broadcasted_iota(jnp.int32, (tk, tq), 1)
    w_rows = [wt_ref[0, hd:hd + 1, :] for hd in range(IDX_HEADS)]

    def score_tile(j, _):
        off = pl.multiple_of(j * tk, tk)
        s_all = _nt_dot(ki_ref[0, pl.ds(off, tk), :], qim_ref[...])
        sc = w_rows[0] * jnp.maximum(s_all[:, 0:tq], 0.0)
        for hd in range(1, n_heads):
            sc = sc + w_rows[hd] * jnp.maximum(s_all[:, hd * tq:(hd + 1) * tq], 0.0)
        sc = jnp.where(kpos0 + off <= qpos, sc, -jnp.inf)
        sc_ref[j] = sc
        sb_ref[j] = sc.astype(jnp.bfloat16)
        return 0

    lax.fori_loop(0, n_kv, score_tile, 0)

    def pattern_value(u):
        key = u ^ INT_MIN
        return lax.bitcast_convert_type(key ^ ((key >> 31) & 0x7FFFFFFF), jnp.float32)

    def count(pred):
        def body(j, acc):
            hit = jnp.where(pred(sc_ref[j], j), 1, 0)
            return acc + jnp.sum(hit.reshape(tk // 8, 8, tq), axis=0)
        acc = lax.fori_loop(0, n_kv, body, jnp.zeros((8, tq), jnp.int32))
        return jnp.sum(acc, axis=0, keepdims=True)

    def count_coarse(cand_b):
        one, zero = jnp.ones((), jnp.bfloat16), jnp.zeros((), jnp.bfloat16)

        def body(j, acc):
            part = None
            for r in range(tk // BF16_ROWS):
                rows = sb_ref[j, r * BF16_ROWS:(r + 1) * BF16_ROWS, :]
                hit = jnp.where(rows >= cand_b, one, zero)
                part = hit if part is None else part + hit
            return acc + part.astype(jnp.float32)
        acc = lax.fori_loop(0, n_kv, body, jnp.zeros((BF16_ROWS, tq), jnp.float32))
        return jnp.sum(acc, axis=0, keepdims=True)

    def coarse_bit(it, t16):
        cand = t16 | jnp.left_shift(jnp.int32(1), 15 - it)
        cand32 = jnp.left_shift(cand, 16) | jnp.where(cand < 0x8000, 0xFFFF, 0)
        cnt = count_coarse(pattern_value(cand32).astype(jnp.bfloat16))
        return jnp.where(cnt >= k_top, cand, t16)

    t16 = lax.fori_loop(0, 16, coarse_bit, jnp.zeros((1, tq), jnp.int32))
    few = t16 < 0x0080
    base = jnp.left_shift(t16, 16) + jnp.where(t16 < 0x8000, 0x7FFF, -0x8000)
    base = jnp.where(few, 0x00800000, base)

    def fine_bit(it, carry):
        t, cnt_t = carry
        cand = t | jnp.left_shift(jnp.int32(1), FINE_BITS - 1 - it)
        cand_f = pattern_value(base + cand)
        cnt = count(lambda sc, j: sc >= cand_f)
        take = cnt >= k_top
        return jnp.where(take, cand, t), jnp.where(take, cnt, cnt_t)

    t, cnt_ge = lax.fori_loop(0, FINE_BITS, fine_bit,
                              (jnp.zeros((1, tq), jnp.int32), jnp.full((1, tq), k_top, jnp.int32)))
    thr = jnp.where(few, F32_LOWEST, pattern_value(base + t))
    tied = jnp.logical_not(few) & ((cnt_ge > k_top) | (t == 0))

    def tie_cut():
        cnt_gt = count(lambda sc, j: sc > thr)
        need = k_top - cnt_gt

        def index_bit(it, c):
            cand = c | jnp.left_shift(jnp.int32(1), 12 - it)
            below = count(lambda sc, j: (sc == thr) & (kpos0 + j * tk < cand))
            return jnp.where(below < need, cand, c)

        c = lax.fori_loop(0, 13, index_bit, jnp.zeros((1, tq), jnp.int32))
        return jnp.where(tied, c, jnp.int32(2 ** 30))

    cut = lax.cond(jnp.max(tied.astype(jnp.int32)) > 0, tie_cut,
                   lambda: jnp.full((1, tq), 2 ** 30, jnp.int32))

    hpd = max(1, MXU_WIDTH // tq)
    n_dots = n_heads // hpd
    wd = hpd * tq

    def attend_all(shift):
        def attend(j, carry):
            off = pl.multiple_of(j * tk, tk)
            kt = kb_ref[0, pl.ds(off, tk), :]
            scores = [_nt_dot(kt, qb_ref[0, hpd * d:hpd * (d + 1)].reshape(wd, LANES)) for d in range(n_dots)]
            sc = sc_ref[j]
            sel = (sc > thr) | ((sc == thr) & (kpos0 + off <= cut))
            sel_d = jnp.concatenate([sel] * hpd, axis=1) if hpd > 1 else sel
            shift_d = None if shift is None else jnp.where(sel_d, shift, -NEG_BIG)
            out = []
            for d in range(n_dots):
                m, l, acc = carry[d]
                s = jnp.where(sel_d, scores[d], NEG_BIG) if shift is None else scores[d]
                m, l, alpha, pb = _softmax_tile(s, m, l, shift_d)
                if alpha is not None:
                    acc = alpha * acc
                for c in range(tk // tv):
                    acc = acc + _bf16_dot(vt_ref[0, j * (tk // tv) + c], pb[c * tv:(c + 1) * tv])
                out.append((m, l, acc))
            return tuple(out)

        init = tuple((jnp.full((1, wd), NEG_BIG, jnp.float32), jnp.zeros((1, wd), jnp.float32),
                      jnp.zeros((LANES, wd), jnp.float32)) for _ in range(n_dots))
        carry = lax.fori_loop(0, n_kv, attend, init)
        o = jnp.concatenate([acc / l for (_, l, acc) in carry], axis=1)
        for j in range(DSA_GROUP):
            lo = o[:, j * tq:(j + 1) * tq].T
            hi = o[:, (DSA_GROUP + j) * tq:(DSA_GROUP + j + 1) * tq].T
            o_ref[0, :, j * LANES:(j + 1) * LANES] = jnp.where(lane_q < DSA_DIM, lo, hi)

    _by_shift(shift_ref, attend_all)


def _merge_kernel(x_ref, oa_ref, ob_ref, g_ref, bm_ref, wa_ref, wb_ref, wo_ref, out_ref):
    f32 = jnp.float32
    ga = g_ref[:, 0:MLA_WIDTH].astype(f32)
    gb = g_ref[:, MLA_WIDTH:MLA_WIDTH + DSA_WIDTH].astype(f32)
    ma = g_ref[:, MLA_WIDTH + DSA_WIDTH:MLA_WIDTH + DSA_WIDTH + D_MODEL].astype(f32)
    mb = g_ref[:, MLA_WIDTH + DSA_WIDTH + D_MODEL:].astype(f32)
    ya = _bf16_dot((oa_ref[...] * (ga * jax.nn.sigmoid(ga))).astype(jnp.bfloat16), wa_ref[...])
    yb = _bf16_dot((ob_ref[...] * (gb * jax.nn.sigmoid(gb))).astype(jnp.bfloat16), wb_ref[...])
    merged = jax.nn.sigmoid(ma + bm_ref[0:1, :]) * ya + jax.nn.sigmoid(mb + bm_ref[1:2, :]) * yb
    out_ref[...] = x_ref[...] + _bf16_dot(merged.astype(jnp.bfloat16), wo_ref[...])


def _inv_freq(rot_dim):
    return ROPE_THETA ** (-jnp.arange(0, rot_dim, 2, dtype=jnp.float32) / rot_dim)


def _rope_inputs():
    invf = jnp.concatenate([_inv_freq(rot) for (_, _, rot, _) in ROPE_LAYOUTS])
    invf = jnp.pad(invf, (0, FREQ_ROWS - invf.shape[0]))[:, None]
    spread = np.zeros((2 * FREQ_ROWS, 3 * len(ROPE_LAYOUTS) * LANES), np.float32)
    for n, (period, start, rot, row0) in enumerate(ROPE_LAYOUTS):
        half = rot // 2
        for lane in range(LANES):
            p = lane % period - start
            cos_col, lo_col, hi_col = ((3 * n + c) * LANES + lane for c in range(3))
            if 0 <= p < rot:
                spread[row0 + p % half, cos_col] = 1.0
                spread[FREQ_ROWS + row0 + p % half, lo_col if p < half else hi_col] = -1.0 if p < half else 1.0
            else:
                spread[ONE_ROW, cos_col] = 1.0
    return invf, jnp.asarray(np.tile(spread, (3, 1)), jnp.bfloat16)


def _pad_lanes(a, start=0):
    return jnp.pad(a, ((0, 0), (start, LANES - start - a.shape[1])))


def _layout_params(w_in, mla_w_uq, mla_w_ukv, mla_q_gain, mla_k_gain, dsa_q_gain, dsa_k_gain,
                   w_branch_dsa):
    offs = np.concatenate([[0], np.cumsum(IN_SPLITS)])
    (c_q, c_kv, k_pe, gate_a, q_b, k_b, v_b, gate_b, q_i, k_i, w_i, m_a, m_b) = [
        w_in[:, offs[n]:offs[n + 1]] for n in range(len(IN_SPLITS))]

    def pair_heads(w):
        rows = w.shape[0]
        w = w.reshape(rows, DSA_KV_HEADS, DSA_GROUP, DSA_DIM)
        return jnp.transpose(w, (0, 2, 1, 3)).reshape(rows, DSA_WIDTH)

    w_all = jnp.concatenate([
        c_q, c_kv, _pad_lanes(k_pe, MLA_NOPE),
        pair_heads(q_b), k_b, v_b,
        q_i, jnp.tile(k_i, (1, LANES // IDX_DIM)), _pad_lanes(w_i),
        gate_a, pair_heads(gate_b), m_a, m_b], axis=1).astype(jnp.bfloat16)
    uq = mla_w_uq.reshape(MLA_Q_RANK, MLA_HEADS, MLA_QK)
    half = MLA_ROPE // 2

    def swap_rope(a):
        return jnp.concatenate([jnp.zeros_like(a[..., :MLA_NOPE]), a[..., MLA_NOPE + half:], a[..., MLA_NOPE:MLA_NOPE + half]], -1)

    pad_qk = lambda a: jnp.pad(a, ((0, 0), (0, 0), (0, LANES - MLA_QK))).reshape(MLA_Q_RANK, MLA_HEADS * LANES)
    wuq = jnp.concatenate([pad_qk(uq), pad_qk(swap_rope(uq))], axis=1)
    ukv = mla_w_ukv.reshape(MLA_KV_RANK, MLA_HEADS, MLA_NOPE + MLA_V)
    wuk = jnp.pad(ukv[:, :, :MLA_NOPE], ((0, 0), (0, 0), (0, LANES - MLA_NOPE))).reshape(MLA_KV_RANK, MLA_HEADS * LANES)
    wuv = ukv[:, :, MLA_NOPE:].reshape(MLA_KV_RANK, MLA_WIDTH)
    gq = jnp.concatenate([_pad_lanes(mla_q_gain[None, :]), _pad_lanes(swap_rope(mla_q_gain[None, :]))], axis=0)
    gk = _pad_lanes(mla_k_gain[None, :])
    gdq = jnp.tile(dsa_q_gain[None, :], (1, LANES // DSA_DIM))
    gdk = jnp.tile(dsa_k_gain[None, :], (1, LANES // DSA_DIM))
    wb_dsa = jnp.transpose(w_branch_dsa.reshape(DSA_KV_HEADS, DSA_GROUP, DSA_DIM, D_MODEL),
                           (1, 0, 2, 3)).reshape(DSA_WIDTH, D_MODEL)
    return (w_all, wuq.astype(jnp.bfloat16), wuk.astype(jnp.bfloat16), wuv.astype(jnp.bfloat16),
            gq, gk, gdq, gdk, wb_dsa.astype(jnp.bfloat16))


def _const_spec(shape):
    return pl.BlockSpec(shape, lambda *_: (0,) * len(shape))


def _layer(x, positions, invf, spread, norm_gain, w_in, b_merge, mla_q_norm, mla_w_uq, mla_kv_norm, mla_w_ukv,
           mla_q_gain, mla_k_gain, dsa_q_gain, dsa_k_gain, w_branch_mla, w_branch_dsa, w_out):
    bsz, s_len, d_model = x.shape
    assert d_model == D_MODEL and TM_PROJ % PROJ_ROWS == 0 and (bsz * s_len) % TM_MERGE == 0
    assert all(s_len % t == 0 for t in (TM_PROJ, TQ_MLA, TK_MLA, TQ_DSA, TK_DSA))
    k_top = min(TOPK_MAX, s_len // 4)
    (w_all, wuq, wuk, wuv, gq, gk, gdq, gdk, wb_dsa) = _layout_params(
        w_in, mla_w_uq, mla_w_ukv, mla_q_gain, mla_k_gain, dsa_q_gain, dsa_k_gain, w_branch_dsa)
    bf16, f32 = jnp.bfloat16, jnp.float32
    gmax = lambda g: jnp.max(jnp.abs(g))
    shift_a = (BOUND_MARGIN * MLA_QK ** 0.5 * LOG2_E * gmax(mla_q_gain) * gmax(mla_k_gain)).reshape(1).astype(f32)
    shift_b = (BOUND_MARGIN * DSA_DIM ** 0.5 * LOG2_E * gmax(dsa_q_gain) * gmax(dsa_k_gain)).reshape(1).astype(f32)
    smem = pl.BlockSpec(memory_space=pltpu.SMEM)

    tm = TM_PROJ
    tok = lambda w: pl.BlockSpec((1, tm, w), lambda b, i: (b, i, 0))
    head = pl.BlockSpec((1, MLA_HEADS, tm, LANES), lambda b, i: (b, 0, i, 0))
    outs = pl.pallas_call(
        _proj_kernel,
        grid=(bsz, s_len // tm),
        in_specs=[tok(D_MODEL), pl.BlockSpec((1, 1, tm), lambda b, i: (b, 0, i)),
                  _const_spec(invf.shape), _const_spec(spread.shape), _const_spec((1, D_MODEL)),
                  _const_spec((1, MLA_Q_RANK)), _const_spec((1, MLA_KV_RANK)),
                  _const_spec((2, LANES)), _const_spec((1, LANES)), _const_spec((1, LANES)), _const_spec((1, LANES)),
                  _const_spec((D_MODEL, W_ALL)), _const_spec((MLA_Q_RANK, 2 * MLA_HEADS * LANES)),
                  _const_spec((MLA_KV_RANK, MLA_HEADS * LANES)), _const_spec((MLA_KV_RANK, MLA_WIDTH))],
        out_specs=[head, head, pl.BlockSpec((1, 1, MLA_WIDTH, tm), lambda b, i: (b, i, 0, 0)),
                   head, tok(LANES),
                   pl.BlockSpec((1, 1, LANES, tm), lambda b, i: (b, i, 0, 0)),
                   tok(IDX_HEADS * IDX_DIM), tok(LANES),
                   pl.BlockSpec((1, IDX_HEADS, tm), lambda b, i: (b, 0, i)), tok(W_G)],
        out_shape=[jax.ShapeDtypeStruct((bsz, MLA_HEADS, s_len, LANES), bf16),
                   jax.ShapeDtypeStruct((bsz, MLA_HEADS, s_len, LANES), bf16),
                   jax.ShapeDtypeStruct((bsz, s_len // tm, MLA_WIDTH, tm), bf16),
                   jax.ShapeDtypeStruct((bsz, DSA_HEADS, s_len, LANES), bf16),
                   jax.ShapeDtypeStruct((bsz, s_len, LANES), bf16),
                   jax.ShapeDtypeStruct((bsz, s_len // tm, LANES, tm), bf16),
                   jax.ShapeDtypeStruct((bsz, s_len, IDX_HEADS * IDX_DIM), bf16),
                   jax.ShapeDtypeStruct((bsz, s_len, LANES), bf16),
                   jax.ShapeDtypeStruct((bsz, IDX_HEADS, s_len), f32),
                   jax.ShapeDtypeStruct((bsz, s_len, W_G), bf16)],
        compiler_params=pltpu.CompilerParams(dimension_semantics=("arbitrary", "arbitrary"),
                                             vmem_limit_bytes=VMEM_LIMIT),
        name="proj",
    )(x, positions[:, None, :], invf, spread, norm_gain[None, :], mla_q_norm[None, :], mla_kv_norm[None, :],
      gq, gk, gdq, gdk, w_all, wuq, wuk, wuv)
    qa, ka, vat, qb, kb, vbt, qi, ki, wit, gates = outs

    tq_a, tk_a, hps = TQ_MLA, TK_MLA, MLA_HEADS_PER_STEP
    o_a = pl.pallas_call(
        functools.partial(_mla_kernel, tq=tq_a, tk=tk_a, heads=hps),
        grid=(bsz, MLA_HEADS // hps, s_len // tq_a),
        in_specs=[smem, pl.BlockSpec((1, hps, tq_a, LANES), lambda b, hp, i: (b, hp, i, 0)),
                  pl.BlockSpec((1, hps, s_len, LANES), lambda b, hp, i: (b, hp, 0, 0)),
                  pl.BlockSpec((1, s_len // tm, hps * MLA_V, tm), lambda b, hp, i: (b, 0, hp, 0))],
        out_specs=pl.BlockSpec((1, tq_a, hps * MLA_V), lambda b, hp, i: (b, i, hp)),
        out_shape=jax.ShapeDtypeStruct((bsz, s_len, MLA_WIDTH), f32),
        compiler_params=pltpu.CompilerParams(dimension_semantics=("arbitrary",) * 3,
                                             vmem_limit_bytes=VMEM_LIMIT),
        name="mla",
    )(shift_a, qa, ka, vat)

    tq_b, tk_b = TQ_DSA, TK_DSA
    row_b = lambda w: pl.BlockSpec((1, tq_b, w), lambda b, i: (b, i, 0))
    full_b = lambda w: pl.BlockSpec((1, s_len, w), lambda b, i: (b, 0, 0))
    o_b = pl.pallas_call(
        functools.partial(_dsa_kernel, tq=tq_b, tk=tk_b, k_top=k_top),
        grid=(bsz, s_len // tq_b),
        in_specs=[smem, pl.BlockSpec((1, DSA_HEADS, tq_b, LANES), lambda b, i: (b, 0, i, 0)), full_b(LANES),
                  pl.BlockSpec((1, s_len // tm, LANES, tm), lambda b, i: (b, 0, 0, 0)),
                  row_b(IDX_HEADS * IDX_DIM), full_b(LANES),
                  pl.BlockSpec((1, IDX_HEADS, tq_b), lambda b, i: (b, 0, i))],
        out_specs=row_b(DSA_WIDTH),
        out_shape=jax.ShapeDtypeStruct((bsz, s_len, DSA_WIDTH), f32),
        scratch_shapes=[pltpu.VMEM((s_len // tk_b, tk_b, tq_b), f32),
                        pltpu.VMEM((s_len // tk_b, tk_b, tq_b), bf16),
                        pltpu.VMEM((IDX_HEADS * tq_b, LANES), bf16)],
        compiler_params=pltpu.CompilerParams(dimension_semantics=("arbitrary", "arbitrary"),
                                             vmem_limit_bytes=VMEM_LIMIT),
        name="dsa",
    )(shift_b, qb, kb, vbt, qi, ki, wit)

    tm_m = TM_MERGE
    n_tok = bsz * s_len
    flat = lambda w: pl.BlockSpec((tm_m, w), lambda i: (i, 0))
    out = pl.pallas_call(
        _merge_kernel,
        grid=(n_tok // tm_m,),
        in_specs=[flat(D_MODEL), flat(MLA_WIDTH), flat(DSA_WIDTH), flat(W_G), _const_spec((2, D_MODEL)),
                  _const_spec((MLA_WIDTH, D_MODEL)), _const_spec((DSA_WIDTH, D_MODEL)),
                  _const_spec((D_MODEL, D_MODEL))],
        out_specs=flat(D_MODEL),
        out_shape=jax.ShapeDtypeStruct((n_tok, D_MODEL), f32),
        compiler_params=pltpu.CompilerParams(dimension_semantics=("arbitrary",),
                                             vmem_limit_bytes=VMEM_LIMIT),
        name="merge",
    )(x.reshape(n_tok, D_MODEL), o_a.reshape(n_tok, MLA_WIDTH), o_b.reshape(n_tok, DSA_WIDTH),
      gates.reshape(n_tok, W_G), b_merge, w_branch_mla.astype(bf16), wb_dsa, w_out.astype(bf16))
    return out.reshape(bsz, s_len, D_MODEL)


def kernel(x, positions, norm_gain, w_in, b_merge, mla_q_norm, mla_w_uq, mla_kv_norm, mla_w_ukv,
           mla_q_gain, mla_k_gain, dsa_q_gain, dsa_k_gain, w_branch_mla, w_branch_dsa, w_out):
    invf, spread = _rope_inputs()
    for l in range(norm_gain.shape[0]):
        x = _layer(x, positions, invf, spread, norm_gain[l], w_in[l], b_merge[l], mla_q_norm[l], mla_w_uq[l],
                   mla_kv_norm[l], mla_w_ukv[l], mla_q_gain[l], mla_k_gain[l], dsa_q_gain[l], dsa_k_gain[l],
                   w_branch_mla[l], w_branch_dsa[l], w_out[l])
    return x
```

```python
import functools

import numpy as np
import jax
import jax.numpy as jnp
from jax import lax
from jax.experimental import pallas as pl
from jax.experimental.pallas import tpu as pltpu

D_MODEL = 1024
MLA_HEADS = 8
MLA_Q_RANK = 256
MLA_KV_RANK = 128
MLA_NOPE = 64
MLA_ROPE = 32
MLA_QK = MLA_NOPE + MLA_ROPE
MLA_V = 64
MLA_WIDTH = MLA_HEADS * MLA_V
DSA_HEADS = 8
DSA_KV_HEADS = 2
DSA_GROUP = DSA_HEADS // DSA_KV_HEADS
DSA_DIM = 64
DSA_WIDTH = DSA_HEADS * DSA_DIM
DSA_ROT = DSA_DIM // 4
IDX_HEADS = 8
IDX_DIM = 32
IDX_ROT = IDX_DIM // 4
TOPK_MAX = 256
ROPE_THETA = 500000.0
NORM_EPS = 1e-6

IN_SPLITS = (
    MLA_Q_RANK, MLA_KV_RANK, MLA_ROPE, MLA_WIDTH,
    DSA_HEADS * DSA_DIM, DSA_KV_HEADS * DSA_DIM, DSA_KV_HEADS * DSA_DIM, DSA_WIDTH,
    IDX_HEADS * IDX_DIM, IDX_DIM, IDX_HEADS,
    D_MODEL, D_MODEL,
)

LANES = 128
MXU_WIDTH = 256
VMEM_LIMIT = 56 * 1024 * 1024
INT_MIN = -(2 ** 31)
F32_LOWEST = float(np.finfo(np.float32).min)
BF16_ROWS = 16
FINE_BITS = 17
NEG_BIG = -1e30
LOG2_E = 1.4426950408889634
FAST_MAX_SHIFT = 60.0
BOUND_MARGIN = 1.02

OFF_A = 0
W_A = MLA_Q_RANK + MLA_KV_RANK + LANES
OFF_B = OFF_A + W_A
W_B = DSA_WIDTH + 2 * DSA_KV_HEADS * DSA_DIM
OFF_I = OFF_B + W_B
W_I = IDX_HEADS * IDX_DIM + 2 * LANES
OFF_G = OFF_I + W_I
W_G = MLA_WIDTH + DSA_WIDTH + 2 * D_MODEL
W_ALL = OFF_G + W_G
PROJ_ROWS = 256
GATE_CHUNK = 512
N_GATE_CHUNKS = W_G // GATE_CHUNK


TM_PROJ = 512
TQ_MLA, TK_MLA = 512, 512
MLA_HEADS_PER_STEP = 8
TQ_DSA, TK_DSA = 512, 512
TM_MERGE = 1024

ROPE_LAYOUTS = ((LANES, MLA_NOPE, MLA_ROPE, 0), (DSA_DIM, 0, DSA_ROT, MLA_ROPE // 2),
                (IDX_DIM, 0, IDX_ROT, MLA_ROPE // 2 + DSA_ROT // 2))
ROW_A, ROW_D, ROW_I = 0, 1, 2
FREQ_ROWS = 32
ONE_ROW = FREQ_ROWS - 1


def _rms(v, width):
    return lax.rsqrt(jnp.sum(v * v, axis=-1, keepdims=True) * (1.0 / width) + NORM_EPS)


def _rope(v, cos, sin_lo, sin_hi, half):
    return (v * cos + pltpu.roll(v, LANES - half, 1) * sin_lo + pltpu.roll(v, half, 1) * sin_hi)


def _bf16_dot(a, b):
    return jnp.dot(a, b, preferred_element_type=jnp.float32)


def _nt_dot(a, b):
    return lax.dot_general(a, b, (((1,), (1,)), ((), ())), preferred_element_type=jnp.float32)


def _proj_kernel(x_ref, pos_ref, invf_ref, spread_ref, gx_ref, gcq_ref, gckv_ref, gq_ref, gk_ref, gdq_ref, gdk_ref,
                 w_ref, wuq_ref, wuk_ref, wuv_ref,
                 qa_ref, ka_ref, va_ref, qb_ref, kb_ref, vb_ref, qi_ref, ki_ref, wi_ref, g_ref):
    bf16 = jnp.bfloat16
    subs = [slice(r0, r0 + PROJ_ROWS) for r0 in range(0, x_ref.shape[1], PROJ_ROWS)]
    hs = []
    for rows in subs:
        x = x_ref[0, rows]
        hs.append((x * _rms(x, D_MODEL) * gx_ref[...]).astype(bf16))
    pas = [_bf16_dot(h, w_ref[:, OFF_A:OFF_A + W_A]) for h in hs]
    tabs = [_rope_tables(pos_ref[0, :, rows], invf_ref, spread_ref) for rows in subs]
    lat = []
    for pa in pas:
        c_q = pa[:, :MLA_Q_RANK]
        c_kv = pa[:, MLA_Q_RANK:MLA_Q_RANK + MLA_KV_RANK]
        lat.append(((c_q * _rms(c_q, MLA_Q_RANK) * gcq_ref[...]).astype(bf16),
                    (c_kv * _rms(c_kv, MLA_KV_RANK) * gckv_ref[...]).astype(bf16)))
    pbs = [_bf16_dot(h, w_ref[:, OFF_B:OFF_B + W_B]) for h in hs]
    pis = [_bf16_dot(h, w_ref[:, OFF_I:OFF_I + W_I]) for h in hs]
    ups = [(_bf16_dot(cqn, wuq_ref[...]),
            _bf16_dot(ckvn, wuk_ref[...]),
            _bf16_dot(ckvn, wuv_ref[...])) for cqn, ckvn in lat]
    for n, rows in enumerate(subs):
        kpe = pas[n][:, MLA_Q_RANK + MLA_KV_RANK:]
        _proj_epilogue(rows, (tabs[n], kpe) + ups[n] + (pbs[n], pis[n]), gq_ref, gk_ref, gdq_ref, gdk_ref,
                       qa_ref, ka_ref, va_ref, qb_ref, kb_ref, vb_ref, qi_ref, ki_ref, wi_ref)
    for rows, h in zip(subs, hs):
        for c in range(N_GATE_CHUNKS):
            cols = slice(c * GATE_CHUNK, (c + 1) * GATE_CHUNK)
            g_ref[0, rows, cols] = _bf16_dot(h, w_ref[:, OFF_G + c * GATE_CHUNK:OFF_G + (c + 1) * GATE_CHUNK]).astype(g_ref.dtype)


def _rope_tables(pos_row, invf_ref, spread_ref):
    bf16 = jnp.bfloat16
    ang = invf_ref[...] * pos_row.astype(jnp.float32)
    trig = jnp.concatenate([jnp.cos(ang), jnp.sin(ang)], axis=0)
    t1 = trig.astype(bf16)
    r1 = trig - t1.astype(jnp.float32)
    t2 = r1.astype(bf16)
    t3 = (r1 - t2.astype(jnp.float32)).astype(bf16)
    return lax.dot_general(jnp.concatenate([t1, t2, t3], axis=0), spread_ref[...], (((0,), (0,)), ((), ())),
                           preferred_element_type=jnp.float32)


def _proj_epilogue(rows, vals, gq_ref, gk_ref, gdq_ref, gdk_ref,
                   qa_ref, ka_ref, va_ref, qb_ref, kb_ref, vb_ref, qi_ref, ki_ref, wi_ref):
    tabs, kpe, q_up, k_up, v_up, pb, pi = vals
    lane = lax.broadcasted_iota(jnp.int32, (1, LANES), 1)
    bf16 = jnp.bfloat16

    def tables(n):
        return tuple(tabs[:, (3 * n + c) * LANES:(3 * n + c + 1) * LANES] for c in range(3))

    for c in range(MLA_WIDTH // LANES):
        va_ref[0, 0, c * LANES:(c + 1) * LANES, rows] = v_up[:, c * LANES:(c + 1) * LANES].T.astype(bf16)
    cos_a, sl_a, sh_a = tables(ROW_A)
    sin_a = sl_a + sh_a
    q_scale = MLA_QK ** -0.5 * LOG2_E
    kpe_g = kpe * gk_ref[...]
    k_rot = (pltpu.roll(kpe_g, LANES - MLA_ROPE // 2, 1) * sl_a + pltpu.roll(kpe_g, MLA_ROPE // 2, 1) * sh_a)
    for hd in range(MLA_HEADS):
        qh = q_up[:, hd * LANES:(hd + 1) * LANES]
        q_sw = q_up[:, (MLA_HEADS + hd) * LANES:(MLA_HEADS + hd + 1) * LANES]
        q_rot = qh * gq_ref[0:1, :] * cos_a + q_sw * gq_ref[1:2, :] * sin_a
        qa_ref[0, hd, rows] = (q_rot * (_rms(qh, MLA_QK) * q_scale)).astype(bf16)
        kh = k_up[:, hd * LANES:(hd + 1) * LANES] + kpe
        ka_ref[0, hd, rows] = ((kh * gk_ref[...] * cos_a + k_rot) * _rms(kh, MLA_QK)).astype(bf16)

    cos_d, sl_d, sh_d = tables(ROW_D)
    low = lane < DSA_DIM

    def dsa_norm_rope(blk, gain):
        sq = blk * blk
        r_lo = lax.rsqrt(jnp.sum(jnp.where(low, sq, 0.0), axis=-1, keepdims=True) * (1.0 / DSA_DIM) + NORM_EPS)
        r_hi = lax.rsqrt(jnp.sum(jnp.where(low, 0.0, sq), axis=-1, keepdims=True) * (1.0 / DSA_DIM) + NORM_EPS)
        return _rope(blk * jnp.where(low, r_lo, r_hi) * gain, cos_d, sl_d, sh_d, DSA_ROT // 2)

    dsa_scale = DSA_DIM ** -0.5 * LOG2_E
    for j in range(DSA_WIDTH // LANES):
        blk = pb[:, j * LANES:(j + 1) * LANES]
        qn = (dsa_norm_rope(blk, gdq_ref[...]) * dsa_scale).astype(bf16)
        qb_ref[0, j, rows] = jnp.where(low, qn, jnp.zeros_like(qn))
        qb_ref[0, DSA_GROUP + j, rows] = jnp.where(low, jnp.zeros_like(qn), qn)
    kb_ref[0, rows] = dsa_norm_rope(pb[:, DSA_WIDTH:DSA_WIDTH + LANES], gdk_ref[...]).astype(bf16)
    vb_ref[0, 0, :, rows] = pb[:, DSA_WIDTH + LANES:].T.astype(bf16)

    cos_i, sl_i, sh_i = tables(ROW_I)
    n_qi = IDX_HEADS * IDX_DIM // LANES
    for j in range(n_qi + 1):
        blk = _rope(pi[:, j * LANES:(j + 1) * LANES], cos_i, sl_i, sh_i, IDX_ROT // 2).astype(bf16)
        if j < n_qi:
            qi_ref[0, rows, j * LANES:(j + 1) * LANES] = blk
        else:
            ki_ref[0, rows] = blk
    wi_t = (pi[:, (n_qi + 1) * LANES:] * (IDX_HEADS ** -0.5 * IDX_DIM ** -0.5)).T
    wi_ref[0, :, rows] = wi_t[0:IDX_HEADS, :]


def _softmax_tile(s, m, l, shift):
    if shift is None:
        m_new = jnp.maximum(m, jnp.max(s, axis=0, keepdims=True))
        alpha = jnp.exp2(m - m_new)
        p = jnp.exp2(s - m_new)
        return m_new, alpha * l + jnp.sum(p, axis=0, keepdims=True), alpha, p.astype(jnp.bfloat16)
    p = jnp.exp2(s - shift)
    return m, l + jnp.sum(p, axis=0, keepdims=True), None, p.astype(jnp.bfloat16)


def _by_shift(shift_ref, body):
    shift = shift_ref[0]

    @pl.when(shift <= FAST_MAX_SHIFT)
    def _():
        body(shift)

    @pl.when(shift > FAST_MAX_SHIFT)
    def _():
        body(None)


def _mla_kernel(shift_ref, q_ref, k_ref, vt_ref, o_ref, *, tq, tk, heads):
    i = pl.program_id(2)
    q_start = i * tq
    n_full = q_start // tk
    n_all = (q_start + tq + tk - 1) // tk
    tv = vt_ref.shape[-1]
    kpos0 = lax.broadcasted_iota(jnp.int32, (tk, tq), 0)
    qpos = q_start + lax.broadcasted_iota(jnp.int32, (tk, tq), 1)

    def attend(shift):
        def step(j, carry, masked):
            off = pl.multiple_of(j * tk, tk)
            scores = [_nt_dot(k_ref[0, hd, pl.ds(off, tk), :], q_ref[0, hd]) for hd in range(heads)]
            out = []
            visible = kpos0 + off <= qpos if masked else None
            shift_t = jnp.where(visible, shift, -NEG_BIG) if masked and shift is not None else shift
            for hd in range(heads):
                m, l, acc = carry[hd]
                s = scores[hd]
                if masked and shift is None:
                    s = jnp.where(visible, s, NEG_BIG)
                m, l, alpha, pb = _softmax_tile(s, m, l, shift_t)
                if alpha is not None:
                    acc = alpha * acc
                for c in range(tk // tv):
                    vt = vt_ref[0, j * (tk // tv) + c, hd * MLA_V:(hd + 1) * MLA_V, :]
                    acc = acc + _bf16_dot(vt, pb[c * tv:(c + 1) * tv])
                out.append((m, l, acc))
            return tuple(out)

        init = tuple((jnp.full((1, tq), NEG_BIG, jnp.float32), jnp.zeros((1, tq), jnp.float32),
                      jnp.zeros((MLA_V, tq), jnp.float32)) for _ in range(heads))
        carry = lax.fori_loop(0, n_full, functools.partial(step, masked=False), init)
        carry = lax.fori_loop(n_full, n_all, functools.partial(step, masked=True), carry)
        outs = [acc / l for (_, l, acc) in carry]
        for c in range(heads // 2):
            o_ref[0, :, c * LANES:(c + 1) * LANES] = jnp.concatenate(outs[2 * c:2 * c + 2], axis=0).T

    _by_shift(shift_ref, attend)


def _dsa_kernel(shift_ref, qb_ref, kb_ref, vt_ref, qi_ref, ki_ref, wt_ref, o_ref,
                sc_ref, sb_ref, qim_ref, *, tq, tk, k_top):
    i = pl.program_id(1)
    q_start = i * tq
    n_kv = (q_start + tq + tk - 1) // tk
    n_heads = IDX_HEADS
    tv = vt_ref.shape[-1]

    lane_q = lax.broadcasted_iota(jnp.int32, (tq, LANES), 1)
    per_blk = LANES // IDX_DIM
    for hd in range(IDX_HEADS):
        blk = qi_ref[0, :, (hd // per_blk) * LANES:(hd // per_blk + 1) * LANES]
        slot = hd % per_blk
        in_head = (lane_q >= slot * IDX_DIM) & (lane_q < (slot + 1) * IDX_DIM)
        qim_ref[hd * tq:(hd + 1) * tq, :] = jnp.where(in_head, blk, jnp.zeros_like(blk))

    kpos0 = lax.broadcasted_iota(jnp.int32, (tk, tq), 0)
    qpos = q_start + lax.broadcasted_iota(jnp.int32, (tk, tq), 1)
    w_rows = [wt_ref[0, hd:hd + 1, :] for hd in range(IDX_HEADS)]

    def score_tile(j, _):
        off = pl.multiple_of(j * tk, tk)
        s_all = _nt_dot(ki_ref[0, pl.ds(off, tk), :], qim_ref[...])
        sc = w_rows[0] * jnp.maximum(s_all[:, 0:tq], 0.0)
        for hd in range(1, n_heads):
            sc = sc + w_rows[hd] * jnp.maximum(s_all[:, hd * tq:(hd + 1) * tq], 0.0)
        sc = jnp.where(kpos0 + off <= qpos, sc, -jnp.inf)
        sc_ref[j] = sc
        sb_ref[j] = sc.astype(jnp.bfloat16)
        return 0

    lax.fori_loop(0, n_kv, score_tile, 0)

    def pattern_value(u):
        key = u ^ INT_MIN
        return lax.bitcast_convert_type(key ^ ((key >> 31) & 0x7FFFFFFF), jnp.float32)

    def count(pred):
        def body(j, acc):
            hit = jnp.where(pred(sc_ref[j], j), 1, 0)
            return acc + jnp.sum(hit.reshape(tk // 8, 8, tq), axis=0)
        acc = lax.fori_loop(0, n_kv, body, jnp.zeros((8, tq), jnp.int32))
        return jnp.sum(acc, axis=0, keepdims=True)

    def count_coarse(cand_b):
        one, zero = jnp.ones((), jnp.bfloat16), jnp.zeros((), jnp.bfloat16)

        def body(j, acc):
            part = None
            for r in range(tk // BF16_ROWS):
                rows = sb_ref[j, r * BF16_ROWS:(r + 1) * BF16_ROWS, :]
                hit = jnp.where(rows >= cand_b, one, zero)
                part = hit if part is None else part + hit
            return acc + part.astype(jnp.float32)
        acc = lax.fori_loop(0, n_kv, body, jnp.zeros((BF16_ROWS, tq), jnp.float32))
        return jnp.sum(acc, axis=0, keepdims=True)

    def coarse_bit(it, t16):
        cand = t16 | jnp.left_shift(jnp.int32(1), 15 - it)
        cand32 = jnp.left_shift(cand, 16) | jnp.where(cand < 0x8000, 0xFFFF, 0)
        cnt = count_coarse(pattern_value(cand32).astype(jnp.bfloat16))
        return jnp.where(cnt >= k_top, cand, t16)

    t16 = lax.fori_loop(0, 16, coarse_bit, jnp.zeros((1, tq), jnp.int32))
    few = t16 < 0x0080
    base = jnp.left_shift(t16, 16) + jnp.where(t16 < 0x8000, 0x7FFF, -0x8000)
    base = jnp.where(few, 0x00800000, base)

    def fine_bit(it, carry):
        t, cnt_t = carry
        cand = t | jnp.left_shift(jnp.int32(1), FINE_BITS - 1 - it)
        cand_f = pattern_value(base + cand)
        cnt = count(lambda sc, j: sc >= cand_f)
        take = cnt >= k_top
        return jnp.where(take, cand, t), jnp.where(take, cnt, cnt_t)

    t, cnt_ge = lax.fori_loop(0, FINE_BITS, fine_bit,
                              (jnp.zeros((1, tq), jnp.int32), jnp.full((1, tq), k_top, jnp.int32)))
    thr = jnp.where(few, F32_LOWEST, pattern_value(base + t))
    tied = jnp.logical_not(few) & ((cnt_ge > k_top) | (t == 0))

    def tie_cut():
        cnt_gt = count(lambda sc, j: sc > thr)
        need = k_top - cnt_gt

        def index_bit(it, c):
            cand = c | jnp.left_shift(jnp.int32(1), 12 - it)
            below = count(lambda sc, j: (sc == thr) & (kpos0 + j * tk < cand))
            return jnp.where(below < need, cand, c)

        c = lax.fori_loop(0, 13, index_bit, jnp.zeros((1, tq), jnp.int32))
        return jnp.where(tied, c, jnp.int32(2 ** 30))

    cut = lax.cond(jnp.max(tied.astype(jnp.int32)) > 0, tie_cut,
                   lambda: jnp.full((1, tq), 2 ** 30, jnp.int32))

    hpd = max(1, MXU_WIDTH // tq)
    n_dots = n_heads // hpd
    wd = hpd * tq

    def attend_all(shift):
        def attend(j, carry):
            off = pl.multiple_of(j * tk, tk)
            kt = kb_ref[0, pl.ds(off, tk), :]
            scores = [_nt_dot(kt, qb_ref[0, hpd * d:hpd * (d + 1)].reshape(wd, LANES)) for d in range(n_dots)]
            sc = sc_ref[j]
            sel = (sc > thr) | ((sc == thr) & (kpos0 + off <= cut))
            sel_d = jnp.concatenate([sel] * hpd, axis=1) if hpd > 1 else sel
            shift_d = None if shift is None else jnp.where(sel_d, shift, -NEG_BIG)
            out = []
            for d in range(n_dots):
                m, l, acc = carry[d]
                s = jnp.where(sel_d, scores[d], NEG_BIG) if shift is None else scores[d]
                m, l, alpha, pb = _softmax_tile(s, m, l, shift_d)
                if alpha is not None:
                    acc = alpha * acc
                for c in range(tk // tv):
                    acc = acc + _bf16_dot(vt_ref[0, j * (tk // tv) + c], pb[c * tv:(c + 1) * tv])
                out.append((m, l, acc))
            return tuple(out)

        init = tuple((jnp.full((1, wd), NEG_BIG, jnp.float32), jnp.zeros((1, wd), jnp.float32),
                      jnp.zeros((LANES, wd), jnp.float32)) for _ in range(n_dots))
        carry = lax.fori_loop(0, n_kv, attend, init)
        o = jnp.concatenate([acc / l for (_, l, acc) in carry], axis=1)
        for j in range(DSA_GROUP):
            lo = o[:, j * tq:(j + 1) * tq].T
            hi = o[:, (DSA_GROUP + j) * tq:(DSA_GROUP + j + 1) * tq].T
            o_ref[0, :, j * LANES:(j + 1) * LANES] = jnp.where(lane_q < DSA_DIM, lo, hi)

    _by_shift(shift_ref, attend_all)


def _merge_kernel(x_ref, oa_ref, ob_ref, g_ref, bm_ref, wa_ref, wb_ref, wo_ref, out_ref):
    f32 = jnp.float32
    ga = g_ref[:, 0:MLA_WIDTH].astype(f32)
    gb = g_ref[:, MLA_WIDTH:MLA_WIDTH + DSA_WIDTH].astype(f32)
    ma = g_ref[:, MLA_WIDTH + DSA_WIDTH:MLA_WIDTH + DSA_WIDTH + D_MODEL].astype(f32)
    mb = g_ref[:, MLA_WIDTH + DSA_WIDTH + D_MODEL:].astype(f32)
    ya = _bf16_dot((oa_ref[...] * (ga * jax.nn.sigmoid(ga))).astype(jnp.bfloat16), wa_ref[...])
    yb = _bf16_dot((ob_ref[...] * (gb * jax.nn.sigmoid(gb))).astype(jnp.bfloat16), wb_ref[...])
    merged = jax.nn.sigmoid(ma + bm_ref[0:1, :]) * ya + jax.nn.sigmoid(mb + bm_ref[1:2, :]) * yb
    out_ref[...] = x_ref[...] + _bf16_dot(merged.astype(jnp.bfloat16), wo_ref[...])


def _inv_freq(rot_dim):
    return ROPE_THETA ** (-jnp.arange(0, rot_dim, 2, dtype=jnp.float32) / rot_dim)


def _rope_inputs():
    invf = jnp.concatenate([_inv_freq(rot) for (_, _, rot, _) in ROPE_LAYOUTS])
    invf = jnp.pad(invf, (0, FREQ_ROWS - invf.shape[0]))[:, None]
    spread = np.zeros((2 * FREQ_ROWS, 3 * len(ROPE_LAYOUTS) * LANES), np.float32)
    for n, (period, start, rot, row0) in enumerate(ROPE_LAYOUTS):
        half = rot // 2
        for lane in range(LANES):
            p = lane % period - start
            cos_col, lo_col, hi_col = ((3 * n + c) * LANES + lane for c in range(3))
            if 0 <= p < rot:
                spread[row0 + p % half, cos_col] = 1.0
                spread[FREQ_ROWS + row0 + p % half, lo_col if p < half else hi_col] = -1.0 if p < half else 1.0
            else:
                spread[ONE_ROW, cos_col] = 1.0
    return invf, jnp.asarray(np.tile(spread, (3, 1)), jnp.bfloat16)


def _pad_lanes(a, start=0):
    return jnp.pad(a, ((0, 0), (start, LANES - start - a.shape[1])))


def _layout_params(w_in, mla_w_uq, mla_w_ukv, mla_q_gain, mla_k_gain, dsa_q_gain, dsa_k_gain,
                   w_branch_dsa):
    offs = np.concatenate([[0], np.cumsum(IN_SPLITS)])
    (c_q, c_kv, k_pe, gate_a, q_b, k_b, v_b, gate_b, q_i, k_i, w_i, m_a, m_b) = [
        w_in[:, offs[n]:offs[n + 1]] for n in range(len(IN_SPLITS))]

    def pair_heads(w):
        rows = w.shape[0]
        w = w.reshape(rows, DSA_KV_HEADS, DSA_GROUP, DSA_DIM)
        return jnp.transpose(w, (0, 2, 1, 3)).reshape(rows, DSA_WIDTH)

    w_all = jnp.concatenate([
        c_q, c_kv, _pad_lanes(k_pe, MLA_NOPE),
        pair_heads(q_b), k_b, v_b,
        q_i, jnp.tile(k_i, (1, LANES // IDX_DIM)), _pad_lanes(w_i),
        gate_a, pair_heads(gate_b), m_a, m_b], axis=1).astype(jnp.bfloat16)
    uq = mla_w_uq.reshape(MLA_Q_RANK, MLA_HEADS, MLA_QK)
    half = MLA_ROPE // 2

    def swap_rope(a):
        return jnp.concatenate([jnp.zeros_like(a[..., :MLA_NOPE]), a[..., MLA_NOPE + half:], a[..., MLA_NOPE:MLA_NOPE + half]], -1)

    pad_qk = lambda a: jnp.pad(a, ((0, 0), (0, 0), (0, LANES - MLA_QK))).reshape(MLA_Q_RANK, MLA_HEADS * LANES)
    wuq = jnp.concatenate([pad_qk(uq), pad_qk(swap_rope(uq))], axis=1)
    ukv = mla_w_ukv.reshape(MLA_KV_RANK, MLA_HEADS, MLA_NOPE + MLA_V)
    wuk = jnp.pad(ukv[:, :, :MLA_NOPE], ((0, 0), (0, 0), (0, LANES - MLA_NOPE))).reshape(MLA_KV_RANK, MLA_HEADS * LANES)
    wuv = ukv[:, :, MLA_NOPE:].reshape(MLA_KV_RANK, MLA_WIDTH)
    gq = jnp.concatenate([_pad_lanes(mla_q_gain[None, :]), _pad_lanes(swap_rope(mla_q_gain[None, :]))], axis=0)
    gk = _pad_lanes(mla_k_gain[None, :])
    gdq = jnp.tile(dsa_q_gain[None, :], (1, LANES // DSA_DIM))
    gdk = jnp.tile(dsa_k_gain[None, :], (1, LANES // DSA_DIM))
    wb_dsa = jnp.transpose(w_branch_dsa.reshape(DSA_KV_HEADS, DSA_GROUP, DSA_DIM, D_MODEL),
                           (1, 0, 2, 3)).reshape(DSA_WIDTH, D_MODEL)
    return (w_all, wuq.astype(jnp.bfloat16), wuk.astype(jnp.bfloat16), wuv.astype(jnp.bfloat16),
            gq, gk, gdq, gdk, wb_dsa.astype(jnp.bfloat16))


def _const_spec(shape):
    return pl.BlockSpec(shape, lambda *_: (0,) * len(shape))


def _layer(x, positions, invf, spread, norm_gain, w_in, b_merge, mla_q_norm, mla_w_uq, mla_kv_norm, mla_w_ukv,
           mla_q_gain, mla_k_gain, dsa_q_gain, dsa_k_gain, w_branch_mla, w_branch_dsa, w_out):
    bsz, s_len, d_model = x.shape
    assert d_model == D_MODEL and TM_PROJ % PROJ_ROWS == 0 and (bsz * s_len) % TM_MERGE == 0
    assert all(s_len % t == 0 for t in (TM_PROJ, TQ_MLA, TK_MLA, TQ_DSA, TK_DSA))
    k_top = min(TOPK_MAX, s_len // 4)
    (w_all, wuq, wuk, wuv, gq, gk, gdq, gdk, wb_dsa) = _layout_params(
        w_in, mla_w_uq, mla_w_ukv, mla_q_gain, mla_k_gain, dsa_q_gain, dsa_k_gain, w_branch_dsa)
    bf16, f32 = jnp.bfloat16, jnp.float32
    gmax = lambda g: jnp.max(jnp.abs(g))
    shift_a = (BOUND_MARGIN * MLA_QK ** 0.5 * LOG2_E * gmax(mla_q_gain) * gmax(mla_k_gain)).reshape(1).astype(f32)
    shift_b = (BOUND_MARGIN * DSA_DIM ** 0.5 * LOG2_E * gmax(dsa_q_gain) * gmax(dsa_k_gain)).reshape(1).astype(f32)
    smem = pl.BlockSpec(memory_space=pltpu.SMEM)

    tm = TM_PROJ
    tok = lambda w: pl.BlockSpec((1, tm, w), lambda b, i: (b, i, 0))
    head = pl.BlockSpec((1, MLA_HEADS, tm, LANES), lambda b, i: (b, 0, i, 0))
    outs = pl.pallas_call(
        _proj_kernel,
        grid=(bsz, s_len // tm),
        in_specs=[tok(D_MODEL), pl.BlockSpec((1, 1, tm), lambda b, i: (b, 0, i)),
                  _const_spec(invf.shape), _const_spec(spread.shape), _const_spec((1, D_MODEL)),
                  _const_spec((1, MLA_Q_RANK)), _const_spec((1, MLA_KV_RANK)),
                  _const_spec((2, LANES)), _const_spec((1, LANES)), _const_spec((1, LANES)), _const_spec((1, LANES)),
                  _const_spec((D_MODEL, W_ALL)), _const_spec((MLA_Q_RANK, 2 * MLA_HEADS * LANES)),
                  _const_spec((MLA_KV_RANK, MLA_HEADS * LANES)), _const_spec((MLA_KV_RANK, MLA_WIDTH))],
        out_specs=[head, head, pl.BlockSpec((1, 1, MLA_WIDTH, tm), lambda b, i: (b, i, 0, 0)),
                   head, tok(LANES),
                   pl.BlockSpec((1, 1, LANES, tm), lambda b, i: (b, i, 0, 0)),
                   tok(IDX_HEADS * IDX_DIM), tok(LANES),
                   pl.BlockSpec((1, IDX_HEADS, tm), lambda b, i: (b, 0, i)), tok(W_G)],
        out_shape=[jax.ShapeDtypeStruct((bsz, MLA_HEADS, s_len, LANES), bf16),
                   jax.ShapeDtypeStruct((bsz, MLA_HEADS, s_len, LANES), bf16),
                   jax.ShapeDtypeStruct((bsz, s_len // tm, MLA_WIDTH, tm), bf16),
                   jax.ShapeDtypeStruct((bsz, DSA_HEADS, s_len, LANES), bf16),
                   jax.ShapeDtypeStruct((bsz, s_len, LANES), bf16),
                   jax.ShapeDtypeStruct((bsz, s_len // tm, LANES, tm), bf16),
                   jax.ShapeDtypeStruct((bsz, s_len, IDX_HEADS * IDX_DIM), bf16),
                   jax.ShapeDtypeStruct((bsz, s_len, LANES), bf16),
                   jax.ShapeDtypeStruct((bsz, IDX_HEADS, s_len), f32),
                   jax.ShapeDtypeStruct((bsz, s_len, W_G), bf16)],
        compiler_params=pltpu.CompilerParams(dimension_semantics=("arbitrary", "arbitrary"),
                                             vmem_limit_bytes=VMEM_LIMIT),
        name="proj",
    )(x, positions[:, None, :], invf, spread, norm_gain[None, :], mla_q_norm[None, :], mla_kv_norm[None, :],
      gq, gk, gdq, gdk, w_all, wuq, wuk, wuv)
    qa, ka, vat, qb, kb, vbt, qi, ki, wit, gates = outs

    tq_a, tk_a, hps = TQ_MLA, TK_MLA, MLA_HEADS_PER_STEP
    o_a = pl.pallas_call(
        functools.partial(_mla_kernel, tq=tq_a, tk=tk_a, heads=hps),
        grid=(bsz, MLA_HEADS // hps, s_len // tq_a),
        in_specs=[smem, pl.BlockSpec((1, hps, tq_a, LANES), lambda b, hp, i: (b, hp, i, 0)),
                  pl.BlockSpec((1, hps, s_len, LANES), lambda b, hp, i: (b, hp, 0, 0)),
                  pl.BlockSpec((1, s_len // tm, hps * MLA_V, tm), lambda b, hp, i: (b, 0, hp, 0))],
        out_specs=pl.BlockSpec((1, tq_a, hps * MLA_V), lambda b, hp, i: (b, i, hp)),
        out_shape=jax.ShapeDtypeStruct((bsz, s_len, MLA_WIDTH), f32),
        compiler_params=pltpu.CompilerParams(dimension_semantics=("arbitrary",) * 3,
                                             vmem_limit_bytes=VMEM_LIMIT),
        name="mla",
    )(shift_a, qa, ka, vat)

    tq_b, tk_b = TQ_DSA, TK_DSA
    row_b = lambda w: pl.BlockSpec((1, tq_b, w), lambda b, i: (b, i, 0))
    full_b = lambda w: pl.BlockSpec((1, s_len, w), lambda b, i: (b, 0, 0))
    o_b = pl.pallas_call(
        functools.partial(_dsa_kernel, tq=tq_b, tk=tk_b, k_top=k_top),
        grid=(bsz, s_len // tq_b),
        in_specs=[smem, pl.BlockSpec((1, DSA_HEADS, tq_b, LANES), lambda b, i: (b, 0, i, 0)), full_b(LANES),
                  pl.BlockSpec((1, s_len // tm, LANES, tm), lambda b, i: (b, 0, 0, 0)),
                  row_b(IDX_HEADS * IDX_DIM), full_b(LANES),
                  pl.BlockSpec((1, IDX_HEADS, tq_b), lambda b, i: (b, 0, i))],
        out_specs=row_b(DSA_WIDTH),
        out_shape=jax.ShapeDtypeStruct((bsz, s_len, DSA_WIDTH), f32),
        scratch_shapes=[pltpu.VMEM((s_len // tk_b, tk_b, tq_b), f32),
                        pltpu.VMEM((s_len // tk_b, tk_b, tq_b), bf16),
                        pltpu.VMEM((IDX_HEADS * tq_b, LANES), bf16)],
        compiler_params=pltpu.CompilerParams(dimension_semantics=("arbitrary", "arbitrary"),
                                             vmem_limit_bytes=VMEM_LIMIT),
        name="dsa",
    )(shift_b, qb, kb, vbt, qi, ki, wit)

    tm_m = TM_MERGE
    n_tok = bsz * s_len
    flat = lambda w: pl.BlockSpec((tm_m, w), lambda i: (i, 0))
    out = pl.pallas_call(
        _merge_kernel,
        grid=(n_tok // tm_m,),
        in_specs=[flat(D_MODEL), flat(MLA_WIDTH), flat(DSA_WIDTH), flat(W_G), _const_spec((2, D_MODEL)),
                  _const_spec((MLA_WIDTH, D_MODEL)), _const_spec((DSA_WIDTH, D_MODEL)),
                  _const_spec((D_MODEL, D_MODEL))],
        out_specs=flat(D_MODEL),
        out_shape=jax.ShapeDtypeStruct((n_tok, D_MODEL), f32),
        compiler_params=pltpu.CompilerParams(dimension_semantics=("arbitrary",),
                                             vmem_limit_bytes=VMEM_LIMIT),
        name="merge",
    )(x.reshape(n_tok, D_MODEL), o_a.reshape(n_tok, MLA_WIDTH), o_b.reshape(n_tok, DSA_WIDTH),
      gates.reshape(n_tok, W_G), b_merge, w_branch_mla.astype(bf16), wb_dsa, w_out.astype(bf16))
    return out.reshape(bsz, s_len, D_MODEL)


def kernel(x, positions, norm_gain, w_in, b_merge, mla_q_norm, mla_w_uq, mla_kv_norm, mla_w_ukv,
           mla_q_gain, mla_k_gain, dsa_q_gain, dsa_k_gain, w_branch_mla, w_branch_dsa, w_out):
    invf, spread = _rope_inputs()
    for l in range(norm_gain.shape[0]):
        x = _layer(x, positions, invf, spread, norm_gain[l], w_in[l], b_merge[l], mla_q_norm[l], mla_w_uq[l],
                   mla_kv_norm[l], mla_w_ukv[l], mla_q_gain[l], mla_k_gain[l], dsa_q_gain[l], dsa_k_gain[l],
                   w_branch_mla[l], w_branch_dsa[l], w_out[l])
    return x
```

```python
import functools

import numpy as np
import jax
import jax.numpy as jnp
from jax import lax
from jax.experimental import pallas as pl
from jax.experimental.pallas import tpu as pltpu

D_MODEL = 1024
MLA_HEADS = 8
MLA_Q_RANK = 256
MLA_KV_RANK = 128
MLA_NOPE = 64
MLA_ROPE = 32
MLA_QK = MLA_NOPE + MLA_ROPE
MLA_V = 64
MLA_WIDTH = MLA_HEADS * MLA_V
DSA_HEADS = 8
DSA_KV_HEADS = 2
DSA_GROUP = DSA_HEADS // DSA_KV_HEADS
DSA_DIM = 64
DSA_WIDTH = DSA_HEADS * DSA_DIM
DSA_ROT = DSA_DIM // 4
IDX_HEADS = 8
IDX_DIM = 32
IDX_ROT = IDX_DIM // 4
TOPK_MAX = 256
ROPE_THETA = 500000.0
NORM_EPS = 1e-6

IN_SPLITS = (
    MLA_Q_RANK, MLA_KV_RANK, MLA_ROPE, MLA_WIDTH,
    DSA_HEADS * DSA_DIM, DSA_KV_HEADS * DSA_DIM, DSA_KV_HEADS * DSA_DIM, DSA_WIDTH,
    IDX_HEADS * IDX_DIM, IDX_DIM, IDX_HEADS,
    D_MODEL, D_MODEL,
)

LANES = 128
MXU_WIDTH = 256
VMEM_LIMIT = 56 * 1024 * 1024
INT_MIN = -(2 ** 31)
F32_LOWEST = float(np.finfo(np.float32).min)
BF16_ROWS = 16
FINE_BITS = 17
NEG_BIG = -1e30
LOG2_E = 1.4426950408889634
FAST_MAX_SHIFT = 60.0
BOUND_MARGIN = 1.02

OFF_A = 0
W_A = MLA_Q_RANK + MLA_KV_RANK + LANES
OFF_B = OFF_A + W_A
W_B = DSA_WIDTH + 2 * DSA_KV_HEADS * DSA_DIM
OFF_I = OFF_B + W_B
W_I = IDX_HEADS * IDX_DIM + 2 * LANES
OFF_G = OFF_I + W_I
W_G = MLA_WIDTH + DSA_WIDTH + 2 * D_MODEL
W_ALL = OFF_G + W_G
PROJ_ROWS = 256
GATE_CHUNK = 512
N_GATE_CHUNKS = W_G // GATE_CHUNK


TM_PROJ = 512
TQ_MLA, TK_MLA = 512, 512
MLA_HEADS_PER_STEP = 8
TQ_DSA, TK_DSA = 512, 512
TM_MERGE = 1024

ROPE_LAYOUTS = ((LANES, MLA_NOPE, MLA_ROPE, 0), (DSA_DIM, 0, DSA_ROT, MLA_ROPE // 2),
                (IDX_DIM, 0, IDX_ROT, MLA_ROPE // 2 + DSA_ROT // 2))
ROW_A, ROW_D, ROW_I = 0, 1, 2
FREQ_ROWS = 32
ONE_ROW = FREQ_ROWS - 1


def _rms(v, width):
    return lax.rsqrt(jnp.sum(v * v, axis=-1, keepdims=True) * (1.0 / width) + NORM_EPS)


def _rope(v, cos, sin_lo, sin_hi, half):
    return (v * cos + pltpu.roll(v, LANES - half, 1) * sin_lo + pltpu.roll(v, half, 1) * sin_hi)


def _bf16_dot(a, b):
    return jnp.dot(a, b, preferred_element_type=jnp.float32)


def _nt_dot(a, b):
    return lax.dot_general(a, b, (((1,), (1,)), ((), ())), preferred_element_type=jnp.float32)


def _proj_kernel(x_ref, pos_ref, invf_ref, spread_ref, gx_ref, gcq_ref, gckv_ref, gq_ref, gk_ref, gdq_ref, gdk_ref,
                 w_ref, wuq_ref, wuk_ref, wuv_ref,
                 qa_ref, ka_ref, va_ref, qb_ref, kb_ref, vb_ref, qi_ref, ki_ref, wi_ref, g_ref):
    bf16 = jnp.bfloat16
    subs = [slice(r0, r0 + PROJ_ROWS) for r0 in range(0, x_ref.shape[1], PROJ_ROWS)]
    hs = []
    for rows in subs:
        x = x_ref[0, rows]
        hs.append((x * _rms(x, D_MODEL) * gx_ref[...]).astype(bf16))
    pas = [_bf16_dot(h, w_ref[:, OFF_A:OFF_A + W_A]) for h in hs]
    tabs = [_rope_tables(pos_ref[0, :, rows], invf_ref, spread_ref) for rows in subs]
    lat = []
    for pa in pas:
        c_q = pa[:, :MLA_Q_RANK]
        c_kv = pa[:, MLA_Q_RANK:MLA_Q_RANK + MLA_KV_RANK]
        lat.append(((c_q * _rms(c_q, MLA_Q_RANK) * gcq_ref[...]).astype(bf16),
                    (c_kv * _rms(c_kv, MLA_KV_RANK) * gckv_ref[...]).astype(bf16)))
    pbs = [_bf16_dot(h, w_ref[:, OFF_B:OFF_B + W_B]) for h in hs]
    pis = [_bf16_dot(h, w_ref[:, OFF_I:OFF_I + W_I]) for h in hs]
    ups = [(_bf16_dot(cqn, wuq_ref[...]),
            _bf16_dot(ckvn, wuk_ref[...]),
            _bf16_dot(ckvn, wuv_ref[...])) for cqn, ckvn in lat]
    for n, rows in enumerate(subs):
        kpe = pas[n][:, MLA_Q_RANK + MLA_KV_RANK:]
        _proj_epilogue(rows, (tabs[n], kpe) + ups[n] + (pbs[n], pis[n]), gq_ref, gk_ref, gdq_ref, gdk_ref,
                       qa_ref, ka_ref, va_ref, qb_ref, kb_ref, vb_ref, qi_ref, ki_ref, wi_ref)
    for rows, h in zip(subs, hs):
        for c in range(N_GATE_CHUNKS):
            cols = slice(c * GATE_CHUNK, (c + 1) * GATE_CHUNK)
            g_ref[0, rows, cols] = _bf16_dot(h, w_ref[:, OFF_G + c * GATE_CHUNK:OFF_G + (c + 1) * GATE_CHUNK]).astype(g_ref.dtype)


def _rope_tables(pos_row, invf_ref, spread_ref):
    bf16 = jnp.bfloat16
    ang = invf_ref[...] * pos_row.astype(jnp.float32)
    trig = jnp.concatenate([jnp.cos(ang), jnp.sin(ang)], axis=0)
    t1 = trig.astype(bf16)
    r1 = trig - t1.astype(jnp.float32)
    t2 = r1.astype(bf16)
    t3 = (r1 - t2.astype(jnp.float32)).astype(bf16)
    return lax.dot_general(jnp.concatenate([t1, t2, t3], axis=0), spread_ref[...], (((0,), (0,)), ((), ())),
                           preferred_element_type=jnp.float32)


def _proj_epilogue(rows, vals, gq_ref, gk_ref, gdq_ref, gdk_ref,
                   qa_ref, ka_ref, va_ref, qb_ref, kb_ref, vb_ref, qi_ref, ki_ref, wi_ref):
    tabs, kpe, q_up, k_up, v_up, pb, pi = vals
    lane = lax.broadcasted_iota(jnp.int32, (1, LANES), 1)
    bf16 = jnp.bfloat16

    def tables(n):
        return tuple(tabs[:, (3 * n + c) * LANES:(3 * n + c + 1) * LANES] for c in range(3))

    for c in range(MLA_WIDTH // LANES):
        va_ref[0, 0, c * LANES:(c + 1) * LANES, rows] = v_up[:, c * LANES:(c + 1) * LANES].T.astype(bf16)
    cos_a, sl_a, sh_a = tables(ROW_A)
    sin_a = sl_a + sh_a
    q_scale = MLA_QK ** -0.5 * LOG2_E
    kpe_g = kpe * gk_ref[...]
    k_rot = (pltpu.roll(kpe_g, LANES - MLA_ROPE // 2, 1) * sl_a + pltpu.roll(kpe_g, MLA_ROPE // 2, 1) * sh_a)
    for hd in range(MLA_HEADS):
        qh = q_up[:, hd * LANES:(hd + 1) * LANES]
        q_sw = q_up[:, (MLA_HEADS + hd) * LANES:(MLA_HEADS + hd + 1) * LANES]
        q_rot = qh * gq_ref[0:1, :] * cos_a + q_sw * gq_ref[1:2, :] * sin_a
        qa_ref[0, hd, rows] = (q_rot * (_rms(qh, MLA_QK) * q_scale)).astype(bf16)
        kh = k_up[:, hd * LANES:(hd + 1) * LANES] + kpe
        ka_ref[0, hd, rows] = ((kh * gk_ref[...] * cos_a + k_rot) * _rms(kh, MLA_QK)).astype(bf16)

    cos_d, sl_d, sh_d = tables(ROW_D)
    low = lane < DSA_DIM

    def dsa_norm_rope(blk, gain):
        sq = blk * blk
        r_lo = lax.rsqrt(jnp.sum(jnp.where(low, sq, 0.0), axis=-1, keepdims=True) * (1.0 / DSA_DIM) + NORM_EPS)
        r_hi = lax.rsqrt(jnp.sum(jnp.where(low, 0.0, sq), axis=-1, keepdims=True) * (1.0 / DSA_DIM) + NORM_EPS)
        return _rope(blk * jnp.where(low, r_lo, r_hi) * gain, cos_d, sl_d, sh_d, DSA_ROT // 2)

    dsa_scale = DSA_DIM ** -0.5 * LOG2_E
    for j in range(DSA_WIDTH // LANES):
        blk = pb[:, j * LANES:(j + 1) * LANES]
        qn = (dsa_norm_rope(blk, gdq_ref[...]) * dsa_scale).astype(bf16)
        qb_ref[0, j, rows] = jnp.where(low, qn, jnp.zeros_like(qn))
        qb_ref[0, DSA_GROUP + j, rows] = jnp.where(low, jnp.zeros_like(qn), qn)
    kb_ref[0, rows] = dsa_norm_rope(pb[:, DSA_WIDTH:DSA_WIDTH + LANES], gdk_ref[...]).astype(bf16)
    vb_ref[0, 0, :, rows] = pb[:, DSA_WIDTH + LANES:].T.astype(bf16)

    cos_i, sl_i, sh_i = tables(ROW_I)
    n_qi = IDX_HEADS * IDX_DIM // LANES
    for j in range(n_qi + 1):
        blk = _rope(pi[:, j * LANES:(j + 1) * LANES], cos_i, sl_i, sh_i, IDX_ROT // 2).astype(bf16)
        if j < n_qi:
            qi_ref[0, rows, j * LANES:(j + 1) * LANES] = blk
        else:
            ki_ref[0, rows] = blk
    wi_t = (pi[:, (n_qi + 1) * LANES:] * (IDX_HEADS ** -0.5 * IDX_DIM ** -0.5)).T
    wi_ref[0, :, rows] = wi_t[0:IDX_HEADS, :]


def _softmax_tile(s, m, l, shift):
    if shift is None:
        m_new = jnp.maximum(m, jnp.max(s, axis=0, keepdims=True))
        alpha = jnp.exp2(m - m_new)
        p = jnp.exp2(s - m_new)
        return m_new, alpha * l + jnp.sum(p, axis=0, keepdims=True), alpha, p.astype(jnp.bfloat16)
    p = jnp.exp2(s - shift)
    return m, l + jnp.sum(p, axis=0, keepdims=True), None, p.astype(jnp.bfloat16)


def _by_shift(shift_ref, body):
    shift = shift_ref[0]

    @pl.when(shift <= FAST_MAX_SHIFT)
    def _():
        body(shift)

    @pl.when(shift > FAST_MAX_SHIFT)
    def _():
        body(None)


def _mla_kernel(shift_ref, q_ref, k_ref, vt_ref, o_ref, *, tq, tk, heads):
    i = pl.program_id(2)
    q_start = i * tq
    n_full = q_start // tk
    n_all = (q_start + tq + tk - 1) // tk
    tv = vt_ref.shape[-1]
    kpos0 = lax.broadcasted_iota(jnp.int32, (tk, tq), 0)
    qpos = q_start + lax.broadcasted_iota(jnp.int32, (tk, tq), 1)

    def attend(shift):
        def step(j, carry, masked):
            off = pl.multiple_of(j * tk, tk)
            scores = [_nt_dot(k_ref[0, hd, pl.ds(off, tk), :], q_ref[0, hd]) for hd in range(heads)]
            out = []
            visible = kpos0 + off <= qpos if masked else None
            shift_t = jnp.where(visible, shift, -NEG_BIG) if masked and shift is not None else shift
            for hd in range(heads):
                m, l, acc = carry[hd]
                s = scores[hd]
                if masked and shift is None:
                    s = jnp.where(visible, s, NEG_BIG)
                m, l, alpha, pb = _softmax_tile(s, m, l, shift_t)
                if alpha is not None:
                    acc = alpha * acc
                for c in range(tk // tv):
                    vt = vt_ref[0, j * (tk // tv) + c, hd * MLA_V:(hd + 1) * MLA_V, :]
                    acc = acc + _bf16_dot(vt, pb[c * tv:(c + 1) * tv])
                out.append((m, l, acc))
            return tuple(out)

        init = tuple((jnp.full((1, tq), NEG_BIG, jnp.float32), jnp.zeros((1, tq), jnp.float32),
                      jnp.zeros((MLA_V, tq), jnp.float32)) for _ in range(heads))
        carry = lax.fori_loop(0, n_full, functools.partial(step, masked=False), init)
        carry = lax.fori_loop(n_full, n_all, functools.partial(step, masked=True), carry)
        outs = [acc / l for (_, l, acc) in carry]
        for c in range(heads // 2):
            o_ref[0, :, c * LANES:(c + 1) * LANES] = jnp.concatenate(outs[2 * c:2 * c + 2], axis=0).T

    _by_shift(shift_ref, attend)


def _dsa_kernel(shift_ref, qb_ref, kb_ref, vt_ref, qi_ref, ki_ref, wt_ref, o_ref,
                sc_ref, sb_ref, qim_ref, *, tq, tk, k_top):
    i = pl.program_id(1)
    q_start = i * tq
    n_kv = (q_start + tq + tk - 1) // tk
    n_heads = IDX_HEADS
    tv = vt_ref.shape[-1]

    lane_q = lax.broadcasted_iota(jnp.int32, (tq, LANES), 1)
    per_blk = LANES // IDX_DIM
    for hd in range(IDX_HEADS):
        blk = qi_ref[0, :, (hd // per_blk) * LANES:(hd // per_blk + 1) * LANES]
        slot = hd % per_blk
        in_head = (lane_q >= slot * IDX_DIM) & (lane_q < (slot + 1) * IDX_DIM)
        qim_ref[hd * tq:(hd + 1) * tq, :] = jnp.where(in_head, blk, jnp.zeros_like(blk))

    kpos0 = lax.broadcasted_iota(jnp.int32, (tk, tq), 0)
    qpos = q_start + lax.broadcasted_iota(jnp.int32, (tk, tq), 1)
    w_rows = [wt_ref[0, hd:hd + 1, :] for hd in range(IDX_HEADS)]

    def score_tile(j, _):
        off = pl.multiple_of(j * tk, tk)
        s_all = _nt_dot(ki_ref[0, pl.ds(off, tk), :], qim_ref[...])
        sc = w_rows[0] * jnp.maximum(s_all[:, 0:tq], 0.0)
        for hd in range(1, n_heads):
            sc = sc + w_rows[hd] * jnp.maximum(s_all[:, hd * tq:(hd + 1) * tq], 0.0)
        sc = jnp.where(kpos0 + off <= qpos, sc, -jnp.inf)
        sc_ref[j] = sc
        sb_ref[j] = sc.astype(jnp.bfloat16)
        return 0

    lax.fori_loop(0, n_kv, score_tile, 0)

    def pattern_value(u):
        key = u ^ INT_MIN
        return lax.bitcast_convert_type(key ^ ((key >> 31) & 0x7FFFFFFF), jnp.float32)

    def count(pred):
        def body(j, acc):
            hit = jnp.where(pred(sc_ref[j], j), 1, 0)
            return acc + jnp.sum(hit.reshape(tk // 8, 8, tq), axis=0)
        acc = lax.fori_loop(0, n_kv, body, jnp.zeros((8, tq), jnp.int32))
        return jnp.sum(acc, axis=0, keepdims=True)

    def count_coarse(cand_b):
        one, zero = jnp.ones((), jnp.bfloat16), jnp.zeros((), jnp.bfloat16)

        def body(j, acc):
            part = None
            for r in range(tk // BF16_ROWS):
                rows = sb_ref[j, r * BF16_ROWS:(r + 1) * BF16_ROWS, :]
                hit = jnp.where(rows >= cand_b, one, zero)
                part = hit if part is None else part + hit
            return acc + part.astype(jnp.float32)
        acc = lax.fori_loop(0, n_kv, body, jnp.zeros((BF16_ROWS, tq), jnp.float32))
        return jnp.sum(acc, axis=0, keepdims=True)

    def coarse_bit(it, t16):
        cand = t16 | jnp.left_shift(jnp.int32(1), 15 - it)
        cand32 = jnp.left_shift(cand, 16) | jnp.where(cand < 0x8000, 0xFFFF, 0)
        cnt = count_coarse(pattern_value(cand32).astype(jnp.bfloat16))
        return jnp.where(cnt >= k_top, cand, t16)

    t16 = lax.fori_loop(0, 16, coarse_bit, jnp.zeros((1, tq), jnp.int32))
    few = t16 < 0x0080
    base = jnp.left_shift(t16, 16) + jnp.where(t16 < 0x8000, 0x7FFF, -0x8000)
    base = jnp.where(few, 0x00800000, base)

    def fine_bit(it, carry):
        t, cnt_t = carry
        cand = t | jnp.left_shift(jnp.int32(1), FINE_BITS - 1 - it)
        cand_f = pattern_value(base + cand)
        cnt = count(lambda sc, j: sc >= cand_f)
        take = cnt >= k_top
        return jnp.where(take, cand, t), jnp.where(take, cnt, cnt_t)

    t, cnt_ge = lax.fori_loop(0, FINE_BITS, fine_bit,
                              (jnp.zeros((1, tq), jnp.int32), jnp.full((1, tq), k_top, jnp.int32)))
    thr = jnp.where(few, F32_LOWEST, pattern_value(base + t))
    tied = jnp.logical_not(few) & ((cnt_ge > k_top) | (t == 0))

    def tie_cut():
        cnt_gt = count(lambda sc, j: sc > thr)
        need = k_top - cnt_gt

        def index_bit(it, c):
            cand = c | jnp.left_shift(jnp.int32(1), 12 - it)
            below = count(lambda sc, j: (sc == thr) & (kpos0 + j * tk < cand))
            return jnp.where(below < need, cand, c)

        c = lax.fori_loop(0, 13, index_bit, jnp.zeros((1, tq), jnp.int32))
        return jnp.where(tied, c, jnp.int32(2 ** 30))

    cut = lax.cond(jnp.max(tied.astype(jnp.int32)) > 0, tie_cut,
                   lambda: jnp.full((1, tq), 2 ** 30, jnp.int32))

    assert tq >= MXU_WIDTH

    def attend_all(shift):
        def attend(j, carry):
            off = pl.multiple_of(j * tk, tk)
            kt = kb_ref[0, pl.ds(off, tk), :]
            scores = [_nt_dot(kt, qb_ref[0, hd]) for hd in range(n_heads)]
            sc = sc_ref[j]
            sel = (sc > thr) | ((sc == thr) & (kpos0 + off <= cut))
            shift_t = None if shift is None else jnp.where(sel, shift, -NEG_BIG)
            out = []
            for hd in range(n_heads):
                m, l, acc = carry[hd]
                s = jnp.where(sel, scores[hd], NEG_BIG) if shift is None else scores[hd]
                m, l, alpha, pb = _softmax_tile(s, m, l, shift_t)
                if alpha is not None:
                    acc = alpha * acc
                g = hd // DSA_GROUP
                for c in range(tk // tv):
                    vt = vt_ref[0, j * (tk // tv) + c, g * DSA_DIM:(g + 1) * DSA_DIM, :]
                    acc = acc + _bf16_dot(vt, pb[c * tv:(c + 1) * tv])
                out.append((m, l, acc))
            return tuple(out)

        init = tuple((jnp.full((1, tq), NEG_BIG, jnp.float32), jnp.zeros((1, tq), jnp.float32),
                      jnp.zeros((DSA_DIM, tq), jnp.float32)) for _ in range(n_heads))
        carry = lax.fori_loop(0, n_kv, attend, init)
        outs = [acc / l for (_, l, acc) in carry]
        for j in range(DSA_GROUP):
            o_ref[0, :, j * LANES:(j + 1) * LANES] = jnp.concatenate([outs[j], outs[DSA_GROUP + j]], axis=0).T

    _by_shift(shift_ref, attend_all)


def _merge_kernel(x_ref, oa_ref, ob_ref, g_ref, bm_ref, wa_ref, wb_ref, wo_ref, out_ref):
    f32 = jnp.float32
    ga = g_ref[:, 0:MLA_WIDTH].astype(f32)
    gb = g_ref[:, MLA_WIDTH:MLA_WIDTH + DSA_WIDTH].astype(f32)
    ma = g_ref[:, MLA_WIDTH + DSA_WIDTH:MLA_WIDTH + DSA_WIDTH + D_MODEL].astype(f32)
    mb = g_ref[:, MLA_WIDTH + DSA_WIDTH + D_MODEL:].astype(f32)
    ya = _bf16_dot((oa_ref[...] * (ga * jax.nn.sigmoid(ga))).astype(jnp.bfloat16), wa_ref[...])
    yb = _bf16_dot((ob_ref[...] * (gb * jax.nn.sigmoid(gb))).astype(jnp.bfloat16), wb_ref[...])
    merged = jax.nn.sigmoid(ma + bm_ref[0:1, :]) * ya + jax.nn.sigmoid(mb + bm_ref[1:2, :]) * yb
    out_ref[...] = x_ref[...] + _bf16_dot(merged.astype(jnp.bfloat16), wo_ref[...])


def _inv_freq(rot_dim):
    return ROPE_THETA ** (-jnp.arange(0, rot_dim, 2, dtype=jnp.float32) / rot_dim)


def _rope_inputs():
    invf = jnp.concatenate([_inv_freq(rot) for (_, _, rot, _) in ROPE_LAYOUTS])
    invf = jnp.pad(invf, (0, FREQ_ROWS - invf.shape[0]))[:, None]
    spread = np.zeros((2 * FREQ_ROWS, 3 * len(ROPE_LAYOUTS) * LANES), np.float32)
    for n, (period, start, rot, row0) in enumerate(ROPE_LAYOUTS):
        half = rot // 2
        for lane in range(LANES):
            p = lane % period - start
            cos_col, lo_col, hi_col = ((3 * n + c) * LANES + lane for c in range(3))
            if 0 <= p < rot:
                spread[row0 + p % half, cos_col] = 1.0
                spread[FREQ_ROWS + row0 + p % half, lo_col if p < half else hi_col] = -1.0 if p < half else 1.0
            else:
                spread[ONE_ROW, cos_col] = 1.0
    return invf, jnp.asarray(np.tile(spread, (3, 1)), jnp.bfloat16)


def _pad_lanes(a, start=0):
    return jnp.pad(a, ((0, 0), (start, LANES - start - a.shape[1])))


def _layout_params(w_in, mla_w_uq, mla_w_ukv, mla_q_gain, mla_k_gain, dsa_q_gain, dsa_k_gain,
                   w_branch_dsa):
    offs = np.concatenate([[0], np.cumsum(IN_SPLITS)])
    (c_q, c_kv, k_pe, gate_a, q_b, k_b, v_b, gate_b, q_i, k_i, w_i, m_a, m_b) = [
        w_in[:, offs[n]:offs[n + 1]] for n in range(len(IN_SPLITS))]

    def pair_heads(w):
        rows = w.shape[0]
        w = w.reshape(rows, DSA_KV_HEADS, DSA_GROUP, DSA_DIM)
        return jnp.transpose(w, (0, 2, 1, 3)).reshape(rows, DSA_WIDTH)

    w_all = jnp.concatenate([
        c_q, c_kv, _pad_lanes(k_pe, MLA_NOPE),
        pair_heads(q_b), k_b, v_b,
        q_i, jnp.tile(k_i, (1, LANES // IDX_DIM)), _pad_lanes(w_i),
        gate_a, pair_heads(gate_b), m_a, m_b], axis=1).astype(jnp.bfloat16)
    uq = mla_w_uq.reshape(MLA_Q_RANK, MLA_HEADS, MLA_QK)
    half = MLA_ROPE // 2

    def swap_rope(a):
        return jnp.concatenate([jnp.zeros_like(a[..., :MLA_NOPE]), a[..., MLA_NOPE + half:], a[..., MLA_NOPE:MLA_NOPE + half]], -1)

    pad_qk = lambda a: jnp.pad(a, ((0, 0), (0, 0), (0, LANES - MLA_QK))).reshape(MLA_Q_RANK, MLA_HEADS * LANES)
    wuq = jnp.concatenate([pad_qk(uq), pad_qk(swap_rope(uq))], axis=1)
    ukv = mla_w_ukv.reshape(MLA_KV_RANK, MLA_HEADS, MLA_NOPE + MLA_V)
    wuk = jnp.pad(ukv[:, :, :MLA_NOPE], ((0, 0), (0, 0), (0, LANES - MLA_NOPE))).reshape(MLA_KV_RANK, MLA_HEADS * LANES)
    wuv = ukv[:, :, MLA_NOPE:].reshape(MLA_KV_RANK, MLA_WIDTH)
    gq = jnp.concatenate([_pad_lanes(mla_q_gain[None, :]), _pad_lanes(swap_rope(mla_q_gain[None, :]))], axis=0)
    gk = _pad_lanes(mla_k_gain[None, :])
    gdq = jnp.tile(dsa_q_gain[None, :], (1, LANES // DSA_DIM))
    gdk = jnp.tile(dsa_k_gain[None, :], (1, LANES // DSA_DIM))
    wb_dsa = jnp.transpose(w_branch_dsa.reshape(DSA_KV_HEADS, DSA_GROUP, DSA_DIM, D_MODEL),
                           (1, 0, 2, 3)).reshape(DSA_WIDTH, D_MODEL)
    return (w_all, wuq.astype(jnp.bfloat16), wuk.astype(jnp.bfloat16), wuv.astype(jnp.bfloat16),
            gq, gk, gdq, gdk, wb_dsa.astype(jnp.bfloat16))


def _const_spec(shape):
    return pl.BlockSpec(shape, lambda *_: (0,) * len(shape))


def _layer(x, positions, invf, spread, norm_gain, w_in, b_merge, mla_q_norm, mla_w_uq, mla_kv_norm, mla_w_ukv,
           mla_q_gain, mla_k_gain, dsa_q_gain, dsa_k_gain, w_branch_mla, w_branch_dsa, w_out):
    bsz, s_len, d_model = x.shape
    assert d_model == D_MODEL and TM_PROJ % PROJ_ROWS == 0 and (bsz * s_len) % TM_MERGE == 0
    assert all(s_len % t == 0 for t in (TM_PROJ, TQ_MLA, TK_MLA, TQ_DSA, TK_DSA))
    k_top = min(TOPK_MAX, s_len // 4)
    (w_all, wuq, wuk, wuv, gq, gk, gdq, gdk, wb_dsa) = _layout_params(
        w_in, mla_w_uq, mla_w_ukv, mla_q_gain, mla_k_gain, dsa_q_gain, dsa_k_gain, w_branch_dsa)
    bf16, f32 = jnp.bfloat16, jnp.float32
    gmax = lambda g: jnp.max(jnp.abs(g))
    shift_a = (BOUND_MARGIN * MLA_QK ** 0.5 * LOG2_E * gmax(mla_q_gain) * gmax(mla_k_gain)).reshape(1).astype(f32)
    shift_b = (BOUND_MARGIN * DSA_DIM ** 0.5 * LOG2_E * gmax(dsa_q_gain) * gmax(dsa_k_gain)).reshape(1).astype(f32)
    smem = pl.BlockSpec(memory_space=pltpu.SMEM)

    tm = TM_PROJ
    tok = lambda w: pl.BlockSpec((1, tm, w), lambda b, i: (b, i, 0))
    head = pl.BlockSpec((1, MLA_HEADS, tm, LANES), lambda b, i: (b, 0, i, 0))
    outs = pl.pallas_call(
        _proj_kernel,
        grid=(bsz, s_len // tm),
        in_specs=[tok(D_MODEL), pl.BlockSpec((1, 1, tm), lambda b, i: (b, 0, i)),
                  _const_spec(invf.shape), _const_spec(spread.shape), _const_spec((1, D_MODEL)),
                  _const_spec((1, MLA_Q_RANK)), _const_spec((1, MLA_KV_RANK)),
                  _const_spec((2, LANES)), _const_spec((1, LANES)), _const_spec((1, LANES)), _const_spec((1, LANES)),
                  _const_spec((D_MODEL, W_ALL)), _const_spec((MLA_Q_RANK, 2 * MLA_HEADS * LANES)),
                  _const_spec((MLA_KV_RANK, MLA_HEADS * LANES)), _const_spec((MLA_KV_RANK, MLA_WIDTH))],
        out_specs=[head, head, pl.BlockSpec((1, 1, MLA_WIDTH, tm), lambda b, i: (b, i, 0, 0)),
                   head, tok(LANES),
                   pl.BlockSpec((1, 1, LANES, tm), lambda b, i: (b, i, 0, 0)),
                   tok(IDX_HEADS * IDX_DIM), tok(LANES),
                   pl.BlockSpec((1, IDX_HEADS, tm), lambda b, i: (b, 0, i)), tok(W_G)],
        out_shape=[jax.ShapeDtypeStruct((bsz, MLA_HEADS, s_len, LANES), bf16),
                   jax.ShapeDtypeStruct((bsz, MLA_HEADS, s_len, LANES), bf16),
                   jax.ShapeDtypeStruct((bsz, s_len // tm, MLA_WIDTH, tm), bf16),
                   jax.ShapeDtypeStruct((bsz, DSA_HEADS, s_len, LANES), bf16),
                   jax.ShapeDtypeStruct((bsz, s_len, LANES), bf16),
                   jax.ShapeDtypeStruct((bsz, s_len // tm, LANES, tm), bf16),
                   jax.ShapeDtypeStruct((bsz, s_len, IDX_HEADS * IDX_DIM), bf16),
                   jax.ShapeDtypeStruct((bsz, s_len, LANES), bf16),
                   jax.ShapeDtypeStruct((bsz, IDX_HEADS, s_len), f32),
                   jax.ShapeDtypeStruct((bsz, s_len, W_G), bf16)],
        compiler_params=pltpu.CompilerParams(dimension_semantics=("arbitrary", "arbitrary"),
                                             vmem_limit_bytes=VMEM_LIMIT),
        name="proj",
    )(x, positions[:, None, :], invf, spread, norm_gain[None, :], mla_q_norm[None, :], mla_kv_norm[None, :],
      gq, gk, gdq, gdk, w_all, wuq, wuk, wuv)
    qa, ka, vat, qb, kb, vbt, qi, ki, wit, gates = outs

    tq_a, tk_a, hps = TQ_MLA, TK_MLA, MLA_HEADS_PER_STEP
    o_a = pl.pallas_call(
        functools.partial(_mla_kernel, tq=tq_a, tk=tk_a, heads=hps),
        grid=(bsz, MLA_HEADS // hps, s_len // tq_a),
        in_specs=[smem, pl.BlockSpec((1, hps, tq_a, LANES), lambda b, hp, i: (b, hp, i, 0)),
                  pl.BlockSpec((1, hps, s_len, LANES), lambda b, hp, i: (b, hp, 0, 0)),
                  pl.BlockSpec((1, s_len // tm, hps * MLA_V, tm), lambda b, hp, i: (b, 0, hp, 0))],
        out_specs=pl.BlockSpec((1, tq_a, hps * MLA_V), lambda b, hp, i: (b, i, hp)),
        out_shape=jax.ShapeDtypeStruct((bsz, s_len, MLA_WIDTH), f32),
        compiler_params=pltpu.CompilerParams(dimension_semantics=("arbitrary",) * 3,
                                             vmem_limit_bytes=VMEM_LIMIT),
        name="mla",
    )(shift_a, qa, ka, vat)

    tq_b, tk_b = TQ_DSA, TK_DSA
    row_b = lambda w: pl.BlockSpec((1, tq_b, w), lambda b, i: (b, i, 0))
    full_b = lambda w: pl.BlockSpec((1, s_len, w), lambda b, i: (b, 0, 0))
    o_b = pl.pallas_call(
        functools.partial(_dsa_kernel, tq=tq_b, tk=tk_b, k_top=k_top),
        grid=(bsz, s_len // tq_b),
        in_specs=[smem, pl.BlockSpec((1, DSA_HEADS, tq_b, LANES), lambda b, i: (b, 0, i, 0)), full_b(LANES),
                  pl.BlockSpec((1, s_len // tm, LANES, tm), lambda b, i: (b, 0, 0, 0)),
                  row_b(IDX_HEADS * IDX_DIM), full_b(LANES),
                  pl.BlockSpec((1, IDX_HEADS, tq_b), lambda b, i: (b, 0, i))],
        out_specs=row_b(DSA_WIDTH),
        out_shape=jax.ShapeDtypeStruct((bsz, s_len, DSA_WIDTH), f32),
        scratch_shapes=[pltpu.VMEM((s_len // tk_b, tk_b, tq_b), f32),
                        pltpu.VMEM((s_len // tk_b, tk_b, tq_b), bf16),
                        pltpu.VMEM((IDX_HEADS * tq_b, LANES), bf16)],
        compiler_params=pltpu.CompilerParams(dimension_semantics=("arbitrary", "arbitrary"),
                                             vmem_limit_bytes=VMEM_LIMIT),
        name="dsa",
    )(shift_b, qb, kb, vbt, qi, ki, wit)

    tm_m = TM_MERGE
    n_tok = bsz * s_len
    flat = lambda w: pl.BlockSpec((tm_m, w), lambda i: (i, 0))
    out = pl.pallas_call(
        _merge_kernel,
        grid=(n_tok // tm_m,),
        in_specs=[flat(D_MODEL), flat(MLA_WIDTH), flat(DSA_WIDTH), flat(W_G), _const_spec((2, D_MODEL)),
                  _const_spec((MLA_WIDTH, D_MODEL)), _const_spec((DSA_WIDTH, D_MODEL)),
                  _const_spec((D_MODEL, D_MODEL))],
        out_specs=flat(D_MODEL),
        out_shape=jax.ShapeDtypeStruct((n_tok, D_MODEL), f32),
        compiler_params=pltpu.CompilerParams(dimension_semantics=("arbitrary",),
                                             vmem_limit_bytes=VMEM_LIMIT),
        name="merge",
    )(x.reshape(n_tok, D_MODEL), o_a.reshape(n_tok, MLA_WIDTH), o_b.reshape(n_tok, DSA_WIDTH),
      gates.reshape(n_tok, W_G), b_merge, w_branch_mla.astype(bf16), wb_dsa, w_out.astype(bf16))
    return out.reshape(bsz, s_len, D_MODEL)


def kernel(x, positions, norm_gain, w_in, b_merge, mla_q_norm, mla_w_uq, mla_kv_norm, mla_w_ukv,
           mla_q_gain, mla_k_gain, dsa_q_gain, dsa_k_gain, w_branch_mla, w_branch_dsa, w_out):
    invf, spread = _rope_inputs()
    for l in range(norm_gain.shape[0]):
        x = _layer(x, positions, invf, spread, norm_gain[l], w_in[l], b_merge[l], mla_q_norm[l], mla_w_uq[l],
                   mla_kv_norm[l], mla_w_ukv[l], mla_q_gain[l], mla_k_gain[l], dsa_q_gain[l], dsa_k_gain[l],
                   w_branch_mla[l], w_branch_dsa[l], w_out[l])
    return x
```

```python
import functools

import numpy as np
import jax
import jax.numpy as jnp
from jax import lax
from jax.experimental import pallas as pl
from jax.experimental.pallas import tpu as pltpu

D_MODEL = 1024
MLA_HEADS = 8
MLA_Q_RANK = 256
MLA_KV_RANK = 128
MLA_NOPE = 64
MLA_ROPE = 32
MLA_QK = MLA_NOPE + MLA_ROPE
MLA_V = 64
MLA_WIDTH = MLA_HEADS * MLA_V
DSA_HEADS = 8
DSA_KV_HEADS = 2
DSA_GROUP = DSA_HEADS // DSA_KV_HEADS
DSA_DIM = 64
DSA_WIDTH = DSA_HEADS * DSA_DIM
DSA_ROT = DSA_DIM // 4
IDX_HEADS = 8
IDX_DIM = 32
IDX_ROT = IDX_DIM // 4
TOPK_MAX = 256
ROPE_THETA = 500000.0
NORM_EPS = 1e-6

IN_SPLITS = (
    MLA_Q_RANK, MLA_KV_RANK, MLA_ROPE, MLA_WIDTH,
    DSA_HEADS * DSA_DIM, DSA_KV_HEADS * DSA_DIM, DSA_KV_HEADS * DSA_DIM, DSA_WIDTH,
    IDX_HEADS * IDX_DIM, IDX_DIM, IDX_HEADS,
    D_MODEL, D_MODEL,
)

LANES = 128
MXU_WIDTH = 256
VMEM_LIMIT = 56 * 1024 * 1024
INT_MIN = -(2 ** 31)
F32_LOWEST = float(np.finfo(np.float32).min)
BF16_ROWS = 16
FINE_BITS = 17
NEG_BIG = -1e30
LOG2_E = 1.4426950408889634
FAST_MAX_SHIFT = 60.0
BOUND_MARGIN = 1.02

OFF_A = 0
W_A = MLA_Q_RANK + MLA_KV_RANK + LANES
OFF_B = OFF_A + W_A
W_B = DSA_WIDTH + 2 * DSA_KV_HEADS * DSA_DIM
OFF_I = OFF_B + W_B
W_I = IDX_HEADS * IDX_DIM + 2 * LANES
OFF_G = OFF_I + W_I
W_G = MLA_WIDTH + DSA_WIDTH + 2 * D_MODEL
W_ALL = OFF_G + W_G
PROJ_ROWS = 256
GATE_CHUNK = 512
N_GATE_CHUNKS = W_G // GATE_CHUNK


TM_PROJ = 512
TQ_MLA, TK_MLA = 512, 512
MLA_HEADS_PER_STEP = 8
TQ_DSA, TK_DSA = 512, 512
TM_MERGE = 1024

ROPE_LAYOUTS = ((LANES, MLA_NOPE, MLA_ROPE, 0), (DSA_DIM, 0, DSA_ROT, MLA_ROPE // 2),
                (IDX_DIM, 0, IDX_ROT, MLA_ROPE // 2 + DSA_ROT // 2))
ROW_A, ROW_D, ROW_I = 0, 1, 2
FREQ_ROWS = 32
ONE_ROW = FREQ_ROWS - 1


def _rms(v, width):
    return lax.rsqrt(jnp.sum(v * v, axis=-1, keepdims=True) * (1.0 / width) + NORM_EPS)


def _rope(v, cos, sin_lo, sin_hi, half):
    return (v * cos + pltpu.roll(v, LANES - half, 1) * sin_lo + pltpu.roll(v, half, 1) * sin_hi)


def _bf16_dot(a, b):
    return jnp.dot(a, b, preferred_element_type=jnp.float32)


def _nt_dot(a, b):
    return lax.dot_general(a, b, (((1,), (1,)), ((), ())), preferred_element_type=jnp.float32)


def _proj_kernel(x_ref, pos_ref, invf_ref, spread_ref, gx_ref, gcq_ref, gckv_ref, gq_ref, gk_ref, gdq_ref, gdk_ref,
                 w_ref, wuq_ref, wuk_ref, wuv_ref,
                 qa_ref, ka_ref, va_ref, qb_ref, kb_ref, vb_ref, qi_ref, ki_ref, wi_ref, g_ref):
    bf16 = jnp.bfloat16
    subs = [slice(r0, r0 + PROJ_ROWS) for r0 in range(0, x_ref.shape[1], PROJ_ROWS)]
    hs = []
    for rows in subs:
        x = x_ref[0, rows]
        hs.append((x * _rms(x, D_MODEL) * gx_ref[...]).astype(bf16))
    pas = [_bf16_dot(h, w_ref[:, OFF_A:OFF_A + W_A]) for h in hs]
    tabs = [_rope_tables(pos_ref[0, :, rows], invf_ref, spread_ref) for rows in subs]
    lat = []
    for pa in pas:
        c_q = pa[:, :MLA_Q_RANK]
        c_kv = pa[:, MLA_Q_RANK:MLA_Q_RANK + MLA_KV_RANK]
        lat.append(((c_q * _rms(c_q, MLA_Q_RANK) * gcq_ref[...]).astype(bf16),
                    (c_kv * _rms(c_kv, MLA_KV_RANK) * gckv_ref[...]).astype(bf16)))
    pbs = [_bf16_dot(h, w_ref[:, OFF_B:OFF_B + W_B]) for h in hs]
    pis = [_bf16_dot(h, w_ref[:, OFF_I:OFF_I + W_I]) for h in hs]
    ups = [(_bf16_dot(cqn, wuq_ref[...]),
            _bf16_dot(ckvn, wuk_ref[...]),
            _bf16_dot(ckvn, wuv_ref[...])) for cqn, ckvn in lat]
    for n, rows in enumerate(subs):
        kpe = pas[n][:, MLA_Q_RANK + MLA_KV_RANK:]
        _proj_epilogue(rows, (tabs[n], kpe) + ups[n] + (pbs[n], pis[n]), gq_ref, gk_ref, gdq_ref, gdk_ref,
                       qa_ref, ka_ref, va_ref, qb_ref, kb_ref, vb_ref, qi_ref, ki_ref, wi_ref)
    for rows, h in zip(subs, hs):
        for c in range(N_GATE_CHUNKS):
            cols = slice(c * GATE_CHUNK, (c + 1) * GATE_CHUNK)
            g_ref[0, rows, cols] = _bf16_dot(h, w_ref[:, OFF_G + c * GATE_CHUNK:OFF_G + (c + 1) * GATE_CHUNK]).astype(g_ref.dtype)


def _rope_tables(pos_row, invf_ref, spread_ref):
    bf16 = jnp.bfloat16
    ang = invf_ref[...] * pos_row.astype(jnp.float32)
    trig = jnp.concatenate([jnp.cos(ang), jnp.sin(ang)], axis=0)
    t1 = trig.astype(bf16)
    r1 = trig - t1.astype(jnp.float32)
    t2 = r1.astype(bf16)
    t3 = (r1 - t2.astype(jnp.float32)).astype(bf16)
    return lax.dot_general(jnp.concatenate([t1, t2, t3], axis=0), spread_ref[...], (((0,), (0,)), ((), ())),
                           preferred_element_type=jnp.float32)


def _proj_epilogue(rows, vals, gq_ref, gk_ref, gdq_ref, gdk_ref,
                   qa_ref, ka_ref, va_ref, qb_ref, kb_ref, vb_ref, qi_ref, ki_ref, wi_ref):
    tabs, kpe, q_up, k_up, v_up, pb, pi = vals
    lane = lax.broadcasted_iota(jnp.int32, (1, LANES), 1)
    bf16 = jnp.bfloat16

    def tables(n):
        return tuple(tabs[:, (3 * n + c) * LANES:(3 * n + c + 1) * LANES] for c in range(3))

    for c in range(MLA_WIDTH // LANES):
        va_ref[0, 0, c * LANES:(c + 1) * LANES, rows] = v_up[:, c * LANES:(c + 1) * LANES].T.astype(bf16)
    cos_a, sl_a, sh_a = tables(ROW_A)
    sin_a = sl_a + sh_a
    q_scale = MLA_QK ** -0.5 * LOG2_E
    kpe_g = kpe * gk_ref[...]
    k_rot = (pltpu.roll(kpe_g, LANES - MLA_ROPE // 2, 1) * sl_a + pltpu.roll(kpe_g, MLA_ROPE // 2, 1) * sh_a)
    for hd in range(MLA_HEADS):
        qh = q_up[:, hd * LANES:(hd + 1) * LANES]
        q_sw = q_up[:, (MLA_HEADS + hd) * LANES:(MLA_HEADS + hd + 1) * LANES]
        q_rot = qh * gq_ref[0:1, :] * cos_a + q_sw * gq_ref[1:2, :] * sin_a
        qa_ref[0, hd, rows] = (q_rot * (_rms(qh, MLA_QK) * q_scale)).astype(bf16)
        kh = k_up[:, hd * LANES:(hd + 1) * LANES] + kpe
        ka_ref[0, hd, rows] = ((kh * gk_ref[...] * cos_a + k_rot) * _rms(kh, MLA_QK)).astype(bf16)

    cos_d, sl_d, sh_d = tables(ROW_D)
    low = lane < DSA_DIM

    def dsa_norm_rope(blk, gain):
        sq = blk * blk
        r_lo = lax.rsqrt(jnp.sum(jnp.where(low, sq, 0.0), axis=-1, keepdims=True) * (1.0 / DSA_DIM) + NORM_EPS)
        r_hi = lax.rsqrt(jnp.sum(jnp.where(low, 0.0, sq), axis=-1, keepdims=True) * (1.0 / DSA_DIM) + NORM_EPS)
        return _rope(blk * jnp.where(low, r_lo, r_hi) * gain, cos_d, sl_d, sh_d, DSA_ROT // 2)

    dsa_scale = DSA_DIM ** -0.5 * LOG2_E
    for j in range(DSA_WIDTH // LANES):
        blk = pb[:, j * LANES:(j + 1) * LANES]
        qn = (dsa_norm_rope(blk, gdq_ref[...]) * dsa_scale).astype(bf16)
        qb_ref[0, j, rows] = jnp.where(low, qn, jnp.zeros_like(qn))
        qb_ref[0, DSA_GROUP + j, rows] = jnp.where(low, jnp.zeros_like(qn), qn)
    kb_ref[0, rows] = dsa_norm_rope(pb[:, DSA_WIDTH:DSA_WIDTH + LANES], gdk_ref[...]).astype(bf16)
    vb_ref[0, 0, :, rows] = pb[:, DSA_WIDTH + LANES:].T.astype(bf16)

    cos_i, sl_i, sh_i = tables(ROW_I)
    n_qi = IDX_HEADS * IDX_DIM // LANES
    for j in range(n_qi + 1):
        blk = _rope(pi[:, j * LANES:(j + 1) * LANES], cos_i, sl_i, sh_i, IDX_ROT // 2).astype(bf16)
        if j < n_qi:
            qi_ref[0, rows, j * LANES:(j + 1) * LANES] = blk
        else:
            ki_ref[0, rows] = blk
    wi_t = (pi[:, (n_qi + 1) * LANES:] * (IDX_HEADS ** -0.5 * IDX_DIM ** -0.5)).T
    wi_ref[0, :, rows] = wi_t[0:IDX_HEADS, :]


def _softmax_tile(s, m, l, shift):
    if shift is None:
        m_new = jnp.maximum(m, jnp.max(s, axis=0, keepdims=True))
        alpha = jnp.exp2(m - m_new)
        p = jnp.exp2(s - m_new)
        return m_new, alpha * l + jnp.sum(p, axis=0, keepdims=True), alpha, p.astype(jnp.bfloat16)
    p = jnp.exp2(s - shift)
    return m, l + jnp.sum(p, axis=0, keepdims=True), None, p.astype(jnp.bfloat16)


def _by_shift(shift_ref, body):
    shift = shift_ref[0]

    @pl.when(shift <= FAST_MAX_SHIFT)
    def _():
        body(shift)

    @pl.when(shift > FAST_MAX_SHIFT)
    def _():
        body(None)


def _mla_kernel(shift_ref, q_ref, k_ref, vt_ref, o_ref, *, tq, tk, heads):
    i = pl.program_id(2)
    q_start = i * tq
    n_full = q_start // tk
    n_all = (q_start + tq + tk - 1) // tk
    tv = vt_ref.shape[-1]
    kpos0 = lax.broadcasted_iota(jnp.int32, (tk, tq), 0)
    qpos = q_start + lax.broadcasted_iota(jnp.int32, (tk, tq), 1)

    def attend(shift):
        def step(j, carry, masked):
            off = pl.multiple_of(j * tk, tk)
            scores = [_nt_dot(k_ref[0, hd, pl.ds(off, tk), :], q_ref[0, hd]) for hd in range(heads)]
            out = []
            visible = kpos0 + off <= qpos if masked else None
            shift_t = jnp.where(visible, shift, -NEG_BIG) if masked and shift is not None else shift
            for hd in range(heads):
                m, l, acc = carry[hd]
                s = scores[hd]
                if masked and shift is None:
                    s = jnp.where(visible, s, NEG_BIG)
                m, l, alpha, pb = _softmax_tile(s, m, l, shift_t)
                if alpha is not None:
                    acc = alpha * acc
                for c in range(tk // tv):
                    vt = vt_ref[0, j * (tk // tv) + c, hd * MLA_V:(hd + 1) * MLA_V, :]
                    acc = acc + _bf16_dot(vt, pb[c * tv:(c + 1) * tv])
                out.append((m, l, acc))
            return tuple(out)

        init = tuple((jnp.full((1, tq), NEG_BIG, jnp.float32), jnp.zeros((1, tq), jnp.float32),
                      jnp.zeros((MLA_V, tq), jnp.float32)) for _ in range(heads))
        carry = lax.fori_loop(0, n_full, functools.partial(step, masked=False), init)
        carry = lax.fori_loop(n_full, n_all, functools.partial(step, masked=True), carry)
        outs = [acc / l for (_, l, acc) in carry]
        for c in range(heads // 2):
            o_ref[0, :, c * LANES:(c + 1) * LANES] = jnp.concatenate(outs[2 * c:2 * c + 2], axis=0).T.astype(o_ref.dtype)

    _by_shift(shift_ref, attend)


def _dsa_kernel(shift_ref, qb_ref, kb_ref, vt_ref, qi_ref, ki_ref, wt_ref, o_ref,
                sc_ref, sb_ref, qim_ref, *, tq, tk, k_top):
    i = pl.program_id(1)
    q_start = i * tq
    n_kv = (q_start + tq + tk - 1) // tk
    n_heads = IDX_HEADS
    tv = vt_ref.shape[-1]

    lane_q = lax.broadcasted_iota(jnp.int32, (tq, LANES), 1)
    per_blk = LANES // IDX_DIM
    for hd in range(IDX_HEADS):
        blk = qi_ref[0, :, (hd // per_blk) * LANES:(hd // per_blk + 1) * LANES]
        slot = hd % per_blk
        in_head = (lane_q >= slot * IDX_DIM) & (lane_q < (slot + 1) * IDX_DIM)
        qim_ref[hd * tq:(hd + 1) * tq, :] = jnp.where(in_head, blk, jnp.zeros_like(blk))

    kpos0 = lax.broadcasted_iota(jnp.int32, (tk, tq), 0)
    qpos = q_start + lax.broadcasted_iota(jnp.int32, (tk, tq), 1)
    w_rows = [wt_ref[0, hd:hd + 1, :] for hd in range(IDX_HEADS)]

    def score_tile(j, _):
        off = pl.multiple_of(j * tk, tk)
        s_all = _nt_dot(ki_ref[0, pl.ds(off, tk), :], qim_ref[...])
        sc = w_rows[0] * jnp.maximum(s_all[:, 0:tq], 0.0)
        for hd in range(1, n_heads):
            sc = sc + w_rows[hd] * jnp.maximum(s_all[:, hd * tq:(hd + 1) * tq], 0.0)
        sc = jnp.where(kpos0 + off <= qpos, sc, -jnp.inf)
        sc_ref[j] = sc
        sb_ref[j] = sc.astype(jnp.bfloat16)
        return 0

    lax.fori_loop(0, n_kv, score_tile, 0)

    def pattern_value(u):
        key = u ^ INT_MIN
        return lax.bitcast_convert_type(key ^ ((key >> 31) & 0x7FFFFFFF), jnp.float32)

    def count(pred):
        def body(j, acc):
            hit = jnp.where(pred(sc_ref[j], j), 1, 0)
            return acc + jnp.sum(hit.reshape(tk // 8, 8, tq), axis=0)
        acc = lax.fori_loop(0, n_kv, body, jnp.zeros((8, tq), jnp.int32))
        return jnp.sum(acc, axis=0, keepdims=True)

    def count_coarse(cand_b):
        one, zero = jnp.ones((), jnp.bfloat16), jnp.zeros((), jnp.bfloat16)

        def body(j, acc):
            part = None
            for r in range(tk // BF16_ROWS):
                rows = sb_ref[j, r * BF16_ROWS:(r + 1) * BF16_ROWS, :]
                hit = jnp.where(rows >= cand_b, one, zero)
                part = hit if part is None else part + hit
            return acc + part.astype(jnp.float32)
        acc = lax.fori_loop(0, n_kv, body, jnp.zeros((BF16_ROWS, tq), jnp.float32))
        return jnp.sum(acc, axis=0, keepdims=True)

    def coarse_bit(it, t16):
        cand = t16 | jnp.left_shift(jnp.int32(1), 15 - it)
        cand32 = jnp.left_shift(cand, 16) | jnp.where(cand < 0x8000, 0xFFFF, 0)
        cnt = count_coarse(pattern_value(cand32).astype(jnp.bfloat16))
        return jnp.where(cnt >= k_top, cand, t16)

    t16 = lax.fori_loop(0, 16, coarse_bit, jnp.zeros((1, tq), jnp.int32))
    few = t16 < 0x0080
    base = jnp.left_shift(t16, 16) + jnp.where(t16 < 0x8000, 0x7FFF, -0x8000)
    base = jnp.where(few, 0x00800000, base)

    def fine_bit(it, carry):
        t, cnt_t = carry
        cand = t | jnp.left_shift(jnp.int32(1), FINE_BITS - 1 - it)
        cand_f = pattern_value(base + cand)
        cnt = count(lambda sc, j: sc >= cand_f)
        take = cnt >= k_top
        return jnp.where(take, cand, t), jnp.where(take, cnt, cnt_t)

    t, cnt_ge = lax.fori_loop(0, FINE_BITS, fine_bit,
                              (jnp.zeros((1, tq), jnp.int32), jnp.full((1, tq), k_top, jnp.int32)))
    thr = jnp.where(few, F32_LOWEST, pattern_value(base + t))
    tied = jnp.logical_not(few) & ((cnt_ge > k_top) | (t == 0))

    def tie_cut():
        cnt_gt = count(lambda sc, j: sc > thr)
        need = k_top - cnt_gt

        def index_bit(it, c):
            cand = c | jnp.left_shift(jnp.int32(1), 12 - it)
            below = count(lambda sc, j: (sc == thr) & (kpos0 + j * tk < cand))
            return jnp.where(below < need, cand, c)

        c = lax.fori_loop(0, 13, index_bit, jnp.zeros((1, tq), jnp.int32))
        return jnp.where(tied, c, jnp.int32(2 ** 30))

    cut = lax.cond(jnp.max(tied.astype(jnp.int32)) > 0, tie_cut,
                   lambda: jnp.full((1, tq), 2 ** 30, jnp.int32))

    assert tq >= MXU_WIDTH

    def attend_all(shift):
        def attend(j, carry):
            off = pl.multiple_of(j * tk, tk)
            kt = kb_ref[0, pl.ds(off, tk), :]
            scores = [_nt_dot(kt, qb_ref[0, hd]) for hd in range(n_heads)]
            sc = sc_ref[j]
            sel = (sc > thr) | ((sc == thr) & (kpos0 + off <= cut))
            shift_t = None if shift is None else jnp.where(sel, shift, -NEG_BIG)
            out = []
            for hd in range(n_heads):
                m, l, acc = carry[hd]
                s = jnp.where(sel, scores[hd], NEG_BIG) if shift is None else scores[hd]
                m, l, alpha, pb = _softmax_tile(s, m, l, shift_t)
                if alpha is not None:
                    acc = alpha * acc
                g = hd // DSA_GROUP
                for c in range(tk // tv):
                    vt = vt_ref[0, j * (tk // tv) + c, g * DSA_DIM:(g + 1) * DSA_DIM, :]
                    acc = acc + _bf16_dot(vt, pb[c * tv:(c + 1) * tv])
                out.append((m, l, acc))
            return tuple(out)

        init = tuple((jnp.full((1, tq), NEG_BIG, jnp.float32), jnp.zeros((1, tq), jnp.float32),
                      jnp.zeros((DSA_DIM, tq), jnp.float32)) for _ in range(n_heads))
        carry = lax.fori_loop(0, n_kv, attend, init)
        outs = [acc / l for (_, l, acc) in carry]
        for j in range(DSA_GROUP):
            o_ref[0, :, j * LANES:(j + 1) * LANES] = jnp.concatenate(
                [outs[j], outs[DSA_GROUP + j]], axis=0).T.astype(o_ref.dtype)

    _by_shift(shift_ref, attend_all)


def _merge_kernel(x_ref, oa_ref, ob_ref, g_ref, bm_ref, wa_ref, wb_ref, wo_ref, out_ref):
    f32 = jnp.float32
    ga = g_ref[:, 0:MLA_WIDTH].astype(f32)
    gb = g_ref[:, MLA_WIDTH:MLA_WIDTH + DSA_WIDTH].astype(f32)
    ma = g_ref[:, MLA_WIDTH + DSA_WIDTH:MLA_WIDTH + DSA_WIDTH + D_MODEL].astype(f32)
    mb = g_ref[:, MLA_WIDTH + DSA_WIDTH + D_MODEL:].astype(f32)
    ya = _bf16_dot((oa_ref[...].astype(f32) * (ga * jax.nn.sigmoid(ga))).astype(jnp.bfloat16), wa_ref[...])
    yb = _bf16_dot((ob_ref[...].astype(f32) * (gb * jax.nn.sigmoid(gb))).astype(jnp.bfloat16), wb_ref[...])
    merged = jax.nn.sigmoid(ma + bm_ref[0:1, :]) * ya + jax.nn.sigmoid(mb + bm_ref[1:2, :]) * yb
    out_ref[...] = x_ref[...] + _bf16_dot(merged.astype(jnp.bfloat16), wo_ref[...])


def _inv_freq(rot_dim):
    return ROPE_THETA ** (-jnp.arange(0, rot_dim, 2, dtype=jnp.float32) / rot_dim)


def _rope_inputs():
    invf = jnp.concatenate([_inv_freq(rot) for (_, _, rot, _) in ROPE_LAYOUTS])
    invf = jnp.pad(invf, (0, FREQ_ROWS - invf.shape[0]))[:, None]
    spread = np.zeros((2 * FREQ_ROWS, 3 * len(ROPE_LAYOUTS) * LANES), np.float32)
    for n, (period, start, rot, row0) in enumerate(ROPE_LAYOUTS):
        half = rot // 2
        for lane in range(LANES):
            p = lane % period - start
            cos_col, lo_col, hi_col = ((3 * n + c) * LANES + lane for c in range(3))
            if 0 <= p < rot:
                spread[row0 + p % half, cos_col] = 1.0
                spread[FREQ_ROWS + row0 + p % half, lo_col if p < half else hi_col] = -1.0 if p < half else 1.0
            else:
                spread[ONE_ROW, cos_col] = 1.0
    return invf, jnp.asarray(np.tile(spread, (3, 1)), jnp.bfloat16)


def _pad_lanes(a, start=0):
    return jnp.pad(a, ((0, 0), (start, LANES - start - a.shape[1])))


def _layout_params(w_in, mla_w_uq, mla_w_ukv, mla_q_gain, mla_k_gain, dsa_q_gain, dsa_k_gain,
                   w_branch_dsa):
    offs = np.concatenate([[0], np.cumsum(IN_SPLITS)])
    (c_q, c_kv, k_pe, gate_a, q_b, k_b, v_b, gate_b, q_i, k_i, w_i, m_a, m_b) = [
        w_in[:, offs[n]:offs[n + 1]] for n in range(len(IN_SPLITS))]

    def pair_heads(w):
        rows = w.shape[0]
        w = w.reshape(rows, DSA_KV_HEADS, DSA_GROUP, DSA_DIM)
        return jnp.transpose(w, (0, 2, 1, 3)).reshape(rows, DSA_WIDTH)

    w_all = jnp.concatenate([
        c_q, c_kv, _pad_lanes(k_pe, MLA_NOPE),
        pair_heads(q_b), k_b, v_b,
        q_i, jnp.tile(k_i, (1, LANES // IDX_DIM)), _pad_lanes(w_i),
        gate_a, pair_heads(gate_b), m_a, m_b], axis=1).astype(jnp.bfloat16)
    uq = mla_w_uq.reshape(MLA_Q_RANK, MLA_HEADS, MLA_QK)
    half = MLA_ROPE // 2

    def swap_rope(a):
        return jnp.concatenate([jnp.zeros_like(a[..., :MLA_NOPE]), a[..., MLA_NOPE + half:], a[..., MLA_NOPE:MLA_NOPE + half]], -1)

    pad_qk = lambda a: jnp.pad(a, ((0, 0), (0, 0), (0, LANES - MLA_QK))).reshape(MLA_Q_RANK, MLA_HEADS * LANES)
    wuq = jnp.concatenate([pad_qk(uq), pad_qk(swap_rope(uq))], axis=1)
    ukv = mla_w_ukv.reshape(MLA_KV_RANK, MLA_HEADS, MLA_NOPE + MLA_V)
    wuk = jnp.pad(ukv[:, :, :MLA_NOPE], ((0, 0), (0, 0), (0, LANES - MLA_NOPE))).reshape(MLA_KV_RANK, MLA_HEADS * LANES)
    wuv = ukv[:, :, MLA_NOPE:].reshape(MLA_KV_RANK, MLA_WIDTH)
    gq = jnp.concatenate([_pad_lanes(mla_q_gain[None, :]), _pad_lanes(swap_rope(mla_q_gain[None, :]))], axis=0)
    gk = _pad_lanes(mla_k_gain[None, :])
    gdq = jnp.tile(dsa_q_gain[None, :], (1, LANES // DSA_DIM))
    gdk = jnp.tile(dsa_k_gain[None, :], (1, LANES // DSA_DIM))
    wb_dsa = jnp.transpose(w_branch_dsa.reshape(DSA_KV_HEADS, DSA_GROUP, DSA_DIM, D_MODEL),
                           (1, 0, 2, 3)).reshape(DSA_WIDTH, D_MODEL)
    return (w_all, wuq.astype(jnp.bfloat16), wuk.astype(jnp.bfloat16), wuv.astype(jnp.bfloat16),
            gq, gk, gdq, gdk, wb_dsa.astype(jnp.bfloat16))


def _const_spec(shape):
    return pl.BlockSpec(shape, lambda *_: (0,) * len(shape))


def _layer(x, positions, invf, spread, norm_gain, w_in, b_merge, mla_q_norm, mla_w_uq, mla_kv_norm, mla_w_ukv,
           mla_q_gain, mla_k_gain, dsa_q_gain, dsa_k_gain, w_branch_mla, w_branch_dsa, w_out):
    bsz, s_len, d_model = x.shape
    assert d_model == D_MODEL and TM_PROJ % PROJ_ROWS == 0 and (bsz * s_len) % TM_MERGE == 0
    assert all(s_len % t == 0 for t in (TM_PROJ, TQ_MLA, TK_MLA, TQ_DSA, TK_DSA))
    k_top = min(TOPK_MAX, s_len // 4)
    (w_all, wuq, wuk, wuv, gq, gk, gdq, gdk, wb_dsa) = _layout_params(
        w_in, mla_w_uq, mla_w_ukv, mla_q_gain, mla_k_gain, dsa_q_gain, dsa_k_gain, w_branch_dsa)
    bf16, f32 = jnp.bfloat16, jnp.float32
    gmax = lambda g: jnp.max(jnp.abs(g))
    shift_a = (BOUND_MARGIN * MLA_QK ** 0.5 * LOG2_E * gmax(mla_q_gain) * gmax(mla_k_gain)).reshape(1).astype(f32)
    shift_b = (BOUND_MARGIN * DSA_DIM ** 0.5 * LOG2_E * gmax(dsa_q_gain) * gmax(dsa_k_gain)).reshape(1).astype(f32)
    smem = pl.BlockSpec(memory_space=pltpu.SMEM)

    tm = TM_PROJ
    tok = lambda w: pl.BlockSpec((1, tm, w), lambda b, i: (b, i, 0))
    head = pl.BlockSpec((1, MLA_HEADS, tm, LANES), lambda b, i: (b, 0, i, 0))
    outs = pl.pallas_call(
        _proj_kernel,
        grid=(bsz, s_len // tm),
        in_specs=[tok(D_MODEL), pl.BlockSpec((1, 1, tm), lambda b, i: (b, 0, i)),
                  _const_spec(invf.shape), _const_spec(spread.shape), _const_spec((1, D_MODEL)),
                  _const_spec((1, MLA_Q_RANK)), _const_spec((1, MLA_KV_RANK)),
                  _const_spec((2, LANES)), _const_spec((1, LANES)), _const_spec((1, LANES)), _const_spec((1, LANES)),
                  _const_spec((D_MODEL, W_ALL)), _const_spec((MLA_Q_RANK, 2 * MLA_HEADS * LANES)),
                  _const_spec((MLA_KV_RANK, MLA_HEADS * LANES)), _const_spec((MLA_KV_RANK, MLA_WIDTH))],
        out_specs=[head, head, pl.BlockSpec((1, 1, MLA_WIDTH, tm), lambda b, i: (b, i, 0, 0)),
                   head, tok(LANES),
                   pl.BlockSpec((1, 1, LANES, tm), lambda b, i: (b, i, 0, 0)),
                   tok(IDX_HEADS * IDX_DIM), tok(LANES),
                   pl.BlockSpec((1, IDX_HEADS, tm), lambda b, i: (b, 0, i)), tok(W_G)],
        out_shape=[jax.ShapeDtypeStruct((bsz, MLA_HEADS, s_len, LANES), bf16),
                   jax.ShapeDtypeStruct((bsz, MLA_HEADS, s_len, LANES), bf16),
                   jax.ShapeDtypeStruct((bsz, s_len // tm, MLA_WIDTH, tm), bf16),
                   jax.ShapeDtypeStruct((bsz, DSA_HEADS, s_len, LANES), bf16),
                   jax.ShapeDtypeStruct((bsz, s_len, LANES), bf16),
                   jax.ShapeDtypeStruct((bsz, s_len // tm, LANES, tm), bf16),
                   jax.ShapeDtypeStruct((bsz, s_len, IDX_HEADS * IDX_DIM), bf16),
                   jax.ShapeDtypeStruct((bsz, s_len, LANES), bf16),
                   jax.ShapeDtypeStruct((bsz, IDX_HEADS, s_len), f32),
                   jax.ShapeDtypeStruct((bsz, s_len, W_G), bf16)],
        compiler_params=pltpu.CompilerParams(dimension_semantics=("arbitrary", "arbitrary"),
                                             vmem_limit_bytes=VMEM_LIMIT),
        name="proj",
    )(x, positions[:, None, :], invf, spread, norm_gain[None, :], mla_q_norm[None, :], mla_kv_norm[None, :],
      gq, gk, gdq, gdk, w_all, wuq, wuk, wuv)
    qa, ka, vat, qb, kb, vbt, qi, ki, wit, gates = outs

    tq_a, tk_a, hps = TQ_MLA, TK_MLA, MLA_HEADS_PER_STEP
    o_a = pl.pallas_call(
        functools.partial(_mla_kernel, tq=tq_a, tk=tk_a, heads=hps),
        grid=(bsz, MLA_HEADS // hps, s_len // tq_a),
        in_specs=[smem, pl.BlockSpec((1, hps, tq_a, LANES), lambda b, hp, i: (b, hp, i, 0)),
                  pl.BlockSpec((1, hps, s_len, LANES), lambda b, hp, i: (b, hp, 0, 0)),
                  pl.BlockSpec((1, s_len // tm, hps * MLA_V, tm), lambda b, hp, i: (b, 0, hp, 0))],
        out_specs=pl.BlockSpec((1, tq_a, hps * MLA_V), lambda b, hp, i: (b, i, hp)),
        out_shape=jax.ShapeDtypeStruct((bsz, s_len, MLA_WIDTH), bf16),
        compiler_params=pltpu.CompilerParams(dimension_semantics=("arbitrary",) * 3,
                                             vmem_limit_bytes=VMEM_LIMIT),
        name="mla",
    )(shift_a, qa, ka, vat)

    tq_b, tk_b = TQ_DSA, TK_DSA
    row_b = lambda w: pl.BlockSpec((1, tq_b, w), lambda b, i: (b, i, 0))
    full_b = lambda w: pl.BlockSpec((1, s_len, w), lambda b, i: (b, 0, 0))
    o_b = pl.pallas_call(
        functools.partial(_dsa_kernel, tq=tq_b, tk=tk_b, k_top=k_top),
        grid=(bsz, s_len // tq_b),
        in_specs=[smem, pl.BlockSpec((1, DSA_HEADS, tq_b, LANES), lambda b, i: (b, 0, i, 0)), full_b(LANES),
                  pl.BlockSpec((1, s_len // tm, LANES, tm), lambda b, i: (b, 0, 0, 0)),
                  row_b(IDX_HEADS * IDX_DIM), full_b(LANES),
                  pl.BlockSpec((1, IDX_HEADS, tq_b), lambda b, i: (b, 0, i))],
        out_specs=row_b(DSA_WIDTH),
        out_shape=jax.ShapeDtypeStruct((bsz, s_len, DSA_WIDTH), bf16),
        scratch_shapes=[pltpu.VMEM((s_len // tk_b, tk_b, tq_b), f32),
                        pltpu.VMEM((s_len // tk_b, tk_b, tq_b), bf16),
                        pltpu.VMEM((IDX_HEADS * tq_b, LANES), bf16)],
        compiler_params=pltpu.CompilerParams(dimension_semantics=("arbitrary", "arbitrary"),
                                             vmem_limit_bytes=VMEM_LIMIT),
        name="dsa",
    )(shift_b, qb, kb, vbt, qi, ki, wit)

    tm_m = TM_MERGE
    n_tok = bsz * s_len
    flat = lambda w: pl.BlockSpec((tm_m, w), lambda i: (i, 0))
    out = pl.pallas_call(
        _merge_kernel,
        grid=(n_tok // tm_m,),
        in_specs=[flat(D_MODEL), flat(MLA_WIDTH), flat(DSA_WIDTH), flat(W_G), _const_spec((2, D_MODEL)),
                  _const_spec((MLA_WIDTH, D_MODEL)), _const_spec((DSA_WIDTH, D_MODEL)),
                  _const_spec((D_MODEL, D_MODEL))],
        out_specs=flat(D_MODEL),
        out_shape=jax.ShapeDtypeStruct((n_tok, D_MODEL), f32),
        compiler_params=pltpu.CompilerParams(dimension_semantics=("arbitrary",),
                                             vmem_limit_bytes=VMEM_LIMIT),
        name="merge",
    )(x.reshape(n_tok, D_MODEL), o_a.reshape(n_tok, MLA_WIDTH), o_b.reshape(n_tok, DSA_WIDTH),
      gates.reshape(n_tok, W_G), b_merge, w_branch_mla.astype(bf16), wb_dsa, w_out.astype(bf16))
    return out.reshape(bsz, s_len, D_MODEL)


def kernel(x, positions, norm_gain, w_in, b_merge, mla_q_norm, mla_w_uq, mla_kv_norm, mla_w_ukv,
           mla_q_gain, mla_k_gain, dsa_q_gain, dsa_k_gain, w_branch_mla, w_branch_dsa, w_out):
    invf, spread = _rope_inputs()
    for l in range(norm_gain.shape[0]):
        x = _layer(x, positions, invf, spread, norm_gain[l], w_in[l], b_merge[l], mla_q_norm[l], mla_w_uq[l],
                   mla_kv_norm[l], mla_w_ukv[l], mla_q_gain[l], mla_k_gain[l], dsa_q_gain[l], dsa_k_gain[l],
                   w_branch_mla[l], w_branch_dsa[l], w_out[l])
    return x
```

```python
import functools

import numpy as np
import jax
import jax.numpy as jnp
from jax import lax
from jax.experimental import pallas as pl
from jax.experimental.pallas import tpu as pltpu

D_MODEL = 1024
MLA_HEADS = 8
MLA_Q_RANK = 256
MLA_KV_RANK = 128
MLA_NOPE = 64
MLA_ROPE = 32
MLA_QK = MLA_NOPE + MLA_ROPE
MLA_V = 64
MLA_WIDTH = MLA_HEADS * MLA_V
DSA_HEADS = 8
DSA_KV_HEADS = 2
DSA_GROUP = DSA_HEADS // DSA_KV_HEADS
DSA_DIM = 64
DSA_WIDTH = DSA_HEADS * DSA_DIM
DSA_ROT = DSA_DIM // 4
IDX_HEADS = 8
IDX_DIM = 32
IDX_ROT = IDX_DIM // 4
TOPK_MAX = 256
ROPE_THETA = 500000.0
NORM_EPS = 1e-6

IN_SPLITS = (
    MLA_Q_RANK, MLA_KV_RANK, MLA_ROPE, MLA_WIDTH,
    DSA_HEADS * DSA_DIM, DSA_KV_HEADS * DSA_DIM, DSA_KV_HEADS * DSA_DIM, DSA_WIDTH,
    IDX_HEADS * IDX_DIM, IDX_DIM, IDX_HEADS,
    D_MODEL, D_MODEL,
)

LANES = 128
MXU_WIDTH = 256
VMEM_LIMIT = 56 * 1024 * 1024
INT_MIN = -(2 ** 31)
F32_LOWEST = float(np.finfo(np.float32).min)
BF16_ROWS = 16
FINE_BITS = 17
NEG_BIG = -1e30
LOG2_E = 1.4426950408889634
FAST_MAX_SHIFT = 60.0
BOUND_MARGIN = 1.02

OFF_A = 0
W_A = MLA_Q_RANK + MLA_KV_RANK + LANES
OFF_B = OFF_A + W_A
W_B = DSA_WIDTH + 2 * DSA_KV_HEADS * DSA_DIM
OFF_I = OFF_B + W_B
W_I = IDX_HEADS * IDX_DIM + 2 * LANES
OFF_G = OFF_I + W_I
W_G = MLA_WIDTH + DSA_WIDTH + 2 * D_MODEL
W_ALL = OFF_G + W_G
PROJ_ROWS = 256
GATE_CHUNK = 512
N_GATE_CHUNKS = W_G // GATE_CHUNK


TM_PROJ = 512
TQ_MLA, TK_MLA = 512, 512
MLA_HEADS_PER_STEP = 8
TQ_DSA, TK_DSA = 512, 512
TM_MERGE = 1024

ROPE_LAYOUTS = ((LANES, MLA_NOPE, MLA_ROPE, 0), (DSA_DIM, 0, DSA_ROT, MLA_ROPE // 2),
                (IDX_DIM, 0, IDX_ROT, MLA_ROPE // 2 + DSA_ROT // 2))
ROW_A, ROW_D, ROW_I = 0, 1, 2
FREQ_ROWS = 32
ONE_ROW = FREQ_ROWS - 1


def _rms(v, width):
    return lax.rsqrt(jnp.sum(v * v, axis=-1, keepdims=True) * (1.0 / width) + NORM_EPS)


def _rope(v, cos, sin_lo, sin_hi, half):
    return (v * cos + pltpu.roll(v, LANES - half, 1) * sin_lo + pltpu.roll(v, half, 1) * sin_hi)


def _bf16_dot(a, b):
    return jnp.dot(a, b, preferred_element_type=jnp.float32)


def _nt_dot(a, b):
    return lax.dot_general(a, b, (((1,), (1,)), ((), ())), preferred_element_type=jnp.float32)


def _proj_kernel(x_ref, pos_ref, invf_ref, spread_ref, gx_ref, gcq_ref, gckv_ref, gq_ref, gk_ref, gdq_ref, gdk_ref,
                 w_ref, wuq_ref, wuk_ref, wuv_ref,
                 qa_ref, ka_ref, va_ref, qb_ref, kb_ref, vb_ref, qi_ref, ki_ref, wi_ref, g_ref):
    bf16 = jnp.bfloat16
    subs = [slice(r0, r0 + PROJ_ROWS) for r0 in range(0, x_ref.shape[1], PROJ_ROWS)]
    hs = []
    for rows in subs:
        x = x_ref[0, rows]
        hs.append((x * _rms(x, D_MODEL) * gx_ref[...]).astype(bf16))
    pas = [_bf16_dot(h, w_ref[:, OFF_A:OFF_A + W_A]) for h in hs]
    tabs = [_rope_tables(pos_ref[0, :, rows], invf_ref, spread_ref) for rows in subs]
    lat = []
    for pa in pas:
        c_q = pa[:, :MLA_Q_RANK]
        c_kv = pa[:, MLA_Q_RANK:MLA_Q_RANK + MLA_KV_RANK]
        lat.append(((c_q * _rms(c_q, MLA_Q_RANK) * gcq_ref[...]).astype(bf16),
                    (c_kv * _rms(c_kv, MLA_KV_RANK) * gckv_ref[...]).astype(bf16)))
    pbs = [_bf16_dot(h, w_ref[:, OFF_B:OFF_B + W_B]) for h in hs]
    pis = [_bf16_dot(h, w_ref[:, OFF_I:OFF_I + W_I]) for h in hs]
    ups = [(_bf16_dot(cqn, wuq_ref[...]),
            _bf16_dot(ckvn, wuk_ref[...]),
            _bf16_dot(ckvn, wuv_ref[...])) for cqn, ckvn in lat]
    for n, rows in enumerate(subs):
        kpe = pas[n][:, MLA_Q_RANK + MLA_KV_RANK:]
        _proj_epilogue(rows, (tabs[n], kpe) + ups[n] + (pbs[n], pis[n]), gq_ref, gk_ref, gdq_ref, gdk_ref,
                       qa_ref, ka_ref, va_ref, qb_ref, kb_ref, vb_ref, qi_ref, ki_ref, wi_ref)
    for rows, h in zip(subs, hs):
        for c in range(N_GATE_CHUNKS):
            cols = slice(c * GATE_CHUNK, (c + 1) * GATE_CHUNK)
            g_ref[0, rows, cols] = _bf16_dot(h, w_ref[:, OFF_G + c * GATE_CHUNK:OFF_G + (c + 1) * GATE_CHUNK]).astype(g_ref.dtype)


def _rope_tables(pos_row, invf_ref, spread_ref):
    bf16 = jnp.bfloat16
    ang = invf_ref[...] * pos_row.astype(jnp.float32)
    trig = jnp.concatenate([jnp.cos(ang), jnp.sin(ang)], axis=0)
    t1 = trig.astype(bf16)
    r1 = trig - t1.astype(jnp.float32)
    t2 = r1.astype(bf16)
    t3 = (r1 - t2.astype(jnp.float32)).astype(bf16)
    return lax.dot_general(jnp.concatenate([t1, t2, t3], axis=0), spread_ref[...], (((0,), (0,)), ((), ())),
                           preferred_element_type=jnp.float32)


def _proj_epilogue(rows, vals, gq_ref, gk_ref, gdq_ref, gdk_ref,
                   qa_ref, ka_ref, va_ref, qb_ref, kb_ref, vb_ref, qi_ref, ki_ref, wi_ref):
    tabs, kpe, q_up, k_up, v_up, pb, pi = vals
    lane = lax.broadcasted_iota(jnp.int32, (1, LANES), 1)
    bf16 = jnp.bfloat16

    def tables(n):
        return tuple(tabs[:, (3 * n + c) * LANES:(3 * n + c + 1) * LANES] for c in range(3))

    for c in range(MLA_WIDTH // LANES):
        va_ref[0, 0, c * LANES:(c + 1) * LANES, rows] = v_up[:, c * LANES:(c + 1) * LANES].T.astype(bf16)
    cos_a, sl_a, sh_a = tables(ROW_A)
    sin_a = sl_a + sh_a
    q_scale = MLA_QK ** -0.5 * LOG2_E
    kpe_g = kpe * gk_ref[...]
    k_rot = (pltpu.roll(kpe_g, LANES - MLA_ROPE // 2, 1) * sl_a + pltpu.roll(kpe_g, MLA_ROPE // 2, 1) * sh_a)
    for hd in range(MLA_HEADS):
        qh = q_up[:, hd * LANES:(hd + 1) * LANES]
        q_sw = q_up[:, (MLA_HEADS + hd) * LANES:(MLA_HEADS + hd + 1) * LANES]
        q_rot = qh * gq_ref[0:1, :] * cos_a + q_sw * gq_ref[1:2, :] * sin_a
        qa_ref[0, hd, rows] = (q_rot * (_rms(qh, MLA_QK) * q_scale)).astype(bf16)
        kh = k_up[:, hd * LANES:(hd + 1) * LANES] + kpe
        ka_ref[0, hd, rows] = ((kh * gk_ref[...] * cos_a + k_rot) * _rms(kh, MLA_QK)).astype(bf16)

    cos_d, sl_d, sh_d = tables(ROW_D)
    low = lane < DSA_DIM

    def dsa_norm_rope(blk, gain):
        sq = blk * blk
        r_lo = lax.rsqrt(jnp.sum(jnp.where(low, sq, 0.0), axis=-1, keepdims=True) * (1.0 / DSA_DIM) + NORM_EPS)
        r_hi = lax.rsqrt(jnp.sum(jnp.where(low, 0.0, sq), axis=-1, keepdims=True) * (1.0 / DSA_DIM) + NORM_EPS)
        return _rope(blk * jnp.where(low, r_lo, r_hi) * gain, cos_d, sl_d, sh_d, DSA_ROT // 2)

    dsa_scale = DSA_DIM ** -0.5 * LOG2_E
    for j in range(DSA_WIDTH // LANES):
        blk = pb[:, j * LANES:(j + 1) * LANES]
        qn = (dsa_norm_rope(blk, gdq_ref[...]) * dsa_scale).astype(bf16)
        qb_ref[0, j, rows] = jnp.where(low, qn, jnp.zeros_like(qn))
        qb_ref[0, DSA_GROUP + j, rows] = jnp.where(low, jnp.zeros_like(qn), qn)
    kb_ref[0, rows] = dsa_norm_rope(pb[:, DSA_WIDTH:DSA_WIDTH + LANES], gdk_ref[...]).astype(bf16)
    vb_ref[0, 0, :, rows] = pb[:, DSA_WIDTH + LANES:].T.astype(bf16)

    cos_i, sl_i, sh_i = tables(ROW_I)
    n_qi = IDX_HEADS * IDX_DIM // LANES
    for j in range(n_qi + 1):
        blk = _rope(pi[:, j * LANES:(j + 1) * LANES], cos_i, sl_i, sh_i, IDX_ROT // 2).astype(bf16)
        if j < n_qi:
            qi_ref[0, rows, j * LANES:(j + 1) * LANES] = blk
        else:
            ki_ref[0, rows] = blk
    wi_t = (pi[:, (n_qi + 1) * LANES:] * (IDX_HEADS ** -0.5 * IDX_DIM ** -0.5)).T
    wi_ref[0, :, rows] = wi_t[0:IDX_HEADS, :]


def _softmax_tile(s, m, l, shift):
    if shift is None:
        m_new = jnp.maximum(m, jnp.max(s, axis=0, keepdims=True))
        alpha = jnp.exp2(m - m_new)
        p = jnp.exp2(s - m_new)
        return m_new, alpha * l + jnp.sum(p, axis=0, keepdims=True), alpha, p.astype(jnp.bfloat16)
    p = jnp.exp2(s - shift)
    return m, l + jnp.sum(p, axis=0, keepdims=True), None, p.astype(jnp.bfloat16)


def _by_shift(shift_ref, body):
    shift = shift_ref[0]

    @pl.when(shift <= FAST_MAX_SHIFT)
    def _():
        body(shift)

    @pl.when(shift > FAST_MAX_SHIFT)
    def _():
        body(None)


def _mla_kernel(shift_ref, q_ref, k_ref, vt_ref, o_ref, *, tq, tk, heads):
    i = pl.program_id(2)
    q_start = i * tq
    n_full = q_start // tk
    n_all = (q_start + tq + tk - 1) // tk
    tv = vt_ref.shape[-1]
    kpos0 = lax.broadcasted_iota(jnp.int32, (tk, tq), 0)
    qpos = q_start + lax.broadcasted_iota(jnp.int32, (tk, tq), 1)

    def attend(shift):
        def step(j, carry, masked):
            off = pl.multiple_of(j * tk, tk)
            scores = [_nt_dot(k_ref[0, hd, pl.ds(off, tk), :], q_ref[0, hd]) for hd in range(heads)]
            out = []
            visible = kpos0 + off <= qpos if masked else None
            shift_t = jnp.where(visible, shift, -NEG_BIG) if masked and shift is not None else shift
            for hd in range(heads):
                m, l, acc = carry[hd]
                s = scores[hd]
                if masked and shift is None:
                    s = jnp.where(visible, s, NEG_BIG)
                m, l, alpha, pb = _softmax_tile(s, m, l, shift_t)
                if alpha is not None:
                    acc = alpha * acc
                for c in range(tk // tv):
                    vt = vt_ref[0, j * (tk // tv) + c, hd * MLA_V:(hd + 1) * MLA_V, :]
                    acc = acc + _bf16_dot(vt, pb[c * tv:(c + 1) * tv])
                out.append((m, l, acc))
            return tuple(out)

        init = tuple((jnp.full((1, tq), NEG_BIG, jnp.float32), jnp.zeros((1, tq), jnp.float32),
                      jnp.zeros((MLA_V, tq), jnp.float32)) for _ in range(heads))
        carry = lax.fori_loop(0, n_full, functools.partial(step, masked=False), init)
        carry = lax.fori_loop(n_full, n_all, functools.partial(step, masked=True), carry)
        outs = [acc / l for (_, l, acc) in carry]
        for c in range(heads // 2):
            o_ref[0, :, c * LANES:(c + 1) * LANES] = jnp.concatenate(outs[2 * c:2 * c + 2], axis=0).T.astype(o_ref.dtype)

    _by_shift(shift_ref, attend)


def _dsa_kernel(shift_ref, qb_ref, kb_ref, vt_ref, qi_ref, ki_ref, wt_ref, o_ref,
                sc_ref, sb_ref, qim_ref, *, tq, tk, k_top):
    i = pl.program_id(1)
    q_start = i * tq
    n_kv = (q_start + tq + tk - 1) // tk
    n_heads = IDX_HEADS
    tv = vt_ref.shape[-1]

    lane_q = lax.broadcasted_iota(jnp.int32, (tq, LANES), 1)
    per_blk = LANES // IDX_DIM
    for hd in range(IDX_HEADS):
        blk = qi_ref[0, :, (hd // per_blk) * LANES:(hd // per_blk + 1) * LANES]
        slot = hd % per_blk
        in_head = (lane_q >= slot * IDX_DIM) & (lane_q < (slot + 1) * IDX_DIM)
        qim_ref[hd * tq:(hd + 1) * tq, :] = jnp.where(in_head, blk, jnp.zeros_like(blk))

    kpos0 = lax.broadcasted_iota(jnp.int32, (tk, tq), 0)
    qpos = q_start + lax.broadcasted_iota(jnp.int32, (tk, tq), 1)
    w_rows = [wt_ref[0, hd:hd + 1, :] for hd in range(IDX_HEADS)]

    def score_tile(j, _):
        off = pl.multiple_of(j * tk, tk)
        s_all = _nt_dot(ki_ref[0, pl.ds(off, tk), :], qim_ref[...])
        sc = w_rows[0] * jnp.maximum(s_all[:, 0:tq], 0.0)
        for hd in range(1, n_heads):
            sc = sc + w_rows[hd] * jnp.maximum(s_all[:, hd * tq:(hd + 1) * tq], 0.0)
        sc = jnp.where(kpos0 + off <= qpos, sc, -jnp.inf)
        sc_ref[j] = sc
        sb_ref[j] = sc.astype(jnp.bfloat16)
        return 0

    lax.fori_loop(0, n_kv, score_tile, 0)

    def pattern_value(u):
        key = u ^ INT_MIN
        return lax.bitcast_convert_type(key ^ ((key >> 31) & 0x7FFFFFFF), jnp.float32)

    def count(pred):
        def body(j, acc):
            hit = jnp.where(pred(sc_ref[j], j), 1, 0)
            return acc + jnp.sum(hit.reshape(tk // 8, 8, tq), axis=0)
        acc = lax.fori_loop(0, n_kv, body, jnp.zeros((8, tq), jnp.int32))
        return jnp.sum(acc, axis=0, keepdims=True)

    def count_coarse(cand_b):
        one, zero = jnp.ones((), jnp.bfloat16), jnp.zeros((), jnp.bfloat16)

        def body(j, acc):
            part = None
            for r in range(tk // BF16_ROWS):
                rows = sb_ref[j, r * BF16_ROWS:(r + 1) * BF16_ROWS, :]
                hit = jnp.where(rows >= cand_b, one, zero)
                part = hit if part is None else part + hit
            return acc + part.astype(jnp.float32)
        acc = lax.fori_loop(0, n_kv, body, jnp.zeros((BF16_ROWS, tq), jnp.float32))
        return jnp.sum(acc, axis=0, keepdims=True)

    def coarse_bit(it, t16):
        cand = t16 | jnp.left_shift(jnp.int32(1), 15 - it)
        cand32 = jnp.left_shift(cand, 16) | jnp.where(cand < 0x8000, 0xFFFF, 0)
        cnt = count_coarse(pattern_value(cand32).astype(jnp.bfloat16))
        return jnp.where(cnt >= k_top, cand, t16)

    t16 = lax.fori_loop(0, 16, coarse_bit, jnp.zeros((1, tq), jnp.int32))
    few = t16 < 0x0080
    base = jnp.left_shift(t16, 16) + jnp.where(t16 < 0x8000, 0x7FFF, -0x8000)
    base = jnp.where(few, 0x00800000, base)

    def fine_bit(it, carry):
        t, cnt_t = carry
        cand = t | jnp.left_shift(jnp.int32(1), FINE_BITS - 1 - it)
        cand_f = pattern_value(base + cand)
        cnt = count(lambda sc, j: sc >= cand_f)
        take = cnt >= k_top
        return jnp.where(take, cand, t), jnp.where(take, cnt, cnt_t)

    t, cnt_ge = lax.fori_loop(0, FINE_BITS, fine_bit,
                              (jnp.zeros((1, tq), jnp.int32), jnp.full((1, tq), k_top, jnp.int32)))
    thr = jnp.where(few, F32_LOWEST, pattern_value(base + t))
    tied = jnp.logical_not(few) & ((cnt_ge > k_top) | (t == 0))

    def tie_cut():
        cnt_gt = count(lambda sc, j: sc > thr)
        need = k_top - cnt_gt

        def index_bit(it, c):
            cand = c | jnp.left_shift(jnp.int32(1), 12 - it)
            below = count(lambda sc, j: (sc == thr) & (kpos0 + j * tk < cand))
            return jnp.where(below < need, cand, c)

        c = lax.fori_loop(0, 13, index_bit, jnp.zeros((1, tq), jnp.int32))
        return jnp.where(tied, c, jnp.int32(2 ** 30))

    cut = lax.cond(jnp.max(tied.astype(jnp.int32)) > 0, tie_cut,
                   lambda: jnp.full((1, tq), 2 ** 30, jnp.int32))

    assert tq >= MXU_WIDTH

    def attend_all(shift):
        def attend(j, carry):
            off = pl.multiple_of(j * tk, tk)
            kt = kb_ref[0, pl.ds(off, tk), :]
            scores = [_nt_dot(kt, qb_ref[0, hd]) for hd in range(n_heads)]
            sc = sc_ref[j]
            sel = (sc > thr) | ((sc == thr) & (kpos0 + off <= cut))
            shift_t = None if shift is None else jnp.where(sel, shift, -NEG_BIG)
            out = []
            for hd in range(n_heads):
                m, l, acc = carry[hd]
                s = jnp.where(sel, scores[hd], NEG_BIG) if shift is None else scores[hd]
                m, l, alpha, pb = _softmax_tile(s, m, l, shift_t)
                if alpha is not None:
                    acc = alpha * acc
                g = hd // DSA_GROUP
                for c in range(tk // tv):
                    vt = vt_ref[0, j * (tk // tv) + c, g * DSA_DIM:(g + 1) * DSA_DIM, :]
                    acc = acc + _bf16_dot(vt, pb[c * tv:(c + 1) * tv])
                out.append((m, l, acc))
            return tuple(out)

        init = tuple((jnp.full((1, tq), NEG_BIG, jnp.float32), jnp.zeros((1, tq), jnp.float32),
                      jnp.zeros((DSA_DIM, tq), jnp.float32)) for _ in range(n_heads))
        carry = lax.fori_loop(0, n_kv, attend, init)
        outs = [acc / l for (_, l, acc) in carry]
        for j in range(DSA_GROUP):
            o_ref[0, :, j * LANES:(j + 1) * LANES] = jnp.concatenate(
                [outs[j], outs[DSA_GROUP + j]], axis=0).T.astype(o_ref.dtype)

    _by_shift(shift_ref, attend_all)


def _merge_kernel(x_ref, oa_ref, ob_ref, g_ref, bm_ref, wa_ref, wb_ref, wo_ref, out_ref):
    f32 = jnp.float32
    ga = g_ref[:, 0:MLA_WIDTH].astype(f32)
    gb = g_ref[:, MLA_WIDTH:MLA_WIDTH + DSA_WIDTH].astype(f32)
    ma = g_ref[:, MLA_WIDTH + DSA_WIDTH:MLA_WIDTH + DSA_WIDTH + D_MODEL].astype(f32)
    mb = g_ref[:, MLA_WIDTH + DSA_WIDTH + D_MODEL:].astype(f32)
    ya = _bf16_dot((oa_ref[...].astype(f32) * (ga * jax.nn.sigmoid(ga))).astype(jnp.bfloat16), wa_ref[...])
    yb = _bf16_dot((ob_ref[...].astype(f32) * (gb * jax.nn.sigmoid(gb))).astype(jnp.bfloat16), wb_ref[...])
    merged = jax.nn.sigmoid(ma + bm_ref[0:1, :]) * ya + jax.nn.sigmoid(mb + bm_ref[1:2, :]) * yb
    out_ref[...] = x_ref[...] + _bf16_dot(merged.astype(jnp.bfloat16), wo_ref[...])


def _inv_freq(rot_dim):
    return ROPE_THETA ** (-jnp.arange(0, rot_dim, 2, dtype=jnp.float32) / rot_dim)


def _rope_inputs():
    invf = jnp.concatenate([_inv_freq(rot) for (_, _, rot, _) in ROPE_LAYOUTS])
    invf = jnp.pad(invf, (0, FREQ_ROWS - invf.shape[0]))[:, None]
    spread = np.zeros((2 * FREQ_ROWS, 3 * len(ROPE_LAYOUTS) * LANES), np.float32)
    for n, (period, start, rot, row0) in enumerate(ROPE_LAYOUTS):
        half = rot // 2
        for lane in range(LANES):
            p = lane % period - start
            cos_col, lo_col, hi_col = ((3 * n + c) * LANES + lane for c in range(3))
            if 0 <= p < rot:
                spread[row0 + p % half, cos_col] = 1.0
                spread[FREQ_ROWS + row0 + p % half, lo_col if p < half else hi_col] = -1.0 if p < half else 1.0
            else:
                spread[ONE_ROW, cos_col] = 1.0
    return invf, jnp.asarray(np.tile(spread, (3, 1)), jnp.bfloat16)


def _pad_lanes(a, start=0):
    return jnp.pad(a, ((0, 0), (start, LANES - start - a.shape[1])))


def _layout_params(w_in, mla_w_uq, mla_w_ukv, mla_q_gain, mla_k_gain, dsa_q_gain, dsa_k_gain,
                   w_branch_dsa):
    offs = np.concatenate([[0], np.cumsum(IN_SPLITS)])
    w_in = w_in.astype(jnp.bfloat16)
    (c_q, c_kv, k_pe, gate_a, q_b, k_b, v_b, gate_b, q_i, k_i, w_i, m_a, m_b) = [
        w_in[:, offs[n]:offs[n + 1]] for n in range(len(IN_SPLITS))]
    zeros = lambda n: jnp.zeros((w_in.shape[0], n), jnp.bfloat16)

    def pair_heads(w):
        rows = w.shape[0]
        w = w.reshape(rows, DSA_KV_HEADS, DSA_GROUP, DSA_DIM)
        return jnp.transpose(w, (0, 2, 1, 3)).reshape(rows, DSA_WIDTH)

    w_all = jnp.concatenate(
        [c_q, c_kv, zeros(MLA_NOPE), k_pe, zeros(LANES - MLA_NOPE - MLA_ROPE),
         pair_heads(q_b), k_b, v_b,
         q_i] + [k_i] * (LANES // IDX_DIM) + [w_i, zeros(LANES - IDX_HEADS),
         gate_a, pair_heads(gate_b), m_a, m_b], axis=1)
    assert w_all.shape[1] == W_ALL
    uq = mla_w_uq.reshape(MLA_Q_RANK, MLA_HEADS, MLA_QK)
    half = MLA_ROPE // 2

    def swap_rope(a):
        return jnp.concatenate([jnp.zeros_like(a[..., :MLA_NOPE]), a[..., MLA_NOPE + half:], a[..., MLA_NOPE:MLA_NOPE + half]], -1)

    pad_qk = lambda a: jnp.pad(a, ((0, 0), (0, 0), (0, LANES - MLA_QK))).reshape(MLA_Q_RANK, MLA_HEADS * LANES)
    wuq = jnp.concatenate([pad_qk(uq), pad_qk(swap_rope(uq))], axis=1)
    ukv = mla_w_ukv.reshape(MLA_KV_RANK, MLA_HEADS, MLA_NOPE + MLA_V)
    wuk = jnp.pad(ukv[:, :, :MLA_NOPE], ((0, 0), (0, 0), (0, LANES - MLA_NOPE))).reshape(MLA_KV_RANK, MLA_HEADS * LANES)
    wuv = ukv[:, :, MLA_NOPE:].reshape(MLA_KV_RANK, MLA_WIDTH)
    gq = jnp.concatenate([_pad_lanes(mla_q_gain[None, :]), _pad_lanes(swap_rope(mla_q_gain[None, :]))], axis=0)
    gk = _pad_lanes(mla_k_gain[None, :])
    gdq = jnp.tile(dsa_q_gain[None, :], (1, LANES // DSA_DIM))
    gdk = jnp.tile(dsa_k_gain[None, :], (1, LANES // DSA_DIM))
    wb_dsa = jnp.transpose(w_branch_dsa.reshape(DSA_KV_HEADS, DSA_GROUP, DSA_DIM, D_MODEL),
                           (1, 0, 2, 3)).reshape(DSA_WIDTH, D_MODEL)
    return (w_all, wuq.astype(jnp.bfloat16), wuk.astype(jnp.bfloat16), wuv.astype(jnp.bfloat16),
            gq, gk, gdq, gdk, wb_dsa.astype(jnp.bfloat16))


def _const_spec(shape):
    return pl.BlockSpec(shape, lambda *_: (0,) * len(shape))


def _layer(x, positions, invf, spread, norm_gain, w_in, b_merge, mla_q_norm, mla_w_uq, mla_kv_norm, mla_w_ukv,
           mla_q_gain, mla_k_gain, dsa_q_gain, dsa_k_gain, w_branch_mla, w_branch_dsa, w_out):
    bsz, s_len, d_model = x.shape
    assert d_model == D_MODEL and TM_PROJ % PROJ_ROWS == 0 and (bsz * s_len) % TM_MERGE == 0
    assert all(s_len % t == 0 for t in (TM_PROJ, TQ_MLA, TK_MLA, TQ_DSA, TK_DSA))
    k_top = min(TOPK_MAX, s_len // 4)
    (w_all, wuq, wuk, wuv, gq, gk, gdq, gdk, wb_dsa) = _layout_params(
        w_in, mla_w_uq, mla_w_ukv, mla_q_gain, mla_k_gain, dsa_q_gain, dsa_k_gain, w_branch_dsa)
    bf16, f32 = jnp.bfloat16, jnp.float32
    gmax = lambda g: jnp.max(jnp.abs(g))
    shift_a = (BOUND_MARGIN * MLA_QK ** 0.5 * LOG2_E * gmax(mla_q_gain) * gmax(mla_k_gain)).reshape(1).astype(f32)
    shift_b = (BOUND_MARGIN * DSA_DIM ** 0.5 * LOG2_E * gmax(dsa_q_gain) * gmax(dsa_k_gain)).reshape(1).astype(f32)
    smem = pl.BlockSpec(memory_space=pltpu.SMEM)

    tm = TM_PROJ
    tok = lambda w: pl.BlockSpec((1, tm, w), lambda b, i: (b, i, 0))
    head = pl.BlockSpec((1, MLA_HEADS, tm, LANES), lambda b, i: (b, 0, i, 0))
    outs = pl.pallas_call(
        _proj_kernel,
        grid=(bsz, s_len // tm),
        in_specs=[tok(D_MODEL), pl.BlockSpec((1, 1, tm), lambda b, i: (b, 0, i)),
                  _const_spec(invf.shape), _const_spec(spread.shape), _const_spec((1, D_MODEL)),
                  _const_spec((1, MLA_Q_RANK)), _const_spec((1, MLA_KV_RANK)),
                  _const_spec((2, LANES)), _const_spec((1, LANES)), _const_spec((1, LANES)), _const_spec((1, LANES)),
                  _const_spec((D_MODEL, W_ALL)), _const_spec((MLA_Q_RANK, 2 * MLA_HEADS * LANES)),
                  _const_spec((MLA_KV_RANK, MLA_HEADS * LANES)), _const_spec((MLA_KV_RANK, MLA_WIDTH))],
        out_specs=[head, head, pl.BlockSpec((1, 1, MLA_WIDTH, tm), lambda b, i: (b, i, 0, 0)),
                   head, tok(LANES),
                   pl.BlockSpec((1, 1, LANES, tm), lambda b, i: (b, i, 0, 0)),
                   tok(IDX_HEADS * IDX_DIM), tok(LANES),
                   pl.BlockSpec((1, IDX_HEADS, tm), lambda b, i: (b, 0, i)), tok(W_G)],
        out_shape=[jax.ShapeDtypeStruct((bsz, MLA_HEADS, s_len, LANES), bf16),
                   jax.ShapeDtypeStruct((bsz, MLA_HEADS, s_len, LANES), bf16),
                   jax.ShapeDtypeStruct((bsz, s_len // tm, MLA_WIDTH, tm), bf16),
                   jax.ShapeDtypeStruct((bsz, DSA_HEADS, s_len, LANES), bf16),
                   jax.ShapeDtypeStruct((bsz, s_len, LANES), bf16),
                   jax.ShapeDtypeStruct((bsz, s_len // tm, LANES, tm), bf16),
                   jax.ShapeDtypeStruct((bsz, s_len, IDX_HEADS * IDX_DIM), bf16),
                   jax.ShapeDtypeStruct((bsz, s_len, LANES), bf16),
                   jax.ShapeDtypeStruct((bsz, IDX_HEADS, s_len), f32),
                   jax.ShapeDtypeStruct((bsz, s_len, W_G), bf16)],
        compiler_params=pltpu.CompilerParams(dimension_semantics=("arbitrary", "arbitrary"),
                                             vmem_limit_bytes=VMEM_LIMIT),
        name="proj",
    )(x, positions[:, None, :], invf, spread, norm_gain[None, :], mla_q_norm[None, :], mla_kv_norm[None, :],
      gq, gk, gdq, gdk, w_all, wuq, wuk, wuv)
    qa, ka, vat, qb, kb, vbt, qi, ki, wit, gates = outs

    tq_a, tk_a, hps = TQ_MLA, TK_MLA, MLA_HEADS_PER_STEP
    o_a = pl.pallas_call(
        functools.partial(_mla_kernel, tq=tq_a, tk=tk_a, heads=hps),
        grid=(bsz, MLA_HEADS // hps, s_len // tq_a),
        in_specs=[smem, pl.BlockSpec((1, hps, tq_a, LANES), lambda b, hp, i: (b, hp, i, 0)),
                  pl.BlockSpec((1, hps, s_len, LANES), lambda b, hp, i: (b, hp, 0, 0)),
                  pl.BlockSpec((1, s_len // tm, hps * MLA_V, tm), lambda b, hp, i: (b, 0, hp, 0))],
        out_specs=pl.BlockSpec((1, tq_a, hps * MLA_V), lambda b, hp, i: (b, i, hp)),
        out_shape=jax.ShapeDtypeStruct((bsz, s_len, MLA_WIDTH), bf16),
        compiler_params=pltpu.CompilerParams(dimension_semantics=("arbitrary",) * 3,
                                             vmem_limit_bytes=VMEM_LIMIT),
        name="mla",
    )(shift_a, qa, ka, vat)

    tq_b, tk_b = TQ_DSA, TK_DSA
    row_b = lambda w: pl.BlockSpec((1, tq_b, w), lambda b, i: (b, i, 0))
    full_b = lambda w: pl.BlockSpec((1, s_len, w), lambda b, i: (b, 0, 0))
    o_b = pl.pallas_call(
        functools.partial(_dsa_kernel, tq=tq_b, tk=tk_b, k_top=k_top),
        grid=(bsz, s_len // tq_b),
        in_specs=[smem, pl.BlockSpec((1, DSA_HEADS, tq_b, LANES), lambda b, i: (b, 0, i, 0)), full_b(LANES),
                  pl.BlockSpec((1, s_len // tm, LANES, tm), lambda b, i: (b, 0, 0, 0)),
                  row_b(IDX_HEADS * IDX_DIM), full_b(LANES),
                  pl.BlockSpec((1, IDX_HEADS, tq_b), lambda b, i: (b, 0, i))],
        out_specs=row_b(DSA_WIDTH),
        out_shape=jax.ShapeDtypeStruct((bsz, s_len, DSA_WIDTH), bf16),
        scratch_shapes=[pltpu.VMEM((s_len // tk_b, tk_b, tq_b), f32),
                        pltpu.VMEM((s_len // tk_b, tk_b, tq_b), bf16),
                        pltpu.VMEM((IDX_HEADS * tq_b, LANES), bf16)],
        compiler_params=pltpu.CompilerParams(dimension_semantics=("arbitrary", "arbitrary"),
                                             vmem_limit_bytes=VMEM_LIMIT),
        name="dsa",
    )(shift_b, qb, kb, vbt, qi, ki, wit)

    tm_m = TM_MERGE
    n_tok = bsz * s_len
    flat = lambda w: pl.BlockSpec((tm_m, w), lambda i: (i, 0))
    out = pl.pallas_call(
        _merge_kernel,
        grid=(n_tok // tm_m,),
        in_specs=[flat(D_MODEL), flat(MLA_WIDTH), flat(DSA_WIDTH), flat(W_G), _const_spec((2, D_MODEL)),
                  _const_spec((MLA_WIDTH, D_MODEL)), _const_spec((DSA_WIDTH, D_MODEL)),
                  _const_spec((D_MODEL, D_MODEL))],
        out_specs=flat(D_MODEL),
        out_shape=jax.ShapeDtypeStruct((n_tok, D_MODEL), f32),
        compiler_params=pltpu.CompilerParams(dimension_semantics=("arbitrary",),
                                             vmem_limit_bytes=VMEM_LIMIT),
        name="merge",
    )(x.reshape(n_tok, D_MODEL), o_a.reshape(n_tok, MLA_WIDTH), o_b.reshape(n_tok, DSA_WIDTH),
      gates.reshape(n_tok, W_G), b_merge, w_branch_mla.astype(bf16), wb_dsa, w_out.astype(bf16))
    return out.reshape(bsz, s_len, D_MODEL)


def kernel(x, positions, norm_gain, w_in, b_merge, mla_q_norm, mla_w_uq, mla_kv_norm, mla_w_ukv,
           mla_q_gain, mla_k_gain, dsa_q_gain, dsa_k_gain, w_branch_mla, w_branch_dsa, w_out):
    invf, spread = _rope_inputs()
    for l in range(norm_gain.shape[0]):
        x = _layer(x, positions, invf, spread, norm_gain[l], w_in[l], b_merge[l], mla_q_norm[l], mla_w_uq[l],
                   mla_kv_norm[l], mla_w_ukv[l], mla_q_gain[l], mla_k_gain[l], dsa_q_gain[l], dsa_k_gain[l],
                   w_branch_mla[l], w_branch_dsa[l], w_out[l])
    return x
```

```python
import functools

import numpy as np
import jax
import jax.numpy as jnp
from jax import lax
from jax.experimental import pallas as pl
from jax.experimental.pallas import tpu as pltpu

D_MODEL = 1024
MLA_HEADS = 8
MLA_Q_RANK = 256
MLA_KV_RANK = 128
MLA_NOPE = 64
MLA_ROPE = 32
MLA_QK = MLA_NOPE + MLA_ROPE
MLA_V = 64
MLA_WIDTH = MLA_HEADS * MLA_V
DSA_HEADS = 8
DSA_KV_HEADS = 2
DSA_GROUP = DSA_HEADS // DSA_KV_HEADS
DSA_DIM = 64
DSA_WIDTH = DSA_HEADS * DSA_DIM
DSA_ROT = DSA_DIM // 4
IDX_HEADS = 8
IDX_DIM = 32
IDX_ROT = IDX_DIM // 4
TOPK_MAX = 256
ROPE_THETA = 500000.0
NORM_EPS = 1e-6

IN_SPLITS = (
    MLA_Q_RANK, MLA_KV_RANK, MLA_ROPE, MLA_WIDTH,
    DSA_HEADS * DSA_DIM, DSA_KV_HEADS * DSA_DIM, DSA_KV_HEADS * DSA_DIM, DSA_WIDTH,
    IDX_HEADS * IDX_DIM, IDX_DIM, IDX_HEADS,
    D_MODEL, D_MODEL,
)

LANES = 128
MXU_WIDTH = 256
VMEM_LIMIT = 56 * 1024 * 1024
INT_MIN = -(2 ** 31)
F32_LOWEST = float(np.finfo(np.float32).min)
BF16_ROWS = 16
HEAD_BITS = 3
FINE_BITS = 17
NEG_BIG = -1e30
LOG2_E = 1.4426950408889634
FAST_MAX_SHIFT = 60.0
BOUND_MARGIN = 1.02

OFF_A = 0
W_A = MLA_Q_RANK + MLA_KV_RANK + LANES
OFF_B = OFF_A + W_A
W_B = DSA_WIDTH + 2 * DSA_KV_HEADS * DSA_DIM
OFF_I = OFF_B + W_B
W_I = IDX_HEADS * IDX_DIM + 2 * LANES
OFF_G = OFF_I + W_I
W_G = MLA_WIDTH + DSA_WIDTH + 2 * D_MODEL
W_ALL = OFF_G + W_G
PROJ_ROWS = 256
GATE_CHUNK = 512
N_GATE_CHUNKS = W_G // GATE_CHUNK


TM_PROJ = 512
TQ_MLA, TK_MLA = 512, 512
MLA_HEADS_PER_STEP = 8
TQ_DSA, TK_DSA = 512, 512
TM_MERGE = 1024

ROPE_LAYOUTS = ((LANES, MLA_NOPE, MLA_ROPE, 0), (DSA_DIM, 0, DSA_ROT, MLA_ROPE // 2),
                (IDX_DIM, 0, IDX_ROT, MLA_ROPE // 2 + DSA_ROT // 2))
ROW_A, ROW_D, ROW_I = 0, 1, 2
FREQ_ROWS = 32
ONE_ROW = FREQ_ROWS - 1


def _bf16_pattern_value(u16):
    u = (u16 << 16) | (0xFFFF if u16 < 0x8000 else 0)
    bits = u ^ 0x80000000 if u >= 0x80000000 else ~u & 0xFFFFFFFF
    return float(np.array([bits], np.uint32).view(np.float32)[0])


def _rms(v, width):
    return lax.rsqrt(jnp.sum(v * v, axis=-1, keepdims=True) * (1.0 / width) + NORM_EPS)


def _rope(v, cos, sin_lo, sin_hi, half):
    return (v * cos + pltpu.roll(v, LANES - half, 1) * sin_lo + pltpu.roll(v, half, 1) * sin_hi)


def _bf16_dot(a, b):
    return jnp.dot(a, b, preferred_element_type=jnp.float32)


def _nt_dot(a, b):
    return lax.dot_general(a, b, (((1,), (1,)), ((), ())), preferred_element_type=jnp.float32)


def _proj_kernel(x_ref, pos_ref, invf_ref, spread_ref, gx_ref, gcq_ref, gckv_ref, gq_ref, gk_ref, gdq_ref, gdk_ref,
                 w_ref, wuq_ref, wuk_ref, wuv_ref,
                 qa_ref, ka_ref, va_ref, qb_ref, kb_ref, vb_ref, qi_ref, ki_ref, wi_ref, g_ref):
    bf16 = jnp.bfloat16
    subs = [slice(r0, r0 + PROJ_ROWS) for r0 in range(0, x_ref.shape[1], PROJ_ROWS)]
    hs = []
    for rows in subs:
        x = x_ref[0, rows]
        hs.append((x * _rms(x, D_MODEL) * gx_ref[...]).astype(bf16))
    pas = [_bf16_dot(h, w_ref[:, OFF_A:OFF_A + W_A]) for h in hs]
    tabs = [_rope_tables(pos_ref[0, :, rows], invf_ref, spread_ref) for rows in subs]
    lat = []
    for pa in pas:
        c_q = pa[:, :MLA_Q_RANK]
        c_kv = pa[:, MLA_Q_RANK:MLA_Q_RANK + MLA_KV_RANK]
        lat.append(((c_q * _rms(c_q, MLA_Q_RANK) * gcq_ref[...]).astype(bf16),
                    (c_kv * _rms(c_kv, MLA_KV_RANK) * gckv_ref[...]).astype(bf16)))
    pbs = [_bf16_dot(h, w_ref[:, OFF_B:OFF_B + W_B]) for h in hs]
    pis = [_bf16_dot(h, w_ref[:, OFF_I:OFF_I + W_I]) for h in hs]
    ups = [(_bf16_dot(cqn, wuq_ref[...]),
            _bf16_dot(ckvn, wuk_ref[...]),
            _bf16_dot(ckvn, wuv_ref[...])) for cqn, ckvn in lat]
    for n, rows in enumerate(subs):
        kpe = pas[n][:, MLA_Q_RANK + MLA_KV_RANK:]
        _proj_epilogue(rows, (tabs[n], kpe) + ups[n] + (pbs[n], pis[n]), gq_ref, gk_ref, gdq_ref, gdk_ref,
                       qa_ref, ka_ref, va_ref, qb_ref, kb_ref, vb_ref, qi_ref, ki_ref, wi_ref)
    for rows, h in zip(subs, hs):
        for c in range(N_GATE_CHUNKS):
            cols = slice(c * GATE_CHUNK, (c + 1) * GATE_CHUNK)
            g_ref[0, rows, cols] = _bf16_dot(h, w_ref[:, OFF_G + c * GATE_CHUNK:OFF_G + (c + 1) * GATE_CHUNK]).astype(g_ref.dtype)


def _rope_tables(pos_row, invf_ref, spread_ref):
    bf16 = jnp.bfloat16
    ang = invf_ref[...] * pos_row.astype(jnp.float32)
    trig = jnp.concatenate([jnp.cos(ang), jnp.sin(ang)], axis=0)
    t1 = trig.astype(bf16)
    r1 = trig - t1.astype(jnp.float32)
    t2 = r1.astype(bf16)
    t3 = (r1 - t2.astype(jnp.float32)).astype(bf16)
    return lax.dot_general(jnp.concatenate([t1, t2, t3], axis=0), spread_ref[...], (((0,), (0,)), ((), ())),
                           preferred_element_type=jnp.float32)


def _proj_epilogue(rows, vals, gq_ref, gk_ref, gdq_ref, gdk_ref,
                   qa_ref, ka_ref, va_ref, qb_ref, kb_ref, vb_ref, qi_ref, ki_ref, wi_ref):
    tabs, kpe, q_up, k_up, v_up, pb, pi = vals
    lane = lax.broadcasted_iota(jnp.int32, (1, LANES), 1)
    bf16 = jnp.bfloat16

    def tables(n):
        return tuple(tabs[:, (3 * n + c) * LANES:(3 * n + c + 1) * LANES] for c in range(3))

    for c in range(MLA_WIDTH // LANES):
        va_ref[0, 0, c * LANES:(c + 1) * LANES, rows] = v_up[:, c * LANES:(c + 1) * LANES].T.astype(bf16)
    cos_a, sl_a, sh_a = tables(ROW_A)
    sin_a = sl_a + sh_a
    q_scale = MLA_QK ** -0.5 * LOG2_E
    kpe_g = kpe * gk_ref[...]
    k_rot = (pltpu.roll(kpe_g, LANES - MLA_ROPE // 2, 1) * sl_a + pltpu.roll(kpe_g, MLA_ROPE // 2, 1) * sh_a)
    for hd in range(MLA_HEADS):
        qh = q_up[:, hd * LANES:(hd + 1) * LANES]
        q_sw = q_up[:, (MLA_HEADS + hd) * LANES:(MLA_HEADS + hd + 1) * LANES]
        q_rot = qh * gq_ref[0:1, :] * cos_a + q_sw * gq_ref[1:2, :] * sin_a
        qa_ref[0, hd, rows] = (q_rot * (_rms(qh, MLA_QK) * q_scale)).astype(bf16)
        kh = k_up[:, hd * LANES:(hd + 1) * LANES] + kpe
        ka_ref[0, hd, rows] = ((kh * gk_ref[...] * cos_a + k_rot) * _rms(kh, MLA_QK)).astype(bf16)

    cos_d, sl_d, sh_d = tables(ROW_D)
    low = lane < DSA_DIM

    def dsa_norm_rope(blk, gain):
        sq = blk * blk
        r_lo = lax.rsqrt(jnp.sum(jnp.where(low, sq, 0.0), axis=-1, keepdims=True) * (1.0 / DSA_DIM) + NORM_EPS)
        r_hi = lax.rsqrt(jnp.sum(jnp.where(low, 0.0, sq), axis=-1, keepdims=True) * (1.0 / DSA_DIM) + NORM_EPS)
        return _rope(blk * jnp.where(low, r_lo, r_hi) * gain, cos_d, sl_d, sh_d, DSA_ROT // 2)

    dsa_scale = DSA_DIM ** -0.5 * LOG2_E
    for j in range(DSA_WIDTH // LANES):
        blk = pb[:, j * LANES:(j + 1) * LANES]
        qn = (dsa_norm_rope(blk, gdq_ref[...]) * dsa_scale).astype(bf16)
        qb_ref[0, j, rows] = jnp.where(low, qn, jnp.zeros_like(qn))
        qb_ref[0, DSA_GROUP + j, rows] = jnp.where(low, jnp.zeros_like(qn), qn)
    kb_ref[0, rows] = dsa_norm_rope(pb[:, DSA_WIDTH:DSA_WIDTH + LANES], gdk_ref[...]).astype(bf16)
    vb_ref[0, 0, :, rows] = pb[:, DSA_WIDTH + LANES:].T.astype(bf16)

    cos_i, sl_i, sh_i = tables(ROW_I)
    n_qi = IDX_HEADS * IDX_DIM // LANES
    for j in range(n_qi + 1):
        blk = _rope(pi[:, j * LANES:(j + 1) * LANES], cos_i, sl_i, sh_i, IDX_ROT // 2).astype(bf16)
        if j < n_qi:
            qi_ref[0, rows, j * LANES:(j + 1) * LANES] = blk
        else:
            ki_ref[0, rows] = blk
    wi_t = (pi[:, (n_qi + 1) * LANES:] * (IDX_HEADS ** -0.5 * IDX_DIM ** -0.5)).T
    wi_ref[0, :, rows] = wi_t[0:IDX_HEADS, :]


def _softmax_tile(s, m, l, shift):
    if shift is None:
        m_new = jnp.maximum(m, jnp.max(s, axis=0, keepdims=True))
        alpha = jnp.exp2(m - m_new)
        p = jnp.exp2(s - m_new)
        return m_new, alpha * l + jnp.sum(p, axis=0, keepdims=True), alpha, p.astype(jnp.bfloat16)
    p = jnp.exp2(s - shift)
    return m, l + jnp.sum(p, axis=0, keepdims=True), None, p.astype(jnp.bfloat16)


def _by_shift(shift_ref, body):
    shift = shift_ref[0]

    @pl.when(shift <= FAST_MAX_SHIFT)
    def _():
        body(shift)

    @pl.when(shift > FAST_MAX_SHIFT)
    def _():
        body(None)


def _mla_kernel(shift_ref, q_ref, k_ref, vt_ref, o_ref, *, tq, tk, heads):
    i = pl.program_id(2)
    q_start = i * tq
    n_full = q_start // tk
    n_all = (q_start + tq + tk - 1) // tk
    tv = vt_ref.shape[-1]
    kpos0 = lax.broadcasted_iota(jnp.int32, (tk, tq), 0)
    qpos = q_start + lax.broadcasted_iota(jnp.int32, (tk, tq), 1)

    def attend(shift):
        def step(j, carry, masked):
            off = pl.multiple_of(j * tk, tk)
            scores = [_nt_dot(k_ref[0, hd, pl.ds(off, tk), :], q_ref[0, hd]) for hd in range(heads)]
            out = []
            visible = kpos0 + off <= qpos if masked else None
            shift_t = jnp.where(visible, shift, -NEG_BIG) if masked and shift is not None else shift
            for hd in range(heads):
                m, l, acc = carry[hd]
                s = scores[hd]
                if masked and shift is None:
                    s = jnp.where(visible, s, NEG_BIG)
                m, l, alpha, pb = _softmax_tile(s, m, l, shift_t)
                if alpha is not None:
                    acc = alpha * acc
                for c in range(tk // tv):
                    vt = vt_ref[0, j * (tk // tv) + c, hd * MLA_V:(hd + 1) * MLA_V, :]
                    acc = acc + _bf16_dot(vt, pb[c * tv:(c + 1) * tv])
                out.append((m, l, acc))
            return tuple(out)

        init = tuple((jnp.full((1, tq), NEG_BIG, jnp.float32), jnp.zeros((1, tq), jnp.float32),
                      jnp.zeros((MLA_V, tq), jnp.float32)) for _ in range(heads))
        carry = lax.fori_loop(0, n_full, functools.partial(step, masked=False), init)
        carry = lax.fori_loop(n_full, n_all, functools.partial(step, masked=True), carry)
        outs = [acc / l for (_, l, acc) in carry]
        for c in range(heads // 2):
            o_ref[0, :, c * LANES:(c + 1) * LANES] = jnp.concatenate(outs[2 * c:2 * c + 2], axis=0).T.astype(o_ref.dtype)

    _by_shift(shift_ref, attend)


def _dsa_kernel(shift_ref, qb_ref, kb_ref, vt_ref, qi_ref, ki_ref, wt_ref, o_ref,
                sc_ref, sb_ref, qim_ref, *, tq, tk, k_top):
    i = pl.program_id(1)
    q_start = i * tq
    n_kv = (q_start + tq + tk - 1) // tk
    n_heads = IDX_HEADS
    tv = vt_ref.shape[-1]

    lane_q = lax.broadcasted_iota(jnp.int32, (tq, LANES), 1)
    per_blk = LANES // IDX_DIM
    for hd in range(IDX_HEADS):
        blk = qi_ref[0, :, (hd // per_blk) * LANES:(hd // per_blk + 1) * LANES]
        slot = hd % per_blk
        in_head = (lane_q >= slot * IDX_DIM) & (lane_q < (slot + 1) * IDX_DIM)
        qim_ref[hd * tq:(hd + 1) * tq, :] = jnp.where(in_head, blk, jnp.zeros_like(blk))

    kpos0 = lax.broadcasted_iota(jnp.int32, (tk, tq), 0)
    qpos = q_start + lax.broadcasted_iota(jnp.int32, (tk, tq), 1)
    w_rows = [wt_ref[0, hd:hd + 1, :] for hd in range(IDX_HEADS)]

    head_cands = [_bf16_pattern_value(b << (16 - HEAD_BITS)) for b in range(1, 2 ** HEAD_BITS)]

    def score_tile(j, head_counts):
        off = pl.multiple_of(j * tk, tk)
        s_all = _nt_dot(ki_ref[0, pl.ds(off, tk), :], qim_ref[...])
        sc = w_rows[0] * jnp.maximum(s_all[:, 0:tq], 0.0)
        for hd in range(1, n_heads):
            sc = sc + w_rows[hd] * jnp.maximum(s_all[:, hd * tq:(hd + 1) * tq], 0.0)
        sc = jnp.where(kpos0 + off <= qpos, sc, -jnp.inf)
        sc_ref[j] = sc
        scb = sc.astype(jnp.bfloat16)
        sb_ref[j] = scb
        one, zero = jnp.ones((), jnp.bfloat16), jnp.zeros((), jnp.bfloat16)
        out = []
        for cand, acc in zip(head_cands, head_counts):
            part = None
            for r in range(tk // BF16_ROWS):
                hit = jnp.where(scb[r * BF16_ROWS:(r + 1) * BF16_ROWS] >= cand, one, zero)
                part = hit if part is None else part + hit
            out.append(acc + part.astype(jnp.float32))
        return tuple(out)

    head_counts = lax.fori_loop(0, n_kv, score_tile,
                                tuple(jnp.zeros((BF16_ROWS, tq), jnp.float32) for _ in head_cands))

    def pattern_value(u):
        key = u ^ INT_MIN
        return lax.bitcast_convert_type(key ^ ((key >> 31) & 0x7FFFFFFF), jnp.float32)

    def count(pred):
        def body(j, acc):
            hit = jnp.where(pred(sc_ref[j], j), 1, 0)
            return acc + jnp.sum(hit.reshape(tk // 8, 8, tq), axis=0)
        acc = lax.fori_loop(0, n_kv, body, jnp.zeros((8, tq), jnp.int32))
        return jnp.sum(acc, axis=0, keepdims=True)

    def count_coarse(cand_b):
        one, zero = jnp.ones((), jnp.bfloat16), jnp.zeros((), jnp.bfloat16)

        def body(j, acc):
            part = None
            for r in range(tk // BF16_ROWS):
                rows = sb_ref[j, r * BF16_ROWS:(r + 1) * BF16_ROWS, :]
                hit = jnp.where(rows >= cand_b, one, zero)
                part = hit if part is None else part + hit
            return acc + part.astype(jnp.float32)
        acc = lax.fori_loop(0, n_kv, body, jnp.zeros((BF16_ROWS, tq), jnp.float32))
        return jnp.sum(acc, axis=0, keepdims=True)

    def coarse_bit(it, t16):
        cand = t16 | jnp.left_shift(jnp.int32(1), 15 - it)
        cand32 = jnp.left_shift(cand, 16) | jnp.where(cand < 0x8000, 0xFFFF, 0)
        cnt = count_coarse(pattern_value(cand32).astype(jnp.bfloat16))
        return jnp.where(cnt >= k_top, cand, t16)

    head = sum((jnp.sum(c, axis=0, keepdims=True) >= k_top).astype(jnp.int32) for c in head_counts)
    t16 = lax.fori_loop(HEAD_BITS, 16, coarse_bit, jnp.left_shift(head, 16 - HEAD_BITS))
    few = t16 < 0x0080
    base = jnp.left_shift(t16, 16) + jnp.where(t16 < 0x8000, 0x7FFF, -0x8000)
    base = jnp.where(few, 0x00800000, base)

    def fine_bit(it, carry):
        t, cnt_t = carry
        cand = t | jnp.left_shift(jnp.int32(1), FINE_BITS - 1 - it)
        cand_f = pattern_value(base + cand)
        cnt = count(lambda sc, j: sc >= cand_f)
        take = cnt >= k_top
        return jnp.where(take, cand, t), jnp.where(take, cnt, cnt_t)

    t, cnt_ge = lax.fori_loop(0, FINE_BITS, fine_bit,
                              (jnp.zeros((1, tq), jnp.int32), jnp.full((1, tq), k_top, jnp.int32)))
    thr = jnp.where(few, F32_LOWEST, pattern_value(base + t))
    tied = jnp.logical_not(few) & ((cnt_ge > k_top) | (t == 0))

    def tie_cut():
        cnt_gt = count(lambda sc, j: sc > thr)
        need = k_top - cnt_gt

        def index_bit(it, c):
            cand = c | jnp.left_shift(jnp.int32(1), 12 - it)
            below = count(lambda sc, j: (sc == thr) & (kpos0 + j * tk < cand))
            return jnp.where(below < need, cand, c)

        c = lax.fori_loop(0, 13, index_bit, jnp.zeros((1, tq), jnp.int32))
        return jnp.where(tied, c, jnp.int32(2 ** 30))

    cut = lax.cond(jnp.max(tied.astype(jnp.int32)) > 0, tie_cut,
                   lambda: jnp.full((1, tq), 2 ** 30, jnp.int32))

    assert tq >= MXU_WIDTH

    def attend_all(shift):
        def attend(j, carry):
            off = pl.multiple_of(j * tk, tk)
            kt = kb_ref[0, pl.ds(off, tk), :]
            scores = [_nt_dot(kt, qb_ref[0, hd]) for hd in range(n_heads)]
            sc = sc_ref[j]
            sel = (sc > thr) | ((sc == thr) & (kpos0 + off <= cut))
            shift_t = None if shift is None else jnp.where(sel, shift, -NEG_BIG)
            out = []
            for hd in range(n_heads):
                m, l, acc = carry[hd]
                s = jnp.where(sel, scores[hd], NEG_BIG) if shift is None else scores[hd]
                m, l, alpha, pb = _softmax_tile(s, m, l, shift_t)
                if alpha is not None:
                    acc = alpha * acc
                g = hd // DSA_GROUP
                for c in range(tk // tv):
                    vt = vt_ref[0, j * (tk // tv) + c, g * DSA_DIM:(g + 1) * DSA_DIM, :]
                    acc = acc + _bf16_dot(vt, pb[c * tv:(c + 1) * tv])
                out.append((m, l, acc))
            return tuple(out)

        init = tuple((jnp.full((1, tq), NEG_BIG, jnp.float32), jnp.zeros((1, tq), jnp.float32),
                      jnp.zeros((DSA_DIM, tq), jnp.float32)) for _ in range(n_heads))
        carry = lax.fori_loop(0, n_kv, attend, init)
        outs = [acc / l for (_, l, acc) in carry]
        for j in range(DSA_GROUP):
            o_ref[0, :, j * LANES:(j + 1) * LANES] = jnp.concatenate(
                [outs[j], outs[DSA_GROUP + j]], axis=0).T.astype(o_ref.dtype)

    _by_shift(shift_ref, attend_all)


def _merge_kernel(x_ref, oa_ref, ob_ref, g_ref, bm_ref, wa_ref, wb_ref, wo_ref, out_ref):
    f32 = jnp.float32
    ga = g_ref[:, 0:MLA_WIDTH].astype(f32)
    gb = g_ref[:, MLA_WIDTH:MLA_WIDTH + DSA_WIDTH].astype(f32)
    ma = g_ref[:, MLA_WIDTH + DSA_WIDTH:MLA_WIDTH + DSA_WIDTH + D_MODEL].astype(f32)
    mb = g_ref[:, MLA_WIDTH + DSA_WIDTH + D_MODEL:].astype(f32)
    ya = _bf16_dot((oa_ref[...].astype(f32) * (ga * jax.nn.sigmoid(ga))).astype(jnp.bfloat16), wa_ref[...])
    yb = _bf16_dot((ob_ref[...].astype(f32) * (gb * jax.nn.sigmoid(gb))).astype(jnp.bfloat16), wb_ref[...])
    merged = jax.nn.sigmoid(ma + bm_ref[0:1, :]) * ya + jax.nn.sigmoid(mb + bm_ref[1:2, :]) * yb
    out_ref[...] = x_ref[...] + _bf16_dot(merged.astype(jnp.bfloat16), wo_ref[...])


def _inv_freq(rot_dim):
    return ROPE_THETA ** (-jnp.arange(0, rot_dim, 2, dtype=jnp.float32) / rot_dim)


def _rope_inputs():
    invf = jnp.concatenate([_inv_freq(rot) for (_, _, rot, _) in ROPE_LAYOUTS])
    invf = jnp.pad(invf, (0, FREQ_ROWS - invf.shape[0]))[:, None]
    spread = np.zeros((2 * FREQ_ROWS, 3 * len(ROPE_LAYOUTS) * LANES), np.float32)
    for n, (period, start, rot, row0) in enumerate(ROPE_LAYOUTS):
        half = rot // 2
        for lane in range(LANES):
            p = lane % period - start
            cos_col, lo_col, hi_col = ((3 * n + c) * LANES + lane for c in range(3))
            if 0 <= p < rot:
                spread[row0 + p % half, cos_col] = 1.0
                spread[FREQ_ROWS + row0 + p % half, lo_col if p < half else hi_col] = -1.0 if p < half else 1.0
            else:
                spread[ONE_ROW, cos_col] = 1.0
    return invf, jnp.asarray(np.tile(spread, (3, 1)), jnp.bfloat16)


def _pad_lanes(a, start=0):
    return jnp.pad(a, ((0, 0), (start, LANES - start - a.shape[1])))


def _layout_params(w_in, mla_w_uq, mla_w_ukv, mla_q_gain, mla_k_gain, dsa_q_gain, dsa_k_gain,
                   w_branch_dsa):
    offs = np.concatenate([[0], np.cumsum(IN_SPLITS)])
    (c_q, c_kv, k_pe, gate_a, q_b, k_b, v_b, gate_b, q_i, k_i, w_i, m_a, m_b) = [
        w_in[:, offs[n]:offs[n + 1]] for n in range(len(IN_SPLITS))]

    def pair_heads(w):
        rows = w.shape[0]
        w = w.reshape(rows, DSA_KV_HEADS, DSA_GROUP, DSA_DIM)
        return jnp.transpose(w, (0, 2, 1, 3)).reshape(rows, DSA_WIDTH)

    w_all = jnp.concatenate([
        c_q, c_kv, _pad_lanes(k_pe, MLA_NOPE),
        pair_heads(q_b), k_b, v_b,
        q_i, jnp.tile(k_i, (1, LANES // IDX_DIM)), _pad_lanes(w_i),
        gate_a, pair_heads(gate_b), m_a, m_b], axis=1).astype(jnp.bfloat16)
    uq = mla_w_uq.reshape(MLA_Q_RANK, MLA_HEADS, MLA_QK)
    half = MLA_ROPE // 2

    def swap_rope(a):
        return jnp.concatenate([jnp.zeros_like(a[..., :MLA_NOPE]), a[..., MLA_NOPE + half:], a[..., MLA_NOPE:MLA_NOPE + half]], -1)

    pad_qk = lambda a: jnp.pad(a, ((0, 0), (0, 0), (0, LANES - MLA_QK))).reshape(MLA_Q_RANK, MLA_HEADS * LANES)
    wuq = jnp.concatenate([pad_qk(uq), pad_qk(swap_rope(uq))], axis=1)
    ukv = mla_w_ukv.reshape(MLA_KV_RANK, MLA_HEADS, MLA_NOPE + MLA_V)
    wuk = jnp.pad(ukv[:, :, :MLA_NOPE], ((0, 0), (0, 0), (0, LANES - MLA_NOPE))).reshape(MLA_KV_RANK, MLA_HEADS * LANES)
    wuv = ukv[:, :, MLA_NOPE:].reshape(MLA_KV_RANK, MLA_WIDTH)
    gq = jnp.concatenate([_pad_lanes(mla_q_gain[None, :]), _pad_lanes(swap_rope(mla_q_gain[None, :]))], axis=0)
    gk = _pad_lanes(mla_k_gain[None, :])
    gdq = jnp.tile(dsa_q_gain[None, :], (1, LANES // DSA_DIM))
    gdk = jnp.tile(dsa_k_gain[None, :], (1, LANES // DSA_DIM))
    wb_dsa = jnp.transpose(w_branch_dsa.reshape(DSA_KV_HEADS, DSA_GROUP, DSA_DIM, D_MODEL),
                           (1, 0, 2, 3)).reshape(DSA_WIDTH, D_MODEL)
    return (w_all, wuq.astype(jnp.bfloat16), wuk.astype(jnp.bfloat16), wuv.astype(jnp.bfloat16),
            gq, gk, gdq, gdk, wb_dsa.astype(jnp.bfloat16))


def _const_spec(shape):
    return pl.BlockSpec(shape, lambda *_: (0,) * len(shape))


def _layer(x, positions, invf, spread, norm_gain, w_in, b_merge, mla_q_norm, mla_w_uq, mla_kv_norm, mla_w_ukv,
           mla_q_gain, mla_k_gain, dsa_q_gain, dsa_k_gain, w_branch_mla, w_branch_dsa, w_out):
    bsz, s_len, d_model = x.shape
    assert d_model == D_MODEL and TM_PROJ % PROJ_ROWS == 0 and (bsz * s_len) % TM_MERGE == 0
    assert all(s_len % t == 0 for t in (TM_PROJ, TQ_MLA, TK_MLA, TQ_DSA, TK_DSA))
    k_top = min(TOPK_MAX, s_len // 4)
    (w_all, wuq, wuk, wuv, gq, gk, gdq, gdk, wb_dsa) = _layout_params(
        w_in, mla_w_uq, mla_w_ukv, mla_q_gain, mla_k_gain, dsa_q_gain, dsa_k_gain, w_branch_dsa)
    bf16, f32 = jnp.bfloat16, jnp.float32
    gmax = lambda g: jnp.max(jnp.abs(g))
    shift_a = (BOUND_MARGIN * MLA_QK ** 0.5 * LOG2_E * gmax(mla_q_gain) * gmax(mla_k_gain)).reshape(1).astype(f32)
    shift_b = (BOUND_MARGIN * DSA_DIM ** 0.5 * LOG2_E * gmax(dsa_q_gain) * gmax(dsa_k_gain)).reshape(1).astype(f32)
    smem = pl.BlockSpec(memory_space=pltpu.SMEM)

    tm = TM_PROJ
    tok = lambda w: pl.BlockSpec((1, tm, w), lambda b, i: (b, i, 0))
    head = pl.BlockSpec((1, MLA_HEADS, tm, LANES), lambda b, i: (b, 0, i, 0))
    outs = pl.pallas_call(
        _proj_kernel,
        grid=(bsz, s_len // tm),
        in_specs=[tok(D_MODEL), pl.BlockSpec((1, 1, tm), lambda b, i: (b, 0, i)),
                  _const_spec(invf.shape), _const_spec(spread.shape), _const_spec((1, D_MODEL)),
                  _const_spec((1, MLA_Q_RANK)), _const_spec((1, MLA_KV_RANK)),
                  _const_spec((2, LANES)), _const_spec((1, LANES)), _const_spec((1, LANES)), _const_spec((1, LANES)),
                  _const_spec((D_MODEL, W_ALL)), _const_spec((MLA_Q_RANK, 2 * MLA_HEADS * LANES)),
                  _const_spec((MLA_KV_RANK, MLA_HEADS * LANES)), _const_spec((MLA_KV_RANK, MLA_WIDTH))],
        out_specs=[head, head, pl.BlockSpec((1, 1, MLA_WIDTH, tm), lambda b, i: (b, i, 0, 0)),
                   head, tok(LANES),
                   pl.BlockSpec((1, 1, LANES, tm), lambda b, i: (b, i, 0, 0)),
                   tok(IDX_HEADS * IDX_DIM), tok(LANES),
                   pl.BlockSpec((1, IDX_HEADS, tm), lambda b, i: (b, 0, i)), tok(W_G)],
        out_shape=[jax.ShapeDtypeStruct((bsz, MLA_HEADS, s_len, LANES), bf16),
                   jax.ShapeDtypeStruct((bsz, MLA_HEADS, s_len, LANES), bf16),
                   jax.ShapeDtypeStruct((bsz, s_len // tm, MLA_WIDTH, tm), bf16),
                   jax.ShapeDtypeStruct((bsz, DSA_HEADS, s_len, LANES), bf16),
                   jax.ShapeDtypeStruct((bsz, s_len, LANES), bf16),
                   jax.ShapeDtypeStruct((bsz, s_len // tm, LANES, tm), bf16),
                   jax.ShapeDtypeStruct((bsz, s_len, IDX_HEADS * IDX_DIM), bf16),
                   jax.ShapeDtypeStruct((bsz, s_len, LANES), bf16),
                   jax.ShapeDtypeStruct((bsz, IDX_HEADS, s_len), f32),
                   jax.ShapeDtypeStruct((bsz, s_len, W_G), bf16)],
        compiler_params=pltpu.CompilerParams(dimension_semantics=("arbitrary", "arbitrary"),
                                             vmem_limit_bytes=VMEM_LIMIT),
        name="proj",
    )(x, positions[:, None, :], invf, spread, norm_gain[None, :], mla_q_norm[None, :], mla_kv_norm[None, :],
      gq, gk, gdq, gdk, w_all, wuq, wuk, wuv)
    qa, ka, vat, qb, kb, vbt, qi, ki, wit, gates = outs

    tq_a, tk_a, hps = TQ_MLA, TK_MLA, MLA_HEADS_PER_STEP
    o_a = pl.pallas_call(
        functools.partial(_mla_kernel, tq=tq_a, tk=tk_a, heads=hps),
        grid=(bsz, MLA_HEADS // hps, s_len // tq_a),
        in_specs=[smem, pl.BlockSpec((1, hps, tq_a, LANES), lambda b, hp, i: (b, hp, i, 0)),
                  pl.BlockSpec((1, hps, s_len, LANES), lambda b, hp, i: (b, hp, 0, 0)),
                  pl.BlockSpec((1, s_len // tm, hps * MLA_V, tm), lambda b, hp, i: (b, 0, hp, 0))],
        out_specs=pl.BlockSpec((1, tq_a, hps * MLA_V), lambda b, hp, i: (b, i, hp)),
        out_shape=jax.ShapeDtypeStruct((bsz, s_len, MLA_WIDTH), bf16),
        compiler_params=pltpu.CompilerParams(dimension_semantics=("arbitrary",) * 3,
                                             vmem_limit_bytes=VMEM_LIMIT),
        name="mla",
    )(shift_a, qa, ka, vat)

    tq_b, tk_b = TQ_DSA, TK_DSA
    row_b = lambda w: pl.BlockSpec((1, tq_b, w), lambda b, i: (b, i, 0))
    full_b = lambda w: pl.BlockSpec((1, s_len, w), lambda b, i: (b, 0, 0))
    o_b = pl.pallas_call(
        functools.partial(_dsa_kernel, tq=tq_b, tk=tk_b, k_top=k_top),
        grid=(bsz, s_len // tq_b),
        in_specs=[smem, pl.BlockSpec((1, DSA_HEADS, tq_b, LANES), lambda b, i: (b, 0, i, 0)), full_b(LANES),
                  pl.BlockSpec((1, s_len // tm, LANES, tm), lambda b, i: (b, 0, 0, 0)),
                  row_b(IDX_HEADS * IDX_DIM), full_b(LANES),
                  pl.BlockSpec((1, IDX_HEADS, tq_b), lambda b, i: (b, 0, i))],
        out_specs=row_b(DSA_WIDTH),
        out_shape=jax.ShapeDtypeStruct((bsz, s_len, DSA_WIDTH), bf16),
        scratch_shapes=[pltpu.VMEM((s_len // tk_b, tk_b, tq_b), f32),
                        pltpu.VMEM((s_len // tk_b, tk_b, tq_b), bf16),
                        pltpu.VMEM((IDX_HEADS * tq_b, LANES), bf16)],
        compiler_params=pltpu.CompilerParams(dimension_semantics=("arbitrary", "arbitrary"),
                                             vmem_limit_bytes=VMEM_LIMIT),
        name="dsa",
    )(shift_b, qb, kb, vbt, qi, ki, wit)

    tm_m = TM_MERGE
    n_tok = bsz * s_len
    flat = lambda w: pl.BlockSpec((tm_m, w), lambda i: (i, 0))
    out = pl.pallas_call(
        _merge_kernel,
        grid=(n_tok // tm_m,),
        in_specs=[flat(D_MODEL), flat(MLA_WIDTH), flat(DSA_WIDTH), flat(W_G), _const_spec((2, D_MODEL)),
                  _const_spec((MLA_WIDTH, D_MODEL)), _const_spec((DSA_WIDTH, D_MODEL)),
                  _const_spec((D_MODEL, D_MODEL))],
        out_specs=flat(D_MODEL),
        out_shape=jax.ShapeDtypeStruct((n_tok, D_MODEL), f32),
        compiler_params=pltpu.CompilerParams(dimension_semantics=("arbitrary",),
                                             vmem_limit_bytes=VMEM_LIMIT),
        name="merge",
    )(x.reshape(n_tok, D_MODEL), o_a.reshape(n_tok, MLA_WIDTH), o_b.reshape(n_tok, DSA_WIDTH),
      gates.reshape(n_tok, W_G), b_merge, w_branch_mla.astype(bf16), wb_dsa, w_out.astype(bf16))
    return out.reshape(bsz, s_len, D_MODEL)


def kernel(x, positions, norm_gain, w_in, b_merge, mla_q_norm, mla_w_uq, mla_kv_norm, mla_w_ukv,
           mla_q_gain, mla_k_gain, dsa_q_gain, dsa_k_gain, w_branch_mla, w_branch_dsa, w_out):
    invf, spread = _rope_inputs()
    for l in range(norm_gain.shape[0]):
        x = _layer(x, positions, invf, spread, norm_gain[l], w_in[l], b_merge[l], mla_q_norm[l], mla_w_uq[l],
                   mla_kv_norm[l], mla_w_ukv[l], mla_q_gain[l], mla_k_gain[l], dsa_q_gain[l], dsa_k_gain[l],
                   w_branch_mla[l], w_branch_dsa[l], w_out[l])
    return x
```

```python
import functools

import numpy as np
import jax
import jax.numpy as jnp
from jax import lax
from jax.experimental import pallas as pl
from jax.experimental.pallas import tpu as pltpu

D_MODEL = 1024
MLA_HEADS = 8
MLA_Q_RANK = 256
MLA_KV_RANK = 128
MLA_NOPE = 64
MLA_ROPE = 32
MLA_QK = MLA_NOPE + MLA_ROPE
MLA_V = 64
MLA_WIDTH = MLA_HEADS * MLA_V
DSA_HEADS = 8
DSA_KV_HEADS = 2
DSA_GROUP = DSA_HEADS // DSA_KV_HEADS
DSA_DIM = 64
DSA_WIDTH = DSA_HEADS * DSA_DIM
DSA_ROT = DSA_DIM // 4
IDX_HEADS = 8
IDX_DIM = 32
IDX_ROT = IDX_DIM // 4
TOPK_MAX = 256
ROPE_THETA = 500000.0
NORM_EPS = 1e-6

IN_SPLITS = (
    MLA_Q_RANK, MLA_KV_RANK, MLA_ROPE, MLA_WIDTH,
    DSA_HEADS * DSA_DIM, DSA_KV_HEADS * DSA_DIM, DSA_KV_HEADS * DSA_DIM, DSA_WIDTH,
    IDX_HEADS * IDX_DIM, IDX_DIM, IDX_HEADS,
    D_MODEL, D_MODEL,
)

LANES = 128
MXU_WIDTH = 256
VMEM_LIMIT = 56 * 1024 * 1024
INT_MIN = -(2 ** 31)
F32_LOWEST = float(np.finfo(np.float32).min)
BF16_ROWS = 16
FINE_BITS = 17
NEG_BIG = -1e30
LOG2_E = 1.4426950408889634
FAST_MAX_SHIFT = 60.0
BOUND_MARGIN = 1.02

OFF_A = 0
W_A = MLA_Q_RANK + MLA_KV_RANK + LANES
OFF_B = OFF_A + W_A
W_B = DSA_WIDTH + 2 * DSA_KV_HEADS * DSA_DIM
OFF_I = OFF_B + W_B
W_I = IDX_HEADS * IDX_DIM + 2 * LANES
OFF_G = OFF_I + W_I
W_G = MLA_WIDTH + DSA_WIDTH + 2 * D_MODEL
W_ALL = OFF_G + W_G
PROJ_ROWS = 256
GATE_CHUNK = 512
N_GATE_CHUNKS = W_G // GATE_CHUNK


TM_PROJ = 512
TQ_MLA, TK_MLA = 512, 512
MLA_HEADS_PER_STEP = 8
TQ_DSA, TK_DSA = 512, 512
TM_MERGE = 1024

ROPE_LAYOUTS = ((LANES, MLA_NOPE, MLA_ROPE, 0), (DSA_DIM, 0, DSA_ROT, MLA_ROPE // 2),
                (IDX_DIM, 0, IDX_ROT, MLA_ROPE // 2 + DSA_ROT // 2))
ROW_A, ROW_D, ROW_I = 0, 1, 2
FREQ_ROWS = 32
ONE_ROW = FREQ_ROWS - 1


def _rms(v, width):
    return lax.rsqrt(jnp.sum(v * v, axis=-1, keepdims=True) * (1.0 / width) + NORM_EPS)


def _rope(v, cos, sin_lo, sin_hi, half):
    return (v * cos + pltpu.roll(v, LANES - half, 1) * sin_lo + pltpu.roll(v, half, 1) * sin_hi)


def _bf16_dot(a, b):
    return jnp.dot(a, b, preferred_element_type=jnp.float32)


def _nt_dot(a, b):
    return lax.dot_general(a, b, (((1,), (1,)), ((), ())), preferred_element_type=jnp.float32)


def _proj_kernel(x_ref, pos_ref, invf_ref, spread_ref, gx_ref, gcq_ref, gckv_ref, gq_ref, gk_ref, gdq_ref, gdk_ref,
                 w_ref, wuq_ref, wuk_ref, wuv_ref,
                 qa_ref, ka_ref, va_ref, qb_ref, kb_ref, vb_ref, qi_ref, ki_ref, wi_ref, g_ref):
    bf16 = jnp.bfloat16
    subs = [slice(r0, r0 + PROJ_ROWS) for r0 in range(0, x_ref.shape[1], PROJ_ROWS)]
    hs = []
    for rows in subs:
        x = x_ref[0, rows]
        hs.append((x * _rms(x, D_MODEL) * gx_ref[...]).astype(bf16))
    pas = [_bf16_dot(h, w_ref[:, OFF_A:OFF_A + W_A]) for h in hs]
    tabs = [_rope_tables(pos_ref[0, :, rows], invf_ref, spread_ref) for rows in subs]
    lat = []
    for pa in pas:
        c_q = pa[:, :MLA_Q_RANK]
        c_kv = pa[:, MLA_Q_RANK:MLA_Q_RANK + MLA_KV_RANK]
        lat.append(((c_q * _rms(c_q, MLA_Q_RANK) * gcq_ref[...]).astype(bf16),
                    (c_kv * _rms(c_kv, MLA_KV_RANK) * gckv_ref[...]).astype(bf16)))
    pbs = [_bf16_dot(h, w_ref[:, OFF_B:OFF_B + W_B]) for h in hs]
    pis = [_bf16_dot(h, w_ref[:, OFF_I:OFF_I + W_I]) for h in hs]
    ups = [(_bf16_dot(cqn, wuq_ref[...]),
            _bf16_dot(ckvn, wuk_ref[...]),
            _bf16_dot(ckvn, wuv_ref[...])) for cqn, ckvn in lat]
    for n, rows in enumerate(subs):
        kpe = pas[n][:, MLA_Q_RANK + MLA_KV_RANK:]
        _proj_epilogue(rows, (tabs[n], kpe) + ups[n] + (pbs[n], pis[n]), gq_ref, gk_ref, gdq_ref, gdk_ref,
                       qa_ref, ka_ref, va_ref, qb_ref, kb_ref, vb_ref, qi_ref, ki_ref, wi_ref)
    for rows, h in zip(subs, hs):
        for c in range(N_GATE_CHUNKS):
            cols = slice(c * GATE_CHUNK, (c + 1) * GATE_CHUNK)
            g_ref[0, rows, cols] = _bf16_dot(h, w_ref[:, OFF_G + c * GATE_CHUNK:OFF_G + (c + 1) * GATE_CHUNK]).astype(g_ref.dtype)


def _rope_tables(pos_row, invf_ref, spread_ref):
    bf16 = jnp.bfloat16
    ang = invf_ref[...] * pos_row.astype(jnp.float32)
    trig = jnp.concatenate([jnp.cos(ang), jnp.sin(ang)], axis=0)
    t1 = trig.astype(bf16)
    r1 = trig - t1.astype(jnp.float32)
    t2 = r1.astype(bf16)
    t3 = (r1 - t2.astype(jnp.float32)).astype(bf16)
    return lax.dot_general(jnp.concatenate([t1, t2, t3], axis=0), spread_ref[...], (((0,), (0,)), ((), ())),
                           preferred_element_type=jnp.float32)


def _proj_epilogue(rows, vals, gq_ref, gk_ref, gdq_ref, gdk_ref,
                   qa_ref, ka_ref, va_ref, qb_ref, kb_ref, vb_ref, qi_ref, ki_ref, wi_ref):
    tabs, kpe, q_up, k_up, v_up, pb, pi = vals
    lane = lax.broadcasted_iota(jnp.int32, (1, LANES), 1)
    bf16 = jnp.bfloat16

    def tables(n):
        return tuple(tabs[:, (3 * n + c) * LANES:(3 * n + c + 1) * LANES] for c in range(3))

    for c in range(MLA_WIDTH // LANES):
        va_ref[0, 0, c * LANES:(c + 1) * LANES, rows] = v_up[:, c * LANES:(c + 1) * LANES].T.astype(bf16)
    cos_a, sl_a, sh_a = tables(ROW_A)
    sin_a = sl_a + sh_a
    q_scale = MLA_QK ** -0.5 * LOG2_E
    kpe_g = kpe * gk_ref[...]
    k_rot = (pltpu.roll(kpe_g, LANES - MLA_ROPE // 2, 1) * sl_a + pltpu.roll(kpe_g, MLA_ROPE // 2, 1) * sh_a)
    for hd in range(MLA_HEADS):
        qh = q_up[:, hd * LANES:(hd + 1) * LANES]
        q_sw = q_up[:, (MLA_HEADS + hd) * LANES:(MLA_HEADS + hd + 1) * LANES]
        q_rot = qh * gq_ref[0:1, :] * cos_a + q_sw * gq_ref[1:2, :] * sin_a
        qa_ref[0, hd, rows] = (q_rot * (_rms(qh, MLA_QK) * q_scale)).astype(bf16)
        kh = k_up[:, hd * LANES:(hd + 1) * LANES] + kpe
        ka_ref[0, hd, rows] = ((kh * gk_ref[...] * cos_a + k_rot) * _rms(kh, MLA_QK)).astype(bf16)

    cos_d, sl_d, sh_d = tables(ROW_D)
    low = lane < DSA_DIM

    def dsa_norm_rope(blk, gain):
        sq = blk * blk
        r_lo = lax.rsqrt(jnp.sum(jnp.where(low, sq, 0.0), axis=-1, keepdims=True) * (1.0 / DSA_DIM) + NORM_EPS)
        r_hi = lax.rsqrt(jnp.sum(jnp.where(low, 0.0, sq), axis=-1, keepdims=True) * (1.0 / DSA_DIM) + NORM_EPS)
        return _rope(blk * jnp.where(low, r_lo, r_hi) * gain, cos_d, sl_d, sh_d, DSA_ROT // 2)

    dsa_scale = DSA_DIM ** -0.5 * LOG2_E
    for j in range(DSA_WIDTH // LANES):
        blk = pb[:, j * LANES:(j + 1) * LANES]
        qn = (dsa_norm_rope(blk, gdq_ref[...]) * dsa_scale).astype(bf16)
        qb_ref[0, j, rows] = jnp.where(low, qn, jnp.zeros_like(qn))
        qb_ref[0, DSA_GROUP + j, rows] = jnp.where(low, jnp.zeros_like(qn), qn)
    kb_ref[0, rows] = dsa_norm_rope(pb[:, DSA_WIDTH:DSA_WIDTH + LANES], gdk_ref[...]).astype(bf16)
    vb_ref[0, 0, :, rows] = pb[:, DSA_WIDTH + LANES:].T.astype(bf16)

    cos_i, sl_i, sh_i = tables(ROW_I)
    n_qi = IDX_HEADS * IDX_DIM // LANES
    for j in range(n_qi + 1):
        blk = _rope(pi[:, j * LANES:(j + 1) * LANES], cos_i, sl_i, sh_i, IDX_ROT // 2).astype(bf16)
        if j < n_qi:
            qi_ref[0, rows, j * LANES:(j + 1) * LANES] = blk
        else:
            ki_ref[0, rows] = blk
    wi_t = (pi[:, (n_qi + 1) * LANES:] * (IDX_HEADS ** -0.5 * IDX_DIM ** -0.5)).T
    wi_ref[0, :, rows] = wi_t[0:IDX_HEADS, :]


def _softmax_tile(s, m, l, shift):
    if shift is None:
        m_new = jnp.maximum(m, jnp.max(s, axis=0, keepdims=True))
        alpha = jnp.exp2(m - m_new)
        p = jnp.exp2(s - m_new)
        return m_new, alpha * l + jnp.sum(p, axis=0, keepdims=True), alpha, p.astype(jnp.bfloat16)
    p = jnp.exp2(s - shift)
    return m, l + jnp.sum(p, axis=0, keepdims=True), None, p.astype(jnp.bfloat16)


def _by_shift(shift_ref, body):
    shift = shift_ref[0]

    @pl.when(shift <= FAST_MAX_SHIFT)
    def _():
        body(shift)

    @pl.when(shift > FAST_MAX_SHIFT)
    def _():
        body(None)


def _mla_kernel(shift_ref, q_ref, k_ref, vt_ref, o_ref, *, tq, tk, heads):
    i = pl.program_id(2)
    q_start = i * tq
    n_full = q_start // tk
    n_all = (q_start + tq + tk - 1) // tk
    tv = vt_ref.shape[-1]
    kpos0 = lax.broadcasted_iota(jnp.int32, (tk, tq), 0)
    qpos = q_start + lax.broadcasted_iota(jnp.int32, (tk, tq), 1)

    def attend(shift):
        def step(j, carry, masked):
            off = pl.multiple_of(j * tk, tk)
            scores = [_nt_dot(k_ref[0, hd, pl.ds(off, tk), :], q_ref[0, hd]) for hd in range(heads)]
            out = []
            visible = kpos0 + off <= qpos if masked else None
            shift_t = jnp.where(visible, shift, -NEG_BIG) if masked and shift is not None else shift
            for hd in range(heads):
                m, l, acc = carry[hd]
                s = scores[hd]
                if masked and shift is None:
                    s = jnp.where(visible, s, NEG_BIG)
                m, l, alpha, pb = _softmax_tile(s, m, l, shift_t)
                if alpha is not None:
                    acc = alpha * acc
                for c in range(tk // tv):
                    vt = vt_ref[0, j * (tk // tv) + c, hd * MLA_V:(hd + 1) * MLA_V, :]
                    acc = acc + _bf16_dot(vt, pb[c * tv:(c + 1) * tv])
                out.append((m, l, acc))
            return tuple(out)

        init = tuple((jnp.full((1, tq), NEG_BIG, jnp.float32), jnp.zeros((1, tq), jnp.float32),
                      jnp.zeros((MLA_V, tq), jnp.float32)) for _ in range(heads))
        carry = lax.fori_loop(0, n_full, functools.partial(step, masked=False), init)
        carry = lax.fori_loop(n_full, n_all, functools.partial(step, masked=True), carry)
        outs = [acc / l for (_, l, acc) in carry]
        for c in range(heads // 2):
            o_ref[0, :, c * LANES:(c + 1) * LANES] = jnp.concatenate(outs[2 * c:2 * c + 2], axis=0).T.astype(o_ref.dtype)

    _by_shift(shift_ref, attend)


def _dsa_kernel(shift_ref, qb_ref, kb_ref, vt_ref, qi_ref, ki_ref, wt_ref, o_ref,
                sc_ref, sb_ref, qim_ref, *, tq, tk, k_top):
    i = pl.program_id(1)
    q_start = i * tq
    n_kv = (q_start + tq + tk - 1) // tk
    n_heads = IDX_HEADS
    tv = vt_ref.shape[-1]

    lane_q = lax.broadcasted_iota(jnp.int32, (tq, LANES), 1)
    per_blk = LANES // IDX_DIM
    for hd in range(IDX_HEADS):
        blk = qi_ref[0, :, (hd // per_blk) * LANES:(hd // per_blk + 1) * LANES]
        slot = hd % per_blk
        in_head = (lane_q >= slot * IDX_DIM) & (lane_q < (slot + 1) * IDX_DIM)
        qim_ref[hd * tq:(hd + 1) * tq, :] = jnp.where(in_head, blk, jnp.zeros_like(blk))

    kpos0 = lax.broadcasted_iota(jnp.int32, (tk, tq), 0)
    qpos = q_start + lax.broadcasted_iota(jnp.int32, (tk, tq), 1)
    w_rows = [wt_ref[0, hd:hd + 1, :] for hd in range(IDX_HEADS)]

    def score_tile(j, _):
        off = pl.multiple_of(j * tk, tk)
        s_all = _nt_dot(ki_ref[0, pl.ds(off, tk), :], qim_ref[...])
        sc = w_rows[0] * jnp.maximum(s_all[:, 0:tq], 0.0)
        for hd in range(1, n_heads):
            sc = sc + w_rows[hd] * jnp.maximum(s_all[:, hd * tq:(hd + 1) * tq], 0.0)
        sc = jnp.where(kpos0 + off <= qpos, sc, -jnp.inf)
        sc_ref[j] = sc
        sb_ref[j] = sc.astype(jnp.bfloat16)
        return 0

    lax.fori_loop(0, n_kv, score_tile, 0)

    def pattern_value(u):
        key = u ^ INT_MIN
        return lax.bitcast_convert_type(key ^ ((key >> 31) & 0x7FFFFFFF), jnp.float32)

    assert tq == tk
    jd = n_kv - 1
    hk, hq = tk // 2, tq // 2
    all_q, late_q = slice(0, tq), slice(hq, tq)

    def lanes(row, q):
        if q is all_q:
            return row
        return jnp.broadcast_to(row, (8, tq))[:, q][0:1]

    def count(pred):
        def body(j, acc):
            hit = jnp.where(pred(sc_ref[j], kpos0 + j * tk, all_q), 1, 0)
            return acc + jnp.sum(hit.reshape(tk // 8, 8, tq), axis=0)
        acc = lax.fori_loop(0, jd, body, jnp.zeros((8, tq), jnp.int32))
        kpos_top = jd * tk + lax.broadcasted_iota(jnp.int32, (hk, tq), 0)
        top = jnp.where(pred(sc_ref[jd, 0:hk, :], kpos_top, all_q), 1, 0)
        acc = acc + jnp.sum(top.reshape(hk // 8, 8, tq), axis=0)
        kpos_low = jd * tk + hk + lax.broadcasted_iota(jnp.int32, (hk, hq), 0)
        low = jnp.where(pred(sc_ref[jd, hk:tk, hq:tq], kpos_low, late_q), 1, 0)
        low = jnp.sum(low.reshape(hk // 8, 8, hq), axis=0)
        acc = acc + jnp.concatenate([jnp.zeros_like(low), low], axis=1)
        return jnp.sum(acc, axis=0, keepdims=True)

    def count_coarse(cand_b):
        one, zero = jnp.ones((), jnp.bfloat16), jnp.zeros((), jnp.bfloat16)

        def slabs(j, first, last, q):
            cand_q = lanes(cand_b.astype(jnp.float32), q).astype(jnp.bfloat16)
            part = None
            for r in range(first, last):
                rows = sb_ref[j, r * BF16_ROWS:(r + 1) * BF16_ROWS, q]
                hit = jnp.where(rows >= cand_q, one, zero)
                part = hit if part is None else part + hit
            return part.astype(jnp.float32)

        n_slab = tk // BF16_ROWS
        acc = lax.fori_loop(0, jd, lambda j, acc: acc + slabs(j, 0, n_slab, all_q),
                            jnp.zeros((BF16_ROWS, tq), jnp.float32))
        low = slabs(jd, n_slab // 2, n_slab, late_q)
        acc = acc + slabs(jd, 0, n_slab // 2, all_q) + jnp.concatenate([jnp.zeros_like(low), low], axis=1)
        return jnp.sum(acc, axis=0, keepdims=True)

    def coarse_bit(it, t16):
        cand = t16 | jnp.left_shift(jnp.int32(1), 15 - it)
        cand32 = jnp.left_shift(cand, 16) | jnp.where(cand < 0x8000, 0xFFFF, 0)
        cnt = count_coarse(pattern_value(cand32).astype(jnp.bfloat16))
        return jnp.where(cnt >= k_top, cand, t16)

    t16 = lax.fori_loop(0, 16, coarse_bit, jnp.zeros((1, tq), jnp.int32))
    few = t16 < 0x0080
    base = jnp.left_shift(t16, 16) + jnp.where(t16 < 0x8000, 0x7FFF, -0x8000)
    base = jnp.where(few, 0x00800000, base)

    def fine_bit(it, carry):
        t, cnt_t = carry
        cand = t | jnp.left_shift(jnp.int32(1), FINE_BITS - 1 - it)
        cand_f = pattern_value(base + cand)
        cnt = count(lambda sc, kpos, q: sc >= lanes(cand_f, q))
        take = cnt >= k_top
        return jnp.where(take, cand, t), jnp.where(take, cnt, cnt_t)

    t, cnt_ge = lax.fori_loop(0, FINE_BITS, fine_bit,
                              (jnp.zeros((1, tq), jnp.int32), jnp.full((1, tq), k_top, jnp.int32)))
    thr = jnp.where(few, F32_LOWEST, pattern_value(base + t))
    tied = jnp.logical_not(few) & ((cnt_ge > k_top) | (t == 0))

    def tie_cut():
        cnt_gt = count(lambda sc, kpos, q: sc > lanes(thr, q))
        need = k_top - cnt_gt

        def index_bit(it, c):
            cand = c | jnp.left_shift(jnp.int32(1), 12 - it)
            below = count(lambda sc, kpos, q: (sc == lanes(thr, q)) & (kpos < lanes(cand, q)))
            return jnp.where(below < need, cand, c)

        c = lax.fori_loop(0, 13, index_bit, jnp.zeros((1, tq), jnp.int32))
        return jnp.where(tied, c, jnp.int32(2 ** 30))

    cut = lax.cond(jnp.max(tied.astype(jnp.int32)) > 0, tie_cut,
                   lambda: jnp.full((1, tq), 2 ** 30, jnp.int32))

    assert tq >= MXU_WIDTH

    def attend_all(shift):
        def attend(j, carry):
            off = pl.multiple_of(j * tk, tk)
            kt = kb_ref[0, pl.ds(off, tk), :]
            scores = [_nt_dot(kt, qb_ref[0, hd]) for hd in range(n_heads)]
            sc = sc_ref[j]
            sel = (sc > thr) | ((sc == thr) & (kpos0 + off <= cut))
            shift_t = None if shift is None else jnp.where(sel, shift, -NEG_BIG)
            out = []
            for hd in range(n_heads):
                m, l, acc = carry[hd]
                s = jnp.where(sel, scores[hd], NEG_BIG) if shift is None else scores[hd]
                m, l, alpha, pb = _softmax_tile(s, m, l, shift_t)
                if alpha is not None:
                    acc = alpha * acc
                g = hd // DSA_GROUP
                for c in range(tk // tv):
                    vt = vt_ref[0, j * (tk // tv) + c, g * DSA_DIM:(g + 1) * DSA_DIM, :]
                    acc = acc + _bf16_dot(vt, pb[c * tv:(c + 1) * tv])
                out.append((m, l, acc))
            return tuple(out)

        init = tuple((jnp.full((1, tq), NEG_BIG, jnp.float32), jnp.zeros((1, tq), jnp.float32),
                      jnp.zeros((DSA_DIM, tq), jnp.float32)) for _ in range(n_heads))
        carry = lax.fori_loop(0, n_kv, attend, init)
        outs = [acc / l for (_, l, acc) in carry]
        for j in range(DSA_GROUP):
            o_ref[0, :, j * LANES:(j + 1) * LANES] = jnp.concatenate(
                [outs[j], outs[DSA_GROUP + j]], axis=0).T.astype(o_ref.dtype)

    _by_shift(shift_ref, attend_all)


def _merge_kernel(x_ref, oa_ref, ob_ref, g_ref, bm_ref, wa_ref, wb_ref, wo_ref, out_ref):
    f32 = jnp.float32
    ga = g_ref[:, 0:MLA_WIDTH].astype(f32)
    gb = g_ref[:, MLA_WIDTH:MLA_WIDTH + DSA_WIDTH].astype(f32)
    ma = g_ref[:, MLA_WIDTH + DSA_WIDTH:MLA_WIDTH + DSA_WIDTH + D_MODEL].astype(f32)
    mb = g_ref[:, MLA_WIDTH + DSA_WIDTH + D_MODEL:].astype(f32)
    ya = _bf16_dot((oa_ref[...].astype(f32) * (ga * jax.nn.sigmoid(ga))).astype(jnp.bfloat16), wa_ref[...])
    yb = _bf16_dot((ob_ref[...].astype(f32) * (gb * jax.nn.sigmoid(gb))).astype(jnp.bfloat16), wb_ref[...])
    merged = jax.nn.sigmoid(ma + bm_ref[0:1, :]) * ya + jax.nn.sigmoid(mb + bm_ref[1:2, :]) * yb
    out_ref[...] = x_ref[...] + _bf16_dot(merged.astype(jnp.bfloat16), wo_ref[...])


def _inv_freq(rot_dim):
    return ROPE_THETA ** (-jnp.arange(0, rot_dim, 2, dtype=jnp.float32) / rot_dim)


def _rope_inputs():
    invf = jnp.concatenate([_inv_freq(rot) for (_, _, rot, _) in ROPE_LAYOUTS])
    invf = jnp.pad(invf, (0, FREQ_ROWS - invf.shape[0]))[:, None]
    spread = np.zeros((2 * FREQ_ROWS, 3 * len(ROPE_LAYOUTS) * LANES), np.float32)
    for n, (period, start, rot, row0) in enumerate(ROPE_LAYOUTS):
        half = rot // 2
        for lane in range(LANES):
            p = lane % period - start
            cos_col, lo_col, hi_col = ((3 * n + c) * LANES + lane for c in range(3))
            if 0 <= p < rot:
                spread[row0 + p % half, cos_col] = 1.0
                spread[FREQ_ROWS + row0 + p % half, lo_col if p < half else hi_col] = -1.0 if p < half else 1.0
            else:
                spread[ONE_ROW, cos_col] = 1.0
    return invf, jnp.asarray(np.tile(spread, (3, 1)), jnp.bfloat16)


def _pad_lanes(a, start=0):
    return jnp.pad(a, ((0, 0), (start, LANES - start - a.shape[1])))


def _layout_params(w_in, mla_w_uq, mla_w_ukv, mla_q_gain, mla_k_gain, dsa_q_gain, dsa_k_gain,
                   w_branch_dsa):
    offs = np.concatenate([[0], np.cumsum(IN_SPLITS)])
    (c_q, c_kv, k_pe, gate_a, q_b, k_b, v_b, gate_b, q_i, k_i, w_i, m_a, m_b) = [
        w_in[:, offs[n]:offs[n + 1]] for n in range(len(IN_SPLITS))]

    def pair_heads(w):
        rows = w.shape[0]
        w = w.reshape(rows, DSA_KV_HEADS, DSA_GROUP, DSA_DIM)
        return jnp.transpose(w, (0, 2, 1, 3)).reshape(rows, DSA_WIDTH)

    w_all = jnp.concatenate([
        c_q, c_kv, _pad_lanes(k_pe, MLA_NOPE),
        pair_heads(q_b), k_b, v_b,
        q_i, jnp.tile(k_i, (1, LANES // IDX_DIM)), _pad_lanes(w_i),
        gate_a, pair_heads(gate_b), m_a, m_b], axis=1).astype(jnp.bfloat16)
    uq = mla_w_uq.reshape(MLA_Q_RANK, MLA_HEADS, MLA_QK)
    half = MLA_ROPE // 2

    def swap_rope(a):
        return jnp.concatenate([jnp.zeros_like(a[..., :MLA_NOPE]), a[..., MLA_NOPE + half:], a[..., MLA_NOPE:MLA_NOPE + half]], -1)

    pad_qk = lambda a: jnp.pad(a, ((0, 0), (0, 0), (0, LANES - MLA_QK))).reshape(MLA_Q_RANK, MLA_HEADS * LANES)
    wuq = jnp.concatenate([pad_qk(uq), pad_qk(swap_rope(uq))], axis=1)
    ukv = mla_w_ukv.reshape(MLA_KV_RANK, MLA_HEADS, MLA_NOPE + MLA_V)
    wuk = jnp.pad(ukv[:, :, :MLA_NOPE], ((0, 0), (0, 0), (0, LANES - MLA_NOPE))).reshape(MLA_KV_RANK, MLA_HEADS * LANES)
    wuv = ukv[:, :, MLA_NOPE:].reshape(MLA_KV_RANK, MLA_WIDTH)
    gq = jnp.concatenate([_pad_lanes(mla_q_gain[None, :]), _pad_lanes(swap_rope(mla_q_gain[None, :]))], axis=0)
    gk = _pad_lanes(mla_k_gain[None, :])
    gdq = jnp.tile(dsa_q_gain[None, :], (1, LANES // DSA_DIM))
    gdk = jnp.tile(dsa_k_gain[None, :], (1, LANES // DSA_DIM))
    wb_dsa = jnp.transpose(w_branch_dsa.reshape(DSA_KV_HEADS, DSA_GROUP, DSA_DIM, D_MODEL),
                           (1, 0, 2, 3)).reshape(DSA_WIDTH, D_MODEL)
    return (w_all, wuq.astype(jnp.bfloat16), wuk.astype(jnp.bfloat16), wuv.astype(jnp.bfloat16),
            gq, gk, gdq, gdk, wb_dsa.astype(jnp.bfloat16))


def _const_spec(shape):
    return pl.BlockSpec(shape, lambda *_: (0,) * len(shape))


def _layer(x, positions, invf, spread, norm_gain, w_in, b_merge, mla_q_norm, mla_w_uq, mla_kv_norm, mla_w_ukv,
           mla_q_gain, mla_k_gain, dsa_q_gain, dsa_k_gain, w_branch_mla, w_branch_dsa, w_out):
    bsz, s_len, d_model = x.shape
    assert d_model == D_MODEL and TM_PROJ % PROJ_ROWS == 0 and (bsz * s_len) % TM_MERGE == 0
    assert all(s_len % t == 0 for t in (TM_PROJ, TQ_MLA, TK_MLA, TQ_DSA, TK_DSA))
    k_top = min(TOPK_MAX, s_len // 4)
    (w_all, wuq, wuk, wuv, gq, gk, gdq, gdk, wb_dsa) = _layout_params(
        w_in, mla_w_uq, mla_w_ukv, mla_q_gain, mla_k_gain, dsa_q_gain, dsa_k_gain, w_branch_dsa)
    bf16, f32 = jnp.bfloat16, jnp.float32
    gmax = lambda g: jnp.max(jnp.abs(g))
    shift_a = (BOUND_MARGIN * MLA_QK ** 0.5 * LOG2_E * gmax(mla_q_gain) * gmax(mla_k_gain)).reshape(1).astype(f32)
    shift_b = (BOUND_MARGIN * DSA_DIM ** 0.5 * LOG2_E * gmax(dsa_q_gain) * gmax(dsa_k_gain)).reshape(1).astype(f32)
    smem = pl.BlockSpec(memory_space=pltpu.SMEM)

    tm = TM_PROJ
    tok = lambda w: pl.BlockSpec((1, tm, w), lambda b, i: (b, i, 0))
    head = pl.BlockSpec((1, MLA_HEADS, tm, LANES), lambda b, i: (b, 0, i, 0))
    outs = pl.pallas_call(
        _proj_kernel,
        grid=(bsz, s_len // tm),
        in_specs=[tok(D_MODEL), pl.BlockSpec((1, 1, tm), lambda b, i: (b, 0, i)),
                  _const_spec(invf.shape), _const_spec(spread.shape), _const_spec((1, D_MODEL)),
                  _const_spec((1, MLA_Q_RANK)), _const_spec((1, MLA_KV_RANK)),
                  _const_spec((2, LANES)), _const_spec((1, LANES)), _const_spec((1, LANES)), _const_spec((1, LANES)),
                  _const_spec((D_MODEL, W_ALL)), _const_spec((MLA_Q_RANK, 2 * MLA_HEADS * LANES)),
                  _const_spec((MLA_KV_RANK, MLA_HEADS * LANES)), _const_spec((MLA_KV_RANK, MLA_WIDTH))],
        out_specs=[head, head, pl.BlockSpec((1, 1, MLA_WIDTH, tm), lambda b, i: (b, i, 0, 0)),
                   head, tok(LANES),
                   pl.BlockSpec((1, 1, LANES, tm), lambda b, i: (b, i, 0, 0)),
                   tok(IDX_HEADS * IDX_DIM), tok(LANES),
                   pl.BlockSpec((1, IDX_HEADS, tm), lambda b, i: (b, 0, i)), tok(W_G)],
        out_shape=[jax.ShapeDtypeStruct((bsz, MLA_HEADS, s_len, LANES), bf16),
                   jax.ShapeDtypeStruct((bsz, MLA_HEADS, s_len, LANES), bf16),
                   jax.ShapeDtypeStruct((bsz, s_len // tm, MLA_WIDTH, tm), bf16),
                   jax.ShapeDtypeStruct((bsz, DSA_HEADS, s_len, LANES), bf16),
                   jax.ShapeDtypeStruct((bsz, s_len, LANES), bf16),
                   jax.ShapeDtypeStruct((bsz, s_len // tm, LANES, tm), bf16),
                   jax.ShapeDtypeStruct((bsz, s_len, IDX_HEADS * IDX_DIM), bf16),
                   jax.ShapeDtypeStruct((bsz, s_len, LANES), bf16),
                   jax.ShapeDtypeStruct((bsz, IDX_HEADS, s_len), f32),
                   jax.ShapeDtypeStruct((bsz, s_len, W_G), bf16)],
        compiler_params=pltpu.CompilerParams(dimension_semantics=("arbitrary", "arbitrary"),
                                             vmem_limit_bytes=VMEM_LIMIT),
        name="proj",
    )(x, positions[:, None, :], invf, spread, norm_gain[None, :], mla_q_norm[None, :], mla_kv_norm[None, :],
      gq, gk, gdq, gdk, w_all, wuq, wuk, wuv)
    qa, ka, vat, qb, kb, vbt, qi, ki, wit, gates = outs

    tq_a, tk_a, hps = TQ_MLA, TK_MLA, MLA_HEADS_PER_STEP
    o_a = pl.pallas_call(
        functools.partial(_mla_kernel, tq=tq_a, tk=tk_a, heads=hps),
        grid=(bsz, MLA_HEADS // hps, s_len // tq_a),
        in_specs=[smem, pl.BlockSpec((1, hps, tq_a, LANES), lambda b, hp, i: (b, hp, i, 0)),
                  pl.BlockSpec((1, hps, s_len, LANES), lambda b, hp, i: (b, hp, 0, 0)),
                  pl.BlockSpec((1, s_len // tm, hps * MLA_V, tm), lambda b, hp, i: (b, 0, hp, 0))],
        out_specs=pl.BlockSpec((1, tq_a, hps * MLA_V), lambda b, hp, i: (b, i, hp)),
        out_shape=jax.ShapeDtypeStruct((bsz, s_len, MLA_WIDTH), bf16),
        compiler_params=pltpu.CompilerParams(dimension_semantics=("arbitrary",) * 3,
                                             vmem_limit_bytes=VMEM_LIMIT),
        name="mla",
    )(shift_a, qa, ka, vat)

    tq_b, tk_b = TQ_DSA, TK_DSA
    row_b = lambda w: pl.BlockSpec((1, tq_b, w), lambda b, i: (b, i, 0))
    full_b = lambda w: pl.BlockSpec((1, s_len, w), lambda b, i: (b, 0, 0))
    o_b = pl.pallas_call(
        functools.partial(_dsa_kernel, tq=tq_b, tk=tk_b, k_top=k_top),
        grid=(bsz, s_len // tq_b),
        in_specs=[smem, pl.BlockSpec((1, DSA_HEADS, tq_b, LANES), lambda b, i: (b, 0, i, 0)), full_b(LANES),
                  pl.BlockSpec((1, s_len // tm, LANES, tm), lambda b, i: (b, 0, 0, 0)),
                  row_b(IDX_HEADS * IDX_DIM), full_b(LANES),
                  pl.BlockSpec((1, IDX_HEADS, tq_b), lambda b, i: (b, 0, i))],
        out_specs=row_b(DSA_WIDTH),
        out_shape=jax.ShapeDtypeStruct((bsz, s_len, DSA_WIDTH), bf16),
        scratch_shapes=[pltpu.VMEM((s_len // tk_b, tk_b, tq_b), f32),
                        pltpu.VMEM((s_len // tk_b, tk_b, tq_b), bf16),
                        pltpu.VMEM((IDX_HEADS * tq_b, LANES), bf16)],
        compiler_params=pltpu.CompilerParams(dimension_semantics=("arbitrary", "arbitrary"),
                                             vmem_limit_bytes=VMEM_LIMIT),
        name="dsa",
    )(shift_b, qb, kb, vbt, qi, ki, wit)

    tm_m = TM_MERGE
    n_tok = bsz * s_len
    flat = lambda w: pl.BlockSpec((tm_m, w), lambda i: (i, 0))
    out = pl.pallas_call(
        _merge_kernel,
        grid=(n_tok // tm_m,),
        in_specs=[flat(D_MODEL), flat(MLA_WIDTH), flat(DSA_WIDTH), flat(W_G), _const_spec((2, D_MODEL)),
                  _const_spec((MLA_WIDTH, D_MODEL)), _const_spec((DSA_WIDTH, D_MODEL)),
                  _const_spec((D_MODEL, D_MODEL))],
        out_specs=flat(D_MODEL),
        out_shape=jax.ShapeDtypeStruct((n_tok, D_MODEL), f32),
        compiler_params=pltpu.CompilerParams(dimension_semantics=("arbitrary",),
                                             vmem_limit_bytes=VMEM_LIMIT),
        name="merge",
    )(x.reshape(n_tok, D_MODEL), o_a.reshape(n_tok, MLA_WIDTH), o_b.reshape(n_tok, DSA_WIDTH),
      gates.reshape(n_tok, W_G), b_merge, w_branch_mla.astype(bf16), wb_dsa, w_out.astype(bf16))
    return out.reshape(bsz, s_len, D_MODEL)


def kernel(x, positions, norm_gain, w_in, b_merge, mla_q_norm, mla_w_uq, mla_kv_norm, mla_w_ukv,
           mla_q_gain, mla_k_gain, dsa_q_gain, dsa_k_gain, w_branch_mla, w_branch_dsa, w_out):
    invf, spread = _rope_inputs()
    for l in range(norm_gain.shape[0]):
        x = _layer(x, positions, invf, spread, norm_gain[l], w_in[l], b_merge[l], mla_q_norm[l], mla_w_uq[l],
                   mla_kv_norm[l], mla_w_ukv[l], mla_q_gain[l], mla_k_gain[l], dsa_q_gain[l], dsa_k_gain[l],
                   w_branch_mla[l], w_branch_dsa[l], w_out[l])
    return x
```

```python
import functools

import numpy as np
import jax
import jax.numpy as jnp
from jax import lax
from jax.experimental import pallas as pl
from jax.experimental.pallas import tpu as pltpu

D_MODEL = 1024
MLA_HEADS = 8
MLA_Q_RANK = 256
MLA_KV_RANK = 128
MLA_NOPE = 64
MLA_ROPE = 32
MLA_QK = MLA_NOPE + MLA_ROPE
MLA_V = 64
MLA_WIDTH = MLA_HEADS * MLA_V
DSA_HEADS = 8
DSA_KV_HEADS = 2
DSA_GROUP = DSA_HEADS // DSA_KV_HEADS
DSA_DIM = 64
DSA_WIDTH = DSA_HEADS * DSA_DIM
DSA_ROT = DSA_DIM // 4
IDX_HEADS = 8
IDX_DIM = 32
IDX_ROT = IDX_DIM // 4
TOPK_MAX = 256
ROPE_THETA = 500000.0
NORM_EPS = 1e-6

IN_SPLITS = (
    MLA_Q_RANK, MLA_KV_RANK, MLA_ROPE, MLA_WIDTH,
    DSA_HEADS * DSA_DIM, DSA_KV_HEADS * DSA_DIM, DSA_KV_HEADS * DSA_DIM, DSA_WIDTH,
    IDX_HEADS * IDX_DIM, IDX_DIM, IDX_HEADS,
    D_MODEL, D_MODEL,
)

LANES = 128
MXU_WIDTH = 256
VMEM_LIMIT = 56 * 1024 * 1024
INT_MIN = -(2 ** 31)
F32_LOWEST = float(np.finfo(np.float32).min)
BF16_ROWS = 16
DIAG_SPLIT = 4
FINE_BITS = 17
NEG_BIG = -1e30
LOG2_E = 1.4426950408889634
FAST_MAX_SHIFT = 60.0
BOUND_MARGIN = 1.02

OFF_A = 0
W_A = MLA_Q_RANK + MLA_KV_RANK + LANES
OFF_B = OFF_A + W_A
W_B = DSA_WIDTH + 2 * DSA_KV_HEADS * DSA_DIM
OFF_I = OFF_B + W_B
W_I = IDX_HEADS * IDX_DIM + 2 * LANES
OFF_G = OFF_I + W_I
W_G = MLA_WIDTH + DSA_WIDTH + 2 * D_MODEL
W_ALL = OFF_G + W_G
PROJ_ROWS = 256
GATE_CHUNK = 512
N_GATE_CHUNKS = W_G // GATE_CHUNK


TM_PROJ = 512
TQ_MLA, TK_MLA = 512, 512
MLA_HEADS_PER_STEP = 8
TQ_DSA, TK_DSA = 512, 512
TM_MERGE = 1024

ROPE_LAYOUTS = ((LANES, MLA_NOPE, MLA_ROPE, 0), (DSA_DIM, 0, DSA_ROT, MLA_ROPE // 2),
                (IDX_DIM, 0, IDX_ROT, MLA_ROPE // 2 + DSA_ROT // 2))
ROW_A, ROW_D, ROW_I = 0, 1, 2
FREQ_ROWS = 32
ONE_ROW = FREQ_ROWS - 1


def _rms(v, width):
    return lax.rsqrt(jnp.sum(v * v, axis=-1, keepdims=True) * (1.0 / width) + NORM_EPS)


def _rope(v, cos, sin_lo, sin_hi, half):
    return (v * cos + pltpu.roll(v, LANES - half, 1) * sin_lo + pltpu.roll(v, half, 1) * sin_hi)


def _bf16_dot(a, b):
    return jnp.dot(a, b, preferred_element_type=jnp.float32)


def _nt_dot(a, b):
    return lax.dot_general(a, b, (((1,), (1,)), ((), ())), preferred_element_type=jnp.float32)


def _proj_kernel(x_ref, pos_ref, invf_ref, spread_ref, gx_ref, gcq_ref, gckv_ref, gq_ref, gk_ref, gdq_ref, gdk_ref,
                 w_ref, wuq_ref, wuk_ref, wuv_ref,
                 qa_ref, ka_ref, va_ref, qb_ref, kb_ref, vb_ref, qi_ref, ki_ref, wi_ref, g_ref):
    bf16 = jnp.bfloat16
    subs = [slice(r0, r0 + PROJ_ROWS) for r0 in range(0, x_ref.shape[1], PROJ_ROWS)]
    hs = []
    for rows in subs:
        x = x_ref[0, rows]
        hs.append((x * _rms(x, D_MODEL) * gx_ref[...]).astype(bf16))
    pas = [_bf16_dot(h, w_ref[:, OFF_A:OFF_A + W_A]) for h in hs]
    tabs = [_rope_tables(pos_ref[0, :, rows], invf_ref, spread_ref) for rows in subs]
    lat = []
    for pa in pas:
        c_q = pa[:, :MLA_Q_RANK]
        c_kv = pa[:, MLA_Q_RANK:MLA_Q_RANK + MLA_KV_RANK]
        lat.append(((c_q * _rms(c_q, MLA_Q_RANK) * gcq_ref[...]).astype(bf16),
                    (c_kv * _rms(c_kv, MLA_KV_RANK) * gckv_ref[...]).astype(bf16)))
    pbs = [_bf16_dot(h, w_ref[:, OFF_B:OFF_B + W_B]) for h in hs]
    pis = [_bf16_dot(h, w_ref[:, OFF_I:OFF_I + W_I]) for h in hs]
    ups = [(_bf16_dot(cqn, wuq_ref[...]),
            _bf16_dot(ckvn, wuk_ref[...]),
            _bf16_dot(ckvn, wuv_ref[...])) for cqn, ckvn in lat]
    for n, rows in enumerate(subs):
        kpe = pas[n][:, MLA_Q_RANK + MLA_KV_RANK:]
        _proj_epilogue(rows, (tabs[n], kpe) + ups[n] + (pbs[n], pis[n]), gq_ref, gk_ref, gdq_ref, gdk_ref,
                       qa_ref, ka_ref, va_ref, qb_ref, kb_ref, vb_ref, qi_ref, ki_ref, wi_ref)
    for rows, h in zip(subs, hs):
        for c in range(N_GATE_CHUNKS):
            cols = slice(c * GATE_CHUNK, (c + 1) * GATE_CHUNK)
            g_ref[0, rows, cols] = _bf16_dot(h, w_ref[:, OFF_G + c * GATE_CHUNK:OFF_G + (c + 1) * GATE_CHUNK]).astype(g_ref.dtype)


def _rope_tables(pos_row, invf_ref, spread_ref):
    bf16 = jnp.bfloat16
    ang = invf_ref[...] * pos_row.astype(jnp.float32)
    trig = jnp.concatenate([jnp.cos(ang), jnp.sin(ang)], axis=0)
    t1 = trig.astype(bf16)
    r1 = trig - t1.astype(jnp.float32)
    t2 = r1.astype(bf16)
    t3 = (r1 - t2.astype(jnp.float32)).astype(bf16)
    return lax.dot_general(jnp.concatenate([t1, t2, t3], axis=0), spread_ref[...], (((0,), (0,)), ((), ())),
                           preferred_element_type=jnp.float32)


def _proj_epilogue(rows, vals, gq_ref, gk_ref, gdq_ref, gdk_ref,
                   qa_ref, ka_ref, va_ref, qb_ref, kb_ref, vb_ref, qi_ref, ki_ref, wi_ref):
    tabs, kpe, q_up, k_up, v_up, pb, pi = vals
    lane = lax.broadcasted_iota(jnp.int32, (1, LANES), 1)
    bf16 = jnp.bfloat16

    def tables(n):
        return tuple(tabs[:, (3 * n + c) * LANES:(3 * n + c + 1) * LANES] for c in range(3))

    for c in range(MLA_WIDTH // LANES):
        va_ref[0, 0, c * LANES:(c + 1) * LANES, rows] = v_up[:, c * LANES:(c + 1) * LANES].T.astype(bf16)
    cos_a, sl_a, sh_a = tables(ROW_A)
    sin_a = sl_a + sh_a
    q_scale = MLA_QK ** -0.5 * LOG2_E
    kpe_g = kpe * gk_ref[...]
    k_rot = (pltpu.roll(kpe_g, LANES - MLA_ROPE // 2, 1) * sl_a + pltpu.roll(kpe_g, MLA_ROPE // 2, 1) * sh_a)
    for hd in range(MLA_HEADS):
        qh = q_up[:, hd * LANES:(hd + 1) * LANES]
        q_sw = q_up[:, (MLA_HEADS + hd) * LANES:(MLA_HEADS + hd + 1) * LANES]
        q_rot = qh * gq_ref[0:1, :] * cos_a + q_sw * gq_ref[1:2, :] * sin_a
        qa_ref[0, hd, rows] = (q_rot * (_rms(qh, MLA_QK) * q_scale)).astype(bf16)
        kh = k_up[:, hd * LANES:(hd + 1) * LANES] + kpe
        ka_ref[0, hd, rows] = ((kh * gk_ref[...] * cos_a + k_rot) * _rms(kh, MLA_QK)).astype(bf16)

    cos_d, sl_d, sh_d = tables(ROW_D)
    low = lane < DSA_DIM

    def dsa_norm_rope(blk, gain):
        sq = blk * blk
        r_lo = lax.rsqrt(jnp.sum(jnp.where(low, sq, 0.0), axis=-1, keepdims=True) * (1.0 / DSA_DIM) + NORM_EPS)
        r_hi = lax.rsqrt(jnp.sum(jnp.where(low, 0.0, sq), axis=-1, keepdims=True) * (1.0 / DSA_DIM) + NORM_EPS)
        return _rope(blk * jnp.where(low, r_lo, r_hi) * gain, cos_d, sl_d, sh_d, DSA_ROT // 2)

    dsa_scale = DSA_DIM ** -0.5 * LOG2_E
    for j in range(DSA_WIDTH // LANES):
        blk = pb[:, j * LANES:(j + 1) * LANES]
        qn = (dsa_norm_rope(blk, gdq_ref[...]) * dsa_scale).astype(bf16)
        qb_ref[0, j, rows] = jnp.where(low, qn, jnp.zeros_like(qn))
        qb_ref[0, DSA_GROUP + j, rows] = jnp.where(low, jnp.zeros_like(qn), qn)
    kb_ref[0, rows] = dsa_norm_rope(pb[:, DSA_WIDTH:DSA_WIDTH + LANES], gdk_ref[...]).astype(bf16)
    vb_ref[0, 0, :, rows] = pb[:, DSA_WIDTH + LANES:].T.astype(bf16)

    cos_i, sl_i, sh_i = tables(ROW_I)
    n_qi = IDX_HEADS * IDX_DIM // LANES
    for j in range(n_qi + 1):
        blk = _rope(pi[:, j * LANES:(j + 1) * LANES], cos_i, sl_i, sh_i, IDX_ROT // 2).astype(bf16)
        if j < n_qi:
            qi_ref[0, rows, j * LANES:(j + 1) * LANES] = blk
        else:
            ki_ref[0, rows] = blk
    wi_t = (pi[:, (n_qi + 1) * LANES:] * (IDX_HEADS ** -0.5 * IDX_DIM ** -0.5)).T
    wi_ref[0, :, rows] = wi_t[0:IDX_HEADS, :]


def _softmax_tile(s, m, l, shift):
    if shift is None:
        m_new = jnp.maximum(m, jnp.max(s, axis=0, keepdims=True))
        alpha = jnp.exp2(m - m_new)
        p = jnp.exp2(s - m_new)
        return m_new, alpha * l + jnp.sum(p, axis=0, keepdims=True), alpha, p.astype(jnp.bfloat16)
    p = jnp.exp2(s - shift)
    return m, l + jnp.sum(p, axis=0, keepdims=True), None, p.astype(jnp.bfloat16)


def _by_shift(shift_ref, body):
    shift = shift_ref[0]

    @pl.when(shift <= FAST_MAX_SHIFT)
    def _():
        body(shift)

    @pl.when(shift > FAST_MAX_SHIFT)
    def _():
        body(None)


def _mla_kernel(shift_ref, q_ref, k_ref, vt_ref, o_ref, *, tq, tk, heads):
    i = pl.program_id(2)
    q_start = i * tq
    n_full = q_start // tk
    n_all = (q_start + tq + tk - 1) // tk
    tv = vt_ref.shape[-1]
    kpos0 = lax.broadcasted_iota(jnp.int32, (tk, tq), 0)
    qpos = q_start + lax.broadcasted_iota(jnp.int32, (tk, tq), 1)

    def attend(shift):
        def step(j, carry, masked):
            off = pl.multiple_of(j * tk, tk)
            scores = [_nt_dot(k_ref[0, hd, pl.ds(off, tk), :], q_ref[0, hd]) for hd in range(heads)]
            out = []
            visible = kpos0 + off <= qpos if masked else None
            shift_t = jnp.where(visible, shift, -NEG_BIG) if masked and shift is not None else shift
            for hd in range(heads):
                m, l, acc = carry[hd]
                s = scores[hd]
                if masked and shift is None:
                    s = jnp.where(visible, s, NEG_BIG)
                m, l, alpha, pb = _softmax_tile(s, m, l, shift_t)
                if alpha is not None:
                    acc = alpha * acc
                for c in range(tk // tv):
                    vt = vt_ref[0, j * (tk // tv) + c, hd * MLA_V:(hd + 1) * MLA_V, :]
                    acc = acc + _bf16_dot(vt, pb[c * tv:(c + 1) * tv])
                out.append((m, l, acc))
            return tuple(out)

        init = tuple((jnp.full((1, tq), NEG_BIG, jnp.float32), jnp.zeros((1, tq), jnp.float32),
                      jnp.zeros((MLA_V, tq), jnp.float32)) for _ in range(heads))
        carry = lax.fori_loop(0, n_full, functools.partial(step, masked=False), init)
        carry = lax.fori_loop(n_full, n_all, functools.partial(step, masked=True), carry)
        outs = [acc / l for (_, l, acc) in carry]
        for c in range(heads // 2):
            o_ref[0, :, c * LANES:(c + 1) * LANES] = jnp.concatenate(outs[2 * c:2 * c + 2], axis=0).T.astype(o_ref.dtype)

    _by_shift(shift_ref, attend)


def _dsa_kernel(shift_ref, qb_ref, kb_ref, vt_ref, qi_ref, ki_ref, wt_ref, o_ref,
                sc_ref, sb_ref, qim_ref, *, tq, tk, k_top):
    i = pl.program_id(1)
    q_start = i * tq
    n_kv = (q_start + tq + tk - 1) // tk
    n_heads = IDX_HEADS
    tv = vt_ref.shape[-1]

    lane_q = lax.broadcasted_iota(jnp.int32, (tq, LANES), 1)
    per_blk = LANES // IDX_DIM
    for hd in range(IDX_HEADS):
        blk = qi_ref[0, :, (hd // per_blk) * LANES:(hd // per_blk + 1) * LANES]
        slot = hd % per_blk
        in_head = (lane_q >= slot * IDX_DIM) & (lane_q < (slot + 1) * IDX_DIM)
        qim_ref[hd * tq:(hd + 1) * tq, :] = jnp.where(in_head, blk, jnp.zeros_like(blk))

    kpos0 = lax.broadcasted_iota(jnp.int32, (tk, tq), 0)
    qpos = q_start + lax.broadcasted_iota(jnp.int32, (tk, tq), 1)
    w_rows = [wt_ref[0, hd:hd + 1, :] for hd in range(IDX_HEADS)]

    def score_tile(j, _):
        off = pl.multiple_of(j * tk, tk)
        s_all = _nt_dot(ki_ref[0, pl.ds(off, tk), :], qim_ref[...])
        sc = w_rows[0] * jnp.maximum(s_all[:, 0:tq], 0.0)
        for hd in range(1, n_heads):
            sc = sc + w_rows[hd] * jnp.maximum(s_all[:, hd * tq:(hd + 1) * tq], 0.0)
        sc = jnp.where(kpos0 + off <= qpos, sc, -jnp.inf)
        sc_ref[j] = sc
        sb_ref[j] = sc.astype(jnp.bfloat16)
        return 0

    lax.fori_loop(0, n_kv, score_tile, 0)

    def pattern_value(u):
        key = u ^ INT_MIN
        return lax.bitcast_convert_type(key ^ ((key >> 31) & 0x7FFFFFFF), jnp.float32)

    assert tq == tk
    jd = n_kv - 1
    step = tk // DIAG_SPLIT
    all_q = slice(0, tq)
    groups = [(slice(g * step, (g + 1) * step), all_q if g == 0 else slice(g * step, tq)) for g in range(DIAG_SPLIT)]

    def lanes(row, q):
        if q is all_q:
            return row
        return jnp.broadcast_to(row, (8, tq))[:, q][0:1]

    def widen(part, q):
        return part if q is all_q else jnp.concatenate([jnp.zeros((part.shape[0], q.start), part.dtype), part], axis=1)

    def count(pred):
        def body(j, acc):
            hit = jnp.where(pred(sc_ref[j], kpos0 + j * tk, all_q), 1, 0)
            return acc + jnp.sum(hit.reshape(tk // 8, 8, tq), axis=0)
        acc = lax.fori_loop(0, jd, body, jnp.zeros((8, tq), jnp.int32))
        for rows, q in groups:
            width = tq - (0 if q is all_q else q.start)
            kpos = jd * tk + rows.start + lax.broadcasted_iota(jnp.int32, (step, width), 0)
            hit = jnp.where(pred(sc_ref[jd, rows, q], kpos, q), 1, 0)
            acc = acc + widen(jnp.sum(hit.reshape(step // 8, 8, width), axis=0), q)
        return jnp.sum(acc, axis=0, keepdims=True)

    def count_coarse(cand_b):
        one, zero = jnp.ones((), jnp.bfloat16), jnp.zeros((), jnp.bfloat16)

        def slabs(j, rows, q):
            cand_q = lanes(cand_b.astype(jnp.float32), q).astype(jnp.bfloat16)
            part = None
            for r in range(rows.start // BF16_ROWS, rows.stop // BF16_ROWS):
                hit = jnp.where(sb_ref[j, r * BF16_ROWS:(r + 1) * BF16_ROWS, q] >= cand_q, one, zero)
                part = hit if part is None else part + hit
            return part.astype(jnp.float32)

        acc = lax.fori_loop(0, jd, lambda j, acc: acc + slabs(j, slice(0, tk), all_q),
                            jnp.zeros((BF16_ROWS, tq), jnp.float32))
        for rows, q in groups:
            acc = acc + widen(slabs(jd, rows, q), q)
        return jnp.sum(acc, axis=0, keepdims=True)

    def coarse_bit(it, t16):
        cand = t16 | jnp.left_shift(jnp.int32(1), 15 - it)
        cand32 = jnp.left_shift(cand, 16) | jnp.where(cand < 0x8000, 0xFFFF, 0)
        cnt = count_coarse(pattern_value(cand32).astype(jnp.bfloat16))
        return jnp.where(cnt >= k_top, cand, t16)

    t16 = lax.fori_loop(0, 16, coarse_bit, jnp.zeros((1, tq), jnp.int32))
    few = t16 < 0x0080
    base = jnp.left_shift(t16, 16) + jnp.where(t16 < 0x8000, 0x7FFF, -0x8000)
    base = jnp.where(few, 0x00800000, base)

    def fine_bit(it, carry):
        t, cnt_t = carry
        cand = t | jnp.left_shift(jnp.int32(1), FINE_BITS - 1 - it)
        cand_f = pattern_value(base + cand)
        cnt = count(lambda sc, kpos, q: sc >= lanes(cand_f, q))
        take = cnt >= k_top
        return jnp.where(take, cand, t), jnp.where(take, cnt, cnt_t)

    t, cnt_ge = lax.fori_loop(0, FINE_BITS, fine_bit,
                              (jnp.zeros((1, tq), jnp.int32), jnp.full((1, tq), k_top, jnp.int32)))
    thr = jnp.where(few, F32_LOWEST, pattern_value(base + t))
    tied = jnp.logical_not(few) & ((cnt_ge > k_top) | (t == 0))

    def tie_cut():
        cnt_gt = count(lambda sc, kpos, q: sc > lanes(thr, q))
        need = k_top - cnt_gt

        def index_bit(it, c):
            cand = c | jnp.left_shift(jnp.int32(1), 12 - it)
            below = count(lambda sc, kpos, q: (sc == lanes(thr, q)) & (kpos < lanes(cand, q)))
            return jnp.where(below < need, cand, c)

        c = lax.fori_loop(0, 13, index_bit, jnp.zeros((1, tq), jnp.int32))
        return jnp.where(tied, c, jnp.int32(2 ** 30))

    cut = lax.cond(jnp.max(tied.astype(jnp.int32)) > 0, tie_cut,
                   lambda: jnp.full((1, tq), 2 ** 30, jnp.int32))

    assert tq >= MXU_WIDTH

    def attend_all(shift):
        def attend(j, carry):
            off = pl.multiple_of(j * tk, tk)
            kt = kb_ref[0, pl.ds(off, tk), :]
            scores = [_nt_dot(kt, qb_ref[0, hd]) for hd in range(n_heads)]
            sc = sc_ref[j]
            sel = (sc > thr) | ((sc == thr) & (kpos0 + off <= cut))
            shift_t = None if shift is None else jnp.where(sel, shift, -NEG_BIG)
            out = []
            for hd in range(n_heads):
                m, l, acc = carry[hd]
                s = jnp.where(sel, scores[hd], NEG_BIG) if shift is None else scores[hd]
                m, l, alpha, pb = _softmax_tile(s, m, l, shift_t)
                if alpha is not None:
                    acc = alpha * acc
                g = hd // DSA_GROUP
                for c in range(tk // tv):
                    vt = vt_ref[0, j * (tk // tv) + c, g * DSA_DIM:(g + 1) * DSA_DIM, :]
                    acc = acc + _bf16_dot(vt, pb[c * tv:(c + 1) * tv])
                out.append((m, l, acc))
            return tuple(out)

        init = tuple((jnp.full((1, tq), NEG_BIG, jnp.float32), jnp.zeros((1, tq), jnp.float32),
                      jnp.zeros((DSA_DIM, tq), jnp.float32)) for _ in range(n_heads))
        carry = lax.fori_loop(0, n_kv, attend, init)
        outs = [acc / l for (_, l, acc) in carry]
        for j in range(DSA_GROUP):
            o_ref[0, :, j * LANES:(j + 1) * LANES] = jnp.concatenate(
                [outs[j], outs[DSA_GROUP + j]], axis=0).T.astype(o_ref.dtype)

    _by_shift(shift_ref, attend_all)


def _merge_kernel(x_ref, oa_ref, ob_ref, g_ref, bm_ref, wa_ref, wb_ref, wo_ref, out_ref):
    f32 = jnp.float32
    ga = g_ref[:, 0:MLA_WIDTH].astype(f32)
    gb = g_ref[:, MLA_WIDTH:MLA_WIDTH + DSA_WIDTH].astype(f32)
    ma = g_ref[:, MLA_WIDTH + DSA_WIDTH:MLA_WIDTH + DSA_WIDTH + D_MODEL].astype(f32)
    mb = g_ref[:, MLA_WIDTH + DSA_WIDTH + D_MODEL:].astype(f32)
    ya = _bf16_dot((oa_ref[...].astype(f32) * (ga * jax.nn.sigmoid(ga))).astype(jnp.bfloat16), wa_ref[...])
    yb = _bf16_dot((ob_ref[...].astype(f32) * (gb * jax.nn.sigmoid(gb))).astype(jnp.bfloat16), wb_ref[...])
    merged = jax.nn.sigmoid(ma + bm_ref[0:1, :]) * ya + jax.nn.sigmoid(mb + bm_ref[1:2, :]) * yb
    out_ref[...] = x_ref[...] + _bf16_dot(merged.astype(jnp.bfloat16), wo_ref[...])


def _inv_freq(rot_dim):
    return ROPE_THETA ** (-jnp.arange(0, rot_dim, 2, dtype=jnp.float32) / rot_dim)


def _rope_inputs():
    invf = jnp.concatenate([_inv_freq(rot) for (_, _, rot, _) in ROPE_LAYOUTS])
    invf = jnp.pad(invf, (0, FREQ_ROWS - invf.shape[0]))[:, None]
    spread = np.zeros((2 * FREQ_ROWS, 3 * len(ROPE_LAYOUTS) * LANES), np.float32)
    for n, (period, start, rot, row0) in enumerate(ROPE_LAYOUTS):
        half = rot // 2
        for lane in range(LANES):
            p = lane % period - start
            cos_col, lo_col, hi_col = ((3 * n + c) * LANES + lane for c in range(3))
            if 0 <= p < rot:
                spread[row0 + p % half, cos_col] = 1.0
                spread[FREQ_ROWS + row0 + p % half, lo_col if p < half else hi_col] = -1.0 if p < half else 1.0
            else:
                spread[ONE_ROW, cos_col] = 1.0
    return invf, jnp.asarray(np.tile(spread, (3, 1)), jnp.bfloat16)


def _pad_lanes(a, start=0):
    return jnp.pad(a, ((0, 0), (start, LANES - start - a.shape[1])))


def _layout_params(w_in, mla_w_uq, mla_w_ukv, mla_q_gain, mla_k_gain, dsa_q_gain, dsa_k_gain,
                   w_branch_dsa):
    offs = np.concatenate([[0], np.cumsum(IN_SPLITS)])
    (c_q, c_kv, k_pe, gate_a, q_b, k_b, v_b, gate_b, q_i, k_i, w_i, m_a, m_b) = [
        w_in[:, offs[n]:offs[n + 1]] for n in range(len(IN_SPLITS))]

    def pair_heads(w):
        rows = w.shape[0]
        w = w.reshape(rows, DSA_KV_HEADS, DSA_GROUP, DSA_DIM)
        return jnp.transpose(w, (0, 2, 1, 3)).reshape(rows, DSA_WIDTH)

    w_all = jnp.concatenate([
        c_q, c_kv, _pad_lanes(k_pe, MLA_NOPE),
        pair_heads(q_b), k_b, v_b,
        q_i, jnp.tile(k_i, (1, LANES // IDX_DIM)), _pad_lanes(w_i),
        gate_a, pair_heads(gate_b), m_a, m_b], axis=1).astype(jnp.bfloat16)
    uq = mla_w_uq.reshape(MLA_Q_RANK, MLA_HEADS, MLA_QK)
    half = MLA_ROPE // 2

    def swap_rope(a):
        return jnp.concatenate([jnp.zeros_like(a[..., :MLA_NOPE]), a[..., MLA_NOPE + half:], a[..., MLA_NOPE:MLA_NOPE + half]], -1)

    pad_qk = lambda a: jnp.pad(a, ((0, 0), (0, 0), (0, LANES - MLA_QK))).reshape(MLA_Q_RANK, MLA_HEADS * LANES)
    wuq = jnp.concatenate([pad_qk(uq), pad_qk(swap_rope(uq))], axis=1)
    ukv = mla_w_ukv.reshape(MLA_KV_RANK, MLA_HEADS, MLA_NOPE + MLA_V)
    wuk = jnp.pad(ukv[:, :, :MLA_NOPE], ((0, 0), (0, 0), (0, LANES - MLA_NOPE))).reshape(MLA_KV_RANK, MLA_HEADS * LANES)
    wuv = ukv[:, :, MLA_NOPE:].reshape(MLA_KV_RANK, MLA_WIDTH)
    gq = jnp.concatenate([_pad_lanes(mla_q_gain[None, :]), _pad_lanes(swap_rope(mla_q_gain[None, :]))], axis=0)
    gk = _pad_lanes(mla_k_gain[None, :])
    gdq = jnp.tile(dsa_q_gain[None, :], (1, LANES // DSA_DIM))
    gdk = jnp.tile(dsa_k_gain[None, :], (1, LANES // DSA_DIM))
    wb_dsa = jnp.transpose(w_branch_dsa.reshape(DSA_KV_HEADS, DSA_GROUP, DSA_DIM, D_MODEL),
                           (1, 0, 2, 3)).reshape(DSA_WIDTH, D_MODEL)
    return (w_all, wuq.astype(jnp.bfloat16), wuk.astype(jnp.bfloat16), wuv.astype(jnp.bfloat16),
            gq, gk, gdq, gdk, wb_dsa.astype(jnp.bfloat16))


def _const_spec(shape):
    return pl.BlockSpec(shape, lambda *_: (0,) * len(shape))


def _layer(x, positions, invf, spread, norm_gain, w_in, b_merge, mla_q_norm, mla_w_uq, mla_kv_norm, mla_w_ukv,
           mla_q_gain, mla_k_gain, dsa_q_gain, dsa_k_gain, w_branch_mla, w_branch_dsa, w_out):
    bsz, s_len, d_model = x.shape
    assert d_model == D_MODEL and TM_PROJ % PROJ_ROWS == 0 and (bsz * s_len) % TM_MERGE == 0
    assert all(s_len % t == 0 for t in (TM_PROJ, TQ_MLA, TK_MLA, TQ_DSA, TK_DSA))
    k_top = min(TOPK_MAX, s_len // 4)
    (w_all, wuq, wuk, wuv, gq, gk, gdq, gdk, wb_dsa) = _layout_params(
        w_in, mla_w_uq, mla_w_ukv, mla_q_gain, mla_k_gain, dsa_q_gain, dsa_k_gain, w_branch_dsa)
    bf16, f32 = jnp.bfloat16, jnp.float32
    gmax = lambda g: jnp.max(jnp.abs(g))
    shift_a = (BOUND_MARGIN * MLA_QK ** 0.5 * LOG2_E * gmax(mla_q_gain) * gmax(mla_k_gain)).reshape(1).astype(f32)
    shift_b = (BOUND_MARGIN * DSA_DIM ** 0.5 * LOG2_E * gmax(dsa_q_gain) * gmax(dsa_k_gain)).reshape(1).astype(f32)
    smem = pl.BlockSpec(memory_space=pltpu.SMEM)

    tm = TM_PROJ
    tok = lambda w: pl.BlockSpec((1, tm, w), lambda b, i: (b, i, 0))
    head = pl.BlockSpec((1, MLA_HEADS, tm, LANES), lambda b, i: (b, 0, i, 0))
    outs = pl.pallas_call(
        _proj_kernel,
        grid=(bsz, s_len // tm),
        in_specs=[tok(D_MODEL), pl.BlockSpec((1, 1, tm), lambda b, i: (b, 0, i)),
                  _const_spec(invf.shape), _const_spec(spread.shape), _const_spec((1, D_MODEL)),
                  _const_spec((1, MLA_Q_RANK)), _const_spec((1, MLA_KV_RANK)),
                  _const_spec((2, LANES)), _const_spec((1, LANES)), _const_spec((1, LANES)), _const_spec((1, LANES)),
                  _const_spec((D_MODEL, W_ALL)), _const_spec((MLA_Q_RANK, 2 * MLA_HEADS * LANES)),
                  _const_spec((MLA_KV_RANK, MLA_HEADS * LANES)), _const_spec((MLA_KV_RANK, MLA_WIDTH))],
        out_specs=[head, head, pl.BlockSpec((1, 1, MLA_WIDTH, tm), lambda b, i: (b, i, 0, 0)),
                   head, tok(LANES),
                   pl.BlockSpec((1, 1, LANES, tm), lambda b, i: (b, i, 0, 0)),
                   tok(IDX_HEADS * IDX_DIM), tok(LANES),
                   pl.BlockSpec((1, IDX_HEADS, tm), lambda b, i: (b, 0, i)), tok(W_G)],
        out_shape=[jax.ShapeDtypeStruct((bsz, MLA_HEADS, s_len, LANES), bf16),
                   jax.ShapeDtypeStruct((bsz, MLA_HEADS, s_len, LANES), bf16),
                   jax.ShapeDtypeStruct((bsz, s_len // tm, MLA_WIDTH, tm), bf16),
                   jax.ShapeDtypeStruct((bsz, DSA_HEADS, s_len, LANES), bf16),
                   jax.ShapeDtypeStruct((bsz, s_len, LANES), bf16),
                   jax.ShapeDtypeStruct((bsz, s_len // tm, LANES, tm), bf16),
                   jax.ShapeDtypeStruct((bsz, s_len, IDX_HEADS * IDX_DIM), bf16),
                   jax.ShapeDtypeStruct((bsz, s_len, LANES), bf16),
                   jax.ShapeDtypeStruct((bsz, IDX_HEADS, s_len), f32),
                   jax.ShapeDtypeStruct((bsz, s_len, W_G), bf16)],
        compiler_params=pltpu.CompilerParams(dimension_semantics=("arbitrary", "arbitrary"),
                                             vmem_limit_bytes=VMEM_LIMIT),
        name="proj",
    )(x, positions[:, None, :], invf, spread, norm_gain[None, :], mla_q_norm[None, :], mla_kv_norm[None, :],
      gq, gk, gdq, gdk, w_all, wuq, wuk, wuv)
    qa, ka, vat, qb, kb, vbt, qi, ki, wit, gates = outs

    tq_a, tk_a, hps = TQ_MLA, TK_MLA, MLA_HEADS_PER_STEP
    o_a = pl.pallas_call(
        functools.partial(_mla_kernel, tq=tq_a, tk=tk_a, heads=hps),
        grid=(bsz, MLA_HEADS // hps, s_len // tq_a),
        in_specs=[smem, pl.BlockSpec((1, hps, tq_a, LANES), lambda b, hp, i: (b, hp, i, 0)),
                  pl.BlockSpec((1, hps, s_len, LANES), lambda b, hp, i: (b, hp, 0, 0)),
                  pl.BlockSpec((1, s_len // tm, hps * MLA_V, tm), lambda b, hp, i: (b, 0, hp, 0))],
        out_specs=pl.BlockSpec((1, tq_a, hps * MLA_V), lambda b, hp, i: (b, i, hp)),
        out_shape=jax.ShapeDtypeStruct((bsz, s_len, MLA_WIDTH), bf16),
        compiler_params=pltpu.CompilerParams(dimension_semantics=("arbitrary",) * 3,
                                             vmem_limit_bytes=VMEM_LIMIT),
        name="mla",
    )(shift_a, qa, ka, vat)

    tq_b, tk_b = TQ_DSA, TK_DSA
    row_b = lambda w: pl.BlockSpec((1, tq_b, w), lambda b, i: (b, i, 0))
    full_b = lambda w: pl.BlockSpec((1, s_len, w), lambda b, i: (b, 0, 0))
    o_b = pl.pallas_call(
        functools.partial(_dsa_kernel, tq=tq_b, tk=tk_b, k_top=k_top),
        grid=(bsz, s_len // tq_b),
        in_specs=[smem, pl.BlockSpec((1, DSA_HEADS, tq_b, LANES), lambda b, i: (b, 0, i, 0)), full_b(LANES),
                  pl.BlockSpec((1, s_len // tm, LANES, tm), lambda b, i: (b, 0, 0, 0)),
                  row_b(IDX_HEADS * IDX_DIM), full_b(LANES),
                  pl.BlockSpec((1, IDX_HEADS, tq_b), lambda b, i: (b, 0, i))],
        out_specs=row_b(DSA_WIDTH),
        out_shape=jax.ShapeDtypeStruct((bsz, s_len, DSA_WIDTH), bf16),
        scratch_shapes=[pltpu.VMEM((s_len // tk_b, tk_b, tq_b), f32),
                        pltpu.VMEM((s_len // tk_b, tk_b, tq_b), bf16),
                        pltpu.VMEM((IDX_HEADS * tq_b, LANES), bf16)],
        compiler_params=pltpu.CompilerParams(dimension_semantics=("arbitrary", "arbitrary"),
                                             vmem_limit_bytes=VMEM_LIMIT),
        name="dsa",
    )(shift_b, qb, kb, vbt, qi, ki, wit)

    tm_m = TM_MERGE
    n_tok = bsz * s_len
    flat = lambda w: pl.BlockSpec((tm_m, w), lambda i: (i, 0))
    out = pl.pallas_call(
        _merge_kernel,
        grid=(n_tok // tm_m,),
        in_specs=[flat(D_MODEL), flat(MLA_WIDTH), flat(DSA_WIDTH), flat(W_G), _const_spec((2, D_MODEL)),
                  _const_spec((MLA_WIDTH, D_MODEL)), _const_spec((DSA_WIDTH, D_MODEL)),
                  _const_spec((D_MODEL, D_MODEL))],
        out_specs=flat(D_MODEL),
        out_shape=jax.ShapeDtypeStruct((n_tok, D_MODEL), f32),
        compiler_params=pltpu.CompilerParams(dimension_semantics=("arbitrary",),
                                             vmem_limit_bytes=VMEM_LIMIT),
        name="merge",
    )(x.reshape(n_tok, D_MODEL), o_a.reshape(n_tok, MLA_WIDTH), o_b.reshape(n_tok, DSA_WIDTH),
      gates.reshape(n_tok, W_G), b_merge, w_branch_mla.astype(bf16), wb_dsa, w_out.astype(bf16))
    return out.reshape(bsz, s_len, D_MODEL)


def kernel(x, positions, norm_gain, w_in, b_merge, mla_q_norm, mla_w_uq, mla_kv_norm, mla_w_ukv,
           mla_q_gain, mla_k_gain, dsa_q_gain, dsa_k_gain, w_branch_mla, w_branch_dsa, w_out):
    invf, spread = _rope_inputs()
    for l in range(norm_gain.shape[0]):
        x = _layer(x, positions, invf, spread, norm_gain[l], w_in[l], b_merge[l], mla_q_norm[l], mla_w_uq[l],
                   mla_kv_norm[l], mla_w_ukv[l], mla_q_gain[l], mla_k_gain[l], dsa_q_gain[l], dsa_k_gain[l],
                   w_branch_mla[l], w_branch_dsa[l], w_out[l])
    return x
```

```python
import functools

import numpy as np
import jax
import jax.numpy as jnp
from jax import lax
from jax.experimental import pallas as pl
from jax.experimental.pallas import tpu as pltpu

D_MODEL = 1024
MLA_HEADS = 8
MLA_Q_RANK = 256
MLA_KV_RANK = 128
MLA_NOPE = 64
MLA_ROPE = 32
MLA_QK = MLA_NOPE + MLA_ROPE
MLA_V = 64
MLA_WIDTH = MLA_HEADS * MLA_V
DSA_HEADS = 8
DSA_KV_HEADS = 2
DSA_GROUP = DSA_HEADS // DSA_KV_HEADS
DSA_DIM = 64
DSA_WIDTH = DSA_HEADS * DSA_DIM
DSA_ROT = DSA_DIM // 4
IDX_HEADS = 8
IDX_DIM = 32
IDX_ROT = IDX_DIM // 4
TOPK_MAX = 256
ROPE_THETA = 500000.0
NORM_EPS = 1e-6

IN_SPLITS = (
    MLA_Q_RANK, MLA_KV_RANK, MLA_ROPE, MLA_WIDTH,
    DSA_HEADS * DSA_DIM, DSA_KV_HEADS * DSA_DIM, DSA_KV_HEADS * DSA_DIM, DSA_WIDTH,
    IDX_HEADS * IDX_DIM, IDX_DIM, IDX_HEADS,
    D_MODEL, D_MODEL,
)

LANES = 128
MXU_WIDTH = 256
VMEM_LIMIT = 56 * 1024 * 1024
INT_MIN = -(2 ** 31)
F32_LOWEST = float(np.finfo(np.float32).min)
BF16_ROWS = 16
DIAG_SPLIT = 4
FINE_BITS = 17
NEG_BIG = -1e30
LOG2_E = 1.4426950408889634
FAST_MAX_SHIFT = 60.0
BOUND_MARGIN = 1.02

OFF_A = 0
W_A = MLA_Q_RANK + MLA_KV_RANK + LANES
OFF_B = OFF_A + W_A
W_B = DSA_WIDTH + 2 * DSA_KV_HEADS * DSA_DIM
OFF_I = OFF_B + W_B
W_I = IDX_HEADS * IDX_DIM + 2 * LANES
OFF_G = OFF_I + W_I
W_G = MLA_WIDTH + DSA_WIDTH + 2 * D_MODEL
W_ALL = OFF_G + W_G
PROJ_ROWS = 256
GATE_CHUNK = 512
N_GATE_CHUNKS = W_G // GATE_CHUNK


TM_PROJ = 512
TQ_MLA, TK_MLA = 512, 512
MLA_HEADS_PER_STEP = 8
TQ_DSA, TK_DSA = 512, 512
TM_MERGE = 1024

ROPE_LAYOUTS = ((LANES, MLA_NOPE, MLA_ROPE, 0), (DSA_DIM, 0, DSA_ROT, MLA_ROPE // 2),
                (IDX_DIM, 0, IDX_ROT, MLA_ROPE // 2 + DSA_ROT // 2))
ROW_A, ROW_D, ROW_I = 0, 1, 2
FREQ_ROWS = 32
ONE_ROW = FREQ_ROWS - 1


def _rms(v, width):
    return lax.rsqrt(jnp.sum(v * v, axis=-1, keepdims=True) * (1.0 / width) + NORM_EPS)


def _rope(v, cos, sin_lo, sin_hi, half):
    return (v * cos + pltpu.roll(v, LANES - half, 1) * sin_lo + pltpu.roll(v, half, 1) * sin_hi)


def _bf16_dot(a, b):
    return jnp.dot(a, b, preferred_element_type=jnp.float32)


def _nt_dot(a, b):
    return lax.dot_general(a, b, (((1,), (1,)), ((), ())), preferred_element_type=jnp.float32)


def _proj_kernel(x_ref, pos_ref, invf_ref, spread_ref, gx_ref, gcq_ref, gckv_ref, gq_ref, gk_ref, gdq_ref, gdk_ref,
                 w_ref, wuq_ref, wuk_ref, wuv_ref,
                 qa_ref, ka_ref, va_ref, qb_ref, kb_ref, vb_ref, qi_ref, ki_ref, wi_ref, g_ref):
    bf16 = jnp.bfloat16
    subs = [slice(r0, r0 + PROJ_ROWS) for r0 in range(0, x_ref.shape[1], PROJ_ROWS)]
    hs = []
    for rows in subs:
        x = x_ref[0, rows]
        hs.append((x * _rms(x, D_MODEL) * gx_ref[...]).astype(bf16))
    pas = [_bf16_dot(h, w_ref[:, OFF_A:OFF_A + W_A]) for h in hs]
    tabs = [_rope_tables(pos_ref[0, :, rows], invf_ref, spread_ref) for rows in subs]
    lat = []
    for pa in pas:
        c_q = pa[:, :MLA_Q_RANK]
        c_kv = pa[:, MLA_Q_RANK:MLA_Q_RANK + MLA_KV_RANK]
        lat.append(((c_q * _rms(c_q, MLA_Q_RANK) * gcq_ref[...]).astype(bf16),
                    (c_kv * _rms(c_kv, MLA_KV_RANK) * gckv_ref[...]).astype(bf16)))
    pbs = [_bf16_dot(h, w_ref[:, OFF_B:OFF_B + W_B]) for h in hs]
    pis = [_bf16_dot(h, w_ref[:, OFF_I:OFF_I + W_I]) for h in hs]
    ups = [(_bf16_dot(cqn, wuq_ref[...]),
            _bf16_dot(ckvn, wuk_ref[...]),
            _bf16_dot(ckvn, wuv_ref[...])) for cqn, ckvn in lat]
    for n, rows in enumerate(subs):
        kpe = pas[n][:, MLA_Q_RANK + MLA_KV_RANK:]
        _proj_epilogue(rows, (tabs[n], kpe) + ups[n] + (pbs[n], pis[n]), gq_ref, gk_ref, gdq_ref, gdk_ref,
                       qa_ref, ka_ref, va_ref, qb_ref, kb_ref, vb_ref, qi_ref, ki_ref, wi_ref)
    for rows, h in zip(subs, hs):
        for c in range(N_GATE_CHUNKS):
            cols = slice(c * GATE_CHUNK, (c + 1) * GATE_CHUNK)
            g_ref[0, rows, cols] = _bf16_dot(h, w_ref[:, OFF_G + c * GATE_CHUNK:OFF_G + (c + 1) * GATE_CHUNK]).astype(g_ref.dtype)


def _rope_tables(pos_row, invf_ref, spread_ref):
    bf16 = jnp.bfloat16
    ang = invf_ref[...] * pos_row.astype(jnp.float32)
    trig = jnp.concatenate([jnp.cos(ang), jnp.sin(ang)], axis=0)
    t1 = trig.astype(bf16)
    r1 = trig - t1.astype(jnp.float32)
    t2 = r1.astype(bf16)
    t3 = (r1 - t2.astype(jnp.float32)).astype(bf16)
    return lax.dot_general(jnp.concatenate([t1, t2, t3], axis=0), spread_ref[...], (((0,), (0,)), ((), ())),
                           preferred_element_type=jnp.float32)


def _proj_epilogue(rows, vals, gq_ref, gk_ref, gdq_ref, gdk_ref,
                   qa_ref, ka_ref, va_ref, qb_ref, kb_ref, vb_ref, qi_ref, ki_ref, wi_ref):
    tabs, kpe, q_up, k_up, v_up, pb, pi = vals
    lane = lax.broadcasted_iota(jnp.int32, (1, LANES), 1)
    bf16 = jnp.bfloat16

    def tables(n):
        return tuple(tabs[:, (3 * n + c) * LANES:(3 * n + c + 1) * LANES] for c in range(3))

    for c in range(MLA_WIDTH // LANES):
        va_ref[0, 0, c * LANES:(c + 1) * LANES, rows] = v_up[:, c * LANES:(c + 1) * LANES].T.astype(bf16)
    cos_a, sl_a, sh_a = tables(ROW_A)
    sin_a = sl_a + sh_a
    q_scale = MLA_QK ** -0.5 * LOG2_E
    kpe_g = kpe * gk_ref[...]
    k_rot = (pltpu.roll(kpe_g, LANES - MLA_ROPE // 2, 1) * sl_a + pltpu.roll(kpe_g, MLA_ROPE // 2, 1) * sh_a)
    for hd in range(MLA_HEADS):
        qh = q_up[:, hd * LANES:(hd + 1) * LANES]
        q_sw = q_up[:, (MLA_HEADS + hd) * LANES:(MLA_HEADS + hd + 1) * LANES]
        q_rot = qh * gq_ref[0:1, :] * cos_a + q_sw * gq_ref[1:2, :] * sin_a
        qa_ref[0, hd, rows] = (q_rot * (_rms(qh, MLA_QK) * q_scale)).astype(bf16)
        kh = k_up[:, hd * LANES:(hd + 1) * LANES] + kpe
        ka_ref[0, hd, rows] = ((kh * gk_ref[...] * cos_a + k_rot) * _rms(kh, MLA_QK)).astype(bf16)

    cos_d, sl_d, sh_d = tables(ROW_D)
    low = lane < DSA_DIM

    def dsa_norm_rope(blk, gain):
        sq = blk * blk
        r_lo = lax.rsqrt(jnp.sum(jnp.where(low, sq, 0.0), axis=-1, keepdims=True) * (1.0 / DSA_DIM) + NORM_EPS)
        r_hi = lax.rsqrt(jnp.sum(jnp.where(low, 0.0, sq), axis=-1, keepdims=True) * (1.0 / DSA_DIM) + NORM_EPS)
        return _rope(blk * jnp.where(low, r_lo, r_hi) * gain, cos_d, sl_d, sh_d, DSA_ROT // 2)

    dsa_scale = DSA_DIM ** -0.5 * LOG2_E
    for j in range(DSA_WIDTH // LANES):
        blk = pb[:, j * LANES:(j + 1) * LANES]
        qn = (dsa_norm_rope(blk, gdq_ref[...]) * dsa_scale).astype(bf16)
        qb_ref[0, j, rows] = jnp.where(low, qn, jnp.zeros_like(qn))
        qb_ref[0, DSA_GROUP + j, rows] = jnp.where(low, jnp.zeros_like(qn), qn)
    kb_ref[0, rows] = dsa_norm_rope(pb[:, DSA_WIDTH:DSA_WIDTH + LANES], gdk_ref[...]).astype(bf16)
    vb_ref[0, 0, :, rows] = pb[:, DSA_WIDTH + LANES:].T.astype(bf16)

    cos_i, sl_i, sh_i = tables(ROW_I)
    n_qi = IDX_HEADS * IDX_DIM // LANES
    for j in range(n_qi + 1):
        blk = _rope(pi[:, j * LANES:(j + 1) * LANES], cos_i, sl_i, sh_i, IDX_ROT // 2).astype(bf16)
        if j < n_qi:
            qi_ref[0, rows, j * LANES:(j + 1) * LANES] = blk
        else:
            ki_ref[0, rows] = blk
    wi_t = (pi[:, (n_qi + 1) * LANES:] * (IDX_HEADS ** -0.5 * IDX_DIM ** -0.5)).T
    wi_ref[0, :, rows] = wi_t[0:IDX_HEADS, :]


def _softmax_tile(s, m, l, shift):
    if shift is None:
        m_new = jnp.maximum(m, jnp.max(s, axis=0, keepdims=True))
        alpha = jnp.exp2(m - m_new)
        p = jnp.exp2(s - m_new)
        return m_new, alpha * l + jnp.sum(p, axis=0, keepdims=True), alpha, p.astype(jnp.bfloat16)
    p = jnp.exp2(s - shift)
    return m, l + jnp.sum(p, axis=0, keepdims=True), None, p.astype(jnp.bfloat16)


def _by_shift(shift_ref, body):
    shift = shift_ref[0]

    @pl.when(shift <= FAST_MAX_SHIFT)
    def _():
        body(shift)

    @pl.when(shift > FAST_MAX_SHIFT)
    def _():
        body(None)


def _mla_kernel(shift_ref, q_ref, k_ref, vt_ref, o_ref, *, tq, tk, heads):
    i = pl.program_id(2)
    q_start = i * tq
    n_full = q_start // tk
    n_all = (q_start + tq + tk - 1) // tk
    tv = vt_ref.shape[-1]
    kpos0 = lax.broadcasted_iota(jnp.int32, (tk, tq), 0)
    qpos = q_start + lax.broadcasted_iota(jnp.int32, (tk, tq), 1)

    def attend(shift):
        def step(j, carry, masked):
            off = pl.multiple_of(j * tk, tk)
            scores = [_nt_dot(k_ref[0, hd, pl.ds(off, tk), :], q_ref[0, hd]) for hd in range(heads)]
            out = []
            visible = kpos0 + off <= qpos if masked else None
            shift_t = jnp.where(visible, shift, -NEG_BIG) if masked and shift is not None else shift
            for hd in range(heads):
                m, l, acc = carry[hd]
                s = scores[hd]
                if masked and shift is None:
                    s = jnp.where(visible, s, NEG_BIG)
                m, l, alpha, pb = _softmax_tile(s, m, l, shift_t)
                if alpha is not None:
                    acc = alpha * acc
                for c in range(tk // tv):
                    vt = vt_ref[0, j * (tk // tv) + c, hd * MLA_V:(hd + 1) * MLA_V, :]
                    acc = acc + _bf16_dot(vt, pb[c * tv:(c + 1) * tv])
                out.append((m, l, acc))
            return tuple(out)

        init = tuple((jnp.full((1, tq), NEG_BIG, jnp.float32), jnp.zeros((1, tq), jnp.float32),
                      jnp.zeros((MLA_V, tq), jnp.float32)) for _ in range(heads))
        carry = lax.fori_loop(0, n_full, functools.partial(step, masked=False), init)
        carry = lax.fori_loop(n_full, n_all, functools.partial(step, masked=True), carry)
        outs = [acc / l for (_, l, acc) in carry]
        for c in range(heads // 2):
            o_ref[0, :, c * LANES:(c + 1) * LANES] = jnp.concatenate(outs[2 * c:2 * c + 2], axis=0).T.astype(o_ref.dtype)

    _by_shift(shift_ref, attend)


def _dsa_kernel(shift_ref, qb_ref, kb_ref, vt_ref, qi_ref, ki_ref, wt_ref, o_ref,
                sc_ref, sb_ref, qim_ref, *, tq, tk, k_top):
    i = pl.program_id(1)
    q_start = i * tq
    n_kv = (q_start + tq + tk - 1) // tk
    n_heads = IDX_HEADS
    tv = vt_ref.shape[-1]

    lane_q = lax.broadcasted_iota(jnp.int32, (tq, LANES), 1)
    per_blk = LANES // IDX_DIM
    for hd in range(IDX_HEADS):
        blk = qi_ref[0, :, (hd // per_blk) * LANES:(hd // per_blk + 1) * LANES]
        slot = hd % per_blk
        in_head = (lane_q >= slot * IDX_DIM) & (lane_q < (slot + 1) * IDX_DIM)
        qim_ref[hd * tq:(hd + 1) * tq, :] = jnp.where(in_head, blk, jnp.zeros_like(blk))

    kpos0 = lax.broadcasted_iota(jnp.int32, (tk, tq), 0)
    qpos = q_start + lax.broadcasted_iota(jnp.int32, (tk, tq), 1)
    w_rows = [wt_ref[0, hd:hd + 1, :] for hd in range(IDX_HEADS)]

    def score_tile(j, _):
        off = pl.multiple_of(j * tk, tk)
        s_all = _nt_dot(ki_ref[0, pl.ds(off, tk), :], qim_ref[...])
        sc = w_rows[0] * jnp.maximum(s_all[:, 0:tq], 0.0)
        for hd in range(1, n_heads):
            sc = sc + w_rows[hd] * jnp.maximum(s_all[:, hd * tq:(hd + 1) * tq], 0.0)
        sc = jnp.where(kpos0 + off <= qpos, sc, -jnp.inf)
        sc_ref[j] = sc
        sb_ref[j] = sc.astype(jnp.bfloat16)
        return 0

    lax.fori_loop(0, n_kv, score_tile, 0)

    def pattern_value(u):
        key = u ^ INT_MIN
        return lax.bitcast_convert_type(key ^ ((key >> 31) & 0x7FFFFFFF), jnp.float32)

    assert tq == tk
    jd = n_kv - 1
    step = tk // DIAG_SPLIT
    all_q = slice(0, tq)
    groups = [(slice(g * step, (g + 1) * step), all_q if g == 0 else slice(g * step, tq)) for g in range(DIAG_SPLIT)]

    def lanes(row, q):
        if q is all_q:
            return row
        return jnp.broadcast_to(row, (8, tq))[:, q][0:1]

    def widen(part, q):
        return part if q is all_q else jnp.concatenate([jnp.zeros((part.shape[0], q.start), part.dtype), part], axis=1)

    def count(pred):
        def body(j, acc):
            hit = jnp.where(pred(sc_ref[j], kpos0 + j * tk, all_q), 1, 0)
            return acc + jnp.sum(hit.reshape(tk // 8, 8, tq), axis=0)
        acc = lax.fori_loop(0, jd, body, jnp.zeros((8, tq), jnp.int32))
        for rows, q in groups:
            width = tq - (0 if q is all_q else q.start)
            kpos = jd * tk + rows.start + lax.broadcasted_iota(jnp.int32, (step, width), 0)
            hit = jnp.where(pred(sc_ref[jd, rows, q], kpos, q), 1, 0)
            acc = acc + widen(jnp.sum(hit.reshape(step // 8, 8, width), axis=0), q)
        return jnp.sum(acc, axis=0, keepdims=True)

    def count_coarse(cand_b):
        one, zero = jnp.ones((), jnp.bfloat16), jnp.zeros((), jnp.bfloat16)

        def slabs(j, rows, q):
            cand_q = lanes(cand_b.astype(jnp.float32), q).astype(jnp.bfloat16)
            part = None
            for r in range(rows.start // BF16_ROWS, rows.stop // BF16_ROWS):
                hit = jnp.where(sb_ref[j, r * BF16_ROWS:(r + 1) * BF16_ROWS, q] >= cand_q, one, zero)
                part = hit if part is None else part + hit
            return part.astype(jnp.float32)

        acc = lax.fori_loop(0, jd, lambda j, acc: acc + slabs(j, slice(0, tk), all_q),
                            jnp.zeros((BF16_ROWS, tq), jnp.float32))
        for rows, q in groups:
            acc = acc + widen(slabs(jd, rows, q), q)
        return jnp.sum(acc, axis=0, keepdims=True)

    def coarse_bit(it, t16):
        cand = t16 | jnp.left_shift(jnp.int32(1), 15 - it)
        cand32 = jnp.left_shift(cand, 16) | jnp.where(cand < 0x8000, 0xFFFF, 0)
        cnt = count_coarse(pattern_value(cand32).astype(jnp.bfloat16))
        return jnp.where(cnt >= k_top, cand, t16)

    t16 = lax.fori_loop(0, 16, coarse_bit, jnp.zeros((1, tq), jnp.int32))
    few = t16 < 0x0080
    base = jnp.left_shift(t16, 16) + jnp.where(t16 < 0x8000, 0x7FFF, -0x8000)
    base = jnp.where(few, 0x00800000, base)

    def fine_bit(it, carry):
        t, cnt_t = carry
        cand = t | jnp.left_shift(jnp.int32(1), FINE_BITS - 1 - it)
        cand_f = pattern_value(base + cand)
        cnt = count(lambda sc, kpos, q: sc >= lanes(cand_f, q))
        take = cnt >= k_top
        return jnp.where(take, cand, t), jnp.where(take, cnt, cnt_t)

    t, cnt_ge = lax.fori_loop(0, FINE_BITS, fine_bit,
                              (jnp.zeros((1, tq), jnp.int32), jnp.full((1, tq), k_top, jnp.int32)))
    thr = jnp.where(few, F32_LOWEST, pattern_value(base + t))
    tied = jnp.logical_not(few) & ((cnt_ge > k_top) | (t == 0))

    def tie_cut():
        cnt_gt = count(lambda sc, kpos, q: sc > lanes(thr, q))
        need = k_top - cnt_gt

        def index_bit(it, c):
            cand = c | jnp.left_shift(jnp.int32(1), 12 - it)
            below = count(lambda sc, kpos, q: (sc == lanes(thr, q)) & (kpos < lanes(cand, q)))
            return jnp.where(below < need, cand, c)

        assert sc_ref.shape[0] <= 8
        tile_bits = (n_kv > 1).astype(jnp.int32) + (n_kv > 2).astype(jnp.int32) + (n_kv > 4).astype(jnp.int32)
        index_bits = (tk.bit_length() - 1) + tile_bits
        c = lax.fori_loop(13 - index_bits, 13, index_bit, jnp.zeros((1, tq), jnp.int32))
        return jnp.where(tied, c, jnp.int32(2 ** 30))

    cut = lax.cond(jnp.max(tied.astype(jnp.int32)) > 0, tie_cut,
                   lambda: jnp.full((1, tq), 2 ** 30, jnp.int32))

    assert tq >= MXU_WIDTH

    def attend_all(shift):
        def attend(j, carry):
            off = pl.multiple_of(j * tk, tk)
            kt = kb_ref[0, pl.ds(off, tk), :]
            scores = [_nt_dot(kt, qb_ref[0, hd]) for hd in range(n_heads)]
            sc = sc_ref[j]
            sel = (sc > thr) | ((sc == thr) & (kpos0 + off <= cut))
            shift_t = None if shift is None else jnp.where(sel, shift, -NEG_BIG)
            out = []
            for hd in range(n_heads):
                m, l, acc = carry[hd]
                s = jnp.where(sel, scores[hd], NEG_BIG) if shift is None else scores[hd]
                m, l, alpha, pb = _softmax_tile(s, m, l, shift_t)
                if alpha is not None:
                    acc = alpha * acc
                g = hd // DSA_GROUP
                for c in range(tk // tv):
                    vt = vt_ref[0, j * (tk // tv) + c, g * DSA_DIM:(g + 1) * DSA_DIM, :]
                    acc = acc + _bf16_dot(vt, pb[c * tv:(c + 1) * tv])
                out.append((m, l, acc))
            return tuple(out)

        init = tuple((jnp.full((1, tq), NEG_BIG, jnp.float32), jnp.zeros((1, tq), jnp.float32),
                      jnp.zeros((DSA_DIM, tq), jnp.float32)) for _ in range(n_heads))
        carry = lax.fori_loop(0, n_kv, attend, init)
        outs = [acc / l for (_, l, acc) in carry]
        for j in range(DSA_GROUP):
            o_ref[0, :, j * LANES:(j + 1) * LANES] = jnp.concatenate(
                [outs[j], outs[DSA_GROUP + j]], axis=0).T.astype(o_ref.dtype)

    _by_shift(shift_ref, attend_all)


def _merge_kernel(x_ref, oa_ref, ob_ref, g_ref, bm_ref, wa_ref, wb_ref, wo_ref, out_ref):
    f32 = jnp.float32
    ga = g_ref[:, 0:MLA_WIDTH].astype(f32)
    gb = g_ref[:, MLA_WIDTH:MLA_WIDTH + DSA_WIDTH].astype(f32)
    ma = g_ref[:, MLA_WIDTH + DSA_WIDTH:MLA_WIDTH + DSA_WIDTH + D_MODEL].astype(f32)
    mb = g_ref[:, MLA_WIDTH + DSA_WIDTH + D_MODEL:].astype(f32)
    ya = _bf16_dot((oa_ref[...].astype(f32) * (ga * jax.nn.sigmoid(ga))).astype(jnp.bfloat16), wa_ref[...])
    yb = _bf16_dot((ob_ref[...].astype(f32) * (gb * jax.nn.sigmoid(gb))).astype(jnp.bfloat16), wb_ref[...])
    merged = jax.nn.sigmoid(ma + bm_ref[0:1, :]) * ya + jax.nn.sigmoid(mb + bm_ref[1:2, :]) * yb
    out_ref[...] = x_ref[...] + _bf16_dot(merged.astype(jnp.bfloat16), wo_ref[...])


def _inv_freq(rot_dim):
    return ROPE_THETA ** (-jnp.arange(0, rot_dim, 2, dtype=jnp.float32) / rot_dim)


def _rope_inputs():
    invf = jnp.concatenate([_inv_freq(rot) for (_, _, rot, _) in ROPE_LAYOUTS])
    invf = jnp.pad(invf, (0, FREQ_ROWS - invf.shape[0]))[:, None]
    spread = np.zeros((2 * FREQ_ROWS, 3 * len(ROPE_LAYOUTS) * LANES), np.float32)
    for n, (period, start, rot, row0) in enumerate(ROPE_LAYOUTS):
        half = rot // 2
        for lane in range(LANES):
            p = lane % period - start
            cos_col, lo_col, hi_col = ((3 * n + c) * LANES + lane for c in range(3))
            if 0 <= p < rot:
                spread[row0 + p % half, cos_col] = 1.0
                spread[FREQ_ROWS + row0 + p % half, lo_col if p < half else hi_col] = -1.0 if p < half else 1.0
            else:
                spread[ONE_ROW, cos_col] = 1.0
    return invf, jnp.asarray(np.tile(spread, (3, 1)), jnp.bfloat16)


def _pad_lanes(a, start=0):
    return jnp.pad(a, ((0, 0), (start, LANES - start - a.shape[1])))


def _layout_params(w_in, mla_w_uq, mla_w_ukv, mla_q_gain, mla_k_gain, dsa_q_gain, dsa_k_gain,
                   w_branch_dsa):
    offs = np.concatenate([[0], np.cumsum(IN_SPLITS)])
    (c_q, c_kv, k_pe, gate_a, q_b, k_b, v_b, gate_b, q_i, k_i, w_i, m_a, m_b) = [
        w_in[:, offs[n]:offs[n + 1]] for n in range(len(IN_SPLITS))]

    def pair_heads(w):
        rows = w.shape[0]
        w = w.reshape(rows, DSA_KV_HEADS, DSA_GROUP, DSA_DIM)
        return jnp.transpose(w, (0, 2, 1, 3)).reshape(rows, DSA_WIDTH)

    w_all = jnp.concatenate([
        c_q, c_kv, _pad_lanes(k_pe, MLA_NOPE),
        pair_heads(q_b), k_b, v_b,
        q_i, jnp.tile(k_i, (1, LANES // IDX_DIM)), _pad_lanes(w_i),
        gate_a, pair_heads(gate_b), m_a, m_b], axis=1).astype(jnp.bfloat16)
    uq = mla_w_uq.reshape(MLA_Q_RANK, MLA_HEADS, MLA_QK)
    half = MLA_ROPE // 2

    def swap_rope(a):
        return jnp.concatenate([jnp.zeros_like(a[..., :MLA_NOPE]), a[..., MLA_NOPE + half:], a[..., MLA_NOPE:MLA_NOPE + half]], -1)

    pad_qk = lambda a: jnp.pad(a, ((0, 0), (0, 0), (0, LANES - MLA_QK))).reshape(MLA_Q_RANK, MLA_HEADS * LANES)
    wuq = jnp.concatenate([pad_qk(uq), pad_qk(swap_rope(uq))], axis=1)
    ukv = mla_w_ukv.reshape(MLA_KV_RANK, MLA_HEADS, MLA_NOPE + MLA_V)
    wuk = jnp.pad(ukv[:, :, :MLA_NOPE], ((0, 0), (0, 0), (0, LANES - MLA_NOPE))).reshape(MLA_KV_RANK, MLA_HEADS * LANES)
    wuv = ukv[:, :, MLA_NOPE:].reshape(MLA_KV_RANK, MLA_WIDTH)
    gq = jnp.concatenate([_pad_lanes(mla_q_gain[None, :]), _pad_lanes(swap_rope(mla_q_gain[None, :]))], axis=0)
    gk = _pad_lanes(mla_k_gain[None, :])
    gdq = jnp.tile(dsa_q_gain[None, :], (1, LANES // DSA_DIM))
    gdk = jnp.tile(dsa_k_gain[None, :], (1, LANES // DSA_DIM))
    wb_dsa = jnp.transpose(w_branch_dsa.reshape(DSA_KV_HEADS, DSA_GROUP, DSA_DIM, D_MODEL),
                           (1, 0, 2, 3)).reshape(DSA_WIDTH, D_MODEL)
    return (w_all, wuq.astype(jnp.bfloat16), wuk.astype(jnp.bfloat16), wuv.astype(jnp.bfloat16),
            gq, gk, gdq, gdk, wb_dsa.astype(jnp.bfloat16))


def _const_spec(shape):
    return pl.BlockSpec(shape, lambda *_: (0,) * len(shape))


def _layer(x, positions, invf, spread, norm_gain, w_in, b_merge, mla_q_norm, mla_w_uq, mla_kv_norm, mla_w_ukv,
           mla_q_gain, mla_k_gain, dsa_q_gain, dsa_k_gain, w_branch_mla, w_branch_dsa, w_out):
    bsz, s_len, d_model = x.shape
    assert d_model == D_MODEL and TM_PROJ % PROJ_ROWS == 0 and (bsz * s_len) % TM_MERGE == 0
    assert all(s_len % t == 0 for t in (TM_PROJ, TQ_MLA, TK_MLA, TQ_DSA, TK_DSA))
    k_top = min(TOPK_MAX, s_len // 4)
    (w_all, wuq, wuk, wuv, gq, gk, gdq, gdk, wb_dsa) = _layout_params(
        w_in, mla_w_uq, mla_w_ukv, mla_q_gain, mla_k_gain, dsa_q_gain, dsa_k_gain, w_branch_dsa)
    bf16, f32 = jnp.bfloat16, jnp.float32
    gmax = lambda g: jnp.max(jnp.abs(g))
    shift_a = (BOUND_MARGIN * MLA_QK ** 0.5 * LOG2_E * gmax(mla_q_gain) * gmax(mla_k_gain)).reshape(1).astype(f32)
    shift_b = (BOUND_MARGIN * DSA_DIM ** 0.5 * LOG2_E * gmax(dsa_q_gain) * gmax(dsa_k_gain)).reshape(1).astype(f32)
    smem = pl.BlockSpec(memory_space=pltpu.SMEM)

    tm = TM_PROJ
    tok = lambda w: pl.BlockSpec((1, tm, w), lambda b, i: (b, i, 0))
    head = pl.BlockSpec((1, MLA_HEADS, tm, LANES), lambda b, i: (b, 0, i, 0))
    outs = pl.pallas_call(
        _proj_kernel,
        grid=(bsz, s_len // tm),
        in_specs=[tok(D_MODEL), pl.BlockSpec((1, 1, tm), lambda b, i: (b, 0, i)),
                  _const_spec(invf.shape), _const_spec(spread.shape), _const_spec((1, D_MODEL)),
                  _const_spec((1, MLA_Q_RANK)), _const_spec((1, MLA_KV_RANK)),
                  _const_spec((2, LANES)), _const_spec((1, LANES)), _const_spec((1, LANES)), _const_spec((1, LANES)),
                  _const_spec((D_MODEL, W_ALL)), _const_spec((MLA_Q_RANK, 2 * MLA_HEADS * LANES)),
                  _const_spec((MLA_KV_RANK, MLA_HEADS * LANES)), _const_spec((MLA_KV_RANK, MLA_WIDTH))],
        out_specs=[head, head, pl.BlockSpec((1, 1, MLA_WIDTH, tm), lambda b, i: (b, i, 0, 0)),
                   head, tok(LANES),
                   pl.BlockSpec((1, 1, LANES, tm), lambda b, i: (b, i, 0, 0)),
                   tok(IDX_HEADS * IDX_DIM), tok(LANES),
                   pl.BlockSpec((1, IDX_HEADS, tm), lambda b, i: (b, 0, i)), tok(W_G)],
        out_shape=[jax.ShapeDtypeStruct((bsz, MLA_HEADS, s_len, LANES), bf16),
                   jax.ShapeDtypeStruct((bsz, MLA_HEADS, s_len, LANES), bf16),
                   jax.ShapeDtypeStruct((bsz, s_len // tm, MLA_WIDTH, tm), bf16),
                   jax.ShapeDtypeStruct((bsz, DSA_HEADS, s_len, LANES), bf16),
                   jax.ShapeDtypeStruct((bsz, s_len, LANES), bf16),
                   jax.ShapeDtypeStruct((bsz, s_len // tm, LANES, tm), bf16),
                   jax.ShapeDtypeStruct((bsz, s_len, IDX_HEADS * IDX_DIM), bf16),
                   jax.ShapeDtypeStruct((bsz, s_len, LANES), bf16),
                   jax.ShapeDtypeStruct((bsz, IDX_HEADS, s_len), f32),
                   jax.ShapeDtypeStruct((bsz, s_len, W_G), bf16)],
        compiler_params=pltpu.CompilerParams(dimension_semantics=("arbitrary", "arbitrary"),
                                             vmem_limit_bytes=VMEM_LIMIT),
        name="proj",
    )(x, positions[:, None, :], invf, spread, norm_gain[None, :], mla_q_norm[None, :], mla_kv_norm[None, :],
      gq, gk, gdq, gdk, w_all, wuq, wuk, wuv)
    qa, ka, vat, qb, kb, vbt, qi, ki, wit, gates = outs

    tq_a, tk_a, hps = TQ_MLA, TK_MLA, MLA_HEADS_PER_STEP
    o_a = pl.pallas_call(
        functools.partial(_mla_kernel, tq=tq_a, tk=tk_a, heads=hps),
        grid=(bsz, MLA_HEADS // hps, s_len // tq_a),
        in_specs=[smem, pl.BlockSpec((1, hps, tq_a, LANES), lambda b, hp, i: (b, hp, i, 0)),
                  pl.BlockSpec((1, hps, s_len, LANES), lambda b, hp, i: (b, hp, 0, 0)),
                  pl.BlockSpec((1, s_len // tm, hps * MLA_V, tm), lambda b, hp, i: (b, 0, hp, 0))],
        out_specs=pl.BlockSpec((1, tq_a, hps * MLA_V), lambda b, hp, i: (b, i, hp)),
        out_shape=jax.ShapeDtypeStruct((bsz, s_len, MLA_WIDTH), bf16),
        compiler_params=pltpu.CompilerParams(dimension_semantics=("arbitrary",) * 3,
                                             vmem_limit_bytes=VMEM_LIMIT),
        name="mla",
    )(shift_a, qa, ka, vat)

    tq_b, tk_b = TQ_DSA, TK_DSA
    row_b = lambda w: pl.BlockSpec((1, tq_b, w), lambda b, i: (b, i, 0))
    full_b = lambda w: pl.BlockSpec((1, s_len, w), lambda b, i: (b, 0, 0))
    o_b = pl.pallas_call(
        functools.partial(_dsa_kernel, tq=tq_b, tk=tk_b, k_top=k_top),
        grid=(bsz, s_len // tq_b),
        in_specs=[smem, pl.BlockSpec((1, DSA_HEADS, tq_b, LANES), lambda b, i: (b, 0, i, 0)), full_b(LANES),
                  pl.BlockSpec((1, s_len // tm, LANES, tm), lambda b, i: (b, 0, 0, 0)),
                  row_b(IDX_HEADS * IDX_DIM), full_b(LANES),
                  pl.BlockSpec((1, IDX_HEADS, tq_b), lambda b, i: (b, 0, i))],
        out_specs=row_b(DSA_WIDTH),
        out_shape=jax.ShapeDtypeStruct((bsz, s_len, DSA_WIDTH), bf16),
        scratch_shapes=[pltpu.VMEM((s_len // tk_b, tk_b, tq_b), f32),
                        pltpu.VMEM((s_len // tk_b, tk_b, tq_b), bf16),
                        pltpu.VMEM((IDX_HEADS * tq_b, LANES), bf16)],
        compiler_params=pltpu.CompilerParams(dimension_semantics=("arbitrary", "arbitrary"),
                                             vmem_limit_bytes=VMEM_LIMIT),
        name="dsa",
    )(shift_b, qb, kb, vbt, qi, ki, wit)

    tm_m = TM_MERGE
    n_tok = bsz * s_len
    flat = lambda w: pl.BlockSpec((tm_m, w), lambda i: (i, 0))
    out = pl.pallas_call(
        _merge_kernel,
        grid=(n_tok // tm_m,),
        in_specs=[flat(D_MODEL), flat(MLA_WIDTH), flat(DSA_WIDTH), flat(W_G), _const_spec((2, D_MODEL)),
                  _const_spec((MLA_WIDTH, D_MODEL)), _const_spec((DSA_WIDTH, D_MODEL)),
                  _const_spec((D_MODEL, D_MODEL))],
        out_specs=flat(D_MODEL),
        out_shape=jax.ShapeDtypeStruct((n_tok, D_MODEL), f32),
        compiler_params=pltpu.CompilerParams(dimension_semantics=("arbitrary",),
                                             vmem_limit_bytes=VMEM_LIMIT),
        name="merge",
    )(x.reshape(n_tok, D_MODEL), o_a.reshape(n_tok, MLA_WIDTH), o_b.reshape(n_tok, DSA_WIDTH),
      gates.reshape(n_tok, W_G), b_merge, w_branch_mla.astype(bf16), wb_dsa, w_out.astype(bf16))
    return out.reshape(bsz, s_len, D_MODEL)


def kernel(x, positions, norm_gain, w_in, b_merge, mla_q_norm, mla_w_uq, mla_kv_norm, mla_w_ukv,
           mla_q_gain, mla_k_gain, dsa_q_gain, dsa_k_gain, w_branch_mla, w_branch_dsa, w_out):
    invf, spread = _rope_inputs()
    for l in range(norm_gain.shape[0]):
        x = _layer(x, positions, invf, spread, norm_gain[l], w_in[l], b_merge[l], mla_q_norm[l], mla_w_uq[l],
                   mla_kv_norm[l], mla_w_ukv[l], mla_q_gain[l], mla_k_gain[l], dsa_q_gain[l], dsa_k_gain[l],
                   w_branch_mla[l], w_branch_dsa[l], w_out[l])
    return x
```
